```python
import jax, jax.numpy as jnp
from jax import lax
import numpy as np

D_MODEL = 1024
BATCH = 8
SEQ = 2048
DEPTH = 4

CHUNK = 64
Q_BLOCK = 128
N_MIXERS = 3
NORM_EPS = 1e-6

RW_HEAD = 64
RW_HEADS = D_MODEL // RW_HEAD
DECAY_LORA = max(32, round(1.8 * D_MODEL ** 0.5 / 32) * 32)
AAA_LORA = max(32, round(1.8 * D_MODEL ** 0.5 / 32) * 32)
MV_LORA = max(32, round(1.3 * D_MODEL ** 0.5 / 32) * 32)
GATE_LORA = max(32, round(0.6 * D_MODEL ** 0.8 / 32) * 32)
GN_EPS = 64e-5

MLA_HEADS = 16
MLA_NOPE = 64
MLA_ROPE = 32
MLA_V = 64
MLA_Q_LORA = 768
MLA_KV_LORA = 256
MLA_SCALE = (MLA_NOPE + MLA_ROPE) ** -0.5
ROPE_BASE = 10000.0

FOX_HEADS = 16
FOX_HEAD = D_MODEL // FOX_HEADS
FOX_SCALE = FOX_HEAD ** -0.5

D_FF = 2816
N_EXPERTS = 8
TOP_K = 2
D_FF_EXPERT = 2816

N_A = (DEPTH + 2) // 3
N_B = (DEPTH + 1) // 3
N_C = DEPTH // 3
N_VRES = max(N_A - 1, 0)
N_DENSE = (DEPTH + 1) // 2
N_MOE = DEPTH // 2

kernel_name = 'hybrid_rwkv7_mla_fox_moe_adaln_trunk'


def rms_norm(x, g):
    xf = x.astype(jnp.float32)
    y = xf * lax.rsqrt(jnp.mean(xf * xf, axis=-1, keepdims=True) + NORM_EPS)
    return (y * g.astype(jnp.float32)).astype(x.dtype)


def ada_modulate(h, shift, scale):
    return h * (1 + scale[:, None, :]) + shift[:, None, :]


def apply_rope(x, positions):
    half = x.shape[-1] // 2
    inv_freq = ROPE_BASE ** (-jnp.arange(half, dtype=jnp.float32) / half)
    ang = positions.astype(jnp.float32)[..., None] * inv_freq
    cos = jnp.cos(ang)[:, :, None, :]
    sin = jnp.sin(ang)[:, :, None, :]
    xf = x.astype(jnp.float32)
    x1, x2 = xf[..., :half], xf[..., half:]
    return jnp.concatenate([x1 * cos - x2 * sin, x2 * cos + x1 * sin], axis=-1).astype(x.dtype)


def block_attention(q, k, v, per_frame, cum_log_f=None):
    seq = q.shape[1]
    outs = []
    for blk in range(seq // Q_BLOCK):
        q0, q1 = blk * Q_BLOCK, (blk + 1) * Q_BLOCK
        logits = jnp.einsum('bqhd,bkhd->bhqk', q[:, q0:q1], k[:, :q1]).astype(jnp.float32)
        t = jnp.arange(q0, q1)[:, None]
        s = jnp.arange(q1)[None, :]
        allowed = (s <= t) if per_frame else (s // CHUNK <= t // CHUNK)
        if cum_log_f is not None:
            logits = logits + cum_log_f[:, :, q0:q1, None] - cum_log_f[:, :, None, :q1]
        logits = jnp.where(allowed, logits, -jnp.inf)
        probs = jax.nn.softmax(logits, axis=-1).astype(v.dtype)
        outs.append(jnp.einsum('bhqk,bkhd->bqhd', probs, v[:, :q1]))
    return jnp.concatenate(outs, axis=1)


def rwkv7_time_mix(h, v_first, vres, mu, w_rkv, w_o, w0, w1, w2, a0, a1, a2, g1, g2,
                   k_k, k_a, r_k, lnx_g, lnx_b):
    B, S, D = h.shape
    H, N = RW_HEADS, RW_HEAD
    f32 = jnp.float32
    prev = jnp.pad(h, ((0, 0), (1, 0), (0, 0)))[:, :-1]
    delta = prev - h
    xr, xw, xk, xv, xa, xg = [h + delta * mu[n] for n in range(6)]
    r = xr @ w_rkv[0]
    k = xk @ w_rkv[1]
    v = xv @ w_rkv[2]
    w_raw = (w0 + jnp.tanh(xw @ w1) @ w2).astype(f32)
    log_w = -jax.nn.softplus(-w_raw) - 0.5
    decay = jnp.exp(-jnp.exp(log_w))
    if vres is None:
        v_first = v
    else:
        v0, v1, v2 = vres
        v = v + (v_first - v) * jax.nn.sigmoid(v0 + (xv @ v1) @ v2)
    a = jax.nn.sigmoid(a0 + (xa @ a1) @ a2)
    g = jax.nn.sigmoid(xg @ g1) @ g2
    split = lambda t: t.reshape(B, S, H, N)
    kk = split(k * k_k).astype(f32)
    kk = kk / jnp.maximum(jnp.sqrt(jnp.sum(kk * kk, axis=-1, keepdims=True)), 1e-12)
    k = k * (1 + (a - 1) * k_a)
    r4, k4, v4, a4, w4 = split(r), split(k), split(v), split(a), split(decay)

    def step(state, inp):
        r_t, w_t, k_t, v_t, kk_t, a_t = inp
        sa = jnp.einsum('bhvk,bhk->bhv', state, -kk_t)
        state = (state * w_t[:, :, None, :]
                 + sa[..., None] * (kk_t * a_t)[:, :, None, :]
                 + v_t[..., None] * k_t[:, :, None, :])
        return state, jnp.einsum('bhvk,bhk->bhv', state, r_t)

    to_time = lambda t: jnp.moveaxis(t.astype(f32), 1, 0)
    xs = (to_time(r4), to_time(w4), to_time(k4), to_time(v4), to_time(kk), to_time(a4))
    state0 = jnp.zeros((B, H, N, N), f32)
    _, y = lax.scan(step, state0, xs)
    y = jnp.moveaxis(y, 0, 1)
    mean = jnp.mean(y, axis=-1, keepdims=True)
    var = jnp.var(y, axis=-1, keepdims=True)
    y = ((y - mean) * lax.rsqrt(var + GN_EPS)).reshape(B, S, D) * lnx_g + lnx_b
    bonus = jnp.sum(r4 * k4 * r_k, axis=-1, keepdims=True) * v4
    y = (y + bonus.reshape(B, S, D)).astype(h.dtype)
    return (y * g) @ w_o, v_first


def mla_mix(h, positions, w_down, q_norm_g, kv_norm_g, w_uq, w_ukv, w_o):
    B, S, _ = h.shape
    down = h @ w_down
    c_q = rms_norm(down[..., :MLA_Q_LORA], q_norm_g)
    c_kv = rms_norm(down[..., MLA_Q_LORA:MLA_Q_LORA + MLA_KV_LORA], kv_norm_g)
    k_rope = apply_rope(down[..., MLA_Q_LORA + MLA_KV_LORA:][:, :, None, :], positions)
    q = (c_q @ w_uq).reshape(B, S, MLA_HEADS, MLA_NOPE + MLA_ROPE)
    q = jnp.concatenate([q[..., :MLA_NOPE], apply_rope(q[..., MLA_NOPE:], positions)], axis=-1) * MLA_SCALE
    kv = (c_kv @ w_ukv).reshape(B, S, MLA_HEADS, MLA_NOPE + MLA_V)
    k = jnp.concatenate([kv[..., :MLA_NOPE],
                         jnp.broadcast_to(k_rope, (B, S, MLA_HEADS, MLA_ROPE))], axis=-1)
    o = block_attention(q, k, kv[..., MLA_NOPE:], per_frame=False)
    return o.reshape(B, S, MLA_HEADS * MLA_V) @ w_o


def fox_mix(h, w_in, b_f, q_norm_g, k_norm_g, w_o):
    B, S, D = h.shape
    proj = h @ w_in
    q, k, v, f_logit, o_gate = jnp.split(proj, [D, 2 * D, 3 * D, 3 * D + FOX_HEADS], axis=-1)
    heads = lambda t: t.reshape(B, S, FOX_HEADS, FOX_HEAD)
    q = rms_norm(heads(q), q_norm_g) * FOX_SCALE
    k = rms_norm(heads(k), k_norm_g)
    log_f = jax.nn.log_sigmoid((f_logit + b_f).astype(jnp.float32))
    cum_log_f = jnp.transpose(lax.cumsum(log_f, axis=1), (0, 2, 1))
    o = block_attention(q, k, heads(v), per_frame=True, cum_log_f=cum_log_f)
    o = o.reshape(B, S, D) * jax.nn.sigmoid(o_gate)
    return o @ w_o


def swiglu(h, w_gate_up, w_down):
    gate, up = jnp.split(h @ w_gate_up, 2, axis=-1)
    return (jax.nn.silu(gate) * up) @ w_down


def moe_swiglu(h, w_router, b_router, w_gate_up, w_down):
    logits = (h @ w_router + b_router).astype(jnp.float32)
    top_vals, top_idx = lax.top_k(logits, TOP_K)
    top_w = jax.nn.softmax(top_vals, axis=-1)
    combine = jnp.sum(jax.nn.one_hot(top_idx, N_EXPERTS, dtype=jnp.float32) * top_w[..., None], axis=-2)
    out = jnp.zeros_like(h)
    for e in range(N_EXPERTS):
        out = out + combine[..., e:e + 1].astype(h.dtype) * swiglu(h, w_gate_up[e], w_down[e])
    return out


def setup_inputs(seed: int = 0) -> dict:
    key = jax.random.key(seed)
    ks = iter(jax.random.split(key, 64))
    D = D_MODEL

    def normal(shape, scale):
        return jax.random.normal(next(ks), shape, jnp.float32) * scale

    def gain(shape):
        return 1.0 + normal(shape, 0.02)

    x = normal((BATCH, SEQ, D), 1.0)
    c = normal((BATCH, D), 1.0)
    positions = (jax.random.randint(next(ks), (BATCH, 1), 0, 1024, dtype=jnp.int32)
                 + jnp.arange(SEQ, dtype=jnp.int32)[None, :])
    return {
        'x': x,
        'c': c,
        'positions': positions,
        'ada_w': normal((DEPTH, D, 6 * D), 0.5 * D ** -0.5),
        'ada_b': normal((DEPTH, 6 * D), 0.02),
        'norm_mix_g': gain((DEPTH, D)),
        'norm_ffn_g': gain((DEPTH, D)),
        'final_norm_g': gain((D,)),
        'rw_mu': jax.random.uniform(next(ks), (N_A, 6, D), jnp.float32),
        'rw_w_rkv': normal((N_A, 3, D, D), D ** -0.5),
        'rw_w_o': normal((N_A, D, D), D ** -0.5),
        'rw_w0': jax.random.uniform(next(ks), (N_A, D), jnp.float32, -6.5, -1.5),
        'rw_w1': normal((N_A, D, DECAY_LORA), D ** -0.5),
        'rw_w2': normal((N_A, DECAY_LORA, D), 0.1 * DECAY_LORA ** -0.5),
        'rw_a0': normal((N_A, D), 0.1),
        'rw_a1': normal((N_A, D, AAA_LORA), D ** -0.5),
        'rw_a2': normal((N_A, AAA_LORA, D), 0.1 * AAA_LORA ** -0.5),
        'rw_g1': normal((N_A, D, GATE_LORA), D ** -0.5),
        'rw_g2': normal((N_A, GATE_LORA, D), GATE_LORA ** -0.5),
        'rw_k_k': 0.85 + normal((N_A, D), 0.05),
        'rw_k_a': 1.0 + normal((N_A, D), 0.05),
        'rw_r_k': normal((N_A, RW_HEADS, RW_HEAD), 0.1),
        'rw_lnx_g': gain((N_A, D)),
        'rw_lnx_b': normal((N_A, D), 0.02),
        'rw_v0': normal((N_VRES, D), 0.1),
        'rw_v1': normal((N_VRES, D, MV_LORA), D ** -0.5),
        'rw_v2': normal((N_VRES, MV_LORA, D), 0.1 * MV_LORA ** -0.5),
        'mla_w_down': normal((N_B, D, MLA_Q_LORA + MLA_KV_LORA + MLA_ROPE), D ** -0.5),
        'mla_q_norm_g': gain((N_B, MLA_Q_LORA)),
        'mla_kv_norm_g': gain((N_B, MLA_KV_LORA)),
        'mla_w_uq': normal((N_B, MLA_Q_LORA, MLA_HEADS * (MLA_NOPE + MLA_ROPE)), MLA_Q_LORA ** -0.5),
        'mla_w_ukv': normal((N_B, MLA_KV_LORA, MLA_HEADS * (MLA_NOPE + MLA_V)), MLA_KV_LORA ** -0.5),
        'mla_w_o': normal((N_B, MLA_HEADS * MLA_V, D), (MLA_HEADS * MLA_V) ** -0.5),
        'fox_w_in': normal((N_C, D, 4 * D + FOX_HEADS), D ** -0.5),
        'fox_b_f': 3.0 + normal((N_C, FOX_HEADS), 0.5),
        'fox_q_norm_g': gain((N_C, FOX_HEAD)),
        'fox_k_norm_g': gain((N_C, FOX_HEAD)),
        'fox_w_o': normal((N_C, D, D), D ** -0.5),
        'ffn_w_gate_up': normal((N_DENSE, D, 2 * D_FF), D ** -0.5),
        'ffn_w_down': normal((N_DENSE, D_FF, D), D_FF ** -0.5),
        'moe_w_router': normal((N_MOE, D, N_EXPERTS), D ** -0.5),
        'moe_b_router': normal((N_MOE, N_EXPERTS), 0.01),
        'moe_w_gate_up': normal((N_MOE, N_EXPERTS, D, 2 * D_FF_EXPERT), D ** -0.5),
        'moe_w_down': normal((N_MOE, N_EXPERTS, D_FF_EXPERT, D), D_FF_EXPERT ** -0.5),
    }


def reference(x, c, positions, ada_w, ada_b, norm_mix_g, norm_ffn_g, final_norm_g,
              rw_mu, rw_w_rkv, rw_w_o, rw_w0, rw_w1, rw_w2, rw_a0, rw_a1, rw_a2, rw_g1, rw_g2,
              rw_k_k, rw_k_a, rw_r_k, rw_lnx_g, rw_lnx_b, rw_v0, rw_v1, rw_v2,
              mla_w_down, mla_q_norm_g, mla_kv_norm_g, mla_w_uq, mla_w_ukv, mla_w_o,
              fox_w_in, fox_b_f, fox_q_norm_g, fox_k_norm_g, fox_w_o,
              ffn_w_gate_up, ffn_w_down,
              moe_w_router, moe_b_router, moe_w_gate_up, moe_w_down):
    cond = jax.nn.silu(c)
    v_first = None
    for i in range(DEPTH):
        mod = cond @ ada_w[i] + ada_b[i]
        sh_m, sc_m, g_m, sh_f, sc_f, g_f = jnp.split(mod, 6, axis=-1)
        h = ada_modulate(rms_norm(x, norm_mix_g[i]), sh_m, sc_m)
        kind, j = i % N_MIXERS, i // N_MIXERS
        if kind == 0:
            vres = None if j == 0 else (rw_v0[j - 1], rw_v1[j - 1], rw_v2[j - 1])
            y, v_first = rwkv7_time_mix(h, v_first, vres, rw_mu[j], rw_w_rkv[j], rw_w_o[j],
                                        rw_w0[j], rw_w1[j], rw_w2[j], rw_a0[j], rw_a1[j], rw_a2[j],
                                        rw_g1[j], rw_g2[j], rw_k_k[j], rw_k_a[j], rw_r_k[j],
                                        rw_lnx_g[j], rw_lnx_b[j])
        elif kind == 1:
            y = mla_mix(h, positions, mla_w_down[j], mla_q_norm_g[j], mla_kv_norm_g[j],
                        mla_w_uq[j], mla_w_ukv[j], mla_w_o[j])
        else:
            y = fox_mix(h, fox_w_in[j], fox_b_f[j], fox_q_norm_g[j], fox_k_norm_g[j], fox_w_o[j])
        x = x + g_m[:, None, :] * y
        h = ada_modulate(rms_norm(x, norm_ffn_g[i]), sh_f, sc_f)
        if i % 2 == 0:
            y = swiglu(h, ffn_w_gate_up[i // 2], ffn_w_down[i // 2])
        else:
            y = moe_swiglu(h, moe_w_router[i // 2], moe_b_router[i // 2],
                           moe_w_gate_up[i // 2], moe_w_down[i // 2])
        x = x + g_f[:, None, :] * y
    return rms_norm(x, final_norm_g)
```

```python
import functools
import math

import jax
import jax.numpy as jnp
from jax import lax
from jax.experimental import pallas as pl
from jax.experimental.pallas import tpu as pltpu

F32 = jnp.float32
BF16 = jnp.bfloat16
HIGHEST = lax.Precision.HIGHEST

D = 1024
B = 8
S = 2048
T = B * S
DEPTH = 4
NORM_EPS = 1e-6

RW_HEAD = 64
GN_EPS = 64e-5
EXP_NEG_HALF = math.exp(-0.5)

MLA_HEADS = 16
MLA_NOPE = 64
MLA_ROPE = 32
MLA_Q_LORA = 768
MLA_KV_LORA = 256
MLA_SCALE = (MLA_NOPE + MLA_ROPE) ** -0.5
ROPE_BASE = 10000.0

FOX_HEADS = 16
FOX_HEAD = 64
FOX_SCALE = FOX_HEAD ** -0.5

D_FF = 2816
N_EXPERTS = 8

LANES = 128
VMEM_LIMIT = 56 * 1024 * 1024

RW_CHUNK = 64
RW_GROUP = 4
ATT_TQ = 256
ATT_TK = 256
FFN_TM = 512
FFN_TF = 1408
PROJ_TM = 256


def _cparams(*sem):
    return pltpu.CompilerParams(dimension_semantics=sem, vmem_limit_bytes=VMEM_LIMIT)


def _dot(a, b):
    return jnp.dot(a.astype(BF16), b.astype(BF16), preferred_element_type=F32)


def _dot_nt(a, b):
    return lax.dot_general(a.astype(BF16), b.astype(BF16), (((1,), (1,)), ((), ())),
                           preferred_element_type=F32)


def _dot_tn(a, b):
    return lax.dot_general(a.astype(BF16), b.astype(BF16), (((0,), (0,)), ((), ())),
                           preferred_element_type=F32)


def _sigmoid(z):
    return 1.0 / (1.0 + jnp.exp(-z))


def _modnorm(x, g, shift, scale):
    ms = jnp.mean(x * x, axis=-1, keepdims=True)
    y = x * lax.rsqrt(ms + NORM_EPS) * g
    return y * (1.0 + scale) + shift


def _group_sum64(x):
    gi = lax.broadcasted_iota(jnp.int32, (LANES, LANES), 0) // 64
    gj = lax.broadcasted_iota(jnp.int32, (LANES, LANES), 1) // 64
    ones = jnp.where(gi == gj, 1.0, 0.0).astype(BF16)
    hi = x.astype(BF16)
    lo = (x - hi.astype(F32)).astype(BF16)
    outs = []
    for s in range(x.shape[1] // LANES):
        sl = slice(s * LANES, (s + 1) * LANES)
        outs.append(jnp.dot(hi[:, sl], ones, preferred_element_type=F32)
                    + jnp.dot(lo[:, sl], ones, preferred_element_type=F32))
    return outs[0] if len(outs) == 1 else jnp.concatenate(outs, axis=1)


def _ada_kernel(c_ref, w_ref, b_ref, o_ref):
    c = c_ref[...]
    cond = c * _sigmoid(c)
    o_ref[0] = _dot(cond, w_ref[0]) + b_ref[0]


def _ada_mods(c, ada_w, ada_b):
    tn = 1536
    out = pl.pallas_call(
        _ada_kernel,
        grid=(DEPTH, 6 * D // tn),
        in_specs=[pl.BlockSpec((B, D), lambda l, j: (0, 0)),
                  pl.BlockSpec((1, D, tn), lambda l, j: (l, 0, j)),
                  pl.BlockSpec((1, 1, tn), lambda l, j: (l, 0, j))],
        out_specs=pl.BlockSpec((1, B, tn), lambda l, j: (l, 0, j)),
        out_shape=jax.ShapeDtypeStruct((DEPTH, B, 6 * D), F32),
        compiler_params=_cparams("parallel", "parallel"),
        name="ada_mods",
    )(c, ada_w, ada_b.reshape(DEPTH, 1, 6 * D))
    return out.reshape(DEPTH, B, 6, D)


def _rwkv_proj_kernel(has_vres, *refs):
    if has_vres:
        (x_ref, xp_ref, mod_ref, ng_ref, mu_ref, wr_ref, wk_ref, wv_ref, w0_ref, w1_ref, w2_ref,
         a0_ref, a1_ref, a2_ref, g1_ref, g2_ref, kk_ref, ka_ref, v0_ref, v1_ref, v2_ref, vf_ref,
         r_out, lw_out, k_out, v_out, kk_out, a_out, g_out) = refs
    else:
        (x_ref, xp_ref, mod_ref, ng_ref, mu_ref, wr_ref, wk_ref, wv_ref, w0_ref, w1_ref, w2_ref,
         a0_ref, a1_ref, a2_ref, g1_ref, g2_ref, kk_ref, ka_ref,
         r_out, lw_out, k_out, v_out, kk_out, a_out, g_out) = refs
    i = pl.program_id(0)
    mod = mod_ref[0]
    shift, scale = mod[0:1], mod[1:2]
    g = ng_ref[...]
    h = _modnorm(x_ref[...], g, shift, scale)
    hp = _modnorm(xp_ref[...], g, shift, scale)
    seq_start = (i % (S // PROJ_TM)) == 0
    prev_row = jnp.where(seq_start, 0.0, hp[7:8, :])
    row = lax.broadcasted_iota(jnp.int32, h.shape, 0)
    prev = jnp.where(row == 0, prev_row, pltpu.roll(h, 1, axis=0))
    delta = prev - h
    mu = mu_ref[...]
    xr = h + delta * mu[0:1]
    xw = h + delta * mu[1:2]
    xk = h + delta * mu[2:3]
    xv = h + delta * mu[3:4]
    xa = h + delta * mu[4:5]
    xg = h + delta * mu[5:6]
    r = _dot(xr, wr_ref[...])
    k = _dot(xk, wk_ref[...])
    v = _dot(xv, wv_ref[...])
    w_raw = w0_ref[...] + _dot(jnp.tanh(_dot(xw, w1_ref[...])), w2_ref[...])
    lw_out[...] = -_sigmoid(w_raw) * EXP_NEG_HALF
    if has_vres:
        mix = _sigmoid(v0_ref[...] + _dot(_dot(xv, v1_ref[...]), v2_ref[...]))
        v = v + (vf_ref[...] - v) * mix
    a = _sigmoid(a0_ref[...] + _dot(_dot(xa, a1_ref[...]), a2_ref[...]))
    g_out[...] = _dot(_sigmoid(_dot(xg, g1_ref[...])), g2_ref[...])
    kk = k * kk_ref[...]
    norm = jnp.sqrt(_group_sum64(kk * kk))
    kk_out[...] = kk / jnp.maximum(norm, 1e-12)
    k_out[...] = k * (1.0 + (a - 1.0) * ka_ref[...])
    r_out[...] = r
    v_out[...] = v
    a_out[...] = a


def _rwkv_proj(x, mod, norm_g, p, v_first):
    has_vres = v_first is not None
    tm = PROJ_TM
    nb = S // tm
    row = lambda a: a.reshape(1, -1)
    full = lambda a: pl.BlockSpec(a.shape, lambda i: (0,) * a.ndim)
    tile = pl.BlockSpec((tm, D), lambda i: (i, 0))
    ops = [x, x, mod, row(norm_g), p["mu"],
           p["w_rkv"][0].astype(BF16), p["w_rkv"][1].astype(BF16), p["w_rkv"][2].astype(BF16),
           row(p["w0"]), p["w1"].astype(BF16), p["w2"].astype(BF16),
           row(p["a0"]), p["a1"].astype(BF16), p["a2"].astype(BF16),
           p["g1"].astype(BF16), p["g2"].astype(BF16), row(p["k_k"]), row(p["k_a"])]
    specs = [tile,
             pl.BlockSpec((8, D), lambda i: (jnp.maximum(i * (tm // 8) - 1, 0), 0)),
             pl.BlockSpec((1, 6, D), lambda i: (i // nb, 0, 0))]
    specs += [full(a) for a in ops[3:]]
    if has_vres:
        extra = [row(p["v0"]), p["v1"].astype(BF16), p["v2"].astype(BF16)]
        ops += extra + [v_first]
        specs += [full(a) for a in extra] + [tile]
    outs = pl.pallas_call(
        functools.partial(_rwkv_proj_kernel, has_vres),
        grid=(T // tm,),
        in_specs=specs,
        out_specs=[tile] * 7,
        out_shape=[jax.ShapeDtypeStruct((T, D), F32)] * 7,
        compiler_params=_cparams("parallel"),
        name="rwkv_proj",
    )(*ops)
    return outs


def _rwkv_scan_kernel(r_ref, lw_ref, k_ref, v_ref, kk_ref, a_ref, g_ref, rk_ref, lg_ref, lb_ref,
                      o_ref, s_ref):
    c = pl.program_id(2)
    C = RW_CHUNK
    W = RW_GROUP * RW_HEAD

    @pl.when(c == 0)
    def _():
        s_ref[...] = jnp.zeros_like(s_ref)

    r = r_ref[...]
    lw = lw_ref[...]
    k = k_ref[...]
    v = v_ref[...]
    kk = kk_ref[...]
    a = a_ref[...]

    ti = lax.broadcasted_iota(jnp.int32, (C, C), 0)
    si = lax.broadcasted_iota(jnp.int32, (C, C), 1)
    tri = jnp.where(ti >= si, 1.0, 0.0).astype(F32)
    cl = jnp.dot(tri, lw, precision=HIGHEST, preferred_element_type=F32)
    cl_end = cl[C - 1:C, :]
    w_t = jnp.exp(cl)
    w_prev = jnp.exp(cl - lw)
    w_inv = jnp.exp(-cl)
    w_rem = jnp.exp(cl_end - cl)
    bv = kk * a
    r_hat = r * w_t
    a_hat = -kk * w_prev
    b_hat = bv * w_inv
    k_hat = k * w_inv
    b_til = bv * w_rem
    k_til = k * w_rem

    lane_head = lax.broadcasted_iota(jnp.int32, (C, W), 1) // RW_HEAD

    def stack(m):
        mb = m.astype(BF16)
        zero = jnp.zeros_like(mb)
        return jnp.concatenate([jnp.where(lane_head == hd, mb, zero) for hd in range(RW_GROUP)],
                               axis=0)

    def fold(m):
        return m[0:C] + m[C:2 * C] + m[2 * C:3 * C] + m[3 * C:4 * C]

    gram = _dot_nt(jnp.concatenate([stack(a_hat), stack(r_hat)], axis=0),
                   jnp.concatenate([stack(b_hat), stack(k_hat)], axis=0))
    n = RW_GROUP * C
    ri = lax.broadcasted_iota(jnp.int32, (n, n), 0)
    ci = lax.broadcasted_iota(jnp.int32, (n, n), 1)
    strict = (ri % C) > (ci % C)
    incl = (ri % C) >= (ci % C)
    same_head = (ri // RW_HEAD) == (ci // RW_HEAD)
    l_ab = jnp.where(strict, gram[:n, :n], 0.0)
    l_ak = jnp.where(strict, gram[:n, n:], 0.0)
    m_rb = jnp.where(incl, gram[n:, :n], 0.0)
    m_rk = jnp.where(incl, gram[n:, n:], 0.0)

    inv = jnp.where(ri == ci, 1.0, 0.0) + l_ab
    pw = l_ab
    for _ in range(int(math.log2(C)) - 1):
        pw = _dot(pw, pw)
        inv = inv + _dot(inv, pw)

    a_p = fold(_dot(inv, stack(a_hat)))
    v_p = fold(_dot(_dot(inv, l_ak), stack(v)))
    r_p = r_hat + fold(_dot(m_rb, stack(a_p)))
    y0 = fold(_dot(jnp.concatenate([m_rb, m_rk], axis=1),
                   jnp.concatenate([stack(v_p), stack(v)], axis=0)))
    a_til = jnp.where(same_head, _dot_tn(b_til, a_p), 0.0)
    d_new = jnp.where(same_head,
                      _dot_tn(jnp.concatenate([v_p, v], axis=0),
                              jnp.concatenate([b_til, k_til], axis=0)), 0.0)
    s_old = s_ref[...]
    y = _dot_nt(r_p, s_old) + y0
    s_ref[...] = s_old * jnp.exp(cl_end) + _dot_nt(s_old, a_til) + d_new

    inv_n = 1.0 / RW_HEAD
    mean = _group_sum64(y) * inv_n
    yc = y - mean
    var = _group_sum64(yc * yc) * inv_n
    yn = yc * lax.rsqrt(var + GN_EPS) * lg_ref[...] + lb_ref[...]
    bonus = _group_sum64(r * k * rk_ref[...]) * v
    o_ref[...] = ((yn + bonus) * g_ref[...]).astype(BF16)


def _rwkv_scan(r, lw, k, v, kk, a, g, r_k, lnx_g, lnx_b):
    C = RW_CHUNK
    W = RW_GROUP * RW_HEAD
    nc = S // C
    slab = pl.BlockSpec((C, W), lambda b, gi, c: (b * nc + c, gi))
    prow = pl.BlockSpec((1, W), lambda b, gi, c: (0, gi))
    return pl.pallas_call(
        _rwkv_scan_kernel,
        grid=(B, D // W, nc),
        in_specs=[slab] * 7 + [prow] * 3,
        out_specs=slab,
        out_shape=jax.ShapeDtypeStruct((T, D), BF16),
        scratch_shapes=[pltpu.VMEM((W, W), F32)],
        compiler_params=_cparams("parallel", "parallel", "arbitrary"),
        name="rwkv_scan",
    )(r, lw, k, v, kk, a, g, r_k.reshape(1, D), lnx_g.reshape(1, D), lnx_b.reshape(1, D))


def _proj_res_kernel(gate_row, a_ref, w_ref, x_ref, mod_ref, o_ref):
    y = jnp.dot(a_ref[...], w_ref[...], preferred_element_type=F32)
    o_ref[...] = x_ref[...] + mod_ref[0][gate_row:gate_row + 1] * y


def _proj_res(a, w, x, mod, gate_row):
    tm = 512
    nb = S // tm
    kdim = a.shape[1]
    return pl.pallas_call(
        functools.partial(_proj_res_kernel, gate_row),
        grid=(T // tm,),
        in_specs=[pl.BlockSpec((tm, kdim), lambda i: (i, 0)),
                  pl.BlockSpec((kdim, D), lambda i: (0, 0)),
                  pl.BlockSpec((tm, D), lambda i: (i, 0)),
                  pl.BlockSpec((1, 6, D), lambda i: (i // nb, 0, 0))],
        out_specs=pl.BlockSpec((tm, D), lambda i: (i, 0)),
        out_shape=jax.ShapeDtypeStruct((T, D), F32),
        compiler_params=_cparams("parallel"),
        name="proj_res",
    )(a, w.astype(BF16), x, mod)


def _ffn_kernel(nf, x_ref, mod_ref, ng_ref, wg_ref, wu_ref, wd_ref, o_ref, h_scr, acc_scr):
    f = pl.program_id(1)

    @pl.when(f == 0)
    def _():
        mod = mod_ref[0]
        h_scr[...] = _modnorm(x_ref[...], ng_ref[...], mod[3:4], mod[4:5]).astype(BF16)
        acc_scr[...] = jnp.zeros_like(acc_scr)

    h = h_scr[...]
    gt = jnp.dot(h, wg_ref[...], preferred_element_type=F32)
    up = jnp.dot(h, wu_ref[...], preferred_element_type=F32)
    act = (gt * _sigmoid(gt) * up).astype(BF16)
    acc_scr[...] += jnp.dot(act, wd_ref[...], preferred_element_type=F32)

    @pl.when(f == nf - 1)
    def _():
        o_ref[...] = x_ref[...] + mod_ref[0][5:6] * acc_scr[...]


def _ffn_dense(x, mod, norm_g, w_gate_up, w_down):
    tm, tf = FFN_TM, FFN_TF
    nf = D_FF // tf
    nb = S // tm
    wgu = w_gate_up.astype(BF16)
    return pl.pallas_call(
        functools.partial(_ffn_kernel, nf),
        grid=(T // tm, nf),
        in_specs=[pl.BlockSpec((tm, D), lambda i, f: (i, 0)),
                  pl.BlockSpec((1, 6, D), lambda i, f: (i // nb, 0, 0)),
                  pl.BlockSpec((1, D), lambda i, f: (0, 0)),
                  pl.BlockSpec((D, tf), lambda i, f: (0, f)),
                  pl.BlockSpec((D, tf), lambda i, f: (0, nf + f)),
                  pl.BlockSpec((tf, D), lambda i, f: (f, 0))],
        out_specs=pl.BlockSpec((tm, D), lambda i, f: (i, 0)),
        out_shape=jax.ShapeDtypeStruct((T, D), F32),
        scratch_shapes=[pltpu.VMEM((tm, D), BF16), pltpu.VMEM((tm, D), F32)],
        compiler_params=_cparams("parallel", "arbitrary"),
        name="ffn_dense",
    )(x, mod, norm_g.reshape(1, D), wgu, wgu, w_down.astype(BF16))


def _moe_kernel(nf, x_ref, mod_ref, ng_ref, wr_ref, br_ref, wg_ref, wu_ref, wd_ref, o_ref,
                h_scr, cw_scr, acc_scr):
    e = pl.program_id(1)
    f = pl.program_id(2)
    lane = lax.broadcasted_iota(jnp.int32, cw_scr.shape, 1)

    @pl.when((e == 0) & (f == 0))
    def _():
        mod = mod_ref[0]
        h = _modnorm(x_ref[...], ng_ref[...], mod[3:4], mod[4:5])
        h_scr[...] = h.astype(BF16)
        acc_scr[...] = jnp.zeros_like(acc_scr)
        logits = jnp.dot(h, wr_ref[...], precision=HIGHEST, preferred_element_type=F32) + br_ref[...]
        neg = -jnp.inf
        logits = jnp.where(lane < N_EXPERTS, logits, neg)
        m1 = jnp.max(logits, axis=-1, keepdims=True)
        i1 = jnp.min(jnp.where(logits == m1, lane, LANES), axis=-1, keepdims=True)
        rest = jnp.where(lane == i1, neg, logits)
        m2 = jnp.max(rest, axis=-1, keepdims=True)
        i2 = jnp.min(jnp.where(rest == m2, lane, LANES), axis=-1, keepdims=True)
        e2 = jnp.exp(m2 - m1)
        w1 = 1.0 / (1.0 + e2)
        w2 = e2 / (1.0 + e2)
        cw_scr[...] = jnp.where(lane == i1, w1, 0.0) + jnp.where(lane == i2, w2, 0.0)

    h = h_scr[...]
    gt = jnp.dot(h, wg_ref[0], preferred_element_type=F32)
    up = jnp.dot(h, wu_ref[0], preferred_element_type=F32)
    act = (gt * _sigmoid(gt) * up).astype(BF16)
    cw_e = jnp.sum(jnp.where(lane == e, cw_scr[...], 0.0), axis=-1, keepdims=True)
    acc_scr[...] += cw_e * jnp.dot(act, wd_ref[0], preferred_element_type=F32)

    @pl.when((e == N_EXPERTS - 1) & (f == nf - 1))
    def _():
        o_ref[...] = x_ref[...] + mod_ref[0][5:6] * acc_scr[...]


def _moe(x, mod, norm_g, w_router, b_router, w_gate_up, w_down):
    tm, tf = FFN_TM, FFN_TF
    nf = D_FF // tf
    nb = S // tm
    wgu = w_gate_up.astype(BF16)
    wr = jnp.zeros((D, LANES), F32).at[:, :N_EXPERTS].set(w_router)
    br = jnp.zeros((1, LANES), F32).at[0, :N_EXPERTS].set(b_router)
    return pl.pallas_call(
        functools.partial(_moe_kernel, nf),
        grid=(T // tm, N_EXPERTS, nf),
        in_specs=[pl.BlockSpec((tm, D), lambda i, e, f: (i, 0)),
                  pl.BlockSpec((1, 6, D), lambda i, e, f: (i // nb, 0, 0)),
                  pl.BlockSpec((1, D), lambda i, e, f: (0, 0)),
                  pl.BlockSpec((D, LANES), lambda i, e, f: (0, 0)),
                  pl.BlockSpec((1, LANES), lambda i, e, f: (0, 0)),
                  pl.BlockSpec((1, D, tf), lambda i, e, f: (e, 0, f)),
                  pl.BlockSpec((1, D, tf), lambda i, e, f: (e, 0, nf + f)),
                  pl.BlockSpec((1, tf, D), lambda i, e, f: (e, f, 0))],
        out_specs=pl.BlockSpec((tm, D), lambda i, e, f: (i, 0)),
        out_shape=jax.ShapeDtypeStruct((T, D), F32),
        scratch_shapes=[pltpu.VMEM((tm, D), BF16), pltpu.VMEM((tm, LANES), F32),
                        pltpu.VMEM((tm, D), F32)],
        compiler_params=_cparams("parallel", "arbitrary", "arbitrary"),
        name="moe",
    )(x, mod, norm_g.reshape(1, D), wr, br, wgu, wgu, w_down.astype(BF16))


def _mla_proj_kernel(x_ref, mod_ref, ng_ref, pos_ref, invf_ref, wd_ref, qg_ref, kvg_ref,
                     wuq_ref, wukv_ref, q_out, k_out, v_out):
    mod = mod_ref[0]
    h = _modnorm(x_ref[...], ng_ref[...], mod[0:1], mod[1:2])
    down = _dot(h, wd_ref[...])
    cq = down[:, :MLA_Q_LORA]
    ckv = down[:, MLA_Q_LORA:MLA_Q_LORA + MLA_KV_LORA]
    kr = down[:, MLA_Q_LORA + MLA_KV_LORA:]
    cq = cq * lax.rsqrt(jnp.mean(cq * cq, axis=-1, keepdims=True) + NORM_EPS) * qg_ref[...]
    ckv = ckv * lax.rsqrt(jnp.mean(ckv * ckv, axis=-1, keepdims=True) + NORM_EPS) * kvg_ref[...]

    ang = pos_ref[...] * invf_ref[...]
    lane = lax.broadcasted_iota(jnp.int32, ang.shape, 1)
    is_rope = (lane >= MLA_NOPE) & (lane < MLA_NOPE + MLA_ROPE)
    cos_r = jnp.where(is_rope, jnp.cos(ang), 0.0)
    sin_r = jnp.where(is_rope, jnp.sin(ang), 0.0)
    cos_q = jnp.where(lane < MLA_NOPE, 1.0, cos_r)
    shift = LANES - MLA_ROPE

    k_rope = kr * cos_r + pltpu.roll(kr, shift, axis=1) * sin_r

    qf = _dot(cq, wuq_ref[...])
    kv = _dot(ckv, wukv_ref[...])
    for hd in range(MLA_HEADS):
        sl = slice(hd * LANES, (hd + 1) * LANES)
        qs = qf[:, sl]
        q_out[:, sl] = ((qs * cos_q + pltpu.roll(qs, shift, axis=1) * sin_r) * MLA_SCALE).astype(BF16)
        k_out[:, sl] = (kv[:, sl] + k_rope).astype(BF16)
    v_out[...] = kv[:, MLA_HEADS * LANES:].astype(BF16)


def _rot_half_cols(w):
    half = MLA_ROPE // 2
    return jnp.concatenate([-w[..., half:], w[..., :half]], axis=-1)


def _mla_proj(x, mod, norm_g, positions, w_down, q_norm_g, kv_norm_g, w_uq, w_ukv):
    tm = PROJ_TM
    nb = S // tm
    half = MLA_ROPE // 2
    wr = w_down[:, MLA_Q_LORA + MLA_KV_LORA:]
    wd = jnp.concatenate([w_down[:, :MLA_Q_LORA + MLA_KV_LORA],
                          jnp.zeros((D, MLA_NOPE), F32), wr, _rot_half_cols(wr)], axis=1).astype(BF16)
    wq = w_uq.reshape(MLA_Q_LORA, MLA_HEADS, MLA_NOPE + MLA_ROPE)
    wq_r = wq[..., MLA_NOPE:]
    wuq = jnp.concatenate([wq[..., :MLA_NOPE], wq_r, _rot_half_cols(wq_r)], axis=-1)
    wuq = wuq.reshape(MLA_Q_LORA, MLA_HEADS * LANES).astype(BF16)
    wkv = w_ukv.reshape(MLA_KV_LORA, MLA_HEADS, 2 * MLA_NOPE)
    wk = jnp.concatenate([wkv[..., :MLA_NOPE], jnp.zeros_like(wkv[..., :MLA_NOPE])], axis=-1)
    wukv = jnp.concatenate([wk.reshape(MLA_KV_LORA, MLA_HEADS * LANES),
                            wkv[..., MLA_NOPE:].reshape(MLA_KV_LORA, D)], axis=1).astype(BF16)
    inv_freq = ROPE_BASE ** (-jnp.arange(half, dtype=F32) / half)
    lane = jnp.arange(LANES)
    invf = jnp.where((lane >= MLA_NOPE) & (lane < MLA_NOPE + MLA_ROPE),
                     inv_freq[(lane - MLA_NOPE) % half], 0.0).reshape(1, LANES).astype(F32)
    pos = jnp.broadcast_to(positions.reshape(T, 1).astype(F32), (T, LANES))
    full = lambda a: pl.BlockSpec(a.shape, lambda i: (0,) * a.ndim)
    ops = [x, mod, norm_g.reshape(1, D), pos, invf, wd, q_norm_g.reshape(1, -1),
           kv_norm_g.reshape(1, -1), wuq, wukv]
    specs = [pl.BlockSpec((tm, D), lambda i: (i, 0)),
             pl.BlockSpec((1, 6, D), lambda i: (i // nb, 0, 0)),
             full(ops[2]),
             pl.BlockSpec((tm, LANES), lambda i: (i, 0))] + [full(a) for a in ops[4:]]
    qw = MLA_HEADS * LANES
    return pl.pallas_call(
        _mla_proj_kernel,
        grid=(T // tm,),
        in_specs=specs,
        out_specs=[pl.BlockSpec((tm, qw), lambda i: (i, 0)),
                   pl.BlockSpec((tm, qw), lambda i: (i, 0)),
                   pl.BlockSpec((tm, D), lambda i: (i, 0))],
        out_shape=[jax.ShapeDtypeStruct((T, qw), BF16), jax.ShapeDtypeStruct((T, qw), BF16),
                   jax.ShapeDtypeStruct((T, D), BF16)],
        compiler_params=_cparams("parallel"),
        name="mla_proj",
    )(*ops)


def _fox_proj_kernel(x_ref, mod_ref, ng_ref, w_ref, wf_ref, bf_ref, qg_ref, kg_ref,
                     q_out, k_out, v_out, og_out, fc_out, fr_out, carry_scr):
    i = pl.program_id(0)
    tm = x_ref.shape[0]
    mod = mod_ref[0]
    h = _modnorm(x_ref[...], ng_ref[...], mod[0:1], mod[1:2]).astype(BF16)
    q = jnp.dot(h, w_ref[:, 0:D], preferred_element_type=F32)
    k = jnp.dot(h, w_ref[:, D:2 * D], preferred_element_type=F32)
    v = jnp.dot(h, w_ref[:, 2 * D:3 * D], preferred_element_type=F32)
    og = jnp.dot(h, w_ref[:, 3 * D:4 * D], preferred_element_type=F32)
    inv_n = 1.0 / FOX_HEAD
    q = q * lax.rsqrt(_group_sum64(q * q) * inv_n + NORM_EPS) * qg_ref[...] * FOX_SCALE
    k = k * lax.rsqrt(_group_sum64(k * k) * inv_n + NORM_EPS) * kg_ref[...]
    q_out[...] = q.astype(BF16)
    k_out[...] = k.astype(BF16)
    v_out[...] = v.astype(BF16)
    og_out[...] = _sigmoid(og).astype(BF16)

    z = jnp.dot(h, wf_ref[...], preferred_element_type=F32) + bf_ref[...]
    lane = lax.broadcasted_iota(jnp.int32, z.shape, 1)
    log_f = jnp.minimum(z, 0.0) - jnp.log(1.0 + jnp.exp(-jnp.abs(z)))
    log_f = jnp.where(lane < FOX_HEADS, log_f, 0.0)
    ti = lax.broadcasted_iota(jnp.int32, (tm, tm), 0)
    si = lax.broadcasted_iota(jnp.int32, (tm, tm), 1)
    tri = jnp.where(ti >= si, 1.0, 0.0).astype(F32)
    seq_start = (i % (S // tm)) == 0
    carry = jnp.where(seq_start, 0.0, carry_scr[...])
    cum = jnp.dot(tri, log_f, precision=HIGHEST, preferred_element_type=F32) + carry
    carry_scr[...] = cum[tm - 1:tm, :]
    fc_out[...] = cum
    fr_out[0] = cum.T


def _fox_proj(x, mod, norm_g, w_in, b_f, q_norm_g, k_norm_g):
    tm = PROJ_TM
    nb = S // tm
    w_main = jnp.concatenate([w_in[:, :3 * D], w_in[:, 3 * D + FOX_HEADS:]], axis=1).astype(BF16)
    w_f = jnp.zeros((D, LANES), F32).at[:, :FOX_HEADS].set(w_in[:, 3 * D:3 * D + FOX_HEADS]).astype(BF16)
    bf = jnp.zeros((1, LANES), F32).at[0, :FOX_HEADS].set(b_f)
    qg = jnp.tile(q_norm_g, FOX_HEADS).reshape(1, D)
    kg = jnp.tile(k_norm_g, FOX_HEADS).reshape(1, D)
    full = lambda a: pl.BlockSpec(a.shape, lambda i: (0,) * a.ndim)
    tile = pl.BlockSpec((tm, D), lambda i: (i, 0))
    ops = [x, mod, norm_g.reshape(1, D), w_main, w_f, bf, qg, kg]
    specs = [tile, pl.BlockSpec((1, 6, D), lambda i: (i // nb, 0, 0))] + [full(a) for a in ops[2:]]
    return pl.pallas_call(
        _fox_proj_kernel,
        grid=(T // tm,),
        in_specs=specs,
        out_specs=[tile, tile, tile, tile,
                   pl.BlockSpec((tm, LANES), lambda i: (i, 0)),
                   pl.BlockSpec((1, LANES, tm), lambda i: (i // nb, 0, i % nb))],
        out_shape=[jax.ShapeDtypeStruct((T, D), BF16)] * 4
        + [jax.ShapeDtypeStruct((T, LANES), F32), jax.ShapeDtypeStruct((B, LANES, S), F32)],
        scratch_shapes=[pltpu.VMEM((1, LANES), F32)],
        compiler_params=_cparams("arbitrary"),
        name="fox_proj",
    )(*ops)


def _attn_kernel(fox, *refs):
    if fox:
        q_ref, k_ref, v_ref, og_ref, fc_ref, fr_ref, o_ref = refs
    else:
        q_ref, k_ref, v_ref, o_ref = refs
    tq, tk = ATT_TQ, ATT_TK
    p = pl.program_id(1)
    qi = pl.program_id(2)
    lane = lax.broadcasted_iota(jnp.int32, (tq, LANES), 1)
    row = lax.broadcasted_iota(jnp.int32, (tq, tk), 0)
    col = lax.broadcasted_iota(jnp.int32, (tq, tk), 1)
    allowed = (col <= row) if fox else ((col // 64) <= (row // 64))

    outs = []
    for j in range(2):
        hd = 2 * p + j
        if fox:
            qb = q_ref[...]
            qj = jnp.where(lane // FOX_HEAD == j, qb, jnp.zeros_like(qb))
            f_t = jnp.sum(jnp.where(lane == hd, fc_ref[...], 0.0), axis=-1, keepdims=True)
        else:
            qj = q_ref[:, j * LANES:(j + 1) * LANES]

        def kv_block(kb, carry, masked, qj=qj, hd=hd, j=j):
            m, l, acc = carry
            start = pl.multiple_of(kb * tk, tk)
            if fox:
                ks = k_ref[pl.ds(start, tk), :]
            else:
                ks = k_ref[pl.ds(start, tk), j * LANES:(j + 1) * LANES]
            s = lax.dot_general(qj, ks, (((1,), (1,)), ((), ())), preferred_element_type=F32)
            if fox:
                s = s + (f_t - fr_ref[0, hd, pl.ds(kb, 1), :])
            if masked:
                s = jnp.where(allowed, s, -jnp.inf)
            m_new = jnp.maximum(m, jnp.max(s, axis=-1, keepdims=True))
            alpha = jnp.exp(m - m_new)
            pm = jnp.exp(s - m_new)
            l = alpha * l + jnp.sum(pm, axis=-1, keepdims=True)
            acc = alpha * acc + jnp.dot(pm.astype(BF16), v_ref[pl.ds(start, tk), :],
                                        preferred_element_type=F32)
            return m_new, l, acc

        carry = (jnp.full((tq, 1), -jnp.inf, F32), jnp.zeros((tq, 1), F32),
                 jnp.zeros((tq, LANES), F32))
        carry = lax.fori_loop(0, qi, lambda kb, cr: kv_block(kb, cr, False), carry)
        _, l, acc = kv_block(qi, carry, True)
        outs.append(acc / l)

    o = jnp.where(lane // 64 == 0, outs[0], outs[1])
    if fox:
        o = o * og_ref[...].astype(F32)
    o_ref[...] = o.astype(BF16)


def _attention(q, k, v, fox_extras=None):
    fox = fox_extras is not None
    tq, tk = ATT_TQ, ATT_TK
    nq = S // tq
    qw = LANES if fox else 2 * LANES
    ops = [q, k, v]
    specs = [pl.BlockSpec((tq, qw), lambda b, p, i: (b * nq + i, p)),
             pl.BlockSpec((S, qw), lambda b, p, i: (b, p)),
             pl.BlockSpec((S, LANES), lambda b, p, i: (b, p))]
    if fox:
        og, fc, fr = fox_extras
        ops += [og, fc, fr]
        specs += [pl.BlockSpec((tq, LANES), lambda b, p, i: (b * nq + i, p)),
                  pl.BlockSpec((tq, LANES), lambda b, p, i: (b * nq + i, 0)),
                  pl.BlockSpec((1, FOX_HEADS, S // tk, tk), lambda b, p, i: (b, 0, 0, 0))]
    return pl.pallas_call(
        functools.partial(_attn_kernel, fox),
        grid=(B, 8, nq),
        in_specs=specs,
        out_specs=pl.BlockSpec((tq, LANES), lambda b, p, i: (b * nq + i, p)),
        out_shape=jax.ShapeDtypeStruct((T, D), BF16),
        compiler_params=_cparams("parallel", "parallel", "arbitrary"),
        name="fox_attn" if fox else "mla_attn",
    )(*ops)


def _final_norm_kernel(x_ref, g_ref, o_ref):
    x = x_ref[...]
    o_ref[...] = x * lax.rsqrt(jnp.mean(x * x, axis=-1, keepdims=True) + NORM_EPS) * g_ref[...]


def _final_norm(x, g):
    tm = 512
    return pl.pallas_call(
        _final_norm_kernel,
        grid=(T // tm,),
        in_specs=[pl.BlockSpec((tm, D), lambda i: (i, 0)), pl.BlockSpec((1, D), lambda i: (0, 0))],
        out_specs=pl.BlockSpec((tm, D), lambda i: (i, 0)),
        out_shape=jax.ShapeDtypeStruct((T, D), F32),
        compiler_params=_cparams("parallel"),
        name="final_norm",
    )(x, g.reshape(1, D))


def kernel(x, c, positions, ada_w, ada_b, norm_mix_g, norm_ffn_g, final_norm_g, rw_mu, rw_w_rkv, rw_w_o, rw_w0, rw_w1, rw_w2, rw_a0, rw_a1, rw_a2, rw_g1, rw_g2, rw_k_k, rw_k_a, rw_r_k, rw_lnx_g, rw_lnx_b, rw_v0, rw_v1, rw_v2, mla_w_down, mla_q_norm_g, mla_kv_norm_g, mla_w_uq, mla_w_ukv, mla_w_o, fox_w_in, fox_b_f, fox_q_norm_g, fox_k_norm_g, fox_w_o, ffn_w_gate_up, ffn_w_down, moe_w_router, moe_b_router, moe_w_gate_up, moe_w_down):
    xf = x.reshape(T, D)
    mods = _ada_mods(c, ada_w, ada_b)
    v_first = None
    for i in range(DEPTH):
        mod = mods[i]
        kind, j = i % 3, i // 3
        if kind == 0:
            p = dict(mu=rw_mu[j], w_rkv=rw_w_rkv[j], w0=rw_w0[j], w1=rw_w1[j], w2=rw_w2[j],
                     a0=rw_a0[j], a1=rw_a1[j], a2=rw_a2[j], g1=rw_g1[j], g2=rw_g2[j],
                     k_k=rw_k_k[j], k_a=rw_k_a[j])
            if j > 0:
                p.update(v0=rw_v0[j - 1], v1=rw_v1[j - 1], v2=rw_v2[j - 1])
            r, lw, k, v, kk, a, g = _rwkv_proj(xf, mod, norm_mix_g[i], p, v_first if j > 0 else None)
            if j == 0:
                v_first = v
            y = _rwkv_scan(r, lw, k, v, kk, a, g, rw_r_k[j], rw_lnx_g[j], rw_lnx_b[j])
            xf = _proj_res(y, rw_w_o[j], xf, mod, 2)
        elif kind == 1:
            q, k, v = _mla_proj(xf, mod, norm_mix_g[i], positions, mla_w_down[j], mla_q_norm_g[j],
                                mla_kv_norm_g[j], mla_w_uq[j], mla_w_ukv[j])
            o = _attention(q, k, v)
            xf = _proj_res(o, mla_w_o[j], xf, mod, 2)
        else:
            q, k, v, og, fc, fr = _fox_proj(xf, mod, norm_mix_g[i], fox_w_in[j], fox_b_f[j],
                                            fox_q_norm_g[j], fox_k_norm_g[j])
            fr = fr[:, :FOX_HEADS, :].reshape(B, FOX_HEADS, S // ATT_TK, ATT_TK)
            o = _attention(q, k, v, (og, fc, fr))
            xf = _proj_res(o, fox_w_o[j], xf, mod, 2)
        if i % 2 == 0:
            xf = _ffn_dense(xf, mod, norm_ffn_g[i], ffn_w_gate_up[i // 2], ffn_w_down[i // 2])
        else:
            xf = _moe(xf, mod, norm_ffn_g[i], moe_w_router[i // 2], moe_b_router[i // 2],
                      moe_w_gate_up[i // 2], moe_w_down[i // 2])
    return _final_norm(xf, final_norm_g).reshape(B, S, D)
```

```python
import functools
import math

import jax
import jax.numpy as jnp
from jax import lax
from jax.experimental import pallas as pl
from jax.experimental.pallas import tpu as pltpu

F32 = jnp.float32
BF16 = jnp.bfloat16
HIGHEST = lax.Precision.HIGHEST

D = 1024
B = 8
S = 2048
T = B * S
DEPTH = 4
NORM_EPS = 1e-6

RW_HEAD = 64
GN_EPS = 64e-5
EXP_NEG_HALF = math.exp(-0.5)

MLA_HEADS = 16
MLA_NOPE = 64
MLA_ROPE = 32
MLA_Q_LORA = 768
MLA_KV_LORA = 256
MLA_SCALE = (MLA_NOPE + MLA_ROPE) ** -0.5
ROPE_BASE = 10000.0

FOX_HEADS = 16
FOX_HEAD = 64
FOX_SCALE = FOX_HEAD ** -0.5

D_FF = 2816
N_EXPERTS = 8

LANES = 128
VMEM_LIMIT = 56 * 1024 * 1024

RW_CHUNK = 64
RW_GROUP = 4
ATT_TQ = 256
ATT_TK = 256
FFN_TM = 512
FFN_TF = 1408
PROJ_TM = 256


def _cparams(*sem):
    return pltpu.CompilerParams(dimension_semantics=sem, vmem_limit_bytes=VMEM_LIMIT)


def _dot(a, b):
    return jnp.dot(a.astype(BF16), b.astype(BF16), preferred_element_type=F32)


def _dot_nt(a, b):
    return lax.dot_general(a.astype(BF16), b.astype(BF16), (((1,), (1,)), ((), ())),
                           preferred_element_type=F32)


def _dot_tn(a, b):
    return lax.dot_general(a.astype(BF16), b.astype(BF16), (((0,), (0,)), ((), ())),
                           preferred_element_type=F32)


def _sigmoid(z):
    return 1.0 / (1.0 + jnp.exp(-z))


def _modnorm(x, g, shift, scale):
    ms = jnp.mean(x * x, axis=-1, keepdims=True)
    y = x * lax.rsqrt(ms + NORM_EPS) * g
    return y * (1.0 + scale) + shift


def _group_sum64(x):
    gi = lax.broadcasted_iota(jnp.int32, (LANES, LANES), 0) // 64
    gj = lax.broadcasted_iota(jnp.int32, (LANES, LANES), 1) // 64
    ones = jnp.where(gi == gj, 1.0, 0.0).astype(BF16)
    hi = x.astype(BF16)
    lo = (x - hi.astype(F32)).astype(BF16)
    outs = []
    for s in range(x.shape[1] // LANES):
        sl = slice(s * LANES, (s + 1) * LANES)
        outs.append(jnp.dot(hi[:, sl], ones, preferred_element_type=F32)
                    + jnp.dot(lo[:, sl], ones, preferred_element_type=F32))
    return outs[0] if len(outs) == 1 else jnp.concatenate(outs, axis=1)


def _store_transposed(vt_ref, v):
    vt = v.T
    for s in range(v.shape[0] // ATT_TK):
        vt_ref[0, s] = vt[:, s * ATT_TK:(s + 1) * ATT_TK].astype(BF16)


def _vt_spec(tm):
    nb = S // tm
    return pl.BlockSpec((1, tm // ATT_TK, D, ATT_TK), lambda i: (i // nb, i % nb, 0, 0))


def _vt_shape():
    return jax.ShapeDtypeStruct((B, S // ATT_TK, D, ATT_TK), BF16)


def _ada_kernel(c_ref, w_ref, b_ref, o_ref):
    c = c_ref[...]
    cond = c * _sigmoid(c)
    o_ref[0] = _dot(cond, w_ref[0]) + b_ref[0]


def _ada_mods(c, ada_w, ada_b):
    tn = 1536
    out = pl.pallas_call(
        _ada_kernel,
        grid=(DEPTH, 6 * D // tn),
        in_specs=[pl.BlockSpec((B, D), lambda l, j: (0, 0)),
                  pl.BlockSpec((1, D, tn), lambda l, j: (l, 0, j)),
                  pl.BlockSpec((1, 1, tn), lambda l, j: (l, 0, j))],
        out_specs=pl.BlockSpec((1, B, tn), lambda l, j: (l, 0, j)),
        out_shape=jax.ShapeDtypeStruct((DEPTH, B, 6 * D), F32),
        compiler_params=_cparams("parallel", "parallel"),
        name="ada_mods",
    )(c, ada_w, ada_b.reshape(DEPTH, 1, 6 * D))
    return out.reshape(DEPTH, B, 6, D)


def _rwkv_proj_kernel(has_vres, *refs):
    if has_vres:
        (x_ref, xp_ref, mod_ref, ng_ref, mu_ref, wr_ref, wk_ref, wv_ref, w0_ref, w1_ref, w2_ref,
         a0_ref, a1_ref, a2_ref, g1_ref, g2_ref, kk_ref, ka_ref, v0_ref, v1_ref, v2_ref, vf_ref,
         r_out, lw_out, k_out, v_out, kk_out, a_out, g_out) = refs
    else:
        (x_ref, xp_ref, mod_ref, ng_ref, mu_ref, wr_ref, wk_ref, wv_ref, w0_ref, w1_ref, w2_ref,
         a0_ref, a1_ref, a2_ref, g1_ref, g2_ref, kk_ref, ka_ref,
         r_out, lw_out, k_out, v_out, kk_out, a_out, g_out) = refs
    i = pl.program_id(0)
    mod = mod_ref[0]
    shift, scale = mod[0:1], mod[1:2]
    g = ng_ref[...]
    h = _modnorm(x_ref[...], g, shift, scale)
    hp = _modnorm(xp_ref[...], g, shift, scale)
    seq_start = (i % (S // PROJ_TM)) == 0
    prev_row = jnp.where(seq_start, 0.0, hp[7:8, :])
    row = lax.broadcasted_iota(jnp.int32, h.shape, 0)
    prev = jnp.where(row == 0, prev_row, pltpu.roll(h, 1, axis=0))
    delta = prev - h
    mu = mu_ref[...]
    xr = h + delta * mu[0:1]
    xw = h + delta * mu[1:2]
    xk = h + delta * mu[2:3]
    xv = h + delta * mu[3:4]
    xa = h + delta * mu[4:5]
    xg = h + delta * mu[5:6]
    r = _dot(xr, wr_ref[...])
    k = _dot(xk, wk_ref[...])
    v = _dot(xv, wv_ref[...])
    w_raw = w0_ref[...] + _dot(jnp.tanh(_dot(xw, w1_ref[...])), w2_ref[...])
    lw_out[...] = -_sigmoid(w_raw) * EXP_NEG_HALF
    if has_vres:
        mix = _sigmoid(v0_ref[...] + _dot(_dot(xv, v1_ref[...]), v2_ref[...]))
        v = v + (vf_ref[...] - v) * mix
    a = _sigmoid(a0_ref[...] + _dot(_dot(xa, a1_ref[...]), a2_ref[...]))
    g_out[...] = _dot(_sigmoid(_dot(xg, g1_ref[...])), g2_ref[...])
    kk = k * kk_ref[...]
    norm = jnp.sqrt(_group_sum64(kk * kk))
    kk_out[...] = kk / jnp.maximum(norm, 1e-12)
    k_out[...] = k * (1.0 + (a - 1.0) * ka_ref[...])
    r_out[...] = r
    v_out[...] = v
    a_out[...] = a


def _rwkv_proj(x, mod, norm_g, p, v_first):
    has_vres = v_first is not None
    tm = PROJ_TM
    nb = S // tm
    row = lambda a: a.reshape(1, -1)
    full = lambda a: pl.BlockSpec(a.shape, lambda i: (0,) * a.ndim)
    tile = pl.BlockSpec((tm, D), lambda i: (i, 0))
    ops = [x, x, mod, row(norm_g), p["mu"],
           p["w_rkv"][0].astype(BF16), p["w_rkv"][1].astype(BF16), p["w_rkv"][2].astype(BF16),
           row(p["w0"]), p["w1"].astype(BF16), p["w2"].astype(BF16),
           row(p["a0"]), p["a1"].astype(BF16), p["a2"].astype(BF16),
           p["g1"].astype(BF16), p["g2"].astype(BF16), row(p["k_k"]), row(p["k_a"])]
    specs = [tile,
             pl.BlockSpec((8, D), lambda i: (jnp.maximum(i * (tm // 8) - 1, 0), 0)),
             pl.BlockSpec((1, 6, D), lambda i: (i // nb, 0, 0))]
    specs += [full(a) for a in ops[3:]]
    if has_vres:
        extra = [row(p["v0"]), p["v1"].astype(BF16), p["v2"].astype(BF16)]
        ops += extra + [v_first]
        specs += [full(a) for a in extra] + [tile]
    outs = pl.pallas_call(
        functools.partial(_rwkv_proj_kernel, has_vres),
        grid=(T // tm,),
        in_specs=specs,
        out_specs=[tile] * 7,
        out_shape=[jax.ShapeDtypeStruct((T, D), F32)] * 7,
        compiler_params=_cparams("parallel"),
        name="rwkv_proj",
    )(*ops)
    return outs


def _rwkv_scan_kernel(r_ref, lw_ref, k_ref, v_ref, kk_ref, a_ref, g_ref, rk_ref, lg_ref, lb_ref,
                      o_ref, s_ref):
    c = pl.program_id(2)
    C = RW_CHUNK
    W = RW_GROUP * RW_HEAD

    @pl.when(c == 0)
    def _():
        s_ref[...] = jnp.zeros_like(s_ref)

    r = r_ref[...]
    lw = lw_ref[...]
    k = k_ref[...]
    v = v_ref[...]
    kk = kk_ref[...]
    a = a_ref[...]

    ti = lax.broadcasted_iota(jnp.int32, (C, C), 0)
    si = lax.broadcasted_iota(jnp.int32, (C, C), 1)
    tri = jnp.where(ti >= si, 1.0, 0.0).astype(F32)
    cl = jnp.dot(tri, lw, precision=HIGHEST, preferred_element_type=F32)
    cl_end = cl[C - 1:C, :]
    w_t = jnp.exp(cl)
    w_prev = jnp.exp(cl - lw)
    w_inv = jnp.exp(-cl)
    w_rem = jnp.exp(cl_end - cl)
    bv = kk * a
    r_hat = r * w_t
    a_hat = -kk * w_prev
    b_hat = bv * w_inv
    k_hat = k * w_inv
    b_til = bv * w_rem
    k_til = k * w_rem

    lane_head = lax.broadcasted_iota(jnp.int32, (C, W), 1) // RW_HEAD

    def stack(m):
        mb = m.astype(BF16)
        zero = jnp.zeros_like(mb)
        return jnp.concatenate([jnp.where(lane_head == hd, mb, zero) for hd in range(RW_GROUP)],
                               axis=0)

    def fold(m):
        return m[0:C] + m[C:2 * C] + m[2 * C:3 * C] + m[3 * C:4 * C]

    gram = _dot_nt(jnp.concatenate([stack(a_hat), stack(r_hat)], axis=0),
                   jnp.concatenate([stack(b_hat), stack(k_hat)], axis=0))
    n = RW_GROUP * C
    ri = lax.broadcasted_iota(jnp.int32, (n, n), 0)
    ci = lax.broadcasted_iota(jnp.int32, (n, n), 1)
    strict = (ri % C) > (ci % C)
    incl = (ri % C) >= (ci % C)
    same_head = (ri // RW_HEAD) == (ci // RW_HEAD)
    l_ab = jnp.where(strict, gram[:n, :n], 0.0)
    l_ak = jnp.where(strict, gram[:n, n:], 0.0)
    m_rb = jnp.where(incl, gram[n:, :n], 0.0)
    m_rk = jnp.where(incl, gram[n:, n:], 0.0)

    inv = jnp.where(ri == ci, 1.0, 0.0) + l_ab
    pw = l_ab
    for _ in range(int(math.log2(C)) - 1):
        pw = _dot(pw, pw)
        inv = inv + _dot(inv, pw)

    a_p = fold(_dot(inv, stack(a_hat)))
    v_p = fold(_dot(_dot(inv, l_ak), stack(v)))
    r_p = r_hat + fold(_dot(m_rb, stack(a_p)))
    y0 = fold(_dot(jnp.concatenate([m_rb, m_rk], axis=1),
                   jnp.concatenate([stack(v_p), stack(v)], axis=0)))
    a_til = jnp.where(same_head, _dot_tn(b_til, a_p), 0.0)
    d_new = jnp.where(same_head,
                      _dot_tn(jnp.concatenate([v_p, v], axis=0),
                              jnp.concatenate([b_til, k_til], axis=0)), 0.0)
    s_old = s_ref[...]
    y = _dot_nt(r_p, s_old) + y0
    s_ref[...] = s_old * jnp.exp(cl_end) + _dot_nt(s_old, a_til) + d_new

    inv_n = 1.0 / RW_HEAD
    mean = _group_sum64(y) * inv_n
    yc = y - mean
    var = _group_sum64(yc * yc) * inv_n
    yn = yc * lax.rsqrt(var + GN_EPS) * lg_ref[...] + lb_ref[...]
    bonus = _group_sum64(r * k * rk_ref[...]) * v
    o_ref[...] = ((yn + bonus) * g_ref[...]).astype(BF16)


def _rwkv_scan(r, lw, k, v, kk, a, g, r_k, lnx_g, lnx_b):
    C = RW_CHUNK
    W = RW_GROUP * RW_HEAD
    nc = S // C
    slab = pl.BlockSpec((C, W), lambda b, gi, c: (b * nc + c, gi))
    prow = pl.BlockSpec((1, W), lambda b, gi, c: (0, gi))
    return pl.pallas_call(
        _rwkv_scan_kernel,
        grid=(B, D // W, nc),
        in_specs=[slab] * 7 + [prow] * 3,
        out_specs=slab,
        out_shape=jax.ShapeDtypeStruct((T, D), BF16),
        scratch_shapes=[pltpu.VMEM((W, W), F32)],
        compiler_params=_cparams("parallel", "parallel", "arbitrary"),
        name="rwkv_scan",
    )(r, lw, k, v, kk, a, g, r_k.reshape(1, D), lnx_g.reshape(1, D), lnx_b.reshape(1, D))


def _proj_res_kernel(gate_row, a_ref, w_ref, x_ref, mod_ref, o_ref):
    y = jnp.dot(a_ref[...], w_ref[...], preferred_element_type=F32)
    o_ref[...] = x_ref[...] + mod_ref[0][gate_row:gate_row + 1] * y


def _proj_res(a, w, x, mod, gate_row):
    tm = 512
    nb = S // tm
    kdim = a.shape[1]
    return pl.pallas_call(
        functools.partial(_proj_res_kernel, gate_row),
        grid=(T // tm,),
        in_specs=[pl.BlockSpec((tm, kdim), lambda i: (i, 0)),
                  pl.BlockSpec((kdim, D), lambda i: (0, 0)),
                  pl.BlockSpec((tm, D), lambda i: (i, 0)),
                  pl.BlockSpec((1, 6, D), lambda i: (i // nb, 0, 0))],
        out_specs=pl.BlockSpec((tm, D), lambda i: (i, 0)),
        out_shape=jax.ShapeDtypeStruct((T, D), F32),
        compiler_params=_cparams("parallel"),
        name="proj_res",
    )(a, w.astype(BF16), x, mod)


def _ffn_kernel(nf, x_ref, mod_ref, ng_ref, wg_ref, wu_ref, wd_ref, o_ref, h_scr, acc_scr):
    f = pl.program_id(1)

    @pl.when(f == 0)
    def _():
        mod = mod_ref[0]
        h_scr[...] = _modnorm(x_ref[...], ng_ref[...], mod[3:4], mod[4:5]).astype(BF16)
        acc_scr[...] = jnp.zeros_like(acc_scr)

    h = h_scr[...]
    gt = jnp.dot(h, wg_ref[...], preferred_element_type=F32)
    up = jnp.dot(h, wu_ref[...], preferred_element_type=F32)
    act = (gt * _sigmoid(gt) * up).astype(BF16)
    acc_scr[...] += jnp.dot(act, wd_ref[...], preferred_element_type=F32)

    @pl.when(f == nf - 1)
    def _():
        o_ref[...] = x_ref[...] + mod_ref[0][5:6] * acc_scr[...]


def _ffn_dense(x, mod, norm_g, w_gate_up, w_down):
    tm, tf = FFN_TM, FFN_TF
    nf = D_FF // tf
    nb = S // tm
    wgu = w_gate_up.astype(BF16)
    return pl.pallas_call(
        functools.partial(_ffn_kernel, nf),
        grid=(T // tm, nf),
        in_specs=[pl.BlockSpec((tm, D), lambda i, f: (i, 0)),
                  pl.BlockSpec((1, 6, D), lambda i, f: (i // nb, 0, 0)),
                  pl.BlockSpec((1, D), lambda i, f: (0, 0)),
                  pl.BlockSpec((D, tf), lambda i, f: (0, f)),
                  pl.BlockSpec((D, tf), lambda i, f: (0, nf + f)),
                  pl.BlockSpec((tf, D), lambda i, f: (f, 0))],
        out_specs=pl.BlockSpec((tm, D), lambda i, f: (i, 0)),
        out_shape=jax.ShapeDtypeStruct((T, D), F32),
        scratch_shapes=[pltpu.VMEM((tm, D), BF16), pltpu.VMEM((tm, D), F32)],
        compiler_params=_cparams("parallel", "arbitrary"),
        name="ffn_dense",
    )(x, mod, norm_g.reshape(1, D), wgu, wgu, w_down.astype(BF16))


def _moe_kernel(nf, x_ref, mod_ref, ng_ref, wr_ref, br_ref, wg_ref, wu_ref, wd_ref, o_ref,
                h_scr, cw_scr, acc_scr):
    e = pl.program_id(1)
    f = pl.program_id(2)
    lane = lax.broadcasted_iota(jnp.int32, cw_scr.shape, 1)

    @pl.when((e == 0) & (f == 0))
    def _():
        mod = mod_ref[0]
        h = _modnorm(x_ref[...], ng_ref[...], mod[3:4], mod[4:5])
        h_scr[...] = h.astype(BF16)
        acc_scr[...] = jnp.zeros_like(acc_scr)
        logits = jnp.dot(h, wr_ref[...], precision=HIGHEST, preferred_element_type=F32) + br_ref[...]
        neg = -jnp.inf
        logits = jnp.where(lane < N_EXPERTS, logits, neg)
        m1 = jnp.max(logits, axis=-1, keepdims=True)
        i1 = jnp.min(jnp.where(logits == m1, lane, LANES), axis=-1, keepdims=True)
        rest = jnp.where(lane == i1, neg, logits)
        m2 = jnp.max(rest, axis=-1, keepdims=True)
        i2 = jnp.min(jnp.where(rest == m2, lane, LANES), axis=-1, keepdims=True)
        e2 = jnp.exp(m2 - m1)
        w1 = 1.0 / (1.0 + e2)
        w2 = e2 / (1.0 + e2)
        cw_scr[...] = jnp.where(lane == i1, w1, 0.0) + jnp.where(lane == i2, w2, 0.0)

    h = h_scr[...]
    gt = jnp.dot(h, wg_ref[0], preferred_element_type=F32)
    up = jnp.dot(h, wu_ref[0], preferred_element_type=F32)
    act = (gt * _sigmoid(gt) * up).astype(BF16)
    cw_e = jnp.sum(jnp.where(lane == e, cw_scr[...], 0.0), axis=-1, keepdims=True)
    acc_scr[...] += cw_e * jnp.dot(act, wd_ref[0], preferred_element_type=F32)

    @pl.when((e == N_EXPERTS - 1) & (f == nf - 1))
    def _():
        o_ref[...] = x_ref[...] + mod_ref[0][5:6] * acc_scr[...]


def _moe(x, mod, norm_g, w_router, b_router, w_gate_up, w_down):
    tm, tf = FFN_TM, FFN_TF
    nf = D_FF // tf
    nb = S // tm
    wgu = w_gate_up.astype(BF16)
    wr = jnp.zeros((D, LANES), F32).at[:, :N_EXPERTS].set(w_router)
    br = jnp.zeros((1, LANES), F32).at[0, :N_EXPERTS].set(b_router)
    return pl.pallas_call(
        functools.partial(_moe_kernel, nf),
        grid=(T // tm, N_EXPERTS, nf),
        in_specs=[pl.BlockSpec((tm, D), lambda i, e, f: (i, 0)),
                  pl.BlockSpec((1, 6, D), lambda i, e, f: (i // nb, 0, 0)),
                  pl.BlockSpec((1, D), lambda i, e, f: (0, 0)),
                  pl.BlockSpec((D, LANES), lambda i, e, f: (0, 0)),
                  pl.BlockSpec((1, LANES), lambda i, e, f: (0, 0)),
                  pl.BlockSpec((1, D, tf), lambda i, e, f: (e, 0, f)),
                  pl.BlockSpec((1, D, tf), lambda i, e, f: (e, 0, nf + f)),
                  pl.BlockSpec((1, tf, D), lambda i, e, f: (e, f, 0))],
        out_specs=pl.BlockSpec((tm, D), lambda i, e, f: (i, 0)),
        out_shape=jax.ShapeDtypeStruct((T, D), F32),
        scratch_shapes=[pltpu.VMEM((tm, D), BF16), pltpu.VMEM((tm, LANES), F32),
                        pltpu.VMEM((tm, D), F32)],
        compiler_params=_cparams("parallel", "arbitrary", "arbitrary"),
        name="moe",
    )(x, mod, norm_g.reshape(1, D), wr, br, wgu, wgu, w_down.astype(BF16))


def _mla_proj_kernel(x_ref, mod_ref, ng_ref, pos_ref, invf_ref, wd_ref, qg_ref, kvg_ref,
                     wuq_ref, wukv_ref, q_out, k_out, v_out):
    mod = mod_ref[0]
    h = _modnorm(x_ref[...], ng_ref[...], mod[0:1], mod[1:2])
    down = _dot(h, wd_ref[...])
    cq = down[:, :MLA_Q_LORA]
    ckv = down[:, MLA_Q_LORA:MLA_Q_LORA + MLA_KV_LORA]
    kr = down[:, MLA_Q_LORA + MLA_KV_LORA:]
    cq = cq * lax.rsqrt(jnp.mean(cq * cq, axis=-1, keepdims=True) + NORM_EPS) * qg_ref[...]
    ckv = ckv * lax.rsqrt(jnp.mean(ckv * ckv, axis=-1, keepdims=True) + NORM_EPS) * kvg_ref[...]

    ang = pos_ref[...] * invf_ref[...]
    lane = lax.broadcasted_iota(jnp.int32, ang.shape, 1)
    is_rope = (lane >= MLA_NOPE) & (lane < MLA_NOPE + MLA_ROPE)
    cos_r = jnp.where(is_rope, jnp.cos(ang), 0.0)
    sin_r = jnp.where(is_rope, jnp.sin(ang), 0.0)
    cos_q = jnp.where(lane < MLA_NOPE, 1.0, cos_r)
    shift = LANES - MLA_ROPE

    k_rope = kr * cos_r + pltpu.roll(kr, shift, axis=1) * sin_r

    qf = _dot(cq, wuq_ref[...])
    kv = _dot(ckv, wukv_ref[...])
    for hd in range(MLA_HEADS):
        sl = slice(hd * LANES, (hd + 1) * LANES)
        qs = qf[:, sl]
        q_out[:, sl] = ((qs * cos_q + pltpu.roll(qs, shift, axis=1) * sin_r) * MLA_SCALE).astype(BF16)
        k_out[:, sl] = (kv[:, sl] + k_rope).astype(BF16)
    _store_transposed(v_out, kv[:, MLA_HEADS * LANES:])


def _rot_half_cols(w):
    half = MLA_ROPE // 2
    return jnp.concatenate([-w[..., half:], w[..., :half]], axis=-1)


def _mla_proj(x, mod, norm_g, positions, w_down, q_norm_g, kv_norm_g, w_uq, w_ukv):
    tm = PROJ_TM
    nb = S // tm
    half = MLA_ROPE // 2
    wr = w_down[:, MLA_Q_LORA + MLA_KV_LORA:]
    wd = jnp.concatenate([w_down[:, :MLA_Q_LORA + MLA_KV_LORA],
                          jnp.zeros((D, MLA_NOPE), F32), wr, _rot_half_cols(wr)], axis=1).astype(BF16)
    wq = w_uq.reshape(MLA_Q_LORA, MLA_HEADS, MLA_NOPE + MLA_ROPE)
    wq_r = wq[..., MLA_NOPE:]
    wuq = jnp.concatenate([wq[..., :MLA_NOPE], wq_r, _rot_half_cols(wq_r)], axis=-1)
    wuq = wuq.reshape(MLA_Q_LORA, MLA_HEADS * LANES).astype(BF16)
    wkv = w_ukv.reshape(MLA_KV_LORA, MLA_HEADS, 2 * MLA_NOPE)
    wk = jnp.concatenate([wkv[..., :MLA_NOPE], jnp.zeros_like(wkv[..., :MLA_NOPE])], axis=-1)
    wukv = jnp.concatenate([wk.reshape(MLA_KV_LORA, MLA_HEADS * LANES),
                            wkv[..., MLA_NOPE:].reshape(MLA_KV_LORA, D)], axis=1).astype(BF16)
    inv_freq = ROPE_BASE ** (-jnp.arange(half, dtype=F32) / half)
    lane = jnp.arange(LANES)
    invf = jnp.where((lane >= MLA_NOPE) & (lane < MLA_NOPE + MLA_ROPE),
                     inv_freq[(lane - MLA_NOPE) % half], 0.0).reshape(1, LANES).astype(F32)
    pos = jnp.broadcast_to(positions.reshape(T, 1).astype(F32), (T, LANES))
    full = lambda a: pl.BlockSpec(a.shape, lambda i: (0,) * a.ndim)
    ops = [x, mod, norm_g.reshape(1, D), pos, invf, wd, q_norm_g.reshape(1, -1),
           kv_norm_g.reshape(1, -1), wuq, wukv]
    specs = [pl.BlockSpec((tm, D), lambda i: (i, 0)),
             pl.BlockSpec((1, 6, D), lambda i: (i // nb, 0, 0)),
             full(ops[2]),
             pl.BlockSpec((tm, LANES), lambda i: (i, 0))] + [full(a) for a in ops[4:]]
    qw = MLA_HEADS * LANES
    return pl.pallas_call(
        _mla_proj_kernel,
        grid=(T // tm,),
        in_specs=specs,
        out_specs=[pl.BlockSpec((tm, qw), lambda i: (i, 0)),
                   pl.BlockSpec((tm, qw), lambda i: (i, 0)),
                   _vt_spec(tm)],
        out_shape=[jax.ShapeDtypeStruct((T, qw), BF16), jax.ShapeDtypeStruct((T, qw), BF16),
                   _vt_shape()],
        compiler_params=_cparams("parallel"),
        name="mla_proj",
    )(*ops)


def _fox_proj_kernel(x_ref, mod_ref, ng_ref, w_ref, wf_ref, bf_ref, qg_ref, kg_ref,
                     q_out, k_out, v_out, og_out, fc_out, fr_out, carry_scr):
    i = pl.program_id(0)
    tm = x_ref.shape[0]
    mod = mod_ref[0]
    h = _modnorm(x_ref[...], ng_ref[...], mod[0:1], mod[1:2]).astype(BF16)
    q = jnp.dot(h, w_ref[:, 0:D], preferred_element_type=F32)
    k = jnp.dot(h, w_ref[:, D:2 * D], preferred_element_type=F32)
    v = jnp.dot(h, w_ref[:, 2 * D:3 * D], preferred_element_type=F32)
    og = jnp.dot(h, w_ref[:, 3 * D:4 * D], preferred_element_type=F32)
    inv_n = 1.0 / FOX_HEAD
    q = q * lax.rsqrt(_group_sum64(q * q) * inv_n + NORM_EPS) * qg_ref[...] * FOX_SCALE
    k = k * lax.rsqrt(_group_sum64(k * k) * inv_n + NORM_EPS) * kg_ref[...]
    q_out[...] = q.astype(BF16)
    k_out[...] = k.astype(BF16)
    _store_transposed(v_out, v)
    og_out[...] = _sigmoid(og).astype(BF16)

    z = jnp.dot(h, wf_ref[...], preferred_element_type=F32) + bf_ref[...]
    lane = lax.broadcasted_iota(jnp.int32, z.shape, 1)
    log_f = jnp.minimum(z, 0.0) - jnp.log(1.0 + jnp.exp(-jnp.abs(z)))
    log_f = jnp.where(lane < FOX_HEADS, log_f, 0.0)
    ti = lax.broadcasted_iota(jnp.int32, (tm, tm), 0)
    si = lax.broadcasted_iota(jnp.int32, (tm, tm), 1)
    tri = jnp.where(ti >= si, 1.0, 0.0).astype(F32)
    seq_start = (i % (S // tm)) == 0
    carry = jnp.where(seq_start, 0.0, carry_scr[...])
    cum = jnp.dot(tri, log_f, precision=HIGHEST, preferred_element_type=F32) + carry
    carry_scr[...] = cum[tm - 1:tm, :]
    fc_out[...] = cum
    fr_out[0] = cum.T


def _fox_proj(x, mod, norm_g, w_in, b_f, q_norm_g, k_norm_g):
    tm = PROJ_TM
    nb = S // tm
    w_main = jnp.concatenate([w_in[:, :3 * D], w_in[:, 3 * D + FOX_HEADS:]], axis=1).astype(BF16)
    w_f = jnp.zeros((D, LANES), F32).at[:, :FOX_HEADS].set(w_in[:, 3 * D:3 * D + FOX_HEADS]).astype(BF16)
    bf = jnp.zeros((1, LANES), F32).at[0, :FOX_HEADS].set(b_f)
    qg = jnp.tile(q_norm_g, FOX_HEADS).reshape(1, D)
    kg = jnp.tile(k_norm_g, FOX_HEADS).reshape(1, D)
    full = lambda a: pl.BlockSpec(a.shape, lambda i: (0,) * a.ndim)
    tile = pl.BlockSpec((tm, D), lambda i: (i, 0))
    ops = [x, mod, norm_g.reshape(1, D), w_main, w_f, bf, qg, kg]
    specs = [tile, pl.BlockSpec((1, 6, D), lambda i: (i // nb, 0, 0))] + [full(a) for a in ops[2:]]
    return pl.pallas_call(
        _fox_proj_kernel,
        grid=(T // tm,),
        in_specs=specs,
        out_specs=[tile, tile, _vt_spec(tm), tile,
                   pl.BlockSpec((tm, LANES), lambda i: (i, 0)),
                   pl.BlockSpec((1, LANES, tm), lambda i: (i // nb, 0, i % nb))],
        out_shape=[jax.ShapeDtypeStruct((T, D), BF16)] * 2 + [_vt_shape(), jax.ShapeDtypeStruct((T, D), BF16),
                   jax.ShapeDtypeStruct((T, LANES), F32), jax.ShapeDtypeStruct((B, LANES, S), F32)],
        scratch_shapes=[pltpu.VMEM((1, LANES), F32)],
        compiler_params=_cparams("arbitrary"),
        name="fox_proj",
    )(*ops)


def _attn_kernel(fox, *refs):
    if fox:
        q_ref, k_ref, vt_ref, og_ref, fc_ref, fr_ref, o_ref, st_scr, p_scr, acc_scr, fs_scr = refs
    else:
        q_ref, k_ref, vt_ref, o_ref, st_scr, p_scr, acc_scr = refs
    tq, tk = ATT_TQ, ATT_TK
    nk = S // tk
    w2 = 2 * tq
    p = pl.program_id(1)
    qi = pl.program_id(2)

    if fox:
        @pl.when(qi == 0)
        def _():
            fcv = fc_ref[...]
            lane_s = lax.broadcasted_iota(jnp.int32, fcv.shape, 1)
            for j in range(2):
                colv = jnp.sum(jnp.where(lane_s == 2 * p + j, fcv, 0.0), axis=-1, keepdims=True)
                full = jnp.broadcast_to(colv, fcv.shape)
                for kb in range(nk):
                    fs_scr[j * nk + kb] = full[kb * tk:(kb + 1) * tk]

    qb = q_ref[...]
    lane = lax.broadcasted_iota(jnp.int32, qb.shape, 1)
    per_head = FOX_HEAD if fox else LANES
    zero = jnp.zeros_like(qb)
    q_bd = jnp.concatenate([jnp.where(lane // per_head == j, qb, zero) for j in range(2)], axis=0)
    if fox:
        f_t = jnp.concatenate([fr_ref[0, pl.ds(2 * p + j, 1), :] for j in range(2)], axis=1)

    srow = lax.broadcasted_iota(jnp.int32, (tk, w2), 0)
    tcol = lax.broadcasted_iota(jnp.int32, (tk, w2), 1) % tq
    diag_mask = (srow <= tcol) if fox else ((srow // 64) <= (tcol // 64))

    def scores(kb, mx, mask):
        start = pl.multiple_of(kb * tk, tk)
        st = lax.dot_general(k_ref[pl.ds(start, tk), :], q_bd, (((1,), (1,)), ((), ())),
                             preferred_element_type=F32)
        if fox:
            reps = tq // LANES
            f_s = jnp.concatenate([fs_scr[kb]] * reps + [fs_scr[nk + kb]] * reps, axis=1)
            st = st + (f_t - f_s)
        if mask is not None:
            st = jnp.where(mask, st, -jnp.inf)
        st_scr[kb] = st
        return jnp.maximum(mx, jnp.max(st, axis=0, keepdims=True))

    def scores_pair(i, mx):
        return scores(2 * i + 1, scores(2 * i, mx, None), None)

    mx = jnp.full((1, w2), -jnp.inf, F32)
    mx = lax.fori_loop(0, qi // 2, scores_pair, mx)
    mx = lax.fori_loop(0, qi % 2, lambda _, m: scores(qi - 1, m, None), mx)
    mx = scores(qi, mx, diag_mask)

    def probs(kb, l):
        pt = jnp.exp(st_scr[kb] - mx)
        p_scr[kb] = pt.astype(BF16)
        return l + jnp.sum(pt, axis=0, keepdims=True)

    l = lax.fori_loop(0, qi + 1, probs, jnp.zeros((1, w2), F32))

    acc_scr[...] = jnp.zeros_like(acc_scr)

    def values(kb):
        acc_scr[...] += jnp.dot(vt_ref[0, kb], p_scr[kb], preferred_element_type=F32)

    def values_pair(i, carry):
        values(2 * i)
        values(2 * i + 1)
        return carry

    lax.fori_loop(0, (qi + 1) // 2, values_pair, 0)

    @pl.when(qi % 2 == 0)
    def _():
        values(qi)

    acc = acc_scr[...] / l
    sub = lax.broadcasted_iota(jnp.int32, (LANES, tq), 0)
    o_t = jnp.where(sub // 64 == 0, acc[:, :tq], acc[:, tq:])
    o = o_t.T
    if fox:
        o = o * og_ref[...].astype(F32)
    o_ref[...] = o.astype(BF16)


def _attention(q, k, vt, fox_extras=None):
    fox = fox_extras is not None
    tq, tk = ATT_TQ, ATT_TK
    nq = S // tq
    nk = S // tk
    qw = LANES if fox else 2 * LANES
    ops = [q, k, vt]
    specs = [pl.BlockSpec((tq, qw), lambda b, p, i: (b * nq + i, p)),
             pl.BlockSpec((S, qw), lambda b, p, i: (b, p)),
             pl.BlockSpec((1, nk, LANES, tk), lambda b, p, i: (b, 0, p, 0))]
    scratch = [pltpu.VMEM((nk, tk, 2 * tq), F32), pltpu.VMEM((nk, tk, 2 * tq), BF16),
               pltpu.VMEM((LANES, 2 * tq), F32)]
    if fox:
        og, fc, fr = fox_extras
        ops += [og, fc, fr]
        specs += [pl.BlockSpec((tq, LANES), lambda b, p, i: (b * nq + i, p)),
                  pl.BlockSpec((S, LANES), lambda b, p, i: (b, 0)),
                  pl.BlockSpec((1, FOX_HEADS, tq), lambda b, p, i: (b, 0, i))]
        scratch.append(pltpu.VMEM((2 * nk, tk, LANES), F32))
    return pl.pallas_call(
        functools.partial(_attn_kernel, fox),
        grid=(B, 8, nq),
        in_specs=specs,
        out_specs=pl.BlockSpec((tq, LANES), lambda b, p, i: (b * nq + i, p)),
        out_shape=jax.ShapeDtypeStruct((T, D), BF16),
        scratch_shapes=scratch,
        compiler_params=_cparams("parallel", "parallel", "arbitrary"),
        name="fox_attn" if fox else "mla_attn",
    )(*ops)


def _final_norm_kernel(x_ref, g_ref, o_ref):
    x = x_ref[...]
    o_ref[...] = x * lax.rsqrt(jnp.mean(x * x, axis=-1, keepdims=True) + NORM_EPS) * g_ref[...]


def _final_norm(x, g):
    tm = 512
    return pl.pallas_call(
        _final_norm_kernel,
        grid=(T // tm,),
        in_specs=[pl.BlockSpec((tm, D), lambda i: (i, 0)), pl.BlockSpec((1, D), lambda i: (0, 0))],
        out_specs=pl.BlockSpec((tm, D), lambda i: (i, 0)),
        out_shape=jax.ShapeDtypeStruct((T, D), F32),
        compiler_params=_cparams("parallel"),
        name="final_norm",
    )(x, g.reshape(1, D))


def kernel(x, c, positions, ada_w, ada_b, norm_mix_g, norm_ffn_g, final_norm_g, rw_mu, rw_w_rkv, rw_w_o, rw_w0, rw_w1, rw_w2, rw_a0, rw_a1, rw_a2, rw_g1, rw_g2, rw_k_k, rw_k_a, rw_r_k, rw_lnx_g, rw_lnx_b, rw_v0, rw_v1, rw_v2, mla_w_down, mla_q_norm_g, mla_kv_norm_g, mla_w_uq, mla_w_ukv, mla_w_o, fox_w_in, fox_b_f, fox_q_norm_g, fox_k_norm_g, fox_w_o, ffn_w_gate_up, ffn_w_down, moe_w_router, moe_b_router, moe_w_gate_up, moe_w_down):
    xf = x.reshape(T, D)
    mods = _ada_mods(c, ada_w, ada_b)
    v_first = None
    for i in range(DEPTH):
        mod = mods[i]
        kind, j = i % 3, i // 3
        if kind == 0:
            p = dict(mu=rw_mu[j], w_rkv=rw_w_rkv[j], w0=rw_w0[j], w1=rw_w1[j], w2=rw_w2[j],
                     a0=rw_a0[j], a1=rw_a1[j], a2=rw_a2[j], g1=rw_g1[j], g2=rw_g2[j],
                     k_k=rw_k_k[j], k_a=rw_k_a[j])
            if j > 0:
                p.update(v0=rw_v0[j - 1], v1=rw_v1[j - 1], v2=rw_v2[j - 1])
            r, lw, k, v, kk, a, g = _rwkv_proj(xf, mod, norm_mix_g[i], p, v_first if j > 0 else None)
            if j == 0:
                v_first = v
            y = _rwkv_scan(r, lw, k, v, kk, a, g, rw_r_k[j], rw_lnx_g[j], rw_lnx_b[j])
            xf = _proj_res(y, rw_w_o[j], xf, mod, 2)
        elif kind == 1:
            q, k, v = _mla_proj(xf, mod, norm_mix_g[i], positions, mla_w_down[j], mla_q_norm_g[j],
                                mla_kv_norm_g[j], mla_w_uq[j], mla_w_ukv[j])
            o = _attention(q, k, v)
            xf = _proj_res(o, mla_w_o[j], xf, mod, 2)
        else:
            q, k, v, og, fc, fr = _fox_proj(xf, mod, norm_mix_g[i], fox_w_in[j], fox_b_f[j],
                                            fox_q_norm_g[j], fox_k_norm_g[j])
            o = _attention(q, k, v, (og, fc, fr))
            xf = _proj_res(o, fox_w_o[j], xf, mod, 2)
        if i % 2 == 0:
            xf = _ffn_dense(xf, mod, norm_ffn_g[i], ffn_w_gate_up[i // 2], ffn_w_down[i // 2])
        else:
            xf = _moe(xf, mod, norm_ffn_g[i], moe_w_router[i // 2], moe_b_router[i // 2],
                      moe_w_gate_up[i // 2], moe_w_down[i // 2])
    return _final_norm(xf, final_norm_g).reshape(B, S, D)
```

```python
import functools
import math

import jax
import jax.numpy as jnp
from jax import lax
from jax.experimental import pallas as pl
from jax.experimental.pallas import tpu as pltpu

F32 = jnp.float32
BF16 = jnp.bfloat16
HIGHEST = lax.Precision.HIGHEST

D = 1024
B = 8
S = 2048
T = B * S
DEPTH = 4
NORM_EPS = 1e-6

RW_HEAD = 64
GN_EPS = 64e-5
EXP_NEG_HALF = math.exp(-0.5)

MLA_HEADS = 16
MLA_NOPE = 64
MLA_ROPE = 32
MLA_Q_LORA = 768
MLA_KV_LORA = 256
MLA_SCALE = (MLA_NOPE + MLA_ROPE) ** -0.5
ROPE_BASE = 10000.0

FOX_HEADS = 16
FOX_HEAD = 64
FOX_SCALE = FOX_HEAD ** -0.5

D_FF = 2816
N_EXPERTS = 8

LANES = 128
VMEM_LIMIT = 56 * 1024 * 1024

RW_CHUNK = 64
RW_GROUP = 4
ATT_TQ = 256
ATT_TK = 256
FFN_TM = 512
FFN_TF = 1408
PROJ_TM = 256
MOE_TW = 1024
MOE_TS = 384


def _cparams(*sem):
    return pltpu.CompilerParams(dimension_semantics=sem, vmem_limit_bytes=VMEM_LIMIT)


def _dot(a, b):
    return jnp.dot(a.astype(BF16), b.astype(BF16), preferred_element_type=F32)


def _dot_nt(a, b):
    return lax.dot_general(a.astype(BF16), b.astype(BF16), (((1,), (1,)), ((), ())),
                           preferred_element_type=F32)


def _dot_tn(a, b):
    return lax.dot_general(a.astype(BF16), b.astype(BF16), (((0,), (0,)), ((), ())),
                           preferred_element_type=F32)


def _sigmoid(z):
    return 1.0 / (1.0 + jnp.exp(-z))


def _modnorm(x, g, shift, scale):
    ms = jnp.mean(x * x, axis=-1, keepdims=True)
    y = x * lax.rsqrt(ms + NORM_EPS) * g
    return y * (1.0 + scale) + shift


def _group_sum64(x):
    gi = lax.broadcasted_iota(jnp.int32, (LANES, LANES), 0) // 64
    gj = lax.broadcasted_iota(jnp.int32, (LANES, LANES), 1) // 64
    ones = jnp.where(gi == gj, 1.0, 0.0).astype(BF16)
    hi = x.astype(BF16)
    lo = (x - hi.astype(F32)).astype(BF16)
    outs = []
    for s in range(x.shape[1] // LANES):
        sl = slice(s * LANES, (s + 1) * LANES)
        outs.append(jnp.dot(hi[:, sl], ones, preferred_element_type=F32)
                    + jnp.dot(lo[:, sl], ones, preferred_element_type=F32))
    return outs[0] if len(outs) == 1 else jnp.concatenate(outs, axis=1)


def _store_transposed(vt_ref, v):
    vt = v.T
    for s in range(v.shape[0] // ATT_TK):
        vt_ref[0, s] = vt[:, s * ATT_TK:(s + 1) * ATT_TK].astype(BF16)


def _vt_spec(tm):
    nb = S // tm
    return pl.BlockSpec((1, tm // ATT_TK, D, ATT_TK), lambda i: (i // nb, i % nb, 0, 0))


def _vt_shape():
    return jax.ShapeDtypeStruct((B, S // ATT_TK, D, ATT_TK), BF16)


def _ada_kernel(c_ref, w_ref, b_ref, o_ref):
    c = c_ref[...]
    cond = c * _sigmoid(c)
    o_ref[0] = _dot(cond, w_ref[0]) + b_ref[0]


def _ada_mods(c, ada_w, ada_b):
    tn = 1536
    out = pl.pallas_call(
        _ada_kernel,
        grid=(DEPTH, 6 * D // tn),
        in_specs=[pl.BlockSpec((B, D), lambda l, j: (0, 0)),
                  pl.BlockSpec((1, D, tn), lambda l, j: (l, 0, j)),
                  pl.BlockSpec((1, 1, tn), lambda l, j: (l, 0, j))],
        out_specs=pl.BlockSpec((1, B, tn), lambda l, j: (l, 0, j)),
        out_shape=jax.ShapeDtypeStruct((DEPTH, B, 6 * D), F32),
        compiler_params=_cparams("parallel", "parallel"),
        name="ada_mods",
    )(c, ada_w, ada_b.reshape(DEPTH, 1, 6 * D))
    return out.reshape(DEPTH, B, 6, D)


def _rwkv_proj_kernel(has_vres, *refs):
    if has_vres:
        (x_ref, xp_ref, mod_ref, ng_ref, mu_ref, wr_ref, wk_ref, wv_ref, w0_ref, w1_ref, w2_ref,
         a0_ref, a1_ref, a2_ref, g1_ref, g2_ref, kk_ref, ka_ref, v0_ref, v1_ref, v2_ref, vf_ref,
         r_out, lw_out, k_out, v_out, kk_out, a_out, g_out) = refs
    else:
        (x_ref, xp_ref, mod_ref, ng_ref, mu_ref, wr_ref, wk_ref, wv_ref, w0_ref, w1_ref, w2_ref,
         a0_ref, a1_ref, a2_ref, g1_ref, g2_ref, kk_ref, ka_ref,
         r_out, lw_out, k_out, v_out, kk_out, a_out, g_out) = refs
    i = pl.program_id(0)
    mod = mod_ref[0]
    shift, scale = mod[0:1], mod[1:2]
    g = ng_ref[...]
    h = _modnorm(x_ref[...], g, shift, scale)
    hp = _modnorm(xp_ref[...], g, shift, scale)
    seq_start = (i % (S // PROJ_TM)) == 0
    prev_row = jnp.where(seq_start, 0.0, hp[7:8, :])
    row = lax.broadcasted_iota(jnp.int32, h.shape, 0)
    prev = jnp.where(row == 0, prev_row, pltpu.roll(h, 1, axis=0))
    delta = prev - h
    mu = mu_ref[...]
    xr = h + delta * mu[0:1]
    xw = h + delta * mu[1:2]
    xk = h + delta * mu[2:3]
    xv = h + delta * mu[3:4]
    xa = h + delta * mu[4:5]
    xg = h + delta * mu[5:6]
    r = _dot(xr, wr_ref[...])
    k = _dot(xk, wk_ref[...])
    v = _dot(xv, wv_ref[...])
    w_raw = w0_ref[...] + _dot(jnp.tanh(_dot(xw, w1_ref[...])), w2_ref[...])
    lw_out[...] = -_sigmoid(w_raw) * EXP_NEG_HALF
    if has_vres:
        mix = _sigmoid(v0_ref[...] + _dot(_dot(xv, v1_ref[...]), v2_ref[...]))
        v = v + (vf_ref[...] - v) * mix
    a = _sigmoid(a0_ref[...] + _dot(_dot(xa, a1_ref[...]), a2_ref[...]))
    g_out[...] = _dot(_sigmoid(_dot(xg, g1_ref[...])), g2_ref[...])
    kk = k * kk_ref[...]
    norm = jnp.sqrt(_group_sum64(kk * kk))
    kk_out[...] = kk / jnp.maximum(norm, 1e-12)
    k_out[...] = k * (1.0 + (a - 1.0) * ka_ref[...])
    r_out[...] = r
    v_out[...] = v
    a_out[...] = a


def _rwkv_proj(x, mod, norm_g, p, v_first):
    has_vres = v_first is not None
    tm = PROJ_TM
    nb = S // tm
    row = lambda a: a.reshape(1, -1)
    full = lambda a: pl.BlockSpec(a.shape, lambda i: (0,) * a.ndim)
    tile = pl.BlockSpec((tm, D), lambda i: (i, 0))
    ops = [x, x, mod, row(norm_g), p["mu"],
           p["w_rkv"][0].astype(BF16), p["w_rkv"][1].astype(BF16), p["w_rkv"][2].astype(BF16),
           row(p["w0"]), p["w1"].astype(BF16), p["w2"].astype(BF16),
           row(p["a0"]), p["a1"].astype(BF16), p["a2"].astype(BF16),
           p["g1"].astype(BF16), p["g2"].astype(BF16), row(p["k_k"]), row(p["k_a"])]
    specs = [tile,
             pl.BlockSpec((8, D), lambda i: (jnp.maximum(i * (tm // 8) - 1, 0), 0)),
             pl.BlockSpec((1, 6, D), lambda i: (i // nb, 0, 0))]
    specs += [full(a) for a in ops[3:]]
    if has_vres:
        extra = [row(p["v0"]), p["v1"].astype(BF16), p["v2"].astype(BF16)]
        ops += extra + [v_first]
        specs += [full(a) for a in extra] + [tile]
    outs = pl.pallas_call(
        functools.partial(_rwkv_proj_kernel, has_vres),
        grid=(T // tm,),
        in_specs=specs,
        out_specs=[tile] * 7,
        out_shape=[jax.ShapeDtypeStruct((T, D), F32)] * 7,
        compiler_params=_cparams("parallel"),
        name="rwkv_proj",
    )(*ops)
    return outs


def _rwkv_scan_kernel(r_ref, lw_ref, k_ref, v_ref, kk_ref, a_ref, g_ref, rk_ref, lg_ref, lb_ref,
                      o_ref, s_ref):
    c = pl.program_id(2)
    C = RW_CHUNK
    W = RW_GROUP * RW_HEAD

    @pl.when(c == 0)
    def _():
        s_ref[...] = jnp.zeros_like(s_ref)

    r = r_ref[...]
    lw = lw_ref[...]
    k = k_ref[...]
    v = v_ref[...]
    kk = kk_ref[...]
    a = a_ref[...]

    ti = lax.broadcasted_iota(jnp.int32, (C, C), 0)
    si = lax.broadcasted_iota(jnp.int32, (C, C), 1)
    tri = jnp.where(ti >= si, 1.0, 0.0).astype(F32)
    cl = jnp.dot(tri, lw, precision=HIGHEST, preferred_element_type=F32)
    cl_end = cl[C - 1:C, :]
    w_t = jnp.exp(cl)
    w_prev = jnp.exp(cl - lw)
    w_inv = jnp.exp(-cl)
    w_rem = jnp.exp(cl_end - cl)
    bv = kk * a
    r_hat = r * w_t
    a_hat = -kk * w_prev
    b_hat = bv * w_inv
    k_hat = k * w_inv
    b_til = bv * w_rem
    k_til = k * w_rem

    lane_head = lax.broadcasted_iota(jnp.int32, (C, W), 1) // RW_HEAD

    def stack(m):
        mb = m.astype(BF16)
        zero = jnp.zeros_like(mb)
        return jnp.concatenate([jnp.where(lane_head == hd, mb, zero) for hd in range(RW_GROUP)],
                               axis=0)

    def fold(m):
        return m[0:C] + m[C:2 * C] + m[2 * C:3 * C] + m[3 * C:4 * C]

    gram = _dot_nt(jnp.concatenate([stack(a_hat), stack(r_hat)], axis=0),
                   jnp.concatenate([stack(b_hat), stack(k_hat)], axis=0))
    n = RW_GROUP * C
    ri = lax.broadcasted_iota(jnp.int32, (n, n), 0)
    ci = lax.broadcasted_iota(jnp.int32, (n, n), 1)
    strict = (ri % C) > (ci % C)
    incl = (ri % C) >= (ci % C)
    same_head = (ri // RW_HEAD) == (ci // RW_HEAD)
    l_ab = jnp.where(strict, gram[:n, :n], 0.0)
    l_ak = jnp.where(strict, gram[:n, n:], 0.0)
    m_rb = jnp.where(incl, gram[n:, :n], 0.0)
    m_rk = jnp.where(incl, gram[n:, n:], 0.0)

    inv = jnp.where(ri == ci, 1.0, 0.0) + l_ab
    pw = l_ab
    for _ in range(int(math.log2(C)) - 1):
        pw = _dot(pw, pw)
        inv = inv + _dot(inv, pw)

    a_p = fold(_dot(inv, stack(a_hat)))
    v_p = fold(_dot(_dot(inv, l_ak), stack(v)))
    r_p = r_hat + fold(_dot(m_rb, stack(a_p)))
    y0 = fold(_dot(jnp.concatenate([m_rb, m_rk], axis=1),
                   jnp.concatenate([stack(v_p), stack(v)], axis=0)))
    a_til = jnp.where(same_head, _dot_tn(b_til, a_p), 0.0)
    d_new = jnp.where(same_head,
                      _dot_tn(jnp.concatenate([v_p, v], axis=0),
                              jnp.concatenate([b_til, k_til], axis=0)), 0.0)
    s_old = s_ref[...]
    y = _dot_nt(r_p, s_old) + y0
    s_ref[...] = s_old * jnp.exp(cl_end) + _dot_nt(s_old, a_til) + d_new

    inv_n = 1.0 / RW_HEAD
    mean = _group_sum64(y) * inv_n
    yc = y - mean
    var = _group_sum64(yc * yc) * inv_n
    yn = yc * lax.rsqrt(var + GN_EPS) * lg_ref[...] + lb_ref[...]
    bonus = _group_sum64(r * k * rk_ref[...]) * v
    o_ref[...] = ((yn + bonus) * g_ref[...]).astype(BF16)


def _rwkv_scan(r, lw, k, v, kk, a, g, r_k, lnx_g, lnx_b):
    C = RW_CHUNK
    W = RW_GROUP * RW_HEAD
    nc = S // C
    slab = pl.BlockSpec((C, W), lambda b, gi, c: (b * nc + c, gi))
    prow = pl.BlockSpec((1, W), lambda b, gi, c: (0, gi))
    return pl.pallas_call(
        _rwkv_scan_kernel,
        grid=(B, D // W, nc),
        in_specs=[slab] * 7 + [prow] * 3,
        out_specs=slab,
        out_shape=jax.ShapeDtypeStruct((T, D), BF16),
        scratch_shapes=[pltpu.VMEM((W, W), F32)],
        compiler_params=_cparams("parallel", "parallel", "arbitrary"),
        name="rwkv_scan",
    )(r, lw, k, v, kk, a, g, r_k.reshape(1, D), lnx_g.reshape(1, D), lnx_b.reshape(1, D))


def _proj_res_kernel(gate_row, a_ref, w_ref, x_ref, mod_ref, o_ref):
    y = jnp.dot(a_ref[...], w_ref[...], preferred_element_type=F32)
    o_ref[...] = x_ref[...] + mod_ref[0][gate_row:gate_row + 1] * y


def _proj_res(a, w, x, mod, gate_row):
    tm = 512
    nb = S // tm
    kdim = a.shape[1]
    return pl.pallas_call(
        functools.partial(_proj_res_kernel, gate_row),
        grid=(T // tm,),
        in_specs=[pl.BlockSpec((tm, kdim), lambda i: (i, 0)),
                  pl.BlockSpec((kdim, D), lambda i: (0, 0)),
                  pl.BlockSpec((tm, D), lambda i: (i, 0)),
                  pl.BlockSpec((1, 6, D), lambda i: (i // nb, 0, 0))],
        out_specs=pl.BlockSpec((tm, D), lambda i: (i, 0)),
        out_shape=jax.ShapeDtypeStruct((T, D), F32),
        compiler_params=_cparams("parallel"),
        name="proj_res",
    )(a, w.astype(BF16), x, mod)


def _ffn_kernel(nf, x_ref, mod_ref, ng_ref, wg_ref, wu_ref, wd_ref, o_ref, h_scr, acc_scr):
    f = pl.program_id(1)

    @pl.when(f == 0)
    def _():
        mod = mod_ref[0]
        h_scr[...] = _modnorm(x_ref[...], ng_ref[...], mod[3:4], mod[4:5]).astype(BF16)
        acc_scr[...] = jnp.zeros_like(acc_scr)

    h = h_scr[...]
    gt = jnp.dot(h, wg_ref[...], preferred_element_type=F32)
    up = jnp.dot(h, wu_ref[...], preferred_element_type=F32)
    act = (gt * _sigmoid(gt) * up).astype(BF16)
    acc_scr[...] += jnp.dot(act, wd_ref[...], preferred_element_type=F32)

    @pl.when(f == nf - 1)
    def _():
        o_ref[...] = x_ref[...] + mod_ref[0][5:6] * acc_scr[...]


def _ffn_dense(x, mod, norm_g, w_gate_up, w_down):
    tm, tf = FFN_TM, FFN_TF
    nf = D_FF // tf
    nb = S // tm
    wgu = w_gate_up.astype(BF16)
    return pl.pallas_call(
        functools.partial(_ffn_kernel, nf),
        grid=(T // tm, nf),
        in_specs=[pl.BlockSpec((tm, D), lambda i, f: (i, 0)),
                  pl.BlockSpec((1, 6, D), lambda i, f: (i // nb, 0, 0)),
                  pl.BlockSpec((1, D), lambda i, f: (0, 0)),
                  pl.BlockSpec((D, tf), lambda i, f: (0, f)),
                  pl.BlockSpec((D, tf), lambda i, f: (0, nf + f)),
                  pl.BlockSpec((tf, D), lambda i, f: (f, 0))],
        out_specs=pl.BlockSpec((tm, D), lambda i, f: (i, 0)),
        out_shape=jax.ShapeDtypeStruct((T, D), F32),
        scratch_shapes=[pltpu.VMEM((tm, D), BF16), pltpu.VMEM((tm, D), F32)],
        compiler_params=_cparams("parallel", "arbitrary"),
        name="ffn_dense",
    )(x, mod, norm_g.reshape(1, D), wgu, wgu, w_down.astype(BF16))


def _route_kernel(x_ref, mod_ref, ng_ref, wr_ref, br_ref,
                  h_out, cw_out, dest_out, dest_t_out, cnt_out, carry_scr):
    i = pl.program_id(0)
    tm = x_ref.shape[0]
    mod = mod_ref[0]
    h = _modnorm(x_ref[...], ng_ref[...], mod[3:4], mod[4:5])
    h_out[...] = h.astype(BF16)
    logits = jnp.dot(h, wr_ref[...], precision=HIGHEST, preferred_element_type=F32) + br_ref[...]
    lane = lax.broadcasted_iota(jnp.int32, logits.shape, 1)
    neg = -jnp.inf
    logits = jnp.where(lane < N_EXPERTS, logits, neg)
    m1 = jnp.max(logits, axis=-1, keepdims=True)
    i1 = jnp.min(jnp.where(logits == m1, lane, LANES), axis=-1, keepdims=True)
    rest = jnp.where(lane == i1, neg, logits)
    m2 = jnp.max(rest, axis=-1, keepdims=True)
    i2 = jnp.min(jnp.where(rest == m2, lane, LANES), axis=-1, keepdims=True)
    e2 = jnp.exp(m2 - m1)
    w1 = 1.0 / (1.0 + e2)
    w2 = e2 / (1.0 + e2)
    cw_out[...] = jnp.where(lane == i1, w1, 0.0) + jnp.where(lane == i2, w2, 0.0)

    sel = jnp.where((lane == i1) | (lane == i2), 1.0, 0.0)
    ti = lax.broadcasted_iota(jnp.int32, (tm, tm), 0)
    si = lax.broadcasted_iota(jnp.int32, (tm, tm), 1)
    before = jnp.where(ti > si, 1.0, 0.0).astype(BF16)
    window_start = (i % (MOE_TW // tm)) == 0
    carry = jnp.where(window_start, 0.0, carry_scr[...])
    rank = jnp.dot(before, sel.astype(BF16), preferred_element_type=F32) + carry
    dest = jnp.where(sel > 0.0, rank, -1.0)
    dest_out[...] = dest
    dest_t_out[...] = dest.T
    total = carry + jnp.sum(sel, axis=0, keepdims=True)
    carry_scr[...] = total
    cnt_out[0] = jnp.broadcast_to(total, (8, LANES))


def _moe_kernel(nf, cnt_ref, h_ref, dest_ref, dest_t_ref, cw_ref, wg_ref, wu_ref, wd_ref, o_ref,
                xg_scr, yc_scr):
    w = pl.program_id(0)
    e = pl.program_id(1)
    f = pl.program_id(2)
    tw, ts = MOE_TW, MOE_TS
    n = cnt_ref[w * N_EXPERTS + e]
    n_sub = (n + ts - 1) // ts

    @pl.when((e == 0) & (f == 0))
    def _():
        o_ref[...] = jnp.zeros_like(o_ref)

    dest_row = dest_t_ref[pl.ds(e, 1), :]
    lane = lax.broadcasted_iota(jnp.int32, (tw, LANES), 1)
    pick = lane == e
    dest_col = jnp.sum(jnp.where(pick, dest_ref[...], 0.0), axis=-1, keepdims=True)
    cw_col = jnp.sum(jnp.where(pick, cw_ref[...], 0.0), axis=-1, keepdims=True)
    slot_sub = lax.broadcasted_iota(jnp.int32, (ts, tw), 0).astype(F32)
    slot_lane = lax.broadcasted_iota(jnp.int32, (tw, ts), 1).astype(F32)

    def sub_tile(s, carry):
        base = (s * ts).astype(F32)

        @pl.when(f == 0)
        def _():
            gather = jnp.where(dest_row == slot_sub + base, 1.0, 0.0).astype(BF16)
            xg_scr[s] = jnp.dot(gather, h_ref[...], preferred_element_type=F32).astype(BF16)

        xs = xg_scr[s]
        gt = jnp.dot(xs, wg_ref[0], preferred_element_type=F32)
        up = jnp.dot(xs, wu_ref[0], preferred_element_type=F32)
        act = (gt * _sigmoid(gt) * up).astype(BF16)
        y = jnp.dot(act, wd_ref[0], preferred_element_type=F32)

        @pl.when(f == 0)
        def _():
            yc_scr[s] = y

        @pl.when(f > 0)
        def _():
            yc_scr[s] += y

        @pl.when(f == nf - 1)
        def _():
            spread = jnp.where(dest_col == slot_lane + base, 1.0, 0.0).astype(BF16)
            o_ref[...] += cw_col * jnp.dot(spread, yc_scr[s].astype(BF16),
                                           preferred_element_type=F32)
        return carry

    lax.fori_loop(0, n_sub, sub_tile, 0)


def _residual_kernel(x_ref, y_ref, mod_ref, o_ref):
    o_ref[...] = x_ref[...] + mod_ref[0][5:6] * y_ref[...]


def _moe(x, mod, norm_g, w_router, b_router, w_gate_up, w_down):
    tm, tf, tw, ts = FFN_TM, FFN_TF, MOE_TW, MOE_TS
    nf = D_FF // tf
    nb = S // tm
    nw = T // tw
    wgu = w_gate_up.astype(BF16)
    wr = jnp.zeros((D, LANES), F32).at[:, :N_EXPERTS].set(w_router)
    br = jnp.zeros((1, LANES), F32).at[0, :N_EXPERTS].set(b_router)
    tile = lambda width: pl.BlockSpec((tm, width), lambda i: (i, 0))
    h, cw, dest, dest_t, cnt = pl.pallas_call(
        _route_kernel,
        grid=(T // tm,),
        in_specs=[tile(D),
                  pl.BlockSpec((1, 6, D), lambda i: (i // nb, 0, 0)),
                  pl.BlockSpec((1, D), lambda i: (0, 0)),
                  pl.BlockSpec((D, LANES), lambda i: (0, 0)),
                  pl.BlockSpec((1, LANES), lambda i: (0, 0))],
        out_specs=[tile(D), tile(LANES), tile(LANES),
                   pl.BlockSpec((LANES, tm), lambda i: (0, i)),
                   pl.BlockSpec((1, 8, LANES), lambda i: (i // (tw // tm), 0, 0))],
        out_shape=[jax.ShapeDtypeStruct((T, D), BF16), jax.ShapeDtypeStruct((T, LANES), F32),
                   jax.ShapeDtypeStruct((T, LANES), F32), jax.ShapeDtypeStruct((LANES, T), F32),
                   jax.ShapeDtypeStruct((nw, 8, LANES), F32)],
        scratch_shapes=[pltpu.VMEM((1, LANES), F32)],
        compiler_params=_cparams("arbitrary"),
        name="moe_route",
    )(x, mod, norm_g.reshape(1, D), wr, br)
    counts = cnt[:, 0, :N_EXPERTS].astype(jnp.int32).reshape(nw * N_EXPERTS)
    n_slots = -(-tw // ts)
    y = pl.pallas_call(
        functools.partial(_moe_kernel, nf),
        grid_spec=pltpu.PrefetchScalarGridSpec(
            num_scalar_prefetch=1,
            grid=(nw, N_EXPERTS, nf),
            in_specs=[pl.BlockSpec((tw, D), lambda w, e, f, c: (w, 0)),
                      pl.BlockSpec((tw, LANES), lambda w, e, f, c: (w, 0)),
                      pl.BlockSpec((LANES, tw), lambda w, e, f, c: (0, w)),
                      pl.BlockSpec((tw, LANES), lambda w, e, f, c: (w, 0)),
                      pl.BlockSpec((1, D, tf), lambda w, e, f, c: (e, 0, f)),
                      pl.BlockSpec((1, D, tf), lambda w, e, f, c: (e, 0, nf + f)),
                      pl.BlockSpec((1, tf, D), lambda w, e, f, c: (e, f, 0))],
            out_specs=pl.BlockSpec((tw, D), lambda w, e, f, c: (w, 0)),
            scratch_shapes=[pltpu.VMEM((n_slots, ts, D), BF16), pltpu.VMEM((n_slots, ts, D), F32)]),
        out_shape=jax.ShapeDtypeStruct((T, D), F32),
        compiler_params=_cparams("parallel", "arbitrary", "arbitrary"),
        name="moe_experts",
    )(counts, h, dest, dest_t, cw, wgu, wgu, w_down.astype(BF16))
    return pl.pallas_call(
        _residual_kernel,
        grid=(T // tm,),
        in_specs=[tile(D), tile(D), pl.BlockSpec((1, 6, D), lambda i: (i // nb, 0, 0))],
        out_specs=tile(D),
        out_shape=jax.ShapeDtypeStruct((T, D), F32),
        compiler_params=_cparams("parallel"),
        name="moe_residual",
    )(x, y, mod)


def _mla_proj_kernel(x_ref, mod_ref, ng_ref, pos_ref, invf_ref, wd_ref, qg_ref, kvg_ref,
                     wuq_ref, wukv_ref, q_out, k_out, v_out):
    mod = mod_ref[0]
    h = _modnorm(x_ref[...], ng_ref[...], mod[0:1], mod[1:2])
    down = _dot(h, wd_ref[...])
    cq = down[:, :MLA_Q_LORA]
    ckv = down[:, MLA_Q_LORA:MLA_Q_LORA + MLA_KV_LORA]
    kr = down[:, MLA_Q_LORA + MLA_KV_LORA:]
    cq = cq * lax.rsqrt(jnp.mean(cq * cq, axis=-1, keepdims=True) + NORM_EPS) * qg_ref[...]
    ckv = ckv * lax.rsqrt(jnp.mean(ckv * ckv, axis=-1, keepdims=True) + NORM_EPS) * kvg_ref[...]

    ang = pos_ref[...] * invf_ref[...]
    lane = lax.broadcasted_iota(jnp.int32, ang.shape, 1)
    is_rope = (lane >= MLA_NOPE) & (lane < MLA_NOPE + MLA_ROPE)
    cos_r = jnp.where(is_rope, jnp.cos(ang), 0.0)
    sin_r = jnp.where(is_rope, jnp.sin(ang), 0.0)
    cos_q = jnp.where(lane < MLA_NOPE, 1.0, cos_r)
    shift = LANES - MLA_ROPE

    k_rope = kr * cos_r + pltpu.roll(kr, shift, axis=1) * sin_r

    qf = _dot(cq, wuq_ref[...])
    kv = _dot(ckv, wukv_ref[...])
    for hd in range(MLA_HEADS):
        sl = slice(hd * LANES, (hd + 1) * LANES)
        qs = qf[:, sl]
        q_out[:, sl] = ((qs * cos_q + pltpu.roll(qs, shift, axis=1) * sin_r) * MLA_SCALE).astype(BF16)
        k_out[:, sl] = (kv[:, sl] + k_rope).astype(BF16)
    _store_transposed(v_out, kv[:, MLA_HEADS * LANES:])


def _rot_half_cols(w):
    half = MLA_ROPE // 2
    return jnp.concatenate([-w[..., half:], w[..., :half]], axis=-1)


def _mla_proj(x, mod, norm_g, positions, w_down, q_norm_g, kv_norm_g, w_uq, w_ukv):
    tm = PROJ_TM
    nb = S // tm
    half = MLA_ROPE // 2
    wr = w_down[:, MLA_Q_LORA + MLA_KV_LORA:]
    wd = jnp.concatenate([w_down[:, :MLA_Q_LORA + MLA_KV_LORA],
                          jnp.zeros((D, MLA_NOPE), F32), wr, _rot_half_cols(wr)], axis=1).astype(BF16)
    wq = w_uq.reshape(MLA_Q_LORA, MLA_HEADS, MLA_NOPE + MLA_ROPE)
    wq_r = wq[..., MLA_NOPE:]
    wuq = jnp.concatenate([wq[..., :MLA_NOPE], wq_r, _rot_half_cols(wq_r)], axis=-1)
    wuq = wuq.reshape(MLA_Q_LORA, MLA_HEADS * LANES).astype(BF16)
    wkv = w_ukv.reshape(MLA_KV_LORA, MLA_HEADS, 2 * MLA_NOPE)
    wk = jnp.concatenate([wkv[..., :MLA_NOPE], jnp.zeros_like(wkv[..., :MLA_NOPE])], axis=-1)
    wukv = jnp.concatenate([wk.reshape(MLA_KV_LORA, MLA_HEADS * LANES),
                            wkv[..., MLA_NOPE:].reshape(MLA_KV_LORA, D)], axis=1).astype(BF16)
    inv_freq = ROPE_BASE ** (-jnp.arange(half, dtype=F32) / half)
    lane = jnp.arange(LANES)
    invf = jnp.where((lane >= MLA_NOPE) & (lane < MLA_NOPE + MLA_ROPE),
                     inv_freq[(lane - MLA_NOPE) % half], 0.0).reshape(1, LANES).astype(F32)
    pos = jnp.broadcast_to(positions.reshape(T, 1).astype(F32), (T, LANES))
    full = lambda a: pl.BlockSpec(a.shape, lambda i: (0,) * a.ndim)
    ops = [x, mod, norm_g.reshape(1, D), pos, invf, wd, q_norm_g.reshape(1, -1),
           kv_norm_g.reshape(1, -1), wuq, wukv]
    specs = [pl.BlockSpec((tm, D), lambda i: (i, 0)),
             pl.BlockSpec((1, 6, D), lambda i: (i // nb, 0, 0)),
             full(ops[2]),
             pl.BlockSpec((tm, LANES), lambda i: (i, 0))] + [full(a) for a in ops[4:]]
    qw = MLA_HEADS * LANES
    return pl.pallas_call(
        _mla_proj_kernel,
        grid=(T // tm,),
        in_specs=specs,
        out_specs=[pl.BlockSpec((tm, qw), lambda i: (i, 0)),
                   pl.BlockSpec((tm, qw), lambda i: (i, 0)),
                   _vt_spec(tm)],
        out_shape=[jax.ShapeDtypeStruct((T, qw), BF16), jax.ShapeDtypeStruct((T, qw), BF16),
                   _vt_shape()],
        compiler_params=_cparams("parallel"),
        name="mla_proj",
    )(*ops)


def _fox_proj_kernel(x_ref, mod_ref, ng_ref, w_ref, wf_ref, bf_ref, qg_ref, kg_ref,
                     q_out, k_out, v_out, og_out, fc_out, fr_out, carry_scr):
    i = pl.program_id(0)
    tm = x_ref.shape[0]
    mod = mod_ref[0]
    h = _modnorm(x_ref[...], ng_ref[...], mod[0:1], mod[1:2]).astype(BF16)
    q = jnp.dot(h, w_ref[:, 0:D], preferred_element_type=F32)
    k = jnp.dot(h, w_ref[:, D:2 * D], preferred_element_type=F32)
    v = jnp.dot(h, w_ref[:, 2 * D:3 * D], preferred_element_type=F32)
    og = jnp.dot(h, w_ref[:, 3 * D:4 * D], preferred_element_type=F32)
    inv_n = 1.0 / FOX_HEAD
    q = q * lax.rsqrt(_group_sum64(q * q) * inv_n + NORM_EPS) * qg_ref[...] * FOX_SCALE
    k = k * lax.rsqrt(_group_sum64(k * k) * inv_n + NORM_EPS) * kg_ref[...]
    q_out[...] = q.astype(BF16)
    k_out[...] = k.astype(BF16)
    _store_transposed(v_out, v)
    og_out[...] = _sigmoid(og).astype(BF16)

    z = jnp.dot(h, wf_ref[...], preferred_element_type=F32) + bf_ref[...]
    lane = lax.broadcasted_iota(jnp.int32, z.shape, 1)
    log_f = jnp.minimum(z, 0.0) - jnp.log(1.0 + jnp.exp(-jnp.abs(z)))
    log_f = jnp.where(lane < FOX_HEADS, log_f, 0.0)
    ti = lax.broadcasted_iota(jnp.int32, (tm, tm), 0)
    si = lax.broadcasted_iota(jnp.int32, (tm, tm), 1)
    tri = jnp.where(ti >= si, 1.0, 0.0).astype(F32)
    seq_start = (i % (S // tm)) == 0
    carry = jnp.where(seq_start, 0.0, carry_scr[...])
    cum = jnp.dot(tri, log_f, precision=HIGHEST, preferred_element_type=F32) + carry
    carry_scr[...] = cum[tm - 1:tm, :]
    fc_out[...] = cum
    fr_out[0] = cum.T


def _fox_proj(x, mod, norm_g, w_in, b_f, q_norm_g, k_norm_g):
    tm = PROJ_TM
    nb = S // tm
    w_main = jnp.concatenate([w_in[:, :3 * D], w_in[:, 3 * D + FOX_HEADS:]], axis=1).astype(BF16)
    w_f = jnp.zeros((D, LANES), F32).at[:, :FOX_HEADS].set(w_in[:, 3 * D:3 * D + FOX_HEADS]).astype(BF16)
    bf = jnp.zeros((1, LANES), F32).at[0, :FOX_HEADS].set(b_f)
    qg = jnp.tile(q_norm_g, FOX_HEADS).reshape(1, D)
    kg = jnp.tile(k_norm_g, FOX_HEADS).reshape(1, D)
    full = lambda a: pl.BlockSpec(a.shape, lambda i: (0,) * a.ndim)
    tile = pl.BlockSpec((tm, D), lambda i: (i, 0))
    ops = [x, mod, norm_g.reshape(1, D), w_main, w_f, bf, qg, kg]
    specs = [tile, pl.BlockSpec((1, 6, D), lambda i: (i // nb, 0, 0))] + [full(a) for a in ops[2:]]
    return pl.pallas_call(
        _fox_proj_kernel,
        grid=(T // tm,),
        in_specs=specs,
        out_specs=[tile, tile, _vt_spec(tm), tile,
                   pl.BlockSpec((tm, LANES), lambda i: (i, 0)),
                   pl.BlockSpec((1, LANES, tm), lambda i: (i // nb, 0, i % nb))],
        out_shape=[jax.ShapeDtypeStruct((T, D), BF16)] * 2 + [_vt_shape(), jax.ShapeDtypeStruct((T, D), BF16),
                   jax.ShapeDtypeStruct((T, LANES), F32), jax.ShapeDtypeStruct((B, LANES, S), F32)],
        scratch_shapes=[pltpu.VMEM((1, LANES), F32)],
        compiler_params=_cparams("arbitrary"),
        name="fox_proj",
    )(*ops)


def _attn_kernel(fox, *refs):
    if fox:
        q_ref, k_ref, vt_ref, og_ref, fc_ref, fr_ref, o_ref, st_scr, p_scr, acc_scr, fs_scr = refs
    else:
        q_ref, k_ref, vt_ref, o_ref, st_scr, p_scr, acc_scr = refs
    tq, tk = ATT_TQ, ATT_TK
    nk = S // tk
    w2 = 2 * tq
    p = pl.program_id(1)
    qi = pl.program_id(2)

    if fox:
        @pl.when(qi == 0)
        def _():
            fcv = fc_ref[...]
            lane_s = lax.broadcasted_iota(jnp.int32, fcv.shape, 1)
            for j in range(2):
                colv = jnp.sum(jnp.where(lane_s == 2 * p + j, fcv, 0.0), axis=-1, keepdims=True)
                full = jnp.broadcast_to(colv, fcv.shape)
                for kb in range(nk):
                    fs_scr[j * nk + kb] = full[kb * tk:(kb + 1) * tk]

    qb = q_ref[...]
    lane = lax.broadcasted_iota(jnp.int32, qb.shape, 1)
    per_head = FOX_HEAD if fox else LANES
    zero = jnp.zeros_like(qb)
    q_bd = jnp.concatenate([jnp.where(lane // per_head == j, qb, zero) for j in range(2)], axis=0)
    if fox:
        f_t = jnp.concatenate([fr_ref[0, pl.ds(2 * p + j, 1), :] for j in range(2)], axis=1)

    srow = lax.broadcasted_iota(jnp.int32, (tk, w2), 0)
    tcol = lax.broadcasted_iota(jnp.int32, (tk, w2), 1) % tq
    diag_mask = (srow <= tcol) if fox else ((srow // 64) <= (tcol // 64))

    def scores(kb, mx, mask):
        start = pl.multiple_of(kb * tk, tk)
        st = lax.dot_general(k_ref[pl.ds(start, tk), :], q_bd, (((1,), (1,)), ((), ())),
                             preferred_element_type=F32)
        if fox:
            reps = tq // LANES
            f_s = jnp.concatenate([fs_scr[kb]] * reps + [fs_scr[nk + kb]] * reps, axis=1)
            st = st + (f_t - f_s)
        if mask is not None:
            st = jnp.where(mask, st, -jnp.inf)
        st_scr[kb] = st
        return jnp.maximum(mx, jnp.max(st, axis=0, keepdims=True))

    def scores_pair(i, mx):
        return scores(2 * i + 1, scores(2 * i, mx, None), None)

    mx = jnp.full((1, w2), -jnp.inf, F32)
    mx = lax.fori_loop(0, qi // 2, scores_pair, mx)
    mx = lax.fori_loop(0, qi % 2, lambda _, m: scores(qi - 1, m, None), mx)
    mx = scores(qi, mx, diag_mask)

    def probs(kb, l):
        pt = jnp.exp(st_scr[kb] - mx)
        p_scr[kb] = pt.astype(BF16)
        return l + jnp.sum(pt, axis=0, keepdims=True)

    l = lax.fori_loop(0, qi + 1, probs, jnp.zeros((1, w2), F32))

    acc_scr[...] = jnp.zeros_like(acc_scr)

    def values(kb):
        acc_scr[...] += jnp.dot(vt_ref[0, kb], p_scr[kb], preferred_element_type=F32)

    def values_pair(i, carry):
        values(2 * i)
        values(2 * i + 1)
        return carry

    lax.fori_loop(0, (qi + 1) // 2, values_pair, 0)

    @pl.when(qi % 2 == 0)
    def _():
        values(qi)

    acc = acc_scr[...] / l
    sub = lax.broadcasted_iota(jnp.int32, (LANES, tq), 0)
    o_t = jnp.where(sub // 64 == 0, acc[:, :tq], acc[:, tq:])
    o = o_t.T
    if fox:
        o = o * og_ref[...].astype(F32)
    o_ref[...] = o.astype(BF16)


def _attention(q, k, vt, fox_extras=None):
    fox = fox_extras is not None
    tq, tk = ATT_TQ, ATT_TK
    nq = S // tq
    nk = S // tk
    qw = LANES if fox else 2 * LANES
    ops = [q, k, vt]
    specs = [pl.BlockSpec((tq, qw), lambda b, p, i: (b * nq + i, p)),
             pl.BlockSpec((S, qw), lambda b, p, i: (b, p)),
             pl.BlockSpec((1, nk, LANES, tk), lambda b, p, i: (b, 0, p, 0))]
    scratch = [pltpu.VMEM((nk, tk, 2 * tq), F32), pltpu.VMEM((nk, tk, 2 * tq), BF16),
               pltpu.VMEM((LANES, 2 * tq), F32)]
    if fox:
        og, fc, fr = fox_extras
        ops += [og, fc, fr]
        specs += [pl.BlockSpec((tq, LANES), lambda b, p, i: (b * nq + i, p)),
                  pl.BlockSpec((S, LANES), lambda b, p, i: (b, 0)),
                  pl.BlockSpec((1, FOX_HEADS, tq), lambda b, p, i: (b, 0, i))]
        scratch.append(pltpu.VMEM((2 * nk, tk, LANES), F32))
    return pl.pallas_call(
        functools.partial(_attn_kernel, fox),
        grid=(B, 8, nq),
        in_specs=specs,
        out_specs=pl.BlockSpec((tq, LANES), lambda b, p, i: (b * nq + i, p)),
        out_shape=jax.ShapeDtypeStruct((T, D), BF16),
        scratch_shapes=scratch,
        compiler_params=_cparams("parallel", "parallel", "arbitrary"),
        name="fox_attn" if fox else "mla_attn",
    )(*ops)


def _final_norm_kernel(x_ref, g_ref, o_ref):
    x = x_ref[...]
    o_ref[...] = x * lax.rsqrt(jnp.mean(x * x, axis=-1, keepdims=True) + NORM_EPS) * g_ref[...]


def _final_norm(x, g):
    tm = 512
    return pl.pallas_call(
        _final_norm_kernel,
        grid=(T // tm,),
        in_specs=[pl.BlockSpec((tm, D), lambda i: (i, 0)), pl.BlockSpec((1, D), lambda i: (0, 0))],
        out_specs=pl.BlockSpec((tm, D), lambda i: (i, 0)),
        out_shape=jax.ShapeDtypeStruct((T, D), F32),
        compiler_params=_cparams("parallel"),
        name="final_norm",
    )(x, g.reshape(1, D))


def kernel(x, c, positions, ada_w, ada_b, norm_mix_g, norm_ffn_g, final_norm_g, rw_mu, rw_w_rkv, rw_w_o, rw_w0, rw_w1, rw_w2, rw_a0, rw_a1, rw_a2, rw_g1, rw_g2, rw_k_k, rw_k_a, rw_r_k, rw_lnx_g, rw_lnx_b, rw_v0, rw_v1, rw_v2, mla_w_down, mla_q_norm_g, mla_kv_norm_g, mla_w_uq, mla_w_ukv, mla_w_o, fox_w_in, fox_b_f, fox_q_norm_g, fox_k_norm_g, fox_w_o, ffn_w_gate_up, ffn_w_down, moe_w_router, moe_b_router, moe_w_gate_up, moe_w_down):
    xf = x.reshape(T, D)
    mods = _ada_mods(c, ada_w, ada_b)
    v_first = None
    for i in range(DEPTH):
        mod = mods[i]
        kind, j = i % 3, i // 3
        if kind == 0:
            p = dict(mu=rw_mu[j], w_rkv=rw_w_rkv[j], w0=rw_w0[j], w1=rw_w1[j], w2=rw_w2[j],
                     a0=rw_a0[j], a1=rw_a1[j], a2=rw_a2[j], g1=rw_g1[j], g2=rw_g2[j],
                     k_k=rw_k_k[j], k_a=rw_k_a[j])
            if j > 0:
                p.update(v0=rw_v0[j - 1], v1=rw_v1[j - 1], v2=rw_v2[j - 1])
            r, lw, k, v, kk, a, g = _rwkv_proj(xf, mod, norm_mix_g[i], p, v_first if j > 0 else None)
            if j == 0:
                v_first = v
            y = _rwkv_scan(r, lw, k, v, kk, a, g, rw_r_k[j], rw_lnx_g[j], rw_lnx_b[j])
            xf = _proj_res(y, rw_w_o[j], xf, mod, 2)
        elif kind == 1:
            q, k, v = _mla_proj(xf, mod, norm_mix_g[i], positions, mla_w_down[j], mla_q_norm_g[j],
                                mla_kv_norm_g[j], mla_w_uq[j], mla_w_ukv[j])
            o = _attention(q, k, v)
            xf = _proj_res(o, mla_w_o[j], xf, mod, 2)
        else:
            q, k, v, og, fc, fr = _fox_proj(xf, mod, norm_mix_g[i], fox_w_in[j], fox_b_f[j],
                                            fox_q_norm_g[j], fox_k_norm_g[j])
            o = _attention(q, k, v, (og, fc, fr))
            xf = _proj_res(o, fox_w_o[j], xf, mod, 2)
        if i % 2 == 0:
            xf = _ffn_dense(xf, mod, norm_ffn_g[i], ffn_w_gate_up[i // 2], ffn_w_down[i // 2])
        else:
            xf = _moe(xf, mod, norm_ffn_g[i], moe_w_router[i // 2], moe_b_router[i // 2],
                      moe_w_gate_up[i // 2], moe_w_down[i // 2])
    return _final_norm(xf, final_norm_g).reshape(B, S, D)
```

```python
import functools
import math

import jax
import jax.numpy as jnp
from jax import lax
from jax.experimental import pallas as pl
from jax.experimental.pallas import tpu as pltpu

F32 = jnp.float32
BF16 = jnp.bfloat16
HIGHEST = lax.Precision.HIGHEST

D = 1024
B = 8
S = 2048
T = B * S
DEPTH = 4
NORM_EPS = 1e-6

RW_HEAD = 64
GN_EPS = 64e-5
EXP_NEG_HALF = math.exp(-0.5)

MLA_HEADS = 16
MLA_NOPE = 64
MLA_ROPE = 32
MLA_Q_LORA = 768
MLA_KV_LORA = 256
MLA_SCALE = (MLA_NOPE + MLA_ROPE) ** -0.5
ROPE_BASE = 10000.0

FOX_HEADS = 16
FOX_HEAD = 64
FOX_SCALE = FOX_HEAD ** -0.5

D_FF = 2816
N_EXPERTS = 8

LANES = 128
VMEM_LIMIT = 56 * 1024 * 1024

RW_CHUNK = 64
RW_GROUP = 4
RW_PAR = 4
ATT_TQ = 256
ATT_TK = 256
FFN_TM = 512
FFN_TF = 1408
PROJ_TM = 256
MOE_TW = 1024
MOE_TS = 384


def _cparams(*sem):
    return pltpu.CompilerParams(dimension_semantics=sem, vmem_limit_bytes=VMEM_LIMIT)


def _dot(a, b):
    return jnp.dot(a.astype(BF16), b.astype(BF16), preferred_element_type=F32)


def _dot_nt(a, b):
    return lax.dot_general(a.astype(BF16), b.astype(BF16), (((1,), (1,)), ((), ())),
                           preferred_element_type=F32)


def _dot_tn(a, b):
    return lax.dot_general(a.astype(BF16), b.astype(BF16), (((0,), (0,)), ((), ())),
                           preferred_element_type=F32)


def _sigmoid(z):
    return 1.0 / (1.0 + jnp.exp(-z))


def _modnorm(x, g, shift, scale):
    ms = jnp.mean(x * x, axis=-1, keepdims=True)
    y = x * lax.rsqrt(ms + NORM_EPS) * g
    return y * (1.0 + scale) + shift


def _group_sum64(x):
    gi = lax.broadcasted_iota(jnp.int32, (LANES, LANES), 0) // 64
    gj = lax.broadcasted_iota(jnp.int32, (LANES, LANES), 1) // 64
    ones = jnp.where(gi == gj, 1.0, 0.0).astype(BF16)
    hi = x.astype(BF16)
    lo = (x - hi.astype(F32)).astype(BF16)
    outs = []
    for s in range(x.shape[1] // LANES):
        sl = slice(s * LANES, (s + 1) * LANES)
        outs.append(jnp.dot(hi[:, sl], ones, preferred_element_type=F32)
                    + jnp.dot(lo[:, sl], ones, preferred_element_type=F32))
    return outs[0] if len(outs) == 1 else jnp.concatenate(outs, axis=1)


def _store_transposed(vt_ref, v):
    vt = v.T
    for s in range(v.shape[0] // ATT_TK):
        vt_ref[0, s] = vt[:, s * ATT_TK:(s + 1) * ATT_TK].astype(BF16)


def _vt_spec(tm):
    nb = S // tm
    return pl.BlockSpec((1, tm // ATT_TK, D, ATT_TK), lambda i: (i // nb, i % nb, 0, 0))


def _vt_shape():
    return jax.ShapeDtypeStruct((B, S // ATT_TK, D, ATT_TK), BF16)


def _ada_kernel(c_ref, w_ref, b_ref, o_ref):
    c = c_ref[...]
    cond = c * _sigmoid(c)
    o_ref[0] = _dot(cond, w_ref[0]) + b_ref[0]


def _ada_mods(c, ada_w, ada_b):
    tn = 1536
    out = pl.pallas_call(
        _ada_kernel,
        grid=(DEPTH, 6 * D // tn),
        in_specs=[pl.BlockSpec((B, D), lambda l, j: (0, 0)),
                  pl.BlockSpec((1, D, tn), lambda l, j: (l, 0, j)),
                  pl.BlockSpec((1, 1, tn), lambda l, j: (l, 0, j))],
        out_specs=pl.BlockSpec((1, B, tn), lambda l, j: (l, 0, j)),
        out_shape=jax.ShapeDtypeStruct((DEPTH, B, 6 * D), F32),
        compiler_params=_cparams("parallel", "parallel"),
        name="ada_mods",
    )(c, ada_w, ada_b.reshape(DEPTH, 1, 6 * D))
    return out.reshape(DEPTH, B, 6, D)


def _rwkv_proj_kernel(has_vres, *refs):
    if has_vres:
        (x_ref, xp_ref, mod_ref, ng_ref, mu_ref, wr_ref, wk_ref, wv_ref, w0_ref, w1_ref, w2_ref,
         a0_ref, a1_ref, a2_ref, g1_ref, g2_ref, kk_ref, ka_ref, v0_ref, v1_ref, v2_ref, vf_ref,
         r_out, lw_out, k_out, v_out, kk_out, a_out, g_out) = refs
    else:
        (x_ref, xp_ref, mod_ref, ng_ref, mu_ref, wr_ref, wk_ref, wv_ref, w0_ref, w1_ref, w2_ref,
         a0_ref, a1_ref, a2_ref, g1_ref, g2_ref, kk_ref, ka_ref,
         r_out, lw_out, k_out, v_out, kk_out, a_out, g_out) = refs
    i = pl.program_id(0)
    mod = mod_ref[0]
    shift, scale = mod[0:1], mod[1:2]
    g = ng_ref[...]
    h = _modnorm(x_ref[...], g, shift, scale)
    hp = _modnorm(xp_ref[...], g, shift, scale)
    seq_start = (i % (S // PROJ_TM)) == 0
    prev_row = jnp.where(seq_start, 0.0, hp[7:8, :])
    row = lax.broadcasted_iota(jnp.int32, h.shape, 0)
    prev = jnp.where(row == 0, prev_row, pltpu.roll(h, 1, axis=0))
    delta = prev - h
    mu = mu_ref[...]
    xr = h + delta * mu[0:1]
    xw = h + delta * mu[1:2]
    xk = h + delta * mu[2:3]
    xv = h + delta * mu[3:4]
    xa = h + delta * mu[4:5]
    xg = h + delta * mu[5:6]
    r = _dot(xr, wr_ref[...])
    k = _dot(xk, wk_ref[...])
    v = _dot(xv, wv_ref[...])
    w_raw = w0_ref[...] + _dot(jnp.tanh(_dot(xw, w1_ref[...])), w2_ref[...])
    lw_out[...] = -_sigmoid(w_raw) * EXP_NEG_HALF
    if has_vres:
        mix = _sigmoid(v0_ref[...] + _dot(_dot(xv, v1_ref[...]), v2_ref[...]))
        v = v + (vf_ref[...] - v) * mix
    a = _sigmoid(a0_ref[...] + _dot(_dot(xa, a1_ref[...]), a2_ref[...]))
    g_out[...] = _dot(_sigmoid(_dot(xg, g1_ref[...])), g2_ref[...])
    kk = k * kk_ref[...]
    norm = jnp.sqrt(_group_sum64(kk * kk))
    kk_out[...] = kk / jnp.maximum(norm, 1e-12)
    k_out[...] = k * (1.0 + (a - 1.0) * ka_ref[...])
    r_out[...] = r
    v_out[...] = v
    a_out[...] = a


def _rwkv_proj(x, mod, norm_g, p, v_first):
    has_vres = v_first is not None
    tm = PROJ_TM
    nb = S // tm
    row = lambda a: a.reshape(1, -1)
    full = lambda a: pl.BlockSpec(a.shape, lambda i: (0,) * a.ndim)
    tile = pl.BlockSpec((tm, D), lambda i: (i, 0))
    ops = [x, x, mod, row(norm_g), p["mu"],
           p["w_rkv"][0].astype(BF16), p["w_rkv"][1].astype(BF16), p["w_rkv"][2].astype(BF16),
           row(p["w0"]), p["w1"].astype(BF16), p["w2"].astype(BF16),
           row(p["a0"]), p["a1"].astype(BF16), p["a2"].astype(BF16),
           p["g1"].astype(BF16), p["g2"].astype(BF16), row(p["k_k"]), row(p["k_a"])]
    specs = [tile,
             pl.BlockSpec((8, D), lambda i: (jnp.maximum(i * (tm // 8) - 1, 0), 0)),
             pl.BlockSpec((1, 6, D), lambda i: (i // nb, 0, 0))]
    specs += [full(a) for a in ops[3:]]
    if has_vres:
        extra = [row(p["v0"]), p["v1"].astype(BF16), p["v2"].astype(BF16)]
        ops += extra + [v_first]
        specs += [full(a) for a in extra] + [tile]
    outs = pl.pallas_call(
        functools.partial(_rwkv_proj_kernel, has_vres),
        grid=(T // tm,),
        in_specs=specs,
        out_specs=[tile] * 7,
        out_shape=[jax.ShapeDtypeStruct((T, D), F32)] * 7,
        compiler_params=_cparams("parallel"),
        name="rwkv_proj",
    )(*ops)
    return outs


def _rwkv_scan_kernel(r_ref, lw_ref, k_ref, v_ref, kk_ref, a_ref, g_ref, rk_ref, lg_ref, lb_ref,
                      o_ref, s_ref):
    c = pl.program_id(2)
    W = RW_GROUP * RW_HEAD

    @pl.when(c == 0)
    def _():
        s_ref[...] = jnp.zeros_like(s_ref)

    slabs = [slice(p * W, (p + 1) * W) for p in range(RW_PAR)]
    cols = lambda ref: [ref[:, sl] for sl in slabs]
    out, s_new = _rwkv_blocks(cols(r_ref), cols(lw_ref), cols(k_ref), cols(v_ref), cols(kk_ref),
                              cols(a_ref), cols(g_ref), cols(rk_ref), cols(lg_ref), cols(lb_ref),
                              [s_ref[p] for p in range(RW_PAR)])
    for p, sl in enumerate(slabs):
        o_ref[:, sl] = out[p]
        s_ref[p] = s_new[p]


def _each(fn, *cols):
    return [fn(*args) for args in zip(*cols)]


def _rwkv_blocks(r, lw, k, v, kk, a, g, r_k, lnx_g, lnx_b, s_old):
    C = RW_CHUNK
    W = RW_GROUP * RW_HEAD

    mul = lambda x, y: x * y
    ti = lax.broadcasted_iota(jnp.int32, (C, C), 0)
    si = lax.broadcasted_iota(jnp.int32, (C, C), 1)
    tri = jnp.where(ti >= si, 1.0, 0.0).astype(F32)
    cl = _each(lambda x: jnp.dot(tri, x, precision=HIGHEST, preferred_element_type=F32), lw)
    cl_end = _each(lambda x: x[C - 1:C, :], cl)
    w_t = _each(jnp.exp, cl)
    w_prev = _each(lambda x, y: jnp.exp(x - y), cl, lw)
    w_inv = _each(lambda x: jnp.exp(-x), cl)
    w_rem = _each(lambda x, y: jnp.exp(x - y), cl_end, cl)
    bv = _each(mul, kk, a)
    r_hat = _each(mul, r, w_t)
    a_hat = _each(lambda x, y: -x * y, kk, w_prev)
    b_hat = _each(mul, bv, w_inv)
    k_hat = _each(mul, k, w_inv)
    b_til = _each(mul, bv, w_rem)
    k_til = _each(mul, k, w_rem)

    lane_head = lax.broadcasted_iota(jnp.int32, (C, W), 1) // RW_HEAD

    def stack(m):
        mb = m.astype(BF16)
        zero = jnp.zeros_like(mb)
        return jnp.concatenate([jnp.where(lane_head == hd, mb, zero) for hd in range(RW_GROUP)],
                               axis=0)

    def fold(m):
        return m[0:C] + m[C:2 * C] + m[2 * C:3 * C] + m[3 * C:4 * C]

    cat0 = lambda x, y: jnp.concatenate([x, y], axis=0)
    gram = _each(lambda ah, rh, bh, kh: _dot_nt(cat0(stack(ah), stack(rh)), cat0(stack(bh), stack(kh))),
                 a_hat, r_hat, b_hat, k_hat)
    n = RW_GROUP * C
    ri = lax.broadcasted_iota(jnp.int32, (n, n), 0)
    ci = lax.broadcasted_iota(jnp.int32, (n, n), 1)
    strict = (ri % C) > (ci % C)
    incl = (ri % C) >= (ci % C)
    same_head = (ri // RW_HEAD) == (ci // RW_HEAD)
    eye = jnp.where(ri == ci, 1.0, 0.0)
    l_ab = _each(lambda x: jnp.where(strict, x[:n, :n], 0.0), gram)
    l_ak = _each(lambda x: jnp.where(strict, x[:n, n:], 0.0), gram)
    m_rb = _each(lambda x: jnp.where(incl, x[n:, :n], 0.0), gram)
    m_rk = _each(lambda x: jnp.where(incl, x[n:, n:], 0.0), gram)

    inv = _each(lambda x: eye + x, l_ab)
    pw = l_ab
    for _ in range(int(math.log2(C)) - 1):
        pw = _each(lambda x: _dot(x, x), pw)
        inv = _each(lambda x, y: x + _dot(x, y), inv, pw)

    a_p = _each(lambda x, y: fold(_dot(x, stack(y))), inv, a_hat)
    t_l = _each(_dot, inv, l_ak)
    v_p = _each(lambda x, y: fold(_dot(x, stack(y))), t_l, v)
    r_p = _each(lambda x, y, z: x + fold(_dot(y, stack(z))), r_hat, m_rb, a_p)
    y0 = _each(lambda mb, mk, vp, vv: fold(_dot(jnp.concatenate([mb, mk], axis=1),
                                                cat0(stack(vp), stack(vv)))), m_rb, m_rk, v_p, v)
    a_til = _each(lambda x, y: jnp.where(same_head, _dot_tn(x, y), 0.0), b_til, a_p)
    d_new = _each(lambda vp, vv, bt, kt: jnp.where(same_head, _dot_tn(cat0(vp, vv), cat0(bt, kt)), 0.0),
                  v_p, v, b_til, k_til)
    y = _each(lambda x, s, z: _dot_nt(x, s) + z, r_p, s_old, y0)
    s_new = _each(lambda s, ce, at, dn: s * jnp.exp(ce) + _dot_nt(s, at) + dn,
                  s_old, cl_end, a_til, d_new)

    inv_n = 1.0 / RW_HEAD
    mean = _each(lambda x: _group_sum64(x) * inv_n, y)
    yc = _each(lambda x, m: x - m, y, mean)
    var = _each(lambda x: _group_sum64(x * x) * inv_n, yc)
    yn = _each(lambda x, vr, lg, lb: x * lax.rsqrt(vr + GN_EPS) * lg + lb, yc, var, lnx_g, lnx_b)
    bonus = _each(lambda rr, kx, rk, vv: _group_sum64(rr * kx * rk) * vv, r, k, r_k, v)
    out = _each(lambda x, bo, gg: ((x + bo) * gg).astype(BF16), yn, bonus, g)
    return out, s_new


def _rwkv_scan(r, lw, k, v, kk, a, g, r_k, lnx_g, lnx_b):
    C = RW_CHUNK
    W = RW_PAR * RW_GROUP * RW_HEAD
    nc = S // C
    slab = pl.BlockSpec((C, W), lambda b, gi, c: (b * nc + c, gi))
    prow = pl.BlockSpec((1, W), lambda b, gi, c: (0, gi))
    return pl.pallas_call(
        _rwkv_scan_kernel,
        grid=(B, D // W, nc),
        in_specs=[slab] * 7 + [prow] * 3,
        out_specs=slab,
        out_shape=jax.ShapeDtypeStruct((T, D), BF16),
        scratch_shapes=[pltpu.VMEM((RW_PAR, RW_GROUP * RW_HEAD, RW_GROUP * RW_HEAD), F32)],
        compiler_params=_cparams("parallel", "parallel", "arbitrary"),
        name="rwkv_scan",
    )(r, lw, k, v, kk, a, g, r_k.reshape(1, D), lnx_g.reshape(1, D), lnx_b.reshape(1, D))


def _proj_res_kernel(gate_row, a_ref, w_ref, x_ref, mod_ref, o_ref):
    y = jnp.dot(a_ref[...], w_ref[...], preferred_element_type=F32)
    o_ref[...] = x_ref[...] + mod_ref[0][gate_row:gate_row + 1] * y


def _proj_res(a, w, x, mod, gate_row):
    tm = 512
    nb = S // tm
    kdim = a.shape[1]
    return pl.pallas_call(
        functools.partial(_proj_res_kernel, gate_row),
        grid=(T // tm,),
        in_specs=[pl.BlockSpec((tm, kdim), lambda i: (i, 0)),
                  pl.BlockSpec((kdim, D), lambda i: (0, 0)),
                  pl.BlockSpec((tm, D), lambda i: (i, 0)),
                  pl.BlockSpec((1, 6, D), lambda i: (i // nb, 0, 0))],
        out_specs=pl.BlockSpec((tm, D), lambda i: (i, 0)),
        out_shape=jax.ShapeDtypeStruct((T, D), F32),
        compiler_params=_cparams("parallel"),
        name="proj_res",
    )(a, w.astype(BF16), x, mod)


def _ffn_kernel(nf, x_ref, mod_ref, ng_ref, wg_ref, wu_ref, wd_ref, o_ref, h_scr, acc_scr):
    f = pl.program_id(1)

    @pl.when(f == 0)
    def _():
        mod = mod_ref[0]
        h_scr[...] = _modnorm(x_ref[...], ng_ref[...], mod[3:4], mod[4:5]).astype(BF16)
        acc_scr[...] = jnp.zeros_like(acc_scr)

    h = h_scr[...]
    gt = jnp.dot(h, wg_ref[...], preferred_element_type=F32)
    up = jnp.dot(h, wu_ref[...], preferred_element_type=F32)
    act = (gt * _sigmoid(gt) * up).astype(BF16)
    acc_scr[...] += jnp.dot(act, wd_ref[...], preferred_element_type=F32)

    @pl.when(f == nf - 1)
    def _():
        o_ref[...] = x_ref[...] + mod_ref[0][5:6] * acc_scr[...]


def _ffn_dense(x, mod, norm_g, w_gate_up, w_down):
    tm, tf = FFN_TM, FFN_TF
    nf = D_FF // tf
    nb = S // tm
    wgu = w_gate_up.astype(BF16)
    return pl.pallas_call(
        functools.partial(_ffn_kernel, nf),
        grid=(T // tm, nf),
        in_specs=[pl.BlockSpec((tm, D), lambda i, f: (i, 0)),
                  pl.BlockSpec((1, 6, D), lambda i, f: (i // nb, 0, 0)),
                  pl.BlockSpec((1, D), lambda i, f: (0, 0)),
                  pl.BlockSpec((D, tf), lambda i, f: (0, f)),
                  pl.BlockSpec((D, tf), lambda i, f: (0, nf + f)),
                  pl.BlockSpec((tf, D), lambda i, f: (f, 0))],
        out_specs=pl.BlockSpec((tm, D), lambda i, f: (i, 0)),
        out_shape=jax.ShapeDtypeStruct((T, D), F32),
        scratch_shapes=[pltpu.VMEM((tm, D), BF16), pltpu.VMEM((tm, D), F32)],
        compiler_params=_cparams("parallel", "arbitrary"),
        name="ffn_dense",
    )(x, mod, norm_g.reshape(1, D), wgu, wgu, w_down.astype(BF16))


def _route_kernel(x_ref, mod_ref, ng_ref, wr_ref, br_ref,
                  h_out, cw_out, dest_out, dest_t_out, cnt_out, carry_scr):
    i = pl.program_id(0)
    tm = x_ref.shape[0]
    mod = mod_ref[0]
    h = _modnorm(x_ref[...], ng_ref[...], mod[3:4], mod[4:5])
    h_out[...] = h.astype(BF16)
    logits = jnp.dot(h, wr_ref[...], precision=HIGHEST, preferred_element_type=F32) + br_ref[...]
    lane = lax.broadcasted_iota(jnp.int32, logits.shape, 1)
    neg = -jnp.inf
    logits = jnp.where(lane < N_EXPERTS, logits, neg)
    m1 = jnp.max(logits, axis=-1, keepdims=True)
    i1 = jnp.min(jnp.where(logits == m1, lane, LANES), axis=-1, keepdims=True)
    rest = jnp.where(lane == i1, neg, logits)
    m2 = jnp.max(rest, axis=-1, keepdims=True)
    i2 = jnp.min(jnp.where(rest == m2, lane, LANES), axis=-1, keepdims=True)
    e2 = jnp.exp(m2 - m1)
    w1 = 1.0 / (1.0 + e2)
    w2 = e2 / (1.0 + e2)
    cw_out[...] = jnp.where(lane == i1, w1, 0.0) + jnp.where(lane == i2, w2, 0.0)

    sel = jnp.where((lane == i1) | (lane == i2), 1.0, 0.0)
    ti = lax.broadcasted_iota(jnp.int32, (tm, tm), 0)
    si = lax.broadcasted_iota(jnp.int32, (tm, tm), 1)
    before = jnp.where(ti > si, 1.0, 0.0).astype(BF16)
    window_start = (i % (MOE_TW // tm)) == 0
    carry = jnp.where(window_start, 0.0, carry_scr[...])
    rank = jnp.dot(before, sel.astype(BF16), preferred_element_type=F32) + carry
    dest = jnp.where(sel > 0.0, rank, -1.0)
    dest_out[...] = dest
    dest_t_out[...] = dest.T
    total = carry + jnp.sum(sel, axis=0, keepdims=True)
    carry_scr[...] = total
    cnt_out[0] = jnp.broadcast_to(total, (8, LANES))


def _moe_kernel(nf, cnt_ref, h_ref, dest_ref, dest_t_ref, cw_ref, wg_ref, wu_ref, wd_ref, o_ref,
                xg_scr, yc_scr):
    w = pl.program_id(0)
    e = pl.program_id(1)
    f = pl.program_id(2)
    tw, ts = MOE_TW, MOE_TS
    n = cnt_ref[w * N_EXPERTS + e]
    n_sub = (n + ts - 1) // ts

    @pl.when((e == 0) & (f == 0))
    def _():
        o_ref[...] = jnp.zeros_like(o_ref)

    dest_row = dest_t_ref[pl.ds(e, 1), :]
    lane = lax.broadcasted_iota(jnp.int32, (tw, LANES), 1)
    pick = lane == e
    dest_col = jnp.sum(jnp.where(pick, dest_ref[...], 0.0), axis=-1, keepdims=True)
    cw_col = jnp.sum(jnp.where(pick, cw_ref[...], 0.0), axis=-1, keepdims=True)
    slot_sub = lax.broadcasted_iota(jnp.int32, (ts, tw), 0).astype(F32)
    slot_lane = lax.broadcasted_iota(jnp.int32, (tw, ts), 1).astype(F32)

    def sub_tile(s, carry):
        base = (s * ts).astype(F32)

        @pl.when(f == 0)
        def _():
            gather = jnp.where(dest_row == slot_sub + base, 1.0, 0.0).astype(BF16)
            xg_scr[s] = jnp.dot(gather, h_ref[...], preferred_element_type=F32).astype(BF16)

        xs = xg_scr[s]
        gt = jnp.dot(xs, wg_ref[0], preferred_element_type=F32)
        up = jnp.dot(xs, wu_ref[0], preferred_element_type=F32)
        act = (gt * _sigmoid(gt) * up).astype(BF16)
        y = jnp.dot(act, wd_ref[0], preferred_element_type=F32)

        @pl.when(f == 0)
        def _():
            yc_scr[s] = y

        @pl.when(f > 0)
        def _():
            yc_scr[s] += y

        @pl.when(f == nf - 1)
        def _():
            spread = jnp.where(dest_col == slot_lane + base, 1.0, 0.0).astype(BF16)
            o_ref[...] += cw_col * jnp.dot(spread, yc_scr[s].astype(BF16),
                                           preferred_element_type=F32)
        return carry

    lax.fori_loop(0, n_sub, sub_tile, 0)


def _residual_kernel(x_ref, y_ref, mod_ref, o_ref):
    o_ref[...] = x_ref[...] + mod_ref[0][5:6] * y_ref[...]


def _moe(x, mod, norm_g, w_router, b_router, w_gate_up, w_down):
    tm, tf, tw, ts = FFN_TM, FFN_TF, MOE_TW, MOE_TS
    nf = D_FF // tf
    nb = S // tm
    nw = T // tw
    wgu = w_gate_up.astype(BF16)
    wr = jnp.zeros((D, LANES), F32).at[:, :N_EXPERTS].set(w_router)
    br = jnp.zeros((1, LANES), F32).at[0, :N_EXPERTS].set(b_router)
    tile = lambda width: pl.BlockSpec((tm, width), lambda i: (i, 0))
    h, cw, dest, dest_t, cnt = pl.pallas_call(
        _route_kernel,
        grid=(T // tm,),
        in_specs=[tile(D),
                  pl.BlockSpec((1, 6, D), lambda i: (i // nb, 0, 0)),
                  pl.BlockSpec((1, D), lambda i: (0, 0)),
                  pl.BlockSpec((D, LANES), lambda i: (0, 0)),
                  pl.BlockSpec((1, LANES), lambda i: (0, 0))],
        out_specs=[tile(D), tile(LANES), tile(LANES),
                   pl.BlockSpec((LANES, tm), lambda i: (0, i)),
                   pl.BlockSpec((1, 8, LANES), lambda i: (i // (tw // tm), 0, 0))],
        out_shape=[jax.ShapeDtypeStruct((T, D), BF16), jax.ShapeDtypeStruct((T, LANES), F32),
                   jax.ShapeDtypeStruct((T, LANES), F32), jax.ShapeDtypeStruct((LANES, T), F32),
                   jax.ShapeDtypeStruct((nw, 8, LANES), F32)],
        scratch_shapes=[pltpu.VMEM((1, LANES), F32)],
        compiler_params=_cparams("arbitrary"),
        name="moe_route",
    )(x, mod, norm_g.reshape(1, D), wr, br)
    counts = cnt[:, 0, :N_EXPERTS].astype(jnp.int32).reshape(nw * N_EXPERTS)
    n_slots = -(-tw // ts)
    y = pl.pallas_call(
        functools.partial(_moe_kernel, nf),
        grid_spec=pltpu.PrefetchScalarGridSpec(
            num_scalar_prefetch=1,
            grid=(nw, N_EXPERTS, nf),
            in_specs=[pl.BlockSpec((tw, D), lambda w, e, f, c: (w, 0)),
                      pl.BlockSpec((tw, LANES), lambda w, e, f, c: (w, 0)),
                      pl.BlockSpec((LANES, tw), lambda w, e, f, c: (0, w)),
                      pl.BlockSpec((tw, LANES), lambda w, e, f, c: (w, 0)),
                      pl.BlockSpec((1, D, tf), lambda w, e, f, c: (e, 0, f)),
                      pl.BlockSpec((1, D, tf), lambda w, e, f, c: (e, 0, nf + f)),
                      pl.BlockSpec((1, tf, D), lambda w, e, f, c: (e, f, 0))],
            out_specs=pl.BlockSpec((tw, D), lambda w, e, f, c: (w, 0)),
            scratch_shapes=[pltpu.VMEM((n_slots, ts, D), BF16), pltpu.VMEM((n_slots, ts, D), F32)]),
        out_shape=jax.ShapeDtypeStruct((T, D), F32),
        compiler_params=_cparams("parallel", "arbitrary", "arbitrary"),
        name="moe_experts",
    )(counts, h, dest, dest_t, cw, wgu, wgu, w_down.astype(BF16))
    return pl.pallas_call(
        _residual_kernel,
        grid=(T // tm,),
        in_specs=[tile(D), tile(D), pl.BlockSpec((1, 6, D), lambda i: (i // nb, 0, 0))],
        out_specs=tile(D),
        out_shape=jax.ShapeDtypeStruct((T, D), F32),
        compiler_params=_cparams("parallel"),
        name="moe_residual",
    )(x, y, mod)


def _mla_proj_kernel(x_ref, mod_ref, ng_ref, pos_ref, invf_ref, wd_ref, qg_ref, kvg_ref,
                     wuq_ref, wukv_ref, q_out, k_out, v_out):
    mod = mod_ref[0]
    h = _modnorm(x_ref[...], ng_ref[...], mod[0:1], mod[1:2])
    down = _dot(h, wd_ref[...])
    cq = down[:, :MLA_Q_LORA]
    ckv = down[:, MLA_Q_LORA:MLA_Q_LORA + MLA_KV_LORA]
    kr = down[:, MLA_Q_LORA + MLA_KV_LORA:]
    cq = cq * lax.rsqrt(jnp.mean(cq * cq, axis=-1, keepdims=True) + NORM_EPS) * qg_ref[...]
    ckv = ckv * lax.rsqrt(jnp.mean(ckv * ckv, axis=-1, keepdims=True) + NORM_EPS) * kvg_ref[...]

    ang = pos_ref[...] * invf_ref[...]
    lane = lax.broadcasted_iota(jnp.int32, ang.shape, 1)
    is_rope = (lane >= MLA_NOPE) & (lane < MLA_NOPE + MLA_ROPE)
    cos_r = jnp.where(is_rope, jnp.cos(ang), 0.0)
    sin_r = jnp.where(is_rope, jnp.sin(ang), 0.0)
    cos_q = jnp.where(lane < MLA_NOPE, 1.0, cos_r)
    shift = LANES - MLA_ROPE

    k_rope = kr * cos_r + pltpu.roll(kr, shift, axis=1) * sin_r

    qf = _dot(cq, wuq_ref[...])
    kv = _dot(ckv, wukv_ref[...])
    for hd in range(MLA_HEADS):
        sl = slice(hd * LANES, (hd + 1) * LANES)
        qs = qf[:, sl]
        q_out[:, sl] = ((qs * cos_q + pltpu.roll(qs, shift, axis=1) * sin_r) * MLA_SCALE).astype(BF16)
        k_out[:, sl] = (kv[:, sl] + k_rope).astype(BF16)
    _store_transposed(v_out, kv[:, MLA_HEADS * LANES:])


def _rot_half_cols(w):
    half = MLA_ROPE // 2
    return jnp.concatenate([-w[..., half:], w[..., :half]], axis=-1)


def _mla_proj(x, mod, norm_g, positions, w_down, q_norm_g, kv_norm_g, w_uq, w_ukv):
    tm = PROJ_TM
    nb = S // tm
    half = MLA_ROPE // 2
    wr = w_down[:, MLA_Q_LORA + MLA_KV_LORA:]
    wd = jnp.concatenate([w_down[:, :MLA_Q_LORA + MLA_KV_LORA],
                          jnp.zeros((D, MLA_NOPE), F32), wr, _rot_half_cols(wr)], axis=1).astype(BF16)
    wq = w_uq.reshape(MLA_Q_LORA, MLA_HEADS, MLA_NOPE + MLA_ROPE)
    wq_r = wq[..., MLA_NOPE:]
    wuq = jnp.concatenate([wq[..., :MLA_NOPE], wq_r, _rot_half_cols(wq_r)], axis=-1)
    wuq = wuq.reshape(MLA_Q_LORA, MLA_HEADS * LANES).astype(BF16)
    wkv = w_ukv.reshape(MLA_KV_LORA, MLA_HEADS, 2 * MLA_NOPE)
    wk = jnp.concatenate([wkv[..., :MLA_NOPE], jnp.zeros_like(wkv[..., :MLA_NOPE])], axis=-1)
    wukv = jnp.concatenate([wk.reshape(MLA_KV_LORA, MLA_HEADS * LANES),
                            wkv[..., MLA_NOPE:].reshape(MLA_KV_LORA, D)], axis=1).astype(BF16)
    inv_freq = ROPE_BASE ** (-jnp.arange(half, dtype=F32) / half)
    lane = jnp.arange(LANES)
    invf = jnp.where((lane >= MLA_NOPE) & (lane < MLA_NOPE + MLA_ROPE),
                     inv_freq[(lane - MLA_NOPE) % half], 0.0).reshape(1, LANES).astype(F32)
    pos = jnp.broadcast_to(positions.reshape(T, 1).astype(F32), (T, LANES))
    full = lambda a: pl.BlockSpec(a.shape, lambda i: (0,) * a.ndim)
    ops = [x, mod, norm_g.reshape(1, D), pos, invf, wd, q_norm_g.reshape(1, -1),
           kv_norm_g.reshape(1, -1), wuq, wukv]
    specs = [pl.BlockSpec((tm, D), lambda i: (i, 0)),
             pl.BlockSpec((1, 6, D), lambda i: (i // nb, 0, 0)),
             full(ops[2]),
             pl.BlockSpec((tm, LANES), lambda i: (i, 0))] + [full(a) for a in ops[4:]]
    qw = MLA_HEADS * LANES
    return pl.pallas_call(
        _mla_proj_kernel,
        grid=(T // tm,),
        in_specs=specs,
        out_specs=[pl.BlockSpec((tm, qw), lambda i: (i, 0)),
                   pl.BlockSpec((tm, qw), lambda i: (i, 0)),
                   _vt_spec(tm)],
        out_shape=[jax.ShapeDtypeStruct((T, qw), BF16), jax.ShapeDtypeStruct((T, qw), BF16),
                   _vt_shape()],
        compiler_params=_cparams("parallel"),
        name="mla_proj",
    )(*ops)


def _fox_proj_kernel(x_ref, mod_ref, ng_ref, w_ref, wf_ref, bf_ref, qg_ref, kg_ref,
                     q_out, k_out, v_out, og_out, fc_out, fr_out, carry_scr):
    i = pl.program_id(0)
    tm = x_ref.shape[0]
    mod = mod_ref[0]
    h = _modnorm(x_ref[...], ng_ref[...], mod[0:1], mod[1:2]).astype(BF16)
    q = jnp.dot(h, w_ref[:, 0:D], preferred_element_type=F32)
    k = jnp.dot(h, w_ref[:, D:2 * D], preferred_element_type=F32)
    v = jnp.dot(h, w_ref[:, 2 * D:3 * D], preferred_element_type=F32)
    og = jnp.dot(h, w_ref[:, 3 * D:4 * D], preferred_element_type=F32)
    inv_n = 1.0 / FOX_HEAD
    q = q * lax.rsqrt(_group_sum64(q * q) * inv_n + NORM_EPS) * qg_ref[...] * FOX_SCALE
    k = k * lax.rsqrt(_group_sum64(k * k) * inv_n + NORM_EPS) * kg_ref[...]
    q_out[...] = q.astype(BF16)
    k_out[...] = k.astype(BF16)
    _store_transposed(v_out, v)
    og_out[...] = _sigmoid(og).astype(BF16)

    z = jnp.dot(h, wf_ref[...], preferred_element_type=F32) + bf_ref[...]
    lane = lax.broadcasted_iota(jnp.int32, z.shape, 1)
    log_f = jnp.minimum(z, 0.0) - jnp.log(1.0 + jnp.exp(-jnp.abs(z)))
    log_f = jnp.where(lane < FOX_HEADS, log_f, 0.0)
    ti = lax.broadcasted_iota(jnp.int32, (tm, tm), 0)
    si = lax.broadcasted_iota(jnp.int32, (tm, tm), 1)
    tri = jnp.where(ti >= si, 1.0, 0.0).astype(F32)
    seq_start = (i % (S // tm)) == 0
    carry = jnp.where(seq_start, 0.0, carry_scr[...])
    cum = jnp.dot(tri, log_f, precision=HIGHEST, preferred_element_type=F32) + carry
    carry_scr[...] = cum[tm - 1:tm, :]
    fc_out[...] = cum
    fr_out[0] = cum.T


def _fox_proj(x, mod, norm_g, w_in, b_f, q_norm_g, k_norm_g):
    tm = PROJ_TM
    nb = S // tm
    w_main = jnp.concatenate([w_in[:, :3 * D], w_in[:, 3 * D + FOX_HEADS:]], axis=1).astype(BF16)
    w_f = jnp.zeros((D, LANES), F32).at[:, :FOX_HEADS].set(w_in[:, 3 * D:3 * D + FOX_HEADS]).astype(BF16)
    bf = jnp.zeros((1, LANES), F32).at[0, :FOX_HEADS].set(b_f)
    qg = jnp.tile(q_norm_g, FOX_HEADS).reshape(1, D)
    kg = jnp.tile(k_norm_g, FOX_HEADS).reshape(1, D)
    full = lambda a: pl.BlockSpec(a.shape, lambda i: (0,) * a.ndim)
    tile = pl.BlockSpec((tm, D), lambda i: (i, 0))
    ops = [x, mod, norm_g.reshape(1, D), w_main, w_f, bf, qg, kg]
    specs = [tile, pl.BlockSpec((1, 6, D), lambda i: (i // nb, 0, 0))] + [full(a) for a in ops[2:]]
    return pl.pallas_call(
        _fox_proj_kernel,
        grid=(T // tm,),
        in_specs=specs,
        out_specs=[tile, tile, _vt_spec(tm), tile,
                   pl.BlockSpec((tm, LANES), lambda i: (i, 0)),
                   pl.BlockSpec((1, LANES, tm), lambda i: (i // nb, 0, i % nb))],
        out_shape=[jax.ShapeDtypeStruct((T, D), BF16)] * 2 + [_vt_shape(), jax.ShapeDtypeStruct((T, D), BF16),
                   jax.ShapeDtypeStruct((T, LANES), F32), jax.ShapeDtypeStruct((B, LANES, S), F32)],
        scratch_shapes=[pltpu.VMEM((1, LANES), F32)],
        compiler_params=_cparams("arbitrary"),
        name="fox_proj",
    )(*ops)


def _attn_kernel(fox, *refs):
    if fox:
        q_ref, k_ref, vt_ref, og_ref, fc_ref, fr_ref, o_ref, st_scr, p_scr, acc_scr, fs_scr = refs
    else:
        q_ref, k_ref, vt_ref, o_ref, st_scr, p_scr, acc_scr = refs
    tq, tk = ATT_TQ, ATT_TK
    nk = S // tk
    w2 = 2 * tq
    p = pl.program_id(1)
    qi = pl.program_id(2)

    if fox:
        @pl.when(qi == 0)
        def _():
            fcv = fc_ref[...]
            lane_s = lax.broadcasted_iota(jnp.int32, fcv.shape, 1)
            for j in range(2):
                colv = jnp.sum(jnp.where(lane_s == 2 * p + j, fcv, 0.0), axis=-1, keepdims=True)
                full = jnp.broadcast_to(colv, fcv.shape)
                for kb in range(nk):
                    fs_scr[j * nk + kb] = full[kb * tk:(kb + 1) * tk]

    qb = q_ref[...]
    lane = lax.broadcasted_iota(jnp.int32, qb.shape, 1)
    per_head = FOX_HEAD if fox else LANES
    zero = jnp.zeros_like(qb)
    q_bd = jnp.concatenate([jnp.where(lane // per_head == j, qb, zero) for j in range(2)], axis=0)
    if fox:
        f_t = jnp.concatenate([fr_ref[0, pl.ds(2 * p + j, 1), :] for j in range(2)], axis=1)

    srow = lax.broadcasted_iota(jnp.int32, (tk, w2), 0)
    tcol = lax.broadcasted_iota(jnp.int32, (tk, w2), 1) % tq
    diag_mask = (srow <= tcol) if fox else ((srow // 64) <= (tcol // 64))

    def scores(kb, mx, mask):
        start = pl.multiple_of(kb * tk, tk)
        st = lax.dot_general(k_ref[pl.ds(start, tk), :], q_bd, (((1,), (1,)), ((), ())),
                             preferred_element_type=F32)
        if fox:
            reps = tq // LANES
            f_s = jnp.concatenate([fs_scr[kb]] * reps + [fs_scr[nk + kb]] * reps, axis=1)
            st = st + (f_t - f_s)
        if mask is not None:
            st = jnp.where(mask, st, -jnp.inf)
        st_scr[kb] = st
        return jnp.maximum(mx, jnp.max(st, axis=0, keepdims=True))

    def scores_pair(i, mx):
        return scores(2 * i + 1, scores(2 * i, mx, None), None)

    mx = jnp.full((1, w2), -jnp.inf, F32)
    mx = lax.fori_loop(0, qi // 2, scores_pair, mx)
    mx = lax.fori_loop(0, qi % 2, lambda _, m: scores(qi - 1, m, None), mx)
    mx = scores(qi, mx, diag_mask)

    def probs(kb, l):
        pt = jnp.exp(st_scr[kb] - mx)
        p_scr[kb] = pt.astype(BF16)
        return l + jnp.sum(pt, axis=0, keepdims=True)

    l = lax.fori_loop(0, qi + 1, probs, jnp.zeros((1, w2), F32))

    acc_scr[...] = jnp.zeros_like(acc_scr)

    def values(kb):
        acc_scr[...] += jnp.dot(vt_ref[0, kb], p_scr[kb], preferred_element_type=F32)

    def values_pair(i, carry):
        values(2 * i)
        values(2 * i + 1)
        return carry

    lax.fori_loop(0, (qi + 1) // 2, values_pair, 0)

    @pl.when(qi % 2 == 0)
    def _():
        values(qi)

    acc = acc_scr[...] / l
    sub = lax.broadcasted_iota(jnp.int32, (LANES, tq), 0)
    o_t = jnp.where(sub // 64 == 0, acc[:, :tq], acc[:, tq:])
    o = o_t.T
    if fox:
        o = o * og_ref[...].astype(F32)
    o_ref[...] = o.astype(BF16)


def _attention(q, k, vt, fox_extras=None):
    fox = fox_extras is not None
    tq, tk = ATT_TQ, ATT_TK
    nq = S // tq
    nk = S // tk
    qw = LANES if fox else 2 * LANES
    ops = [q, k, vt]
    specs = [pl.BlockSpec((tq, qw), lambda b, p, i: (b * nq + i, p)),
             pl.BlockSpec((S, qw), lambda b, p, i: (b, p)),
             pl.BlockSpec((1, nk, LANES, tk), lambda b, p, i: (b, 0, p, 0))]
    scratch = [pltpu.VMEM((nk, tk, 2 * tq), F32), pltpu.VMEM((nk, tk, 2 * tq), BF16),
               pltpu.VMEM((LANES, 2 * tq), F32)]
    if fox:
        og, fc, fr = fox_extras
        ops += [og, fc, fr]
        specs += [pl.BlockSpec((tq, LANES), lambda b, p, i: (b * nq + i, p)),
                  pl.BlockSpec((S, LANES), lambda b, p, i: (b, 0)),
                  pl.BlockSpec((1, FOX_HEADS, tq), lambda b, p, i: (b, 0, i))]
        scratch.append(pltpu.VMEM((2 * nk, tk, LANES), F32))
    return pl.pallas_call(
        functools.partial(_attn_kernel, fox),
        grid=(B, 8, nq),
        in_specs=specs,
        out_specs=pl.BlockSpec((tq, LANES), lambda b, p, i: (b * nq + i, p)),
        out_shape=jax.ShapeDtypeStruct((T, D), BF16),
        scratch_shapes=scratch,
        compiler_params=_cparams("parallel", "parallel", "arbitrary"),
        name="fox_attn" if fox else "mla_attn",
    )(*ops)


def _final_norm_kernel(x_ref, g_ref, o_ref):
    x = x_ref[...]
    o_ref[...] = x * lax.rsqrt(jnp.mean(x * x, axis=-1, keepdims=True) + NORM_EPS) * g_ref[...]


def _final_norm(x, g):
    tm = 512
    return pl.pallas_call(
        _final_norm_kernel,
        grid=(T // tm,),
        in_specs=[pl.BlockSpec((tm, D), lambda i: (i, 0)), pl.BlockSpec((1, D), lambda i: (0, 0))],
        out_specs=pl.BlockSpec((tm, D), lambda i: (i, 0)),
        out_shape=jax.ShapeDtypeStruct((T, D), F32),
        compiler_params=_cparams("parallel"),
        name="final_norm",
    )(x, g.reshape(1, D))


def kernel(x, c, positions, ada_w, ada_b, norm_mix_g, norm_ffn_g, final_norm_g, rw_mu, rw_w_rkv, rw_w_o, rw_w0, rw_w1, rw_w2, rw_a0, rw_a1, rw_a2, rw_g1, rw_g2, rw_k_k, rw_k_a, rw_r_k, rw_lnx_g, rw_lnx_b, rw_v0, rw_v1, rw_v2, mla_w_down, mla_q_norm_g, mla_kv_norm_g, mla_w_uq, mla_w_ukv, mla_w_o, fox_w_in, fox_b_f, fox_q_norm_g, fox_k_norm_g, fox_w_o, ffn_w_gate_up, ffn_w_down, moe_w_router, moe_b_router, moe_w_gate_up, moe_w_down):
    xf = x.reshape(T, D)
    mods = _ada_mods(c, ada_w, ada_b)
    v_first = None
    for i in range(DEPTH):
        mod = mods[i]
        kind, j = i % 3, i // 3
        if kind == 0:
            p = dict(mu=rw_mu[j], w_rkv=rw_w_rkv[j], w0=rw_w0[j], w1=rw_w1[j], w2=rw_w2[j],
                     a0=rw_a0[j], a1=rw_a1[j], a2=rw_a2[j], g1=rw_g1[j], g2=rw_g2[j],
                     k_k=rw_k_k[j], k_a=rw_k_a[j])
            if j > 0:
                p.update(v0=rw_v0[j - 1], v1=rw_v1[j - 1], v2=rw_v2[j - 1])
            r, lw, k, v, kk, a, g = _rwkv_proj(xf, mod, norm_mix_g[i], p, v_first if j > 0 else None)
            if j == 0:
                v_first = v
            y = _rwkv_scan(r, lw, k, v, kk, a, g, rw_r_k[j], rw_lnx_g[j], rw_lnx_b[j])
            xf = _proj_res(y, rw_w_o[j], xf, mod, 2)
        elif kind == 1:
            q, k, v = _mla_proj(xf, mod, norm_mix_g[i], positions, mla_w_down[j], mla_q_norm_g[j],
                                mla_kv_norm_g[j], mla_w_uq[j], mla_w_ukv[j])
            o = _attention(q, k, v)
            xf = _proj_res(o, mla_w_o[j], xf, mod, 2)
        else:
            q, k, v, og, fc, fr = _fox_proj(xf, mod, norm_mix_g[i], fox_w_in[j], fox_b_f[j],
                                            fox_q_norm_g[j], fox_k_norm_g[j])
            o = _attention(q, k, v, (og, fc, fr))
            xf = _proj_res(o, fox_w_o[j], xf, mod, 2)
        if i % 2 == 0:
            xf = _ffn_dense(xf, mod, norm_ffn_g[i], ffn_w_gate_up[i // 2], ffn_w_down[i // 2])
        else:
            xf = _moe(xf, mod, norm_ffn_g[i], moe_w_router[i // 2], moe_b_router[i // 2],
                      moe_w_gate_up[i // 2], moe_w_down[i // 2])
    return _final_norm(xf, final_norm_g).reshape(B, S, D)
```

```python
import functools
import math

import jax
import jax.numpy as jnp
from jax import lax
from jax.experimental import pallas as pl
from jax.experimental.pallas import tpu as pltpu

F32 = jnp.float32
BF16 = jnp.bfloat16
HIGHEST = lax.Precision.HIGHEST

D = 1024
B = 8
S = 2048
T = B * S
DEPTH = 4
NORM_EPS = 1e-6

RW_HEAD = 64
GN_EPS = 64e-5
EXP_NEG_HALF = math.exp(-0.5)

MLA_HEADS = 16
MLA_NOPE = 64
MLA_ROPE = 32
MLA_Q_LORA = 768
MLA_KV_LORA = 256
MLA_SCALE = (MLA_NOPE + MLA_ROPE) ** -0.5
ROPE_BASE = 10000.0

FOX_HEADS = 16
FOX_HEAD = 64
FOX_SCALE = FOX_HEAD ** -0.5

D_FF = 2816
N_EXPERTS = 8

LANES = 128
VMEM_LIMIT = 56 * 1024 * 1024

RW_CHUNK = 64
RW_GROUP = 4
RW_PAR = 4
ATT_TQ = 256
ATT_TK = 256
FFN_TM = 512
FFN_TF = 1408
PROJ_TM = 256
MOE_TW = 1024
MOE_TS = 288


def _cparams(*sem):
    return pltpu.CompilerParams(dimension_semantics=sem, vmem_limit_bytes=VMEM_LIMIT)


def _dot(a, b):
    return jnp.dot(a.astype(BF16), b.astype(BF16), preferred_element_type=F32)


def _dot_nt(a, b):
    return lax.dot_general(a.astype(BF16), b.astype(BF16), (((1,), (1,)), ((), ())),
                           preferred_element_type=F32)


def _dot_tn(a, b):
    return lax.dot_general(a.astype(BF16), b.astype(BF16), (((0,), (0,)), ((), ())),
                           preferred_element_type=F32)


def _sigmoid(z):
    return 1.0 / (1.0 + jnp.exp(-z))


def _modnorm(x, g, shift, scale):
    ms = jnp.mean(x * x, axis=-1, keepdims=True)
    y = x * lax.rsqrt(ms + NORM_EPS) * g
    return y * (1.0 + scale) + shift


def _group_sum64(x):
    gi = lax.broadcasted_iota(jnp.int32, (LANES, LANES), 0) // 64
    gj = lax.broadcasted_iota(jnp.int32, (LANES, LANES), 1) // 64
    ones = jnp.where(gi == gj, 1.0, 0.0).astype(BF16)
    hi = x.astype(BF16)
    lo = (x - hi.astype(F32)).astype(BF16)
    outs = []
    for s in range(x.shape[1] // LANES):
        sl = slice(s * LANES, (s + 1) * LANES)
        outs.append(jnp.dot(hi[:, sl], ones, preferred_element_type=F32)
                    + jnp.dot(lo[:, sl], ones, preferred_element_type=F32))
    return outs[0] if len(outs) == 1 else jnp.concatenate(outs, axis=1)


def _store_transposed(vt_ref, v):
    vt = v.T
    for s in range(v.shape[0] // ATT_TK):
        vt_ref[0, s] = vt[:, s * ATT_TK:(s + 1) * ATT_TK].astype(BF16)


def _vt_spec(tm):
    nb = S // tm
    return pl.BlockSpec((1, tm // ATT_TK, D, ATT_TK), lambda i: (i // nb, i % nb, 0, 0))


def _vt_shape():
    return jax.ShapeDtypeStruct((B, S // ATT_TK, D, ATT_TK), BF16)


def _ada_kernel(c_ref, w_ref, b_ref, o_ref):
    c = c_ref[...]
    cond = c * _sigmoid(c)
    o_ref[0] = _dot(cond, w_ref[0]) + b_ref[0]


def _ada_mods(c, ada_w, ada_b):
    tn = 1536
    out = pl.pallas_call(
        _ada_kernel,
        grid=(DEPTH, 6 * D // tn),
        in_specs=[pl.BlockSpec((B, D), lambda l, j: (0, 0)),
                  pl.BlockSpec((1, D, tn), lambda l, j: (l, 0, j)),
                  pl.BlockSpec((1, 1, tn), lambda l, j: (l, 0, j))],
        out_specs=pl.BlockSpec((1, B, tn), lambda l, j: (l, 0, j)),
        out_shape=jax.ShapeDtypeStruct((DEPTH, B, 6 * D), F32),
        compiler_params=_cparams("parallel", "parallel"),
        name="ada_mods",
    )(c, ada_w, ada_b.reshape(DEPTH, 1, 6 * D))
    return out.reshape(DEPTH, B, 6, D)


def _rwkv_proj_kernel(has_vres, *refs):
    if has_vres:
        (x_ref, xp_ref, mod_ref, ng_ref, mu_ref, wr_ref, wk_ref, wv_ref, w0_ref, w1_ref, w2_ref,
         a0_ref, a1_ref, a2_ref, g1_ref, g2_ref, kk_ref, ka_ref, v0_ref, v1_ref, v2_ref, vf_ref,
         r_out, lw_out, k_out, v_out, kk_out, a_out, g_out) = refs
    else:
        (x_ref, xp_ref, mod_ref, ng_ref, mu_ref, wr_ref, wk_ref, wv_ref, w0_ref, w1_ref, w2_ref,
         a0_ref, a1_ref, a2_ref, g1_ref, g2_ref, kk_ref, ka_ref,
         r_out, lw_out, k_out, v_out, kk_out, a_out, g_out) = refs
    i = pl.program_id(0)
    mod = mod_ref[0]
    shift, scale = mod[0:1], mod[1:2]
    g = ng_ref[...]
    h = _modnorm(x_ref[...], g, shift, scale)
    hp = _modnorm(xp_ref[...], g, shift, scale)
    seq_start = (i % (S // PROJ_TM)) == 0
    prev_row = jnp.where(seq_start, 0.0, hp[7:8, :])
    row = lax.broadcasted_iota(jnp.int32, h.shape, 0)
    prev = jnp.where(row == 0, prev_row, pltpu.roll(h, 1, axis=0))
    delta = prev - h
    mu = mu_ref[...]
    xr = h + delta * mu[0:1]
    xw = h + delta * mu[1:2]
    xk = h + delta * mu[2:3]
    xv = h + delta * mu[3:4]
    xa = h + delta * mu[4:5]
    xg = h + delta * mu[5:6]
    r = _dot(xr, wr_ref[...])
    k = _dot(xk, wk_ref[...])
    v = _dot(xv, wv_ref[...])
    w_raw = w0_ref[...] + _dot(jnp.tanh(_dot(xw, w1_ref[...])), w2_ref[...])
    lw_out[...] = -_sigmoid(w_raw) * EXP_NEG_HALF
    if has_vres:
        mix = _sigmoid(v0_ref[...] + _dot(_dot(xv, v1_ref[...]), v2_ref[...]))
        v = v + (vf_ref[...] - v) * mix
    a = _sigmoid(a0_ref[...] + _dot(_dot(xa, a1_ref[...]), a2_ref[...]))
    g_out[...] = _dot(_sigmoid(_dot(xg, g1_ref[...])), g2_ref[...])
    kk = k * kk_ref[...]
    norm = jnp.sqrt(_group_sum64(kk * kk))
    kk_out[...] = kk / jnp.maximum(norm, 1e-12)
    k_out[...] = k * (1.0 + (a - 1.0) * ka_ref[...])
    r_out[...] = r
    v_out[...] = v
    a_out[...] = a


def _rwkv_proj(x, mod, norm_g, p, v_first):
    has_vres = v_first is not None
    tm = PROJ_TM
    nb = S // tm
    row = lambda a: a.reshape(1, -1)
    full = lambda a: pl.BlockSpec(a.shape, lambda i: (0,) * a.ndim)
    tile = pl.BlockSpec((tm, D), lambda i: (i, 0))
    ops = [x, x, mod, row(norm_g), p["mu"],
           p["w_rkv"][0].astype(BF16), p["w_rkv"][1].astype(BF16), p["w_rkv"][2].astype(BF16),
           row(p["w0"]), p["w1"].astype(BF16), p["w2"].astype(BF16),
           row(p["a0"]), p["a1"].astype(BF16), p["a2"].astype(BF16),
           p["g1"].astype(BF16), p["g2"].astype(BF16), row(p["k_k"]), row(p["k_a"])]
    specs = [tile,
             pl.BlockSpec((8, D), lambda i: (jnp.maximum(i * (tm // 8) - 1, 0), 0)),
             pl.BlockSpec((1, 6, D), lambda i: (i // nb, 0, 0))]
    specs += [full(a) for a in ops[3:]]
    if has_vres:
        extra = [row(p["v0"]), p["v1"].astype(BF16), p["v2"].astype(BF16)]
        ops += extra + [v_first]
        specs += [full(a) for a in extra] + [tile]
    outs = pl.pallas_call(
        functools.partial(_rwkv_proj_kernel, has_vres),
        grid=(T // tm,),
        in_specs=specs,
        out_specs=[tile] * 7,
        out_shape=[jax.ShapeDtypeStruct((T, D), F32)] * 7,
        compiler_params=_cparams("parallel"),
        name="rwkv_proj",
    )(*ops)
    return outs


def _rwkv_scan_kernel(r_ref, lw_ref, k_ref, v_ref, kk_ref, a_ref, g_ref, rk_ref, lg_ref, lb_ref,
                      o_ref, s_ref):
    c = pl.program_id(2)
    W = RW_GROUP * RW_HEAD

    @pl.when(c == 0)
    def _():
        s_ref[...] = jnp.zeros_like(s_ref)

    slabs = [slice(p * W, (p + 1) * W) for p in range(RW_PAR)]
    cols = lambda ref: [ref[:, sl] for sl in slabs]
    out, s_new = _rwkv_blocks(cols(r_ref), cols(lw_ref), cols(k_ref), cols(v_ref), cols(kk_ref),
                              cols(a_ref), cols(g_ref), cols(rk_ref), cols(lg_ref), cols(lb_ref),
                              [s_ref[p] for p in range(RW_PAR)])
    for p, sl in enumerate(slabs):
        o_ref[:, sl] = out[p]
        s_ref[p] = s_new[p]


def _each(fn, *cols):
    return [fn(*args) for args in zip(*cols)]


def _rwkv_blocks(r, lw, k, v, kk, a, g, r_k, lnx_g, lnx_b, s_old):
    C = RW_CHUNK
    W = RW_GROUP * RW_HEAD

    mul = lambda x, y: x * y
    ti = lax.broadcasted_iota(jnp.int32, (C, C), 0)
    si = lax.broadcasted_iota(jnp.int32, (C, C), 1)
    tri = jnp.where(ti >= si, 1.0, 0.0).astype(F32)
    cl = _each(lambda x: jnp.dot(tri, x, precision=HIGHEST, preferred_element_type=F32), lw)
    cl_end = _each(lambda x: x[C - 1:C, :], cl)
    w_t = _each(jnp.exp, cl)
    w_prev = _each(lambda x, y: jnp.exp(x - y), cl, lw)
    w_inv = _each(lambda x: jnp.exp(-x), cl)
    w_rem = _each(lambda x, y: jnp.exp(x - y), cl_end, cl)
    bv = _each(mul, kk, a)
    r_hat = _each(mul, r, w_t)
    a_hat = _each(lambda x, y: -x * y, kk, w_prev)
    b_hat = _each(mul, bv, w_inv)
    k_hat = _each(mul, k, w_inv)
    b_til = _each(mul, bv, w_rem)
    k_til = _each(mul, k, w_rem)

    lane_head = lax.broadcasted_iota(jnp.int32, (C, W), 1) // RW_HEAD

    def stack(m):
        mb = m.astype(BF16)
        zero = jnp.zeros_like(mb)
        return jnp.concatenate([jnp.where(lane_head == hd, mb, zero) for hd in range(RW_GROUP)],
                               axis=0)

    def fold(m):
        return m[0:C] + m[C:2 * C] + m[2 * C:3 * C] + m[3 * C:4 * C]

    cat0 = lambda x, y: jnp.concatenate([x, y], axis=0)
    gram = _each(lambda ah, rh, bh, kh: _dot_nt(cat0(stack(ah), stack(rh)), cat0(stack(bh), stack(kh))),
                 a_hat, r_hat, b_hat, k_hat)
    n = RW_GROUP * C
    ri = lax.broadcasted_iota(jnp.int32, (n, n), 0)
    ci = lax.broadcasted_iota(jnp.int32, (n, n), 1)
    strict = (ri % C) > (ci % C)
    incl = (ri % C) >= (ci % C)
    same_head = (ri // RW_HEAD) == (ci // RW_HEAD)
    eye = jnp.where(ri == ci, 1.0, 0.0)
    l_ab = _each(lambda x: jnp.where(strict, x[:n, :n], 0.0), gram)
    l_ak = _each(lambda x: jnp.where(strict, x[:n, n:], 0.0), gram)
    m_rb = _each(lambda x: jnp.where(incl, x[n:, :n], 0.0), gram)
    m_rk = _each(lambda x: jnp.where(incl, x[n:, n:], 0.0), gram)

    inv = _each(lambda x: eye + x, l_ab)
    pw = l_ab
    for _ in range(int(math.log2(C)) - 1):
        pw = _each(lambda x: _dot(x, x), pw)
        inv = _each(lambda x, y: x + _dot(x, y), inv, pw)

    a_p = _each(lambda x, y: fold(_dot(x, stack(y))), inv, a_hat)
    t_l = _each(_dot, inv, l_ak)
    v_p = _each(lambda x, y: fold(_dot(x, stack(y))), t_l, v)
    r_p = _each(lambda x, y, z: x + fold(_dot(y, stack(z))), r_hat, m_rb, a_p)
    y0 = _each(lambda mb, mk, vp, vv: fold(_dot(jnp.concatenate([mb, mk], axis=1),
                                                cat0(stack(vp), stack(vv)))), m_rb, m_rk, v_p, v)
    a_til = _each(lambda x, y: jnp.where(same_head, _dot_tn(x, y), 0.0), b_til, a_p)
    d_new = _each(lambda vp, vv, bt, kt: jnp.where(same_head, _dot_tn(cat0(vp, vv), cat0(bt, kt)), 0.0),
                  v_p, v, b_til, k_til)
    y = _each(lambda x, s, z: _dot_nt(x, s) + z, r_p, s_old, y0)
    s_new = _each(lambda s, ce, at, dn: s * jnp.exp(ce) + _dot_nt(s, at) + dn,
                  s_old, cl_end, a_til, d_new)

    inv_n = 1.0 / RW_HEAD
    mean = _each(lambda x: _group_sum64(x) * inv_n, y)
    yc = _each(lambda x, m: x - m, y, mean)
    var = _each(lambda x: _group_sum64(x * x) * inv_n, yc)
    yn = _each(lambda x, vr, lg, lb: x * lax.rsqrt(vr + GN_EPS) * lg + lb, yc, var, lnx_g, lnx_b)
    bonus = _each(lambda rr, kx, rk, vv: _group_sum64(rr * kx * rk) * vv, r, k, r_k, v)
    out = _each(lambda x, bo, gg: ((x + bo) * gg).astype(BF16), yn, bonus, g)
    return out, s_new


def _rwkv_scan(r, lw, k, v, kk, a, g, r_k, lnx_g, lnx_b):
    C = RW_CHUNK
    W = RW_PAR * RW_GROUP * RW_HEAD
    nc = S // C
    slab = pl.BlockSpec((C, W), lambda b, gi, c: (b * nc + c, gi))
    prow = pl.BlockSpec((1, W), lambda b, gi, c: (0, gi))
    return pl.pallas_call(
        _rwkv_scan_kernel,
        grid=(B, D // W, nc),
        in_specs=[slab] * 7 + [prow] * 3,
        out_specs=slab,
        out_shape=jax.ShapeDtypeStruct((T, D), BF16),
        scratch_shapes=[pltpu.VMEM((RW_PAR, RW_GROUP * RW_HEAD, RW_GROUP * RW_HEAD), F32)],
        compiler_params=_cparams("parallel", "parallel", "arbitrary"),
        name="rwkv_scan",
    )(r, lw, k, v, kk, a, g, r_k.reshape(1, D), lnx_g.reshape(1, D), lnx_b.reshape(1, D))


def _proj_res_kernel(gate_row, a_ref, w_ref, x_ref, mod_ref, o_ref):
    y = jnp.dot(a_ref[...], w_ref[...], preferred_element_type=F32)
    o_ref[...] = x_ref[...] + mod_ref[0][gate_row:gate_row + 1] * y


def _proj_res(a, w, x, mod, gate_row):
    tm = 512
    nb = S // tm
    kdim = a.shape[1]
    return pl.pallas_call(
        functools.partial(_proj_res_kernel, gate_row),
        grid=(T // tm,),
        in_specs=[pl.BlockSpec((tm, kdim), lambda i: (i, 0)),
                  pl.BlockSpec((kdim, D), lambda i: (0, 0)),
                  pl.BlockSpec((tm, D), lambda i: (i, 0)),
                  pl.BlockSpec((1, 6, D), lambda i: (i // nb, 0, 0))],
        out_specs=pl.BlockSpec((tm, D), lambda i: (i, 0)),
        out_shape=jax.ShapeDtypeStruct((T, D), F32),
        compiler_params=_cparams("parallel"),
        name="proj_res",
    )(a, w.astype(BF16), x, mod)


def _ffn_kernel(nf, x_ref, mod_ref, ng_ref, wg_ref, wu_ref, wd_ref, o_ref, h_scr, acc_scr):
    f = pl.program_id(1)

    @pl.when(f == 0)
    def _():
        mod = mod_ref[0]
        h_scr[...] = _modnorm(x_ref[...], ng_ref[...], mod[3:4], mod[4:5]).astype(BF16)
        acc_scr[...] = jnp.zeros_like(acc_scr)

    h = h_scr[...]
    gt = jnp.dot(h, wg_ref[0], preferred_element_type=F32)
    up = jnp.dot(h, wu_ref[0], preferred_element_type=F32)
    act = (gt * _sigmoid(gt) * up).astype(BF16)
    acc_scr[...] += jnp.dot(act, wd_ref[0], preferred_element_type=F32)

    @pl.when(f == nf - 1)
    def _():
        o_ref[...] = x_ref[...] + mod_ref[0][5:6] * acc_scr[...]


def _ffn_dense(x, mod, norm_g, wgu, wd, layer):
    tm, tf = FFN_TM, FFN_TF
    nf = D_FF // tf
    nb = S // tm
    return pl.pallas_call(
        functools.partial(_ffn_kernel, nf),
        grid=(T // tm, nf),
        in_specs=[pl.BlockSpec((tm, D), lambda i, f: (i, 0)),
                  pl.BlockSpec((1, 6, D), lambda i, f: (i // nb, 0, 0)),
                  pl.BlockSpec((1, D), lambda i, f: (0, 0)),
                  pl.BlockSpec((1, D, tf), lambda i, f: (layer, 0, f)),
                  pl.BlockSpec((1, D, tf), lambda i, f: (layer, 0, nf + f)),
                  pl.BlockSpec((1, tf, D), lambda i, f: (layer, f, 0))],
        out_specs=pl.BlockSpec((tm, D), lambda i, f: (i, 0)),
        out_shape=jax.ShapeDtypeStruct((T, D), F32),
        scratch_shapes=[pltpu.VMEM((tm, D), BF16), pltpu.VMEM((tm, D), F32)],
        compiler_params=_cparams("parallel", "arbitrary"),
        name="ffn_dense",
    )(x, mod, norm_g.reshape(1, D), wgu, wgu, wd)


def _route_kernel(x_ref, mod_ref, ng_ref, wr_ref, br_ref,
                  h_out, cw_t_out, dest_t_out, cnt_out, carry_scr):
    i = pl.program_id(0)
    tm = x_ref.shape[0]
    mod = mod_ref[0]
    h = _modnorm(x_ref[...], ng_ref[...], mod[3:4], mod[4:5])
    h_out[...] = h.astype(BF16)
    logits = jnp.dot(h, wr_ref[...], precision=HIGHEST, preferred_element_type=F32) + br_ref[...]
    lane = lax.broadcasted_iota(jnp.int32, logits.shape, 1)
    neg = -jnp.inf
    logits = jnp.where(lane < N_EXPERTS, logits, neg)
    m1 = jnp.max(logits, axis=-1, keepdims=True)
    i1 = jnp.min(jnp.where(logits == m1, lane, LANES), axis=-1, keepdims=True)
    rest = jnp.where(lane == i1, neg, logits)
    m2 = jnp.max(rest, axis=-1, keepdims=True)
    i2 = jnp.min(jnp.where(rest == m2, lane, LANES), axis=-1, keepdims=True)
    e2 = jnp.exp(m2 - m1)
    w1 = 1.0 / (1.0 + e2)
    w2 = e2 / (1.0 + e2)
    cw_t_out[...] = (jnp.where(lane == i1, w1, 0.0) + jnp.where(lane == i2, w2, 0.0)).T

    sel =jnp.where((lane == i1) | (lane == i2), 1.0, 0.0)
    ti = lax.broadcasted_iota(jnp.int32, (tm, tm), 0)
    si = lax.broadcasted_iota(jnp.int32, (tm, tm), 1)
    before = jnp.where(ti > si, 1.0, 0.0).astype(BF16)
    window_start = (i % (MOE_TW // tm)) == 0
    carry = jnp.where(window_start, 0.0, carry_scr[...])
    rank = jnp.dot(before, sel.astype(BF16), preferred_element_type=F32) + carry
    dest_t_out[...] = jnp.where(sel > 0.0, rank, -1.0).T
    total = carry + jnp.sum(sel, axis=0, keepdims=True)
    carry_scr[...] = total
    cnt_out[0] = jnp.broadcast_to(total, (8, LANES))


def _moe_kernel(nf, cnt_ref, h_ref, dest_t_ref, cw_t_ref, wg_ref, wu_ref, wd_ref, o_ref,
                xg_scr, yc_scr):
    w = pl.program_id(0)
    e = pl.program_id(1)
    f = pl.program_id(2)
    tw, ts = MOE_TW, MOE_TS
    n = cnt_ref[w * N_EXPERTS + e]
    n_sub = (n + ts - 1) // ts

    @pl.when((e == 0) & (f == 0))
    def _():
        o_ref[...] = jnp.zeros_like(o_ref)

    dest_row = dest_t_ref[pl.ds(e, 1), :]
    cw_row = cw_t_ref[pl.ds(e, 1), :]
    slot_sub = lax.broadcasted_iota(jnp.int32, (ts, tw), 0).astype(F32)

    def sub_tile(s, carry):
        match = dest_row == slot_sub + (s * ts).astype(F32)
        select = jnp.where(match, 1.0, 0.0).astype(BF16)

        @pl.when(f == 0)
        def _():
            xg_scr[s] = jnp.dot(select, h_ref[...], preferred_element_type=F32).astype(BF16)

        xs = xg_scr[s]
        gt = jnp.dot(xs, wg_ref[0, 0], preferred_element_type=F32)
        up = jnp.dot(xs, wu_ref[0, 0], preferred_element_type=F32)
        act = (gt * _sigmoid(gt) * up).astype(BF16)
        y = jnp.dot(act, wd_ref[0, 0], preferred_element_type=F32)

        @pl.when(f == 0)
        def _():
            yc_scr[s] = y

        @pl.when((f > 0) & (f < nf - 1))
        def _():
            yc_scr[s] += y

        @pl.when(f == nf - 1)
        def _():
            w_slot = jnp.sum(jnp.where(match, cw_row, 0.0), axis=-1, keepdims=True)
            weighted = ((yc_scr[s] + y) * w_slot).astype(BF16)
            o_ref[...] += _dot_tn(select, weighted)
        return carry

    lax.fori_loop(0, n_sub, sub_tile, 0)


def _residual_kernel(x_ref, y_ref, mod_ref, o_ref):
    o_ref[...] = x_ref[...] + mod_ref[0][5:6] * y_ref[...]


def _moe(x, mod, norm_g, w_router, b_router, wgu, wd, layer):
    tm, tf, tw, ts = FFN_TM, FFN_TF, MOE_TW, MOE_TS
    nf = D_FF // tf
    nb = S // tm
    nw = T // tw
    wr =jnp.zeros((D, LANES), F32).at[:, :N_EXPERTS].set(w_router)
    br = jnp.zeros((1, LANES), F32).at[0, :N_EXPERTS].set(b_router)
    tile = lambda width: pl.BlockSpec((tm, width), lambda i: (i, 0))
    assert nf >= 2
    lanes_t = pl.BlockSpec((LANES, tm), lambda i: (0, i))
    h, cw_t, dest_t, cnt = pl.pallas_call(
        _route_kernel,
        grid=(T // tm,),
        in_specs=[tile(D),
                  pl.BlockSpec((1, 6, D), lambda i: (i // nb, 0, 0)),
                  pl.BlockSpec((1, D), lambda i: (0, 0)),
                  pl.BlockSpec((D, LANES), lambda i: (0, 0)),
                  pl.BlockSpec((1, LANES), lambda i: (0, 0))],
        out_specs=[tile(D), lanes_t, lanes_t,
                   pl.BlockSpec((1, 8, LANES), lambda i: (i // (tw // tm), 0, 0))],
        out_shape=[jax.ShapeDtypeStruct((T, D), BF16), jax.ShapeDtypeStruct((LANES, T), F32),
                   jax.ShapeDtypeStruct((LANES, T), F32), jax.ShapeDtypeStruct((nw, 8, LANES), F32)],
        scratch_shapes=[pltpu.VMEM((1, LANES), F32)],
        compiler_params=_cparams("arbitrary"),
        name="moe_route",
    )(x, mod, norm_g.reshape(1, D), wr, br)
    counts = cnt[:, 0, :N_EXPERTS].astype(jnp.int32).reshape(nw * N_EXPERTS)
    n_slots = -(-tw // ts)
    y = pl.pallas_call(
        functools.partial(_moe_kernel, nf),
        grid_spec=pltpu.PrefetchScalarGridSpec(
            num_scalar_prefetch=1,
            grid=(nw, N_EXPERTS, nf),
            in_specs=[pl.BlockSpec((tw, D), lambda w, e, f, c: (w, 0)),
                      pl.BlockSpec((LANES, tw), lambda w, e, f, c: (0, w)),
                      pl.BlockSpec((LANES, tw), lambda w, e, f, c: (0, w)),
                      pl.BlockSpec((1, 1, D, tf), lambda w, e, f, c: (layer, e, 0, f)),
                      pl.BlockSpec((1, 1, D, tf), lambda w, e, f, c: (layer, e, 0, nf + f)),
                      pl.BlockSpec((1, 1, tf, D), lambda w, e, f, c: (layer, e, f, 0))],
            out_specs=pl.BlockSpec((tw, D), lambda w, e, f, c: (w, 0)),
            scratch_shapes=[pltpu.VMEM((n_slots, ts, D), BF16), pltpu.VMEM((n_slots, ts, D), F32)]),
        out_shape=jax.ShapeDtypeStruct((T, D), F32),
        compiler_params=_cparams("parallel", "arbitrary", "arbitrary"),
        name="moe_experts",
    )(counts, h, dest_t, cw_t, wgu, wgu, wd)
    return pl.pallas_call(
        _residual_kernel,
        grid=(T // tm,),
        in_specs=[tile(D), tile(D), pl.BlockSpec((1, 6, D), lambda i: (i // nb, 0, 0))],
        out_specs=tile(D),
        out_shape=jax.ShapeDtypeStruct((T, D), F32),
        compiler_params=_cparams("parallel"),
        name="moe_residual",
    )(x, y, mod)


def _mla_proj_kernel(x_ref, mod_ref, ng_ref, pos_ref, invf_ref, wd_ref, qg_ref, kvg_ref,
                     wuq_ref, wukv_ref, q_out, k_out, v_out):
    mod = mod_ref[0]
    h = _modnorm(x_ref[...], ng_ref[...], mod[0:1], mod[1:2])
    down = _dot(h, wd_ref[...])
    cq = down[:, :MLA_Q_LORA]
    ckv = down[:, MLA_Q_LORA:MLA_Q_LORA + MLA_KV_LORA]
    kr = down[:, MLA_Q_LORA + MLA_KV_LORA:]
    cq = cq * lax.rsqrt(jnp.mean(cq * cq, axis=-1, keepdims=True) + NORM_EPS) * qg_ref[...]
    ckv = ckv * lax.rsqrt(jnp.mean(ckv * ckv, axis=-1, keepdims=True) + NORM_EPS) * kvg_ref[...]

    ang = pos_ref[...] * invf_ref[...]
    lane = lax.broadcasted_iota(jnp.int32, ang.shape, 1)
    is_rope = (lane >= MLA_NOPE) & (lane < MLA_NOPE + MLA_ROPE)
    cos_r = jnp.where(is_rope, jnp.cos(ang), 0.0)
    sin_r = jnp.where(is_rope, jnp.sin(ang), 0.0)
    cos_q = jnp.where(lane < MLA_NOPE, 1.0, cos_r)
    shift = LANES - MLA_ROPE

    k_rope = kr * cos_r + pltpu.roll(kr, shift, axis=1) * sin_r

    qf = _dot(cq, wuq_ref[...])
    kv = _dot(ckv, wukv_ref[...])
    for hd in range(MLA_HEADS):
        sl = slice(hd * LANES, (hd + 1) * LANES)
        qs = qf[:, sl]
        q_out[:, sl] = ((qs * cos_q + pltpu.roll(qs, shift, axis=1) * sin_r) * MLA_SCALE).astype(BF16)
        k_out[:, sl] = (kv[:, sl] + k_rope).astype(BF16)
    _store_transposed(v_out, kv[:, MLA_HEADS * LANES:])


def _rot_half_cols(w):
    half = MLA_ROPE // 2
    return jnp.concatenate([-w[..., half:], w[..., :half]], axis=-1)


def _mla_proj(x, mod, norm_g, positions, w_down, q_norm_g, kv_norm_g, w_uq, w_ukv):
    tm = PROJ_TM
    nb = S // tm
    half = MLA_ROPE // 2
    wr = w_down[:, MLA_Q_LORA + MLA_KV_LORA:]
    wd = jnp.concatenate([w_down[:, :MLA_Q_LORA + MLA_KV_LORA],
                          jnp.zeros((D, MLA_NOPE), F32), wr, _rot_half_cols(wr)], axis=1).astype(BF16)
    wq = w_uq.reshape(MLA_Q_LORA, MLA_HEADS, MLA_NOPE + MLA_ROPE)
    wq_r = wq[..., MLA_NOPE:]
    wuq = jnp.concatenate([wq[..., :MLA_NOPE], wq_r, _rot_half_cols(wq_r)], axis=-1)
    wuq = wuq.reshape(MLA_Q_LORA, MLA_HEADS * LANES).astype(BF16)
    wkv = w_ukv.reshape(MLA_KV_LORA, MLA_HEADS, 2 * MLA_NOPE)
    wk = jnp.concatenate([wkv[..., :MLA_NOPE], jnp.zeros_like(wkv[..., :MLA_NOPE])], axis=-1)
    wukv = jnp.concatenate([wk.reshape(MLA_KV_LORA, MLA_HEADS * LANES),
                            wkv[..., MLA_NOPE:].reshape(MLA_KV_LORA, D)], axis=1).astype(BF16)
    inv_freq = ROPE_BASE ** (-jnp.arange(half, dtype=F32) / half)
    lane = jnp.arange(LANES)
    invf = jnp.where((lane >= MLA_NOPE) & (lane < MLA_NOPE + MLA_ROPE),
                     inv_freq[(lane - MLA_NOPE) % half], 0.0).reshape(1, LANES).astype(F32)
    pos = jnp.broadcast_to(positions.reshape(T, 1).astype(F32), (T, LANES))
    full = lambda a: pl.BlockSpec(a.shape, lambda i: (0,) * a.ndim)
    ops = [x, mod, norm_g.reshape(1, D), pos, invf, wd, q_norm_g.reshape(1, -1),
           kv_norm_g.reshape(1, -1), wuq, wukv]
    specs = [pl.BlockSpec((tm, D), lambda i: (i, 0)),
             pl.BlockSpec((1, 6, D), lambda i: (i // nb, 0, 0)),
             full(ops[2]),
             pl.BlockSpec((tm, LANES), lambda i: (i, 0))] + [full(a) for a in ops[4:]]
    qw = MLA_HEADS * LANES
    return pl.pallas_call(
        _mla_proj_kernel,
        grid=(T // tm,),
        in_specs=specs,
        out_specs=[pl.BlockSpec((tm, qw), lambda i: (i, 0)),
                   pl.BlockSpec((tm, qw), lambda i: (i, 0)),
                   _vt_spec(tm)],
        out_shape=[jax.ShapeDtypeStruct((T, qw), BF16), jax.ShapeDtypeStruct((T, qw), BF16),
                   _vt_shape()],
        compiler_params=_cparams("parallel"),
        name="mla_proj",
    )(*ops)


def _fox_proj_kernel(x_ref, mod_ref, ng_ref, w_ref, wf_ref, bf_ref, qg_ref, kg_ref,
                     q_out, k_out, v_out, og_out, fc_out, fr_out, carry_scr):
    i = pl.program_id(0)
    tm = x_ref.shape[0]
    mod = mod_ref[0]
    h = _modnorm(x_ref[...], ng_ref[...], mod[0:1], mod[1:2]).astype(BF16)
    q = jnp.dot(h, w_ref[:, 0:D], preferred_element_type=F32)
    k = jnp.dot(h, w_ref[:, D:2 * D], preferred_element_type=F32)
    v = jnp.dot(h, w_ref[:, 2 * D:3 * D], preferred_element_type=F32)
    og = jnp.dot(h, w_ref[:, 3 * D:4 * D], preferred_element_type=F32)
    inv_n = 1.0 / FOX_HEAD
    q = q * lax.rsqrt(_group_sum64(q * q) * inv_n + NORM_EPS) * qg_ref[...] * FOX_SCALE
    k = k * lax.rsqrt(_group_sum64(k * k) * inv_n + NORM_EPS) * kg_ref[...]
    q_out[...] = q.astype(BF16)
    k_out[...] = k.astype(BF16)
    _store_transposed(v_out, v)
    og_out[...] = _sigmoid(og).astype(BF16)

    z = jnp.dot(h, wf_ref[...], preferred_element_type=F32) + bf_ref[...]
    lane = lax.broadcasted_iota(jnp.int32, z.shape, 1)
    log_f = jnp.minimum(z, 0.0) - jnp.log(1.0 + jnp.exp(-jnp.abs(z)))
    log_f = jnp.where(lane < FOX_HEADS, log_f, 0.0)
    ti = lax.broadcasted_iota(jnp.int32, (tm, tm), 0)
    si = lax.broadcasted_iota(jnp.int32, (tm, tm), 1)
    tri = jnp.where(ti >= si, 1.0, 0.0).astype(F32)
    seq_start = (i % (S // tm)) == 0
    carry = jnp.where(seq_start, 0.0, carry_scr[...])
    cum = jnp.dot(tri, log_f, precision=HIGHEST, preferred_element_type=F32) + carry
    carry_scr[...] = cum[tm - 1:tm, :]
    fc_out[...] = cum
    fr_out[0] = cum.T


def _fox_proj(x, mod, norm_g, w_in, b_f, q_norm_g, k_norm_g):
    tm = PROJ_TM
    nb = S // tm
    w_main = jnp.concatenate([w_in[:, :3 * D], w_in[:, 3 * D + FOX_HEADS:]], axis=1).astype(BF16)
    w_f = jnp.zeros((D, LANES), F32).at[:, :FOX_HEADS].set(w_in[:, 3 * D:3 * D + FOX_HEADS]).astype(BF16)
    bf = jnp.zeros((1, LANES), F32).at[0, :FOX_HEADS].set(b_f)
    qg = jnp.tile(q_norm_g, FOX_HEADS).reshape(1, D)
    kg = jnp.tile(k_norm_g, FOX_HEADS).reshape(1, D)
    full = lambda a: pl.BlockSpec(a.shape, lambda i: (0,) * a.ndim)
    tile = pl.BlockSpec((tm, D), lambda i: (i, 0))
    ops = [x, mod, norm_g.reshape(1, D), w_main, w_f, bf, qg, kg]
    specs = [tile, pl.BlockSpec((1, 6, D), lambda i: (i // nb, 0, 0))] + [full(a) for a in ops[2:]]
    return pl.pallas_call(
        _fox_proj_kernel,
        grid=(T // tm,),
        in_specs=specs,
        out_specs=[tile, tile, _vt_spec(tm), tile,
                   pl.BlockSpec((tm, LANES), lambda i: (i, 0)),
                   pl.BlockSpec((1, LANES, tm), lambda i: (i // nb, 0, i % nb))],
        out_shape=[jax.ShapeDtypeStruct((T, D), BF16)] * 2 + [_vt_shape(), jax.ShapeDtypeStruct((T, D), BF16),
                   jax.ShapeDtypeStruct((T, LANES), F32), jax.ShapeDtypeStruct((B, LANES, S), F32)],
        scratch_shapes=[pltpu.VMEM((1, LANES), F32)],
        compiler_params=_cparams("arbitrary"),
        name="fox_proj",
    )(*ops)


def _attn_kernel(fox, *refs):
    if fox:
        q_ref, k_ref, vt_ref, og_ref, fc_ref, fr_ref, o_ref, st_scr, p_scr, acc_scr, fs_scr = refs
    else:
        q_ref, k_ref, vt_ref, o_ref, st_scr, p_scr, acc_scr = refs
    tq, tk = ATT_TQ, ATT_TK
    nk = S // tk
    w2 = 2 * tq
    p = pl.program_id(1)
    qi = pl.program_id(2)

    if fox:
        @pl.when(qi == 0)
        def _():
            fcv = fc_ref[...]
            lane_s = lax.broadcasted_iota(jnp.int32, fcv.shape, 1)
            for j in range(2):
                colv = jnp.sum(jnp.where(lane_s == 2 * p + j, fcv, 0.0), axis=-1, keepdims=True)
                full = jnp.broadcast_to(colv, fcv.shape)
                for kb in range(nk):
                    fs_scr[j * nk + kb] = full[kb * tk:(kb + 1) * tk]

    qb = q_ref[...]
    lane = lax.broadcasted_iota(jnp.int32, qb.shape, 1)
    per_head = FOX_HEAD if fox else LANES
    zero = jnp.zeros_like(qb)
    q_bd = jnp.concatenate([jnp.where(lane // per_head == j, qb, zero) for j in range(2)], axis=0)
    if fox:
        f_t = jnp.concatenate([fr_ref[0, pl.ds(2 * p + j, 1), :] for j in range(2)], axis=1)

    srow = lax.broadcasted_iota(jnp.int32, (tk, w2), 0)
    tcol = lax.broadcasted_iota(jnp.int32, (tk, w2), 1) % tq
    diag_mask = (srow <= tcol) if fox else ((srow // 64) <= (tcol // 64))

    def scores(kb, mx, mask):
        start = pl.multiple_of(kb * tk, tk)
        st = lax.dot_general(k_ref[pl.ds(start, tk), :], q_bd, (((1,), (1,)), ((), ())),
                             preferred_element_type=F32)
        if fox:
            reps = tq // LANES
            f_s = jnp.concatenate([fs_scr[kb]] * reps + [fs_scr[nk + kb]] * reps, axis=1)
            st = st + (f_t - f_s)
        if mask is not None:
            st = jnp.where(mask, st, -jnp.inf)
        st_scr[kb] = st
        return jnp.maximum(mx, jnp.max(st, axis=0, keepdims=True))

    def scores_pair(i, mx):
        return scores(2 * i + 1, scores(2 * i, mx, None), None)

    mx = jnp.full((1, w2), -jnp.inf, F32)
    mx = lax.fori_loop(0, qi // 2, scores_pair, mx)
    mx = lax.fori_loop(0, qi % 2, lambda _, m: scores(qi - 1, m, None), mx)
    mx = scores(qi, mx, diag_mask)

    def probs(kb, l):
        pt = jnp.exp(st_scr[kb] - mx)
        p_scr[kb] = pt.astype(BF16)
        return l + jnp.sum(pt, axis=0, keepdims=True)

    l = lax.fori_loop(0, qi + 1, probs, jnp.zeros((1, w2), F32))

    acc_scr[...] = jnp.zeros_like(acc_scr)

    def values(kb):
        acc_scr[...] += jnp.dot(vt_ref[0, kb], p_scr[kb], preferred_element_type=F32)

    def values_pair(i, carry):
        values(2 * i)
        values(2 * i + 1)
        return carry

    lax.fori_loop(0, (qi + 1) // 2, values_pair, 0)

    @pl.when(qi % 2 == 0)
    def _():
        values(qi)

    acc = acc_scr[...] / l
    sub = lax.broadcasted_iota(jnp.int32, (LANES, tq), 0)
    o_t = jnp.where(sub // 64 == 0, acc[:, :tq], acc[:, tq:])
    o = o_t.T
    if fox:
        o = o * og_ref[...].astype(F32)
    o_ref[...] = o.astype(BF16)


def _attention(q, k, vt, fox_extras=None):
    fox = fox_extras is not None
    tq, tk = ATT_TQ, ATT_TK
    nq = S // tq
    nk = S // tk
    qw = LANES if fox else 2 * LANES
    ops = [q, k, vt]
    specs = [pl.BlockSpec((tq, qw), lambda b, p, i: (b * nq + i, p)),
             pl.BlockSpec((S, qw), lambda b, p, i: (b, p)),
             pl.BlockSpec((1, nk, LANES, tk), lambda b, p, i: (b, 0, p, 0))]
    scratch = [pltpu.VMEM((nk, tk, 2 * tq), F32), pltpu.VMEM((nk, tk, 2 * tq), BF16),
               pltpu.VMEM((LANES, 2 * tq), F32)]
    if fox:
        og, fc, fr = fox_extras
        ops += [og, fc, fr]
        specs += [pl.BlockSpec((tq, LANES), lambda b, p, i: (b * nq + i, p)),
                  pl.BlockSpec((S, LANES), lambda b, p, i: (b, 0)),
                  pl.BlockSpec((1, FOX_HEADS, tq), lambda b, p, i: (b, 0, i))]
        scratch.append(pltpu.VMEM((2 * nk, tk, LANES), F32))
    return pl.pallas_call(
        functools.partial(_attn_kernel, fox),
        grid=(B, 8, nq),
        in_specs=specs,
        out_specs=pl.BlockSpec((tq, LANES), lambda b, p, i: (b * nq + i, p)),
        out_shape=jax.ShapeDtypeStruct((T, D), BF16),
        scratch_shapes=scratch,
        compiler_params=_cparams("parallel", "parallel", "arbitrary"),
        name="fox_attn" if fox else "mla_attn",
    )(*ops)


def _final_norm_kernel(x_ref, g_ref, o_ref):
    x = x_ref[...]
    o_ref[...] = x * lax.rsqrt(jnp.mean(x * x, axis=-1, keepdims=True) + NORM_EPS) * g_ref[...]


def _final_norm(x, g):
    tm = 512
    return pl.pallas_call(
        _final_norm_kernel,
        grid=(T // tm,),
        in_specs=[pl.BlockSpec((tm, D), lambda i: (i, 0)), pl.BlockSpec((1, D), lambda i: (0, 0))],
        out_specs=pl.BlockSpec((tm, D), lambda i: (i, 0)),
        out_shape=jax.ShapeDtypeStruct((T, D), F32),
        compiler_params=_cparams("parallel"),
        name="final_norm",
    )(x, g.reshape(1, D))


def kernel(x, c, positions, ada_w, ada_b, norm_mix_g, norm_ffn_g, final_norm_g, rw_mu, rw_w_rkv, rw_w_o, rw_w0, rw_w1, rw_w2, rw_a0, rw_a1, rw_a2, rw_g1, rw_g2, rw_k_k, rw_k_a, rw_r_k, rw_lnx_g, rw_lnx_b, rw_v0, rw_v1, rw_v2, mla_w_down, mla_q_norm_g, mla_kv_norm_g, mla_w_uq, mla_w_ukv, mla_w_o, fox_w_in, fox_b_f, fox_q_norm_g, fox_k_norm_g, fox_w_o, ffn_w_gate_up, ffn_w_down, moe_w_router, moe_b_router, moe_w_gate_up, moe_w_down):
    xf = x.reshape(T, D)
    mods = _ada_mods(c, ada_w, ada_b)
    ffn_wgu, ffn_wd = ffn_w_gate_up.astype(BF16), ffn_w_down.astype(BF16)
    moe_wgu, moe_wd = moe_w_gate_up.astype(BF16), moe_w_down.astype(BF16)
    v_first = None
    for i in range(DEPTH):
        mod = mods[i]
        kind, j = i % 3, i // 3
        if kind == 0:
            p = dict(mu=rw_mu[j], w_rkv=rw_w_rkv[j], w0=rw_w0[j], w1=rw_w1[j], w2=rw_w2[j],
                     a0=rw_a0[j], a1=rw_a1[j], a2=rw_a2[j], g1=rw_g1[j], g2=rw_g2[j],
                     k_k=rw_k_k[j], k_a=rw_k_a[j])
            if j > 0:
                p.update(v0=rw_v0[j - 1], v1=rw_v1[j - 1], v2=rw_v2[j - 1])
            r, lw, k, v, kk, a, g = _rwkv_proj(xf, mod, norm_mix_g[i], p, v_first if j > 0 else None)
            if j == 0:
                v_first = v
            y = _rwkv_scan(r, lw, k, v, kk, a, g, rw_r_k[j], rw_lnx_g[j], rw_lnx_b[j])
            xf = _proj_res(y, rw_w_o[j], xf, mod, 2)
        elif kind == 1:
            q, k, v = _mla_proj(xf, mod, norm_mix_g[i], positions, mla_w_down[j], mla_q_norm_g[j],
                                mla_kv_norm_g[j], mla_w_uq[j], mla_w_ukv[j])
            o = _attention(q, k, v)
            xf = _proj_res(o, mla_w_o[j], xf, mod, 2)
        else:
            q, k, v, og, fc, fr = _fox_proj(xf, mod, norm_mix_g[i], fox_w_in[j], fox_b_f[j],
                                            fox_q_norm_g[j], fox_k_norm_g[j])
            o = _attention(q, k, v, (og, fc, fr))
            xf = _proj_res(o, fox_w_o[j], xf, mod, 2)
        if i % 2 == 0:
            xf = _ffn_dense(xf, mod, norm_ffn_g[i], ffn_wgu, ffn_wd, i // 2)
        else:
            xf = _moe(xf, mod, norm_ffn_g[i], moe_w_router[i // 2], moe_b_router[i // 2],
                      moe_wgu, moe_wd, i // 2)
    return _final_norm(xf, final_norm_g).reshape(B, S, D)
```

```python
import functools
import math

import jax
import jax.numpy as jnp
from jax import lax
from jax.experimental import pallas as pl
from jax.experimental.pallas import tpu as pltpu

F32 = jnp.float32
BF16 = jnp.bfloat16
HIGHEST = lax.Precision.HIGHEST

D = 1024
B = 8
S = 2048
T = B * S
DEPTH = 4
NORM_EPS = 1e-6

RW_HEAD = 64
GN_EPS = 64e-5
EXP_NEG_HALF = math.exp(-0.5)

MLA_HEADS = 16
MLA_NOPE = 64
MLA_ROPE = 32
MLA_Q_LORA = 768
MLA_KV_LORA = 256
MLA_SCALE = (MLA_NOPE + MLA_ROPE) ** -0.5
ROPE_BASE = 10000.0

FOX_HEADS = 16
FOX_HEAD = 64
FOX_SCALE = FOX_HEAD ** -0.5

D_FF = 2816
N_EXPERTS = 8

LANES = 128
GROUP_SUM_WIDTH = 256
VMEM_LIMIT = 56 * 1024 * 1024

RW_CHUNK = 64
RW_GROUP = 4
RW_PAR = 4
ATT_TQ = 256
ATT_TK = 256
FFN_TM = 1024
FFN_SUB = 512
ROUTE_TM = 512
FFN_TF = 1408
PROJ_TM = 256
MOE_TW = 1024
MOE_TS = 288


def _cparams(*sem):
    return pltpu.CompilerParams(dimension_semantics=sem, vmem_limit_bytes=VMEM_LIMIT)


def _dot(a, b):
    return jnp.dot(a.astype(BF16), b.astype(BF16), preferred_element_type=F32)


def _dot_nt(a, b):
    return lax.dot_general(a.astype(BF16), b.astype(BF16), (((1,), (1,)), ((), ())),
                           preferred_element_type=F32)


def _dot_tn(a, b):
    return lax.dot_general(a.astype(BF16), b.astype(BF16), (((0,), (0,)), ((), ())),
                           preferred_element_type=F32)


def _sigmoid(z):
    return 1.0 / (1.0 + jnp.exp(-z))


def _modnorm(x, g, shift, scale):
    ms = jnp.mean(x * x, axis=-1, keepdims=True)
    y = x * lax.rsqrt(ms + NORM_EPS) * g
    return y * (1.0 + scale) + shift


def _group_sum64(x):
    width = GROUP_SUM_WIDTH
    gi = lax.broadcasted_iota(jnp.int32, (width, width), 0) // 64
    gj = lax.broadcasted_iota(jnp.int32, (width, width), 1) // 64
    ones = jnp.where(gi == gj, 1.0, 0.0).astype(BF16)
    xb = x.astype(BF16)
    outs = [jnp.dot(xb[:, s * width:(s + 1) * width], ones, preferred_element_type=F32)
            for s in range(x.shape[1] // width)]
    return outs[0] if len(outs) == 1 else jnp.concatenate(outs, axis=1)


def _store_transposed(vt_ref, v):
    vt = v.T
    for s in range(v.shape[0] // ATT_TK):
        vt_ref[0, s] = vt[:, s * ATT_TK:(s + 1) * ATT_TK].astype(BF16)


def _vt_spec(tm):
    nb = S // tm
    return pl.BlockSpec((1, tm // ATT_TK, D, ATT_TK), lambda i: (i // nb, i % nb, 0, 0))


def _vt_shape():
    return jax.ShapeDtypeStruct((B, S // ATT_TK, D, ATT_TK), BF16)


def _ada_kernel(c_ref, w_ref, b_ref, o_ref):
    c = c_ref[...]
    cond = c * _sigmoid(c)
    o_ref[0] = _dot(cond, w_ref[0]) + b_ref[0]


def _ada_mods(c, ada_w, ada_b):
    tn = 1536
    out = pl.pallas_call(
        _ada_kernel,
        grid=(DEPTH, 6 * D // tn),
        in_specs=[pl.BlockSpec((B, D), lambda l, j: (0, 0)),
                  pl.BlockSpec((1, D, tn), lambda l, j: (l, 0, j)),
                  pl.BlockSpec((1, 1, tn), lambda l, j: (l, 0, j))],
        out_specs=pl.BlockSpec((1, B, tn), lambda l, j: (l, 0, j)),
        out_shape=jax.ShapeDtypeStruct((DEPTH, B, 6 * D), F32),
        compiler_params=_cparams("parallel", "parallel"),
        name="ada_mods",
    )(c, ada_w, ada_b.reshape(DEPTH, 1, 6 * D))
    return out.reshape(DEPTH, B, 6, D)


def _rwkv_proj_kernel(has_vres, *refs):
    if has_vres:
        (x_ref, xp_ref, mod_ref, ng_ref, mu_ref, wr_ref, wk_ref, wv_ref, w0_ref, w1_ref, w2_ref,
         a0_ref, a1_ref, a2_ref, g1_ref, g2_ref, kk_ref, ka_ref, v0_ref, v1_ref, v2_ref, vf_ref,
         r_out, lw_out, k_out, v_out, kk_out, a_out, g_out) = refs
    else:
        (x_ref, xp_ref, mod_ref, ng_ref, mu_ref, wr_ref, wk_ref, wv_ref, w0_ref, w1_ref, w2_ref,
         a0_ref, a1_ref, a2_ref, g1_ref, g2_ref, kk_ref, ka_ref,
         r_out, lw_out, k_out, v_out, kk_out, a_out, g_out) = refs
    i = pl.program_id(0)
    mod = mod_ref[0]
    shift, scale = mod[0:1], mod[1:2]
    g = ng_ref[...]
    h = _modnorm(x_ref[...], g, shift, scale)
    hp = _modnorm(xp_ref[...], g, shift, scale)
    seq_start = (i % (S // PROJ_TM)) == 0
    prev_row = jnp.where(seq_start, 0.0, hp[7:8, :])
    row = lax.broadcasted_iota(jnp.int32, h.shape, 0)
    prev = jnp.where(row == 0, prev_row, pltpu.roll(h, 1, axis=0))
    delta = prev - h
    mu = mu_ref[...]
    xr = h + delta * mu[0:1]
    xw = h + delta * mu[1:2]
    xk = h + delta * mu[2:3]
    xv = h + delta * mu[3:4]
    xa = h + delta * mu[4:5]
    xg = h + delta * mu[5:6]
    r = _dot(xr, wr_ref[...])
    k = _dot(xk, wk_ref[...])
    v = _dot(xv, wv_ref[...])
    w_raw = w0_ref[...] + _dot(jnp.tanh(_dot(xw, w1_ref[...])), w2_ref[...])
    lw_out[...] = -_sigmoid(w_raw) * EXP_NEG_HALF
    if has_vres:
        mix = _sigmoid(v0_ref[...] + _dot(_dot(xv, v1_ref[...]), v2_ref[...]))
        v = v + (vf_ref[...] - v) * mix
    a = _sigmoid(a0_ref[...] + _dot(_dot(xa, a1_ref[...]), a2_ref[...]))
    g_out[...] = _dot(_sigmoid(_dot(xg, g1_ref[...])), g2_ref[...])
    kk = k * kk_ref[...]
    norm = jnp.sqrt(_group_sum64(kk * kk))
    kk_out[...] = kk / jnp.maximum(norm, 1e-12)
    k_out[...] = k * (1.0 + (a - 1.0) * ka_ref[...])
    r_out[...] = r
    v_out[...] = v
    a_out[...] = a


def _rwkv_proj(x, mod, norm_g, p, v_first):
    has_vres = v_first is not None
    tm = PROJ_TM
    nb = S // tm
    row = lambda a: a.reshape(1, -1)
    full = lambda a: pl.BlockSpec(a.shape, lambda i: (0,) * a.ndim)
    tile = pl.BlockSpec((tm, D), lambda i: (i, 0))
    ops = [x, x, mod, row(norm_g), p["mu"],
           p["w_rkv"][0].astype(BF16), p["w_rkv"][1].astype(BF16), p["w_rkv"][2].astype(BF16),
           row(p["w0"]), p["w1"].astype(BF16), p["w2"].astype(BF16),
           row(p["a0"]), p["a1"].astype(BF16), p["a2"].astype(BF16),
           p["g1"].astype(BF16), p["g2"].astype(BF16), row(p["k_k"]), row(p["k_a"])]
    specs = [tile,
             pl.BlockSpec((8, D), lambda i: (jnp.maximum(i * (tm // 8) - 1, 0), 0)),
             pl.BlockSpec((1, 6, D), lambda i: (i // nb, 0, 0))]
    specs += [full(a) for a in ops[3:]]
    if has_vres:
        extra = [row(p["v0"]), p["v1"].astype(BF16), p["v2"].astype(BF16)]
        ops += extra + [v_first]
        specs += [full(a) for a in extra] + [tile]
    outs = pl.pallas_call(
        functools.partial(_rwkv_proj_kernel, has_vres),
        grid=(T // tm,),
        in_specs=specs,
        out_specs=[tile] * 7,
        out_shape=[jax.ShapeDtypeStruct((T, D), F32)] * 7,
        compiler_params=_cparams("parallel"),
        name="rwkv_proj",
    )(*ops)
    return outs


def _rwkv_scan_kernel(r_ref, lw_ref, k_ref, v_ref, kk_ref, a_ref, g_ref, rk_ref, lg_ref, lb_ref,
                      o_ref, s_ref):
    c = pl.program_id(2)
    W = RW_GROUP * RW_HEAD

    @pl.when(c == 0)
    def _():
        s_ref[...] = jnp.zeros_like(s_ref)

    slabs = [slice(p * W, (p + 1) * W) for p in range(RW_PAR)]
    cols = lambda ref: [ref[:, sl] for sl in slabs]
    out, s_new = _rwkv_blocks(cols(r_ref), cols(lw_ref), cols(k_ref), cols(v_ref), cols(kk_ref),
                              cols(a_ref), cols(g_ref), cols(rk_ref), cols(lg_ref), cols(lb_ref),
                              [s_ref[p] for p in range(RW_PAR)])
    for p, sl in enumerate(slabs):
        o_ref[:, sl] = out[p]
        s_ref[p] = s_new[p]


def _each(fn, *cols):
    return [fn(*args) for args in zip(*cols)]


def _rwkv_blocks(r, lw, k, v, kk, a, g, r_k, lnx_g, lnx_b, s_old):
    C = RW_CHUNK
    W = RW_GROUP * RW_HEAD

    mul = lambda x, y: x * y
    ti = lax.broadcasted_iota(jnp.int32, (C, C), 0)
    si = lax.broadcasted_iota(jnp.int32, (C, C), 1)
    tri = jnp.where(ti >= si, 1.0, 0.0).astype(F32)
    cl = _each(lambda x: jnp.dot(tri, x, precision=HIGHEST, preferred_element_type=F32), lw)
    cl_end = _each(lambda x: x[C - 1:C, :], cl)
    w_t = _each(jnp.exp, cl)
    w_prev = _each(lambda x, y: jnp.exp(x - y), cl, lw)
    w_inv = _each(lambda x: jnp.exp(-x), cl)
    w_rem = _each(lambda x, y: jnp.exp(x - y), cl_end, cl)
    bv = _each(mul, kk, a)
    r_hat = _each(mul, r, w_t)
    a_hat = _each(lambda x, y: -x * y, kk, w_prev)
    b_hat = _each(mul, bv, w_inv)
    k_hat = _each(mul, k, w_inv)
    b_til = _each(mul, bv, w_rem)
    k_til = _each(mul, k, w_rem)

    lane_head = lax.broadcasted_iota(jnp.int32, (C, W), 1) // RW_HEAD

    def stack(m):
        mb = m.astype(BF16)
        zero = jnp.zeros_like(mb)
        return jnp.concatenate([jnp.where(lane_head == hd, mb, zero) for hd in range(RW_GROUP)],
                               axis=0)

    def fold(m):
        return m[0:C] + m[C:2 * C] + m[2 * C:3 * C] + m[3 * C:4 * C]

    cat0 = lambda x, y: jnp.concatenate([x, y], axis=0)
    gram = _each(lambda ah, rh, bh, kh: _dot_nt(cat0(stack(ah), stack(rh)), cat0(stack(bh), stack(kh))),
                 a_hat, r_hat, b_hat, k_hat)
    n = RW_GROUP * C
    ri = lax.broadcasted_iota(jnp.int32, (n, n), 0)
    ci = lax.broadcasted_iota(jnp.int32, (n, n), 1)
    strict = (ri % C) > (ci % C)
    incl = (ri % C) >= (ci % C)
    same_head = (ri // RW_HEAD) == (ci // RW_HEAD)
    eye = jnp.where(ri == ci, 1.0, 0.0)
    l_ab = _each(lambda x: jnp.where(strict, x[:n, :n], 0.0), gram)
    l_ak = _each(lambda x: jnp.where(strict, x[:n, n:], 0.0), gram)
    m_rb = _each(lambda x: jnp.where(incl, x[n:, :n], 0.0), gram)
    m_rk = _each(lambda x: jnp.where(incl, x[n:, n:], 0.0), gram)

    inv = _each(lambda x: eye + x, l_ab)
    pw = l_ab
    for _ in range(int(math.log2(C)) - 1):
        pw = _each(lambda x: _dot(x, x), pw)
        inv = _each(lambda x, y: x + _dot(x, y), inv, pw)

    a_p = _each(lambda x, y: fold(_dot(x, stack(y))), inv, a_hat)
    t_l = _each(_dot, inv, l_ak)
    v_p = _each(lambda x, y: fold(_dot(x, stack(y))), t_l, v)
    r_p = _each(lambda x, y, z: x + fold(_dot(y, stack(z))), r_hat, m_rb, a_p)
    y0 = _each(lambda mb, mk, vp, vv: fold(_dot(jnp.concatenate([mb, mk], axis=1),
                                                cat0(stack(vp), stack(vv)))), m_rb, m_rk, v_p, v)
    a_til = _each(lambda x, y: jnp.where(same_head, _dot_tn(x, y), 0.0), b_til, a_p)
    d_new = _each(lambda vp, vv, bt, kt: jnp.where(same_head, _dot_tn(cat0(vp, vv), cat0(bt, kt)), 0.0),
                  v_p, v, b_til, k_til)
    y = _each(lambda x, s, z: _dot_nt(x, s) + z, r_p, s_old, y0)
    s_new = _each(lambda s, ce, at, dn: s * jnp.exp(ce) + _dot_nt(s, at) + dn,
                  s_old, cl_end, a_til, d_new)

    inv_n = 1.0 / RW_HEAD
    mean = _each(lambda x: _group_sum64(x) * inv_n, y)
    yc = _each(lambda x, m: x - m, y, mean)
    var = _each(lambda x: _group_sum64(x * x) * inv_n, yc)
    yn = _each(lambda x, vr, lg, lb: x * lax.rsqrt(vr + GN_EPS) * lg + lb, yc, var, lnx_g, lnx_b)
    bonus = _each(lambda rr, kx, rk, vv: _group_sum64(rr * kx * rk) * vv, r, k, r_k, v)
    out = _each(lambda x, bo, gg: ((x + bo) * gg).astype(BF16), yn, bonus, g)
    return out, s_new


def _rwkv_scan(r, lw, k, v, kk, a, g, r_k, lnx_g, lnx_b):
    C = RW_CHUNK
    W = RW_PAR * RW_GROUP * RW_HEAD
    nc = S // C
    slab = pl.BlockSpec((C, W), lambda b, gi, c: (b * nc + c, gi))
    prow = pl.BlockSpec((1, W), lambda b, gi, c: (0, gi))
    return pl.pallas_call(
        _rwkv_scan_kernel,
        grid=(B, D // W, nc),
        in_specs=[slab] * 7 + [prow] * 3,
        out_specs=slab,
        out_shape=jax.ShapeDtypeStruct((T, D), BF16),
        scratch_shapes=[pltpu.VMEM((RW_PAR, RW_GROUP * RW_HEAD, RW_GROUP * RW_HEAD), F32)],
        compiler_params=_cparams("parallel", "parallel", "arbitrary"),
        name="rwkv_scan",
    )(r, lw, k, v, kk, a, g, r_k.reshape(1, D), lnx_g.reshape(1, D), lnx_b.reshape(1, D))


def _proj_res_kernel(gate_row, a_ref, w_ref, x_ref, mod_ref, o_ref):
    y = jnp.dot(a_ref[...], w_ref[...], preferred_element_type=F32)
    o_ref[...] = x_ref[...] + mod_ref[0][gate_row:gate_row + 1] * y


def _proj_res(a, w, x, mod, gate_row):
    tm = 512
    nb = S // tm
    kdim = a.shape[1]
    return pl.pallas_call(
        functools.partial(_proj_res_kernel, gate_row),
        grid=(T // tm,),
        in_specs=[pl.BlockSpec((tm, kdim), lambda i: (i, 0)),
                  pl.BlockSpec((kdim, D), lambda i: (0, 0)),
                  pl.BlockSpec((tm, D), lambda i: (i, 0)),
                  pl.BlockSpec((1, 6, D), lambda i: (i // nb, 0, 0))],
        out_specs=pl.BlockSpec((tm, D), lambda i: (i, 0)),
        out_shape=jax.ShapeDtypeStruct((T, D), F32),
        compiler_params=_cparams("parallel"),
        name="proj_res",
    )(a, w.astype(BF16), x, mod)


def _ffn_kernel(nf, x_ref, mod_ref, ng_ref, wg_ref, wu_ref, wd_ref, o_ref, h_scr, acc_scr):
    f = pl.program_id(1)

    @pl.when(f == 0)
    def _():
        mod = mod_ref[0]
        h_scr[...] = _modnorm(x_ref[...], ng_ref[...], mod[3:4], mod[4:5]).astype(BF16)
        acc_scr[...] = jnp.zeros_like(acc_scr)

    for sub in range(h_scr.shape[0] // FFN_SUB):
        rows = slice(sub * FFN_SUB, (sub + 1) * FFN_SUB)
        h = h_scr[rows]
        gt = jnp.dot(h, wg_ref[0], preferred_element_type=F32)
        up = jnp.dot(h, wu_ref[0], preferred_element_type=F32)
        act = (gt * _sigmoid(gt) * up).astype(BF16)
        acc_scr[rows] += jnp.dot(act, wd_ref[0], preferred_element_type=F32)

    @pl.when(f == nf - 1)
    def _():
        o_ref[...] = x_ref[...] + mod_ref[0][5:6] * acc_scr[...]


def _ffn_dense(x, mod, norm_g, wgu, wd, layer):
    tm, tf = FFN_TM, FFN_TF
    nf = D_FF // tf
    nb = S // tm
    return pl.pallas_call(
        functools.partial(_ffn_kernel, nf),
        grid=(T // tm, nf),
        in_specs=[pl.BlockSpec((tm, D), lambda i, f: (i, 0)),
                  pl.BlockSpec((1, 6, D), lambda i, f: (i // nb, 0, 0)),
                  pl.BlockSpec((1, D), lambda i, f: (0, 0)),
                  pl.BlockSpec((1, D, tf), lambda i, f: (layer, 0, f)),
                  pl.BlockSpec((1, D, tf), lambda i, f: (layer, 0, nf + f)),
                  pl.BlockSpec((1, tf, D), lambda i, f: (layer, f, 0))],
        out_specs=pl.BlockSpec((tm, D), lambda i, f: (i, 0)),
        out_shape=jax.ShapeDtypeStruct((T, D), F32),
        scratch_shapes=[pltpu.VMEM((tm, D), BF16), pltpu.VMEM((tm, D), F32)],
        compiler_params=_cparams("parallel", "arbitrary"),
        name="ffn_dense",
    )(x, mod, norm_g.reshape(1, D), wgu, wgu, wd)


def _route_kernel(x_ref, mod_ref, ng_ref, wr_ref, br_ref,
                  h_out, cw_t_out, dest_t_out, cnt_out, carry_scr):
    i = pl.program_id(0)
    tm = x_ref.shape[0]
    mod = mod_ref[0]
    h = _modnorm(x_ref[...], ng_ref[...], mod[3:4], mod[4:5])
    h_out[...] = h.astype(BF16)
    logits = jnp.dot(h, wr_ref[...], precision=HIGHEST, preferred_element_type=F32) + br_ref[...]
    lane = lax.broadcasted_iota(jnp.int32, logits.shape, 1)
    neg = -jnp.inf
    logits = jnp.where(lane < N_EXPERTS, logits, neg)
    m1 = jnp.max(logits, axis=-1, keepdims=True)
    i1 = jnp.min(jnp.where(logits == m1, lane, LANES), axis=-1, keepdims=True)
    rest = jnp.where(lane == i1, neg, logits)
    m2 = jnp.max(rest, axis=-1, keepdims=True)
    i2 = jnp.min(jnp.where(rest == m2, lane, LANES), axis=-1, keepdims=True)
    e2 = jnp.exp(m2 - m1)
    w1 = 1.0 / (1.0 + e2)
    w2 = e2 / (1.0 + e2)
    cw_t_out[...] = (jnp.where(lane == i1, w1, 0.0) + jnp.where(lane == i2, w2, 0.0)).T

    sel =jnp.where((lane == i1) | (lane == i2), 1.0, 0.0)
    ti = lax.broadcasted_iota(jnp.int32, (tm, tm), 0)
    si = lax.broadcasted_iota(jnp.int32, (tm, tm), 1)
    before = jnp.where(ti > si, 1.0, 0.0).astype(BF16)
    window_start = (i % (MOE_TW // tm)) == 0
    carry = jnp.where(window_start, 0.0, carry_scr[...])
    rank = jnp.dot(before, sel.astype(BF16), preferred_element_type=F32) + carry
    dest_t_out[...] = jnp.where(sel > 0.0, rank, -1.0).T
    total = carry + jnp.sum(sel, axis=0, keepdims=True)
    carry_scr[...] = total
    cnt_out[0] = jnp.broadcast_to(total, (8, LANES))


def _moe_kernel(nf, cnt_ref, x_ref, mod_ref, h_ref, dest_t_ref, cw_t_ref, wg_ref, wu_ref, wd_ref,
                o_ref, xg_scr, yc_scr):
    w = pl.program_id(0)
    e = pl.program_id(1)
    f = pl.program_id(2)
    tw, ts = MOE_TW, MOE_TS
    n = cnt_ref[w * N_EXPERTS + e]
    n_sub = (n + ts - 1) // ts

    @pl.when((e == 0) & (f == 0))
    def _():
        o_ref[...] = jnp.zeros_like(o_ref)

    dest_row = dest_t_ref[pl.ds(e, 1), :]
    cw_row = cw_t_ref[pl.ds(e, 1), :]
    slot_sub = lax.broadcasted_iota(jnp.int32, (ts, tw), 0).astype(F32)

    def sub_tile(s, carry):
        match = dest_row == slot_sub + (s * ts).astype(F32)
        select = jnp.where(match, 1.0, 0.0).astype(BF16)

        @pl.when(f == 0)
        def _():
            xg_scr[s] = jnp.dot(select, h_ref[...], preferred_element_type=F32).astype(BF16)

        xs = xg_scr[s]
        gt = jnp.dot(xs, wg_ref[0, 0, 0], preferred_element_type=F32)
        up = jnp.dot(xs, wu_ref[0, 0, 0], preferred_element_type=F32)
        act = (gt * _sigmoid(gt) * up).astype(BF16)
        y = jnp.dot(act, wd_ref[0, 0], preferred_element_type=F32)

        @pl.when(f == 0)
        def _():
            yc_scr[s] = y

        @pl.when((f > 0) & (f < nf - 1))
        def _():
            yc_scr[s] += y

        @pl.when(f == nf - 1)
        def _():
            w_slot = jnp.sum(jnp.where(match, cw_row, 0.0), axis=-1, keepdims=True)
            weighted = ((yc_scr[s] + y) * w_slot).astype(BF16)
            o_ref[...] += _dot_tn(select, weighted)
        return carry

    lax.fori_loop(0, n_sub, sub_tile, 0)

    @pl.when((e == N_EXPERTS - 1) & (f == nf - 1))
    def _():
        o_ref[...] = x_ref[...] + mod_ref[0][5:6] * o_ref[...]


def _moe(x, mod, norm_g, w_router, b_router, wgu, wd, layer):
    tm, tf, tw, ts = ROUTE_TM, FFN_TF, MOE_TW, MOE_TS
    nf = D_FF // tf
    nb = S // tm
    nw = T // tw
    wr =jnp.zeros((D, LANES), F32).at[:, :N_EXPERTS].set(w_router)
    br = jnp.zeros((1, LANES), F32).at[0, :N_EXPERTS].set(b_router)
    tile = lambda width: pl.BlockSpec((tm, width), lambda i: (i, 0))
    assert nf >= 2
    lanes_t = pl.BlockSpec((LANES, tm), lambda i: (0, i))
    h, cw_t, dest_t, cnt = pl.pallas_call(
        _route_kernel,
        grid=(T // tm,),
        in_specs=[tile(D),
                  pl.BlockSpec((1, 6, D), lambda i: (i // nb, 0, 0)),
                  pl.BlockSpec((1, D), lambda i: (0, 0)),
                  pl.BlockSpec((D, LANES), lambda i: (0, 0)),
                  pl.BlockSpec((1, LANES), lambda i: (0, 0))],
        out_specs=[tile(D), lanes_t, lanes_t,
                   pl.BlockSpec((1, 8, LANES), lambda i: (i // (tw // tm), 0, 0))],
        out_shape=[jax.ShapeDtypeStruct((T, D), BF16), jax.ShapeDtypeStruct((LANES, T), F32),
                   jax.ShapeDtypeStruct((LANES, T), F32), jax.ShapeDtypeStruct((nw, 8, LANES), F32)],
        scratch_shapes=[pltpu.VMEM((1, LANES), F32)],
        compiler_params=_cparams("arbitrary"),
        name="moe_route",
    )(x, mod, norm_g.reshape(1, D), wr, br)
    counts = cnt[:, 0, :N_EXPERTS].astype(jnp.int32).reshape(nw * N_EXPERTS)
    n_slots = -(-tw // ts)
    return pl.pallas_call(
        functools.partial(_moe_kernel, nf),
        grid_spec=pltpu.PrefetchScalarGridSpec(
            num_scalar_prefetch=1,
            grid=(nw, N_EXPERTS, nf),
            in_specs=[pl.BlockSpec((tw, D), lambda w, e, f, c: (w, 0)),
                      pl.BlockSpec((1, 6, D), lambda w, e, f, c: (w // (S // tw), 0, 0)),
                      pl.BlockSpec((tw, D), lambda w, e, f, c: (w, 0)),
                      pl.BlockSpec((LANES, tw), lambda w, e, f, c: (0, w)),
                      pl.BlockSpec((LANES, tw), lambda w, e, f, c: (0, w)),
                      pl.BlockSpec((1, 1, 1, D, tf), lambda w, e, f, c: (layer, e, f, 0, 0)),
                      pl.BlockSpec((1, 1, 1, D, tf), lambda w, e, f, c: (layer, e, nf + f, 0, 0)),
                      pl.BlockSpec((1, 1, tf, D), lambda w, e, f, c: (layer, e, f, 0))],
            out_specs=pl.BlockSpec((tw, D), lambda w, e, f, c: (w, 0)),
            scratch_shapes=[pltpu.VMEM((n_slots, ts, D), BF16), pltpu.VMEM((n_slots, ts, D), F32)]),
        out_shape=jax.ShapeDtypeStruct((T, D), F32),
        compiler_params=_cparams("parallel", "arbitrary", "arbitrary"),
        name="moe_experts",
    )(counts, x, mod, h, dest_t, cw_t, wgu, wgu, wd)


def _mla_proj_kernel(x_ref, mod_ref, ng_ref, pos_ref, invf_ref, wd_ref, qg_ref, kvg_ref,
                     wuq_ref, wukv_ref, q_out, k_out, v_out):
    mod = mod_ref[0]
    h = _modnorm(x_ref[...], ng_ref[...], mod[0:1], mod[1:2])
    down = _dot(h, wd_ref[...])
    cq = down[:, :MLA_Q_LORA]
    ckv = down[:, MLA_Q_LORA:MLA_Q_LORA + MLA_KV_LORA]
    kr = down[:, MLA_Q_LORA + MLA_KV_LORA:]
    cq = cq * lax.rsqrt(jnp.mean(cq * cq, axis=-1, keepdims=True) + NORM_EPS) * qg_ref[...]
    ckv = ckv * lax.rsqrt(jnp.mean(ckv * ckv, axis=-1, keepdims=True) + NORM_EPS) * kvg_ref[...]

    ang = pos_ref[...] * invf_ref[...]
    lane = lax.broadcasted_iota(jnp.int32, ang.shape, 1)
    is_rope = (lane >= MLA_NOPE) & (lane < MLA_NOPE + MLA_ROPE)
    cos_r = jnp.where(is_rope, jnp.cos(ang), 0.0)
    sin_r = jnp.where(is_rope, jnp.sin(ang), 0.0)
    cos_q = jnp.where(lane < MLA_NOPE, 1.0, cos_r)
    shift = LANES - MLA_ROPE

    k_rope = kr * cos_r + pltpu.roll(kr, shift, axis=1) * sin_r

    qf = _dot(cq, wuq_ref[...])
    kv = _dot(ckv, wukv_ref[...])
    for hd in range(MLA_HEADS):
        sl = slice(hd * LANES, (hd + 1) * LANES)
        qs = qf[:, sl]
        q_out[:, sl] = ((qs * cos_q + pltpu.roll(qs, shift, axis=1) * sin_r) * MLA_SCALE).astype(BF16)
        k_out[:, sl] = (kv[:, sl] + k_rope).astype(BF16)
    _store_transposed(v_out, kv[:, MLA_HEADS * LANES:])


def _rot_half_cols(w):
    half = MLA_ROPE // 2
    return jnp.concatenate([-w[..., half:], w[..., :half]], axis=-1)


def _mla_proj(x, mod, norm_g, positions, w_down, q_norm_g, kv_norm_g, w_uq, w_ukv):
    tm = PROJ_TM
    nb = S // tm
    half = MLA_ROPE // 2
    wr = w_down[:, MLA_Q_LORA + MLA_KV_LORA:]
    wd = jnp.concatenate([w_down[:, :MLA_Q_LORA + MLA_KV_LORA],
                          jnp.zeros((D, MLA_NOPE), F32), wr, _rot_half_cols(wr)], axis=1).astype(BF16)
    wq = w_uq.reshape(MLA_Q_LORA, MLA_HEADS, MLA_NOPE + MLA_ROPE)
    wq_r = wq[..., MLA_NOPE:]
    wuq = jnp.concatenate([wq[..., :MLA_NOPE], wq_r, _rot_half_cols(wq_r)], axis=-1)
    wuq = wuq.reshape(MLA_Q_LORA, MLA_HEADS * LANES).astype(BF16)
    wkv = w_ukv.reshape(MLA_KV_LORA, MLA_HEADS, 2 * MLA_NOPE)
    wk = jnp.concatenate([wkv[..., :MLA_NOPE], jnp.zeros_like(wkv[..., :MLA_NOPE])], axis=-1)
    wukv = jnp.concatenate([wk.reshape(MLA_KV_LORA, MLA_HEADS * LANES),
                            wkv[..., MLA_NOPE:].reshape(MLA_KV_LORA, D)], axis=1).astype(BF16)
    inv_freq = ROPE_BASE ** (-jnp.arange(half, dtype=F32) / half)
    lane = jnp.arange(LANES)
    invf = jnp.where((lane >= MLA_NOPE) & (lane < MLA_NOPE + MLA_ROPE),
                     inv_freq[(lane - MLA_NOPE) % half], 0.0).reshape(1, LANES).astype(F32)
    pos = jnp.broadcast_to(positions.reshape(T, 1).astype(F32), (T, LANES))
    full = lambda a: pl.BlockSpec(a.shape, lambda i: (0,) * a.ndim)
    ops = [x, mod, norm_g.reshape(1, D), pos, invf, wd, q_norm_g.reshape(1, -1),
           kv_norm_g.reshape(1, -1), wuq, wukv]
    specs = [pl.BlockSpec((tm, D), lambda i: (i, 0)),
             pl.BlockSpec((1, 6, D), lambda i: (i // nb, 0, 0)),
             full(ops[2]),
             pl.BlockSpec((tm, LANES), lambda i: (i, 0))] + [full(a) for a in ops[4:]]
    qw = MLA_HEADS * LANES
    return pl.pallas_call(
        _mla_proj_kernel,
        grid=(T // tm,),
        in_specs=specs,
        out_specs=[pl.BlockSpec((tm, qw), lambda i: (i, 0)),
                   pl.BlockSpec((tm, qw), lambda i: (i, 0)),
                   _vt_spec(tm)],
        out_shape=[jax.ShapeDtypeStruct((T, qw), BF16), jax.ShapeDtypeStruct((T, qw), BF16),
                   _vt_shape()],
        compiler_params=_cparams("parallel"),
        name="mla_proj",
    )(*ops)


def _fox_proj_kernel(x_ref, mod_ref, ng_ref, w_ref, wf_ref, bf_ref, qg_ref, kg_ref,
                     q_out, k_out, v_out, og_out, fc_out, fr_out, carry_scr):
    i = pl.program_id(0)
    tm = x_ref.shape[0]
    mod = mod_ref[0]
    h = _modnorm(x_ref[...], ng_ref[...], mod[0:1], mod[1:2]).astype(BF16)
    q = jnp.dot(h, w_ref[:, 0:D], preferred_element_type=F32)
    k = jnp.dot(h, w_ref[:, D:2 * D], preferred_element_type=F32)
    v = jnp.dot(h, w_ref[:, 2 * D:3 * D], preferred_element_type=F32)
    og = jnp.dot(h, w_ref[:, 3 * D:4 * D], preferred_element_type=F32)
    inv_n = 1.0 / FOX_HEAD
    q = q * lax.rsqrt(_group_sum64(q * q) * inv_n + NORM_EPS) * qg_ref[...] * FOX_SCALE
    k = k * lax.rsqrt(_group_sum64(k * k) * inv_n + NORM_EPS) * kg_ref[...]
    q_out[...] = q.astype(BF16)
    k_out[...] = k.astype(BF16)
    _store_transposed(v_out, v)
    og_out[...] = _sigmoid(og).astype(BF16)

    z = jnp.dot(h, wf_ref[...], preferred_element_type=F32) + bf_ref[...]
    lane = lax.broadcasted_iota(jnp.int32, z.shape, 1)
    log_f = jnp.minimum(z, 0.0) - jnp.log(1.0 + jnp.exp(-jnp.abs(z)))
    log_f = jnp.where(lane < FOX_HEADS, log_f, 0.0)
    ti = lax.broadcasted_iota(jnp.int32, (tm, tm), 0)
    si = lax.broadcasted_iota(jnp.int32, (tm, tm), 1)
    tri = jnp.where(ti >= si, 1.0, 0.0).astype(F32)
    seq_start = (i % (S // tm)) == 0
    carry = jnp.where(seq_start, 0.0, carry_scr[...])
    cum = jnp.dot(tri, log_f, precision=HIGHEST, preferred_element_type=F32) + carry
    carry_scr[...] = cum[tm - 1:tm, :]
    fc_out[...] = cum
    fr_out[0] = cum.T


def _fox_proj(x, mod, norm_g, w_in, b_f, q_norm_g, k_norm_g):
    tm = PROJ_TM
    nb = S // tm
    w_main = jnp.concatenate([w_in[:, :3 * D], w_in[:, 3 * D + FOX_HEADS:]], axis=1).astype(BF16)
    w_f = jnp.zeros((D, LANES), F32).at[:, :FOX_HEADS].set(w_in[:, 3 * D:3 * D + FOX_HEADS]).astype(BF16)
    bf = jnp.zeros((1, LANES), F32).at[0, :FOX_HEADS].set(b_f)
    qg = jnp.tile(q_norm_g, FOX_HEADS).reshape(1, D)
    kg = jnp.tile(k_norm_g, FOX_HEADS).reshape(1, D)
    full = lambda a: pl.BlockSpec(a.shape, lambda i: (0,) * a.ndim)
    tile = pl.BlockSpec((tm, D), lambda i: (i, 0))
    ops = [x, mod, norm_g.reshape(1, D), w_main, w_f, bf, qg, kg]
    specs = [tile, pl.BlockSpec((1, 6, D), lambda i: (i // nb, 0, 0))] + [full(a) for a in ops[2:]]
    return pl.pallas_call(
        _fox_proj_kernel,
        grid=(T // tm,),
        in_specs=specs,
        out_specs=[tile, tile, _vt_spec(tm), tile,
                   pl.BlockSpec((tm, LANES), lambda i: (i, 0)),
                   pl.BlockSpec((1, LANES, tm), lambda i: (i // nb, 0, i % nb))],
        out_shape=[jax.ShapeDtypeStruct((T, D), BF16)] * 2 + [_vt_shape(), jax.ShapeDtypeStruct((T, D), BF16),
                   jax.ShapeDtypeStruct((T, LANES), F32), jax.ShapeDtypeStruct((B, LANES, S), F32)],
        scratch_shapes=[pltpu.VMEM((1, LANES), F32)],
        compiler_params=_cparams("arbitrary"),
        name="fox_proj",
    )(*ops)


def _attn_kernel(fox, *refs):
    if fox:
        q_ref, k_ref, vt_ref, og_ref, fc_ref, fr_ref, o_ref, st_scr, p_scr, acc_scr, fs_scr = refs
    else:
        q_ref, k_ref, vt_ref, o_ref, st_scr, p_scr, acc_scr = refs
    tq, tk = ATT_TQ, ATT_TK
    nk = S // tk
    w2 = 2 * tq
    p = pl.program_id(1)
    qi = pl.program_id(2)

    if fox:
        @pl.when(qi == 0)
        def _():
            fcv = fc_ref[...]
            lane_s = lax.broadcasted_iota(jnp.int32, fcv.shape, 1)
            for j in range(2):
                colv = jnp.sum(jnp.where(lane_s == 2 * p + j, fcv, 0.0), axis=-1, keepdims=True)
                full = jnp.broadcast_to(colv, fcv.shape)
                for kb in range(nk):
                    fs_scr[j * nk + kb] = full[kb * tk:(kb + 1) * tk]

    qb = q_ref[...]
    lane = lax.broadcasted_iota(jnp.int32, qb.shape, 1)
    per_head = FOX_HEAD if fox else LANES
    zero = jnp.zeros_like(qb)
    q_bd = jnp.concatenate([jnp.where(lane // per_head == j, qb, zero) for j in range(2)], axis=0)
    if fox:
        f_t = jnp.concatenate([fr_ref[0, pl.ds(2 * p + j, 1), :] for j in range(2)], axis=1)

    srow = lax.broadcasted_iota(jnp.int32, (tk, w2), 0)
    tcol = lax.broadcasted_iota(jnp.int32, (tk, w2), 1) % tq
    diag_mask = (srow <= tcol) if fox else ((srow // 64) <= (tcol // 64))

    def scores(kb, mx, mask):
        start = pl.multiple_of(kb * tk, tk)
        st = lax.dot_general(k_ref[pl.ds(start, tk), :], q_bd, (((1,), (1,)), ((), ())),
                             preferred_element_type=F32)
        if fox:
            reps = tq // LANES
            f_s = jnp.concatenate([fs_scr[kb]] * reps + [fs_scr[nk + kb]] * reps, axis=1)
            st = st + (f_t - f_s)
        if mask is not None:
            st = jnp.where(mask, st, -jnp.inf)
        st_scr[kb] = st
        return jnp.maximum(mx, jnp.max(st, axis=0, keepdims=True))

    def scores_pair(i, mx):
        return scores(2 * i + 1, scores(2 * i, mx, None), None)

    mx = jnp.full((1, w2), -jnp.inf, F32)
    mx = lax.fori_loop(0, qi // 2, scores_pair, mx)
    mx = lax.fori_loop(0, qi % 2, lambda _, m: scores(qi - 1, m, None), mx)
    mx = scores(qi, mx, diag_mask)

    def probs(kb, l):
        pt = jnp.exp(st_scr[kb] - mx)
        p_scr[kb] = pt.astype(BF16)
        return l + jnp.sum(pt, axis=0, keepdims=True)

    l = lax.fori_loop(0, qi + 1, probs, jnp.zeros((1, w2), F32))

    acc_scr[...] = jnp.zeros_like(acc_scr)

    def values(kb):
        acc_scr[...] += jnp.dot(vt_ref[0, kb], p_scr[kb], preferred_element_type=F32)

    def values_pair(i, carry):
        values(2 * i)
        values(2 * i + 1)
        return carry

    lax.fori_loop(0, (qi + 1) // 2, values_pair, 0)

    @pl.when(qi % 2 == 0)
    def _():
        values(qi)

    acc = acc_scr[...] / l
    sub = lax.broadcasted_iota(jnp.int32, (LANES, tq), 0)
    o_t = jnp.where(sub // 64 == 0, acc[:, :tq], acc[:, tq:])
    o = o_t.T
    if fox:
        o = o * og_ref[...].astype(F32)
    o_ref[...] = o.astype(BF16)


def _attention(q, k, vt, fox_extras=None):
    fox = fox_extras is not None
    tq, tk = ATT_TQ, ATT_TK
    nq = S // tq
    nk = S // tk
    qw = LANES if fox else 2 * LANES
    ops = [q, k, vt]
    specs = [pl.BlockSpec((tq, qw), lambda b, p, i: (b * nq + i, p)),
             pl.BlockSpec((S, qw), lambda b, p, i: (b, p)),
             pl.BlockSpec((1, nk, LANES, tk), lambda b, p, i: (b, 0, p, 0))]
    scratch = [pltpu.VMEM((nk, tk, 2 * tq), F32), pltpu.VMEM((nk, tk, 2 * tq), BF16),
               pltpu.VMEM((LANES, 2 * tq), F32)]
    if fox:
        og, fc, fr = fox_extras
        ops += [og, fc, fr]
        specs += [pl.BlockSpec((tq, LANES), lambda b, p, i: (b * nq + i, p)),
                  pl.BlockSpec((S, LANES), lambda b, p, i: (b, 0)),
                  pl.BlockSpec((1, FOX_HEADS, tq), lambda b, p, i: (b, 0, i))]
        scratch.append(pltpu.VMEM((2 * nk, tk, LANES), F32))
    return pl.pallas_call(
        functools.partial(_attn_kernel, fox),
        grid=(B, 8, nq),
        in_specs=specs,
        out_specs=pl.BlockSpec((tq, LANES), lambda b, p, i: (b * nq + i, p)),
        out_shape=jax.ShapeDtypeStruct((T, D), BF16),
        scratch_shapes=scratch,
        compiler_params=_cparams("parallel", "parallel", "arbitrary"),
        name="fox_attn" if fox else "mla_attn",
    )(*ops)


def _final_norm_kernel(x_ref, g_ref, o_ref):
    x = x_ref[...]
    o_ref[...] = x * lax.rsqrt(jnp.mean(x * x, axis=-1, keepdims=True) + NORM_EPS) * g_ref[...]


def _final_norm(x, g):
    tm = 512
    return pl.pallas_call(
        _final_norm_kernel,
        grid=(T // tm,),
        in_specs=[pl.BlockSpec((tm, D), lambda i: (i, 0)), pl.BlockSpec((1, D), lambda i: (0, 0))],
        out_specs=pl.BlockSpec((tm, D), lambda i: (i, 0)),
        out_shape=jax.ShapeDtypeStruct((T, D), F32),
        compiler_params=_cparams("parallel"),
        name="final_norm",
    )(x, g.reshape(1, D))


def kernel(x, c, positions, ada_w, ada_b, norm_mix_g, norm_ffn_g, final_norm_g, rw_mu, rw_w_rkv, rw_w_o, rw_w0, rw_w1, rw_w2, rw_a0, rw_a1, rw_a2, rw_g1, rw_g2, rw_k_k, rw_k_a, rw_r_k, rw_lnx_g, rw_lnx_b, rw_v0, rw_v1, rw_v2, mla_w_down, mla_q_norm_g, mla_kv_norm_g, mla_w_uq, mla_w_ukv, mla_w_o, fox_w_in, fox_b_f, fox_q_norm_g, fox_k_norm_g, fox_w_o, ffn_w_gate_up, ffn_w_down, moe_w_router, moe_b_router, moe_w_gate_up, moe_w_down):
    xf = x.reshape(T, D)
    mods = _ada_mods(c, ada_w, ada_b)
    ffn_wgu, ffn_wd = ffn_w_gate_up.astype(BF16), ffn_w_down.astype(BF16)
    n_chunks = 2 * D_FF // FFN_TF
    moe_wgu = moe_w_gate_up.astype(BF16).reshape(-1, N_EXPERTS, D, n_chunks, FFN_TF)
    moe_wgu = jnp.transpose(moe_wgu, (0, 1, 3, 2, 4))
    moe_wd = moe_w_down.astype(BF16)
    v_first = None
    for i in range(DEPTH):
        mod = mods[i]
        kind, j = i % 3, i // 3
        if kind == 0:
            p = dict(mu=rw_mu[j], w_rkv=rw_w_rkv[j], w0=rw_w0[j], w1=rw_w1[j], w2=rw_w2[j],
                     a0=rw_a0[j], a1=rw_a1[j], a2=rw_a2[j], g1=rw_g1[j], g2=rw_g2[j],
                     k_k=rw_k_k[j], k_a=rw_k_a[j])
            if j > 0:
                p.update(v0=rw_v0[j - 1], v1=rw_v1[j - 1], v2=rw_v2[j - 1])
            r, lw, k, v, kk, a, g = _rwkv_proj(xf, mod, norm_mix_g[i], p, v_first if j > 0 else None)
            if j == 0:
                v_first = v
            y = _rwkv_scan(r, lw, k, v, kk, a, g, rw_r_k[j], rw_lnx_g[j], rw_lnx_b[j])
            xf = _proj_res(y, rw_w_o[j], xf, mod, 2)
        elif kind == 1:
            q, k, v = _mla_proj(xf, mod, norm_mix_g[i], positions, mla_w_down[j], mla_q_norm_g[j],
                                mla_kv_norm_g[j], mla_w_uq[j], mla_w_ukv[j])
            o = _attention(q, k, v)
            xf = _proj_res(o, mla_w_o[j], xf, mod, 2)
        else:
            q, k, v, og, fc, fr = _fox_proj(xf, mod, norm_mix_g[i], fox_w_in[j], fox_b_f[j],
                                            fox_q_norm_g[j], fox_k_norm_g[j])
            o = _attention(q, k, v, (og, fc, fr))
            xf = _proj_res(o, fox_w_o[j], xf, mod, 2)
        if i % 2 == 0:
            xf = _ffn_dense(xf, mod, norm_ffn_g[i], ffn_wgu, ffn_wd, i // 2)
        else:
            xf = _moe(xf, mod, norm_ffn_g[i], moe_w_router[i // 2], moe_b_router[i // 2],
                      moe_wgu, moe_wd, i // 2)
    return _final_norm(xf, final_norm_g).reshape(B, S, D)
```

```python
import functools
import math

import jax
import jax.numpy as jnp
from jax import lax
from jax.experimental import pallas as pl
from jax.experimental.pallas import tpu as pltpu

F32 = jnp.float32
BF16 = jnp.bfloat16
HIGHEST = lax.Precision.HIGHEST

D = 1024
B = 8
S = 2048
T = B * S
DEPTH = 4
NORM_EPS = 1e-6

RW_HEAD = 64
GN_EPS = 64e-5
EXP_NEG_HALF = math.exp(-0.5)

MLA_HEADS = 16
MLA_NOPE = 64
MLA_ROPE = 32
MLA_Q_LORA = 768
MLA_KV_LORA = 256
MLA_SCALE = (MLA_NOPE + MLA_ROPE) ** -0.5
ROPE_BASE = 10000.0

FOX_HEADS = 16
FOX_HEAD = 64
FOX_SCALE = FOX_HEAD ** -0.5

D_FF = 2816
N_EXPERTS = 8

LANES = 128
GROUP_SUM_WIDTH = 256
VMEM_LIMIT = 56 * 1024 * 1024

RW_CHUNK = 64
RW_GROUP = 4
RW_PAR = 4
ATT_TQ = 256
FFN_TM = 1024
FFN_SUB = 512
ROUTE_TM = 512
FFN_TF = 1408
PROJ_TM = 256
MOE_TW = 1024
MOE_TS = 288


def _cparams(*sem):
    return pltpu.CompilerParams(dimension_semantics=sem, vmem_limit_bytes=VMEM_LIMIT)


def _dot(a, b):
    return jnp.dot(a.astype(BF16), b.astype(BF16), preferred_element_type=F32)


def _dot_nt(a, b):
    return lax.dot_general(a.astype(BF16), b.astype(BF16), (((1,), (1,)), ((), ())),
                           preferred_element_type=F32)


def _dot_tn(a, b):
    return lax.dot_general(a.astype(BF16), b.astype(BF16), (((0,), (0,)), ((), ())),
                           preferred_element_type=F32)


def _sigmoid(z):
    return 1.0 / (1.0 + jnp.exp(-z))


def _modnorm(x, g, shift, scale):
    ms = jnp.mean(x * x, axis=-1, keepdims=True)
    y = x * lax.rsqrt(ms + NORM_EPS) * g
    return y * (1.0 + scale) + shift


def _group_sum64(x):
    width = GROUP_SUM_WIDTH
    gi = lax.broadcasted_iota(jnp.int32, (width, width), 0) // 64
    gj = lax.broadcasted_iota(jnp.int32, (width, width), 1) // 64
    ones = jnp.where(gi == gj, 1.0, 0.0).astype(BF16)
    xb = x.astype(BF16)
    outs = [jnp.dot(xb[:, s * width:(s + 1) * width], ones, preferred_element_type=F32)
            for s in range(x.shape[1] // width)]
    return outs[0] if len(outs) == 1 else jnp.concatenate(outs, axis=1)


def _store_transposed(vt_ref, v):
    vt_ref[0] = v.T.astype(BF16)


def _vt_spec(tm):
    nb = S // tm
    return pl.BlockSpec((1, D, tm), lambda i: (i // nb, 0, i % nb))


def _vt_shape():
    return jax.ShapeDtypeStruct((B, D, S), BF16)


def _ada_kernel(c_ref, w_ref, b_ref, o_ref):
    c = c_ref[...]
    cond = c * _sigmoid(c)
    o_ref[0] = _dot(cond, w_ref[0]) + b_ref[0]


def _ada_mods(c, ada_w, ada_b):
    tn = 1536
    out = pl.pallas_call(
        _ada_kernel,
        grid=(DEPTH, 6 * D // tn),
        in_specs=[pl.BlockSpec((B, D), lambda l, j: (0, 0)),
                  pl.BlockSpec((1, D, tn), lambda l, j: (l, 0, j)),
                  pl.BlockSpec((1, 1, tn), lambda l, j: (l, 0, j))],
        out_specs=pl.BlockSpec((1, B, tn), lambda l, j: (l, 0, j)),
        out_shape=jax.ShapeDtypeStruct((DEPTH, B, 6 * D), F32),
        compiler_params=_cparams("parallel", "parallel"),
        name="ada_mods",
    )(c, ada_w, ada_b.reshape(DEPTH, 1, 6 * D))
    return out.reshape(DEPTH, B, 6, D)


def _rwkv_proj_kernel(has_vres, *refs):
    if has_vres:
        (x_ref, xp_ref, mod_ref, ng_ref, mu_ref, wr_ref, wk_ref, wv_ref, w0_ref, w1_ref, w2_ref,
         a0_ref, a1_ref, a2_ref, g1_ref, g2_ref, kk_ref, ka_ref, v0_ref, v1_ref, v2_ref, vf_ref,
         r_out, lw_out, k_out, v_out, kk_out, a_out, g_out) = refs
    else:
        (x_ref, xp_ref, mod_ref, ng_ref, mu_ref, wr_ref, wk_ref, wv_ref, w0_ref, w1_ref, w2_ref,
         a0_ref, a1_ref, a2_ref, g1_ref, g2_ref, kk_ref, ka_ref,
         r_out, lw_out, k_out, v_out, kk_out, a_out, g_out) = refs
    i = pl.program_id(0)
    mod = mod_ref[0]
    shift, scale = mod[0:1], mod[1:2]
    g = ng_ref[...]
    h = _modnorm(x_ref[...], g, shift, scale)
    hp = _modnorm(xp_ref[...], g, shift, scale)
    seq_start = (i % (S // PROJ_TM)) == 0
    prev_row = jnp.where(seq_start, 0.0, hp[7:8, :])
    row = lax.broadcasted_iota(jnp.int32, h.shape, 0)
    prev = jnp.where(row == 0, prev_row, pltpu.roll(h, 1, axis=0))
    delta = prev - h
    mu = mu_ref[...]
    xr = h + delta * mu[0:1]
    xw = h + delta * mu[1:2]
    xk = h + delta * mu[2:3]
    xv = h + delta * mu[3:4]
    xa = h + delta * mu[4:5]
    xg = h + delta * mu[5:6]
    r = _dot(xr, wr_ref[...])
    k = _dot(xk, wk_ref[...])
    v = _dot(xv, wv_ref[...])
    w_raw = w0_ref[...] + _dot(jnp.tanh(_dot(xw, w1_ref[...])), w2_ref[...])
    lw_out[...] = -_sigmoid(w_raw) * EXP_NEG_HALF
    if has_vres:
        mix = _sigmoid(v0_ref[...] + _dot(_dot(xv, v1_ref[...]), v2_ref[...]))
        v = v + (vf_ref[...] - v) * mix
    a = _sigmoid(a0_ref[...] + _dot(_dot(xa, a1_ref[...]), a2_ref[...]))
    g_out[...] = _dot(_sigmoid(_dot(xg, g1_ref[...])), g2_ref[...])
    kk = k * kk_ref[...]
    norm = jnp.sqrt(_group_sum64(kk * kk))
    kk_out[...] = kk / jnp.maximum(norm, 1e-12)
    k_out[...] = k * (1.0 + (a - 1.0) * ka_ref[...])
    r_out[...] = r
    v_out[...] = v
    a_out[...] = a


def _rwkv_proj(x, mod, norm_g, p, v_first):
    has_vres = v_first is not None
    tm = PROJ_TM
    nb = S // tm
    row = lambda a: a.reshape(1, -1)
    full = lambda a: pl.BlockSpec(a.shape, lambda i: (0,) * a.ndim)
    tile = pl.BlockSpec((tm, D), lambda i: (i, 0))
    ops = [x, x, mod, row(norm_g), p["mu"],
           p["w_rkv"][0].astype(BF16), p["w_rkv"][1].astype(BF16), p["w_rkv"][2].astype(BF16),
           row(p["w0"]), p["w1"].astype(BF16), p["w2"].astype(BF16),
           row(p["a0"]), p["a1"].astype(BF16), p["a2"].astype(BF16),
           p["g1"].astype(BF16), p["g2"].astype(BF16), row(p["k_k"]), row(p["k_a"])]
    specs = [tile,
             pl.BlockSpec((8, D), lambda i: (jnp.maximum(i * (tm // 8) - 1, 0), 0)),
             pl.BlockSpec((1, 6, D), lambda i: (i // nb, 0, 0))]
    specs += [full(a) for a in ops[3:]]
    if has_vres:
        extra = [row(p["v0"]), p["v1"].astype(BF16), p["v2"].astype(BF16)]
        ops += extra + [v_first]
        specs += [full(a) for a in extra] + [tile]
    outs = pl.pallas_call(
        functools.partial(_rwkv_proj_kernel, has_vres),
        grid=(T // tm,),
        in_specs=specs,
        out_specs=[tile] * 7,
        out_shape=[jax.ShapeDtypeStruct((T, D), F32)] * 7,
        compiler_params=_cparams("parallel"),
        name="rwkv_proj",
    )(*ops)
    return outs


def _rwkv_scan_kernel(r_ref, lw_ref, k_ref, v_ref, kk_ref, a_ref, g_ref, rk_ref, lg_ref, lb_ref,
                      o_ref, s_ref):
    c = pl.program_id(2)
    W = RW_GROUP * RW_HEAD

    @pl.when(c == 0)
    def _():
        s_ref[...] = jnp.zeros_like(s_ref)

    slabs = [slice(p * W, (p + 1) * W) for p in range(RW_PAR)]
    cols = lambda ref: [ref[:, sl] for sl in slabs]
    out, s_new = _rwkv_blocks(cols(r_ref), cols(lw_ref), cols(k_ref), cols(v_ref), cols(kk_ref),
                              cols(a_ref), cols(g_ref), cols(rk_ref), cols(lg_ref), cols(lb_ref),
                              [s_ref[p] for p in range(RW_PAR)])
    for p, sl in enumerate(slabs):
        o_ref[:, sl] = out[p]
        s_ref[p] = s_new[p]


def _each(fn, *cols):
    return [fn(*args) for args in zip(*cols)]


def _rwkv_blocks(r, lw, k, v, kk, a, g, r_k, lnx_g, lnx_b, s_old):
    C = RW_CHUNK
    W = RW_GROUP * RW_HEAD

    mul = lambda x, y: x * y
    ti = lax.broadcasted_iota(jnp.int32, (C, C), 0)
    si = lax.broadcasted_iota(jnp.int32, (C, C), 1)
    tri = jnp.where(ti >= si, 1.0, 0.0).astype(F32)
    cl = _each(lambda x: jnp.dot(tri, x, precision=HIGHEST, preferred_element_type=F32), lw)
    cl_end = _each(lambda x: x[C - 1:C, :], cl)
    w_t = _each(jnp.exp, cl)
    w_prev = _each(lambda x, y: jnp.exp(x - y), cl, lw)
    w_inv = _each(lambda x: jnp.exp(-x), cl)
    w_rem = _each(lambda x, y: jnp.exp(x - y), cl_end, cl)
    bv = _each(mul, kk, a)
    r_hat = _each(mul, r, w_t)
    a_hat = _each(lambda x, y: -x * y, kk, w_prev)
    b_hat = _each(mul, bv, w_inv)
    k_hat = _each(mul, k, w_inv)
    b_til = _each(mul, bv, w_rem)
    k_til = _each(mul, k, w_rem)

    lane_head = lax.broadcasted_iota(jnp.int32, (C, W), 1) // RW_HEAD

    def stack(m):
        mb = m.astype(BF16)
        zero = jnp.zeros_like(mb)
        return jnp.concatenate([jnp.where(lane_head == hd, mb, zero) for hd in range(RW_GROUP)],
                               axis=0)

    def fold(m):
        return m[0:C] + m[C:2 * C] + m[2 * C:3 * C] + m[3 * C:4 * C]

    cat0 = lambda x, y: jnp.concatenate([x, y], axis=0)
    gram = _each(lambda ah, rh, bh, kh: _dot_nt(cat0(stack(ah), stack(rh)), cat0(stack(bh), stack(kh))),
                 a_hat, r_hat, b_hat, k_hat)
    n = RW_GROUP * C
    ri = lax.broadcasted_iota(jnp.int32, (n, n), 0)
    ci = lax.broadcasted_iota(jnp.int32, (n, n), 1)
    strict = (ri % C) > (ci % C)
    incl = (ri % C) >= (ci % C)
    same_head = (ri // RW_HEAD) == (ci // RW_HEAD)
    eye = jnp.where(ri == ci, 1.0, 0.0)
    l_ab = _each(lambda x: jnp.where(strict, x[:n, :n], 0.0), gram)
    l_ak = _each(lambda x: jnp.where(strict, x[:n, n:], 0.0), gram)
    m_rb = _each(lambda x: jnp.where(incl, x[n:, :n], 0.0), gram)
    m_rk = _each(lambda x: jnp.where(incl, x[n:, n:], 0.0), gram)

    inv = _each(lambda x: eye + x, l_ab)
    pw = l_ab
    for _ in range(int(math.log2(C)) - 1):
        pw = _each(lambda x: _dot(x, x), pw)
        inv = _each(lambda x, y: x + _dot(x, y), inv, pw)

    a_p = _each(lambda x, y: fold(_dot(x, stack(y))), inv, a_hat)
    t_l = _each(_dot, inv, l_ak)
    v_p = _each(lambda x, y: fold(_dot(x, stack(y))), t_l, v)
    r_p = _each(lambda x, y, z: x + fold(_dot(y, stack(z))), r_hat, m_rb, a_p)
    y0 = _each(lambda mb, mk, vp, vv: fold(_dot(jnp.concatenate([mb, mk], axis=1),
                                                cat0(stack(vp), stack(vv)))), m_rb, m_rk, v_p, v)
    a_til = _each(lambda x, y: jnp.where(same_head, _dot_tn(x, y), 0.0), b_til, a_p)
    d_new = _each(lambda vp, vv, bt, kt: jnp.where(same_head, _dot_tn(cat0(vp, vv), cat0(bt, kt)), 0.0),
                  v_p, v, b_til, k_til)
    y = _each(lambda x, s, z: _dot_nt(x, s) + z, r_p, s_old, y0)
    s_new = _each(lambda s, ce, at, dn: s * jnp.exp(ce) + _dot_nt(s, at) + dn,
                  s_old, cl_end, a_til, d_new)

    inv_n = 1.0 / RW_HEAD
    mean = _each(lambda x: _group_sum64(x) * inv_n, y)
    yc = _each(lambda x, m: x - m, y, mean)
    var = _each(lambda x: _group_sum64(x * x) * inv_n, yc)
    yn = _each(lambda x, vr, lg, lb: x * lax.rsqrt(vr + GN_EPS) * lg + lb, yc, var, lnx_g, lnx_b)
    bonus = _each(lambda rr, kx, rk, vv: _group_sum64(rr * kx * rk) * vv, r, k, r_k, v)
    out = _each(lambda x, bo, gg: ((x + bo) * gg).astype(BF16), yn, bonus, g)
    return out, s_new


def _rwkv_scan(r, lw, k, v, kk, a, g, r_k, lnx_g, lnx_b):
    C = RW_CHUNK
    W = RW_PAR * RW_GROUP * RW_HEAD
    nc = S // C
    slab = pl.BlockSpec((C, W), lambda b, gi, c: (b * nc + c, gi))
    prow = pl.BlockSpec((1, W), lambda b, gi, c: (0, gi))
    return pl.pallas_call(
        _rwkv_scan_kernel,
        grid=(B, D // W, nc),
        in_specs=[slab] * 7 + [prow] * 3,
        out_specs=slab,
        out_shape=jax.ShapeDtypeStruct((T, D), BF16),
        scratch_shapes=[pltpu.VMEM((RW_PAR, RW_GROUP * RW_HEAD, RW_GROUP * RW_HEAD), F32)],
        compiler_params=_cparams("parallel", "parallel", "arbitrary"),
        name="rwkv_scan",
    )(r, lw, k, v, kk, a, g, r_k.reshape(1, D), lnx_g.reshape(1, D), lnx_b.reshape(1, D))


def _proj_res_kernel(gate_row, a_ref, w_ref, x_ref, mod_ref, o_ref):
    y = jnp.dot(a_ref[...], w_ref[...], preferred_element_type=F32)
    o_ref[...] = x_ref[...] + mod_ref[0][gate_row:gate_row + 1] * y


def _proj_res(a, w, x, mod, gate_row):
    tm = 512
    nb = S // tm
    kdim = a.shape[1]
    return pl.pallas_call(
        functools.partial(_proj_res_kernel, gate_row),
        grid=(T // tm,),
        in_specs=[pl.BlockSpec((tm, kdim), lambda i: (i, 0)),
                  pl.BlockSpec((kdim, D), lambda i: (0, 0)),
                  pl.BlockSpec((tm, D), lambda i: (i, 0)),
                  pl.BlockSpec((1, 6, D), lambda i: (i // nb, 0, 0))],
        out_specs=pl.BlockSpec((tm, D), lambda i: (i, 0)),
        out_shape=jax.ShapeDtypeStruct((T, D), F32),
        compiler_params=_cparams("parallel"),
        name="proj_res",
    )(a, w.astype(BF16), x, mod)


def _ffn_kernel(nf, x_ref, mod_ref, ng_ref, wg_ref, wu_ref, wd_ref, o_ref, h_scr, acc_scr):
    f = pl.program_id(1)

    @pl.when(f == 0)
    def _():
        mod = mod_ref[0]
        h_scr[...] = _modnorm(x_ref[...], ng_ref[...], mod[3:4], mod[4:5]).astype(BF16)
        acc_scr[...] = jnp.zeros_like(acc_scr)

    for sub in range(h_scr.shape[0] // FFN_SUB):
        rows = slice(sub * FFN_SUB, (sub + 1) * FFN_SUB)
        h = h_scr[rows]
        gt = jnp.dot(h, wg_ref[0], preferred_element_type=F32)
        up = jnp.dot(h, wu_ref[0], preferred_element_type=F32)
        act = (gt * _sigmoid(gt) * up).astype(BF16)
        acc_scr[rows] += jnp.dot(act, wd_ref[0], preferred_element_type=F32)

    @pl.when(f == nf - 1)
    def _():
        o_ref[...] = x_ref[...] + mod_ref[0][5:6] * acc_scr[...]


def _ffn_dense(x, mod, norm_g, wgu, wd, layer):
    tm, tf = FFN_TM, FFN_TF
    nf = D_FF // tf
    nb = S // tm
    return pl.pallas_call(
        functools.partial(_ffn_kernel, nf),
        grid=(T // tm, nf),
        in_specs=[pl.BlockSpec((tm, D), lambda i, f: (i, 0)),
                  pl.BlockSpec((1, 6, D), lambda i, f: (i // nb, 0, 0)),
                  pl.BlockSpec((1, D), lambda i, f: (0, 0)),
                  pl.BlockSpec((1, D, tf), lambda i, f: (layer, 0, f)),
                  pl.BlockSpec((1, D, tf), lambda i, f: (layer, 0, nf + f)),
                  pl.BlockSpec((1, tf, D), lambda i, f: (layer, f, 0))],
        out_specs=pl.BlockSpec((tm, D), lambda i, f: (i, 0)),
        out_shape=jax.ShapeDtypeStruct((T, D), F32),
        scratch_shapes=[pltpu.VMEM((tm, D), BF16), pltpu.VMEM((tm, D), F32)],
        compiler_params=_cparams("parallel", "arbitrary"),
        name="ffn_dense",
    )(x, mod, norm_g.reshape(1, D), wgu, wgu, wd)


def _route_kernel(x_ref, mod_ref, ng_ref, wr_ref, br_ref,
                  h_out, cw_t_out, dest_t_out, cnt_out, carry_scr):
    i = pl.program_id(0)
    tm = x_ref.shape[0]
    mod = mod_ref[0]
    h = _modnorm(x_ref[...], ng_ref[...], mod[3:4], mod[4:5])
    h_out[...] = h.astype(BF16)
    logits = jnp.dot(h, wr_ref[...], precision=HIGHEST, preferred_element_type=F32) + br_ref[...]
    lane = lax.broadcasted_iota(jnp.int32, logits.shape, 1)
    neg = -jnp.inf
    logits = jnp.where(lane < N_EXPERTS, logits, neg)
    m1 = jnp.max(logits, axis=-1, keepdims=True)
    i1 = jnp.min(jnp.where(logits == m1, lane, LANES), axis=-1, keepdims=True)
    rest = jnp.where(lane == i1, neg, logits)
    m2 = jnp.max(rest, axis=-1, keepdims=True)
    i2 = jnp.min(jnp.where(rest == m2, lane, LANES), axis=-1, keepdims=True)
    e2 = jnp.exp(m2 - m1)
    w1 = 1.0 / (1.0 + e2)
    w2 = e2 / (1.0 + e2)
    cw_t_out[...] = (jnp.where(lane == i1, w1, 0.0) + jnp.where(lane == i2, w2, 0.0)).T

    sel =jnp.where((lane == i1) | (lane == i2), 1.0, 0.0)
    ti = lax.broadcasted_iota(jnp.int32, (tm, tm), 0)
    si = lax.broadcasted_iota(jnp.int32, (tm, tm), 1)
    before = jnp.where(ti > si, 1.0, 0.0).astype(BF16)
    window_start = (i % (MOE_TW // tm)) == 0
    carry = jnp.where(window_start, 0.0, carry_scr[...])
    rank = jnp.dot(before, sel.astype(BF16), preferred_element_type=F32) + carry
    dest_t_out[...] = jnp.where(sel > 0.0, rank, -1.0).T
    total = carry + jnp.sum(sel, axis=0, keepdims=True)
    carry_scr[...] = total
    cnt_out[0] = jnp.broadcast_to(total, (8, LANES))


def _moe_kernel(nf, cnt_ref, x_ref, mod_ref, h_ref, dest_t_ref, cw_t_ref, wg_ref, wu_ref, wd_ref,
                o_ref, xg_scr, yc_scr):
    w = pl.program_id(0)
    e = pl.program_id(1)
    f = pl.program_id(2)
    tw, ts = MOE_TW, MOE_TS
    n = cnt_ref[w * N_EXPERTS + e]
    n_sub = (n + ts - 1) // ts

    @pl.when((e == 0) & (f == 0))
    def _():
        o_ref[...] = jnp.zeros_like(o_ref)

    dest_row = dest_t_ref[pl.ds(e, 1), :]
    cw_row = cw_t_ref[pl.ds(e, 1), :]
    slot_sub = lax.broadcasted_iota(jnp.int32, (ts, tw), 0).astype(F32)

    def sub_tile(s, carry):
        match = dest_row == slot_sub + (s * ts).astype(F32)
        select = jnp.where(match, 1.0, 0.0).astype(BF16)

        @pl.when(f == 0)
        def _():
            xg_scr[s] = jnp.dot(select, h_ref[...], preferred_element_type=F32).astype(BF16)

        xs = xg_scr[s]
        gt = jnp.dot(xs, wg_ref[0, 0], preferred_element_type=F32)
        up = jnp.dot(xs, wu_ref[0, 0], preferred_element_type=F32)
        act = (gt * _sigmoid(gt) * up).astype(BF16)
        y = jnp.dot(act, wd_ref[0, 0], preferred_element_type=F32)

        @pl.when(f == 0)
        def _():
            yc_scr[s] = y

        @pl.when((f > 0) & (f < nf - 1))
        def _():
            yc_scr[s] += y

        @pl.when(f == nf - 1)
        def _():
            w_slot = jnp.sum(jnp.where(match, cw_row, 0.0), axis=-1, keepdims=True)
            weighted = ((yc_scr[s] + y) * w_slot).astype(BF16)
            o_ref[...] += _dot_tn(select, weighted)
        return carry

    lax.fori_loop(0, n_sub, sub_tile, 0)

    @pl.when((e == N_EXPERTS - 1) & (f == nf - 1))
    def _():
        o_ref[...] = x_ref[...] + mod_ref[0][5:6] * o_ref[...]


def _moe(x, mod, norm_g, w_router, b_router, wgu, wd, layer):
    tm, tf, tw, ts = ROUTE_TM, FFN_TF, MOE_TW, MOE_TS
    nf = D_FF // tf
    nb = S // tm
    nw = T // tw
    wr =jnp.zeros((D, LANES), F32).at[:, :N_EXPERTS].set(w_router)
    br = jnp.zeros((1, LANES), F32).at[0, :N_EXPERTS].set(b_router)
    tile = lambda width: pl.BlockSpec((tm, width), lambda i: (i, 0))
    assert nf >= 2
    lanes_t = pl.BlockSpec((LANES, tm), lambda i: (0, i))
    h, cw_t, dest_t, cnt = pl.pallas_call(
        _route_kernel,
        grid=(T // tm,),
        in_specs=[tile(D),
                  pl.BlockSpec((1, 6, D), lambda i: (i // nb, 0, 0)),
                  pl.BlockSpec((1, D), lambda i: (0, 0)),
                  pl.BlockSpec((D, LANES), lambda i: (0, 0)),
                  pl.BlockSpec((1, LANES), lambda i: (0, 0))],
        out_specs=[tile(D), lanes_t, lanes_t,
                   pl.BlockSpec((1, 8, LANES), lambda i: (i // (tw // tm), 0, 0))],
        out_shape=[jax.ShapeDtypeStruct((T, D), BF16), jax.ShapeDtypeStruct((LANES, T), F32),
                   jax.ShapeDtypeStruct((LANES, T), F32), jax.ShapeDtypeStruct((nw, 8, LANES), F32)],
        scratch_shapes=[pltpu.VMEM((1, LANES), F32)],
        compiler_params=_cparams("arbitrary"),
        name="moe_route",
    )(x, mod, norm_g.reshape(1, D), wr, br)
    counts = cnt[:, 0, :N_EXPERTS].astype(jnp.int32).reshape(nw * N_EXPERTS)
    n_slots = -(-tw // ts)
    return pl.pallas_call(
        functools.partial(_moe_kernel, nf),
        grid_spec=pltpu.PrefetchScalarGridSpec(
            num_scalar_prefetch=1,
            grid=(nw, N_EXPERTS, nf),
            in_specs=[pl.BlockSpec((tw, D), lambda w, e, f, c: (w, 0)),
                      pl.BlockSpec((1, 6, D), lambda w, e, f, c: (w // (S // tw), 0, 0)),
                      pl.BlockSpec((tw, D), lambda w, e, f, c: (w, 0)),
                      pl.BlockSpec((LANES, tw), lambda w, e, f, c: (0, w)),
                      pl.BlockSpec((LANES, tw), lambda w, e, f, c: (0, w)),
                      pl.BlockSpec((1, 1, D, tf), lambda w, e, f, c: (layer, e, 0, f)),
                      pl.BlockSpec((1, 1, D, tf), lambda w, e, f, c: (layer, e, 0, nf + f)),
                      pl.BlockSpec((1, 1, tf, D), lambda w, e, f, c: (layer, e, f, 0))],
            out_specs=pl.BlockSpec((tw, D), lambda w, e, f, c: (w, 0)),
            scratch_shapes=[pltpu.VMEM((n_slots, ts, D), BF16), pltpu.VMEM((n_slots, ts, D), F32)]),
        out_shape=jax.ShapeDtypeStruct((T, D), F32),
        compiler_params=_cparams("parallel", "arbitrary", "arbitrary"),
        name="moe_experts",
    )(counts, x, mod, h, dest_t, cw_t, wgu, wgu, wd)


def _mla_proj_kernel(x_ref, mod_ref, ng_ref, pos_ref, invf_ref, wd_ref, qg_ref, kvg_ref,
                     wuq_ref, wukv_ref, q_out, k_out, v_out):
    mod = mod_ref[0]
    h = _modnorm(x_ref[...], ng_ref[...], mod[0:1], mod[1:2])
    down = _dot(h, wd_ref[...])
    cq = down[:, :MLA_Q_LORA]
    ckv = down[:, MLA_Q_LORA:MLA_Q_LORA + MLA_KV_LORA]
    kr = down[:, MLA_Q_LORA + MLA_KV_LORA:]
    cq = cq * lax.rsqrt(jnp.mean(cq * cq, axis=-1, keepdims=True) + NORM_EPS) * qg_ref[...]
    ckv = ckv * lax.rsqrt(jnp.mean(ckv * ckv, axis=-1, keepdims=True) + NORM_EPS) * kvg_ref[...]

    ang = pos_ref[...] * invf_ref[...]
    lane = lax.broadcasted_iota(jnp.int32, ang.shape, 1)
    is_rope = (lane >= MLA_NOPE) & (lane < MLA_NOPE + MLA_ROPE)
    cos_r = jnp.where(is_rope, jnp.cos(ang), 0.0)
    sin_r = jnp.where(is_rope, jnp.sin(ang), 0.0)
    cos_q = jnp.where(lane < MLA_NOPE, 1.0, cos_r)
    shift = LANES - MLA_ROPE

    k_rope = kr * cos_r + pltpu.roll(kr, shift, axis=1) * sin_r

    qf = _dot(cq, wuq_ref[...])
    kv = _dot(ckv, wukv_ref[...])
    for hd in range(MLA_HEADS):
        sl = slice(hd * LANES, (hd + 1) * LANES)
        qs = qf[:, sl]
        q_out[:, sl] = ((qs * cos_q + pltpu.roll(qs, shift, axis=1) * sin_r) * MLA_SCALE).astype(BF16)
        k_out[:, sl] = (kv[:, sl] + k_rope).astype(BF16)
    _store_transposed(v_out, kv[:, MLA_HEADS * LANES:])


def _rot_half_cols(w):
    half = MLA_ROPE // 2
    return jnp.concatenate([-w[..., half:], w[..., :half]], axis=-1)


def _mla_proj(x, mod, norm_g, positions, w_down, q_norm_g, kv_norm_g, w_uq, w_ukv):
    tm = PROJ_TM
    nb = S // tm
    half = MLA_ROPE // 2
    wr = w_down[:, MLA_Q_LORA + MLA_KV_LORA:]
    wd = jnp.concatenate([w_down[:, :MLA_Q_LORA + MLA_KV_LORA],
                          jnp.zeros((D, MLA_NOPE), F32), wr, _rot_half_cols(wr)], axis=1).astype(BF16)
    wq = w_uq.reshape(MLA_Q_LORA, MLA_HEADS, MLA_NOPE + MLA_ROPE)
    wq_r = wq[..., MLA_NOPE:]
    wuq = jnp.concatenate([wq[..., :MLA_NOPE], wq_r, _rot_half_cols(wq_r)], axis=-1)
    wuq = wuq.reshape(MLA_Q_LORA, MLA_HEADS * LANES).astype(BF16)
    wkv = w_ukv.reshape(MLA_KV_LORA, MLA_HEADS, 2 * MLA_NOPE)
    wk = jnp.concatenate([wkv[..., :MLA_NOPE], jnp.zeros_like(wkv[..., :MLA_NOPE])], axis=-1)
    wukv = jnp.concatenate([wk.reshape(MLA_KV_LORA, MLA_HEADS * LANES),
                            wkv[..., MLA_NOPE:].reshape(MLA_KV_LORA, D)], axis=1).astype(BF16)
    inv_freq = ROPE_BASE ** (-jnp.arange(half, dtype=F32) / half)
    lane = jnp.arange(LANES)
    invf = jnp.where((lane >= MLA_NOPE) & (lane < MLA_NOPE + MLA_ROPE),
                     inv_freq[(lane - MLA_NOPE) % half], 0.0).reshape(1, LANES).astype(F32)
    pos = jnp.broadcast_to(positions.reshape(T, 1).astype(F32), (T, LANES))
    full = lambda a: pl.BlockSpec(a.shape, lambda i: (0,) * a.ndim)
    ops = [x, mod, norm_g.reshape(1, D), pos, invf, wd, q_norm_g.reshape(1, -1),
           kv_norm_g.reshape(1, -1), wuq, wukv]
    specs = [pl.BlockSpec((tm, D), lambda i: (i, 0)),
             pl.BlockSpec((1, 6, D), lambda i: (i // nb, 0, 0)),
             full(ops[2]),
             pl.BlockSpec((tm, LANES), lambda i: (i, 0))] + [full(a) for a in ops[4:]]
    qw = MLA_HEADS * LANES
    return pl.pallas_call(
        _mla_proj_kernel,
        grid=(T // tm,),
        in_specs=specs,
        out_specs=[pl.BlockSpec((tm, qw), lambda i: (i, 0)),
                   pl.BlockSpec((tm, qw), lambda i: (i, 0)),
                   _vt_spec(tm)],
        out_shape=[jax.ShapeDtypeStruct((T, qw), BF16), jax.ShapeDtypeStruct((T, qw), BF16),
                   _vt_shape()],
        compiler_params=_cparams("parallel"),
        name="mla_proj",
    )(*ops)


def _fox_proj_kernel(x_ref, mod_ref, ng_ref, w_ref, wf_ref, bf_ref, qg_ref, kg_ref,
                     q_out, k_out, v_out, og_out, fc_out, fr_out, carry_scr):
    i = pl.program_id(0)
    tm = x_ref.shape[0]
    mod = mod_ref[0]
    h = _modnorm(x_ref[...], ng_ref[...], mod[0:1], mod[1:2]).astype(BF16)
    q = jnp.dot(h, w_ref[:, 0:D], preferred_element_type=F32)
    k = jnp.dot(h, w_ref[:, D:2 * D], preferred_element_type=F32)
    v = jnp.dot(h, w_ref[:, 2 * D:3 * D], preferred_element_type=F32)
    og = jnp.dot(h, w_ref[:, 3 * D:4 * D], preferred_element_type=F32)
    inv_n = 1.0 / FOX_HEAD
    q = q * lax.rsqrt(_group_sum64(q * q) * inv_n + NORM_EPS) * qg_ref[...] * FOX_SCALE
    k = k * lax.rsqrt(_group_sum64(k * k) * inv_n + NORM_EPS) * kg_ref[...]
    q_out[...] = q.astype(BF16)
    k_out[...] = k.astype(BF16)
    _store_transposed(v_out, v)
    og_out[...] = _sigmoid(og).astype(BF16)

    z = jnp.dot(h, wf_ref[...], preferred_element_type=F32) + bf_ref[...]
    lane = lax.broadcasted_iota(jnp.int32, z.shape, 1)
    log_f = jnp.minimum(z, 0.0) - jnp.log(1.0 + jnp.exp(-jnp.abs(z)))
    log_f = jnp.where(lane < FOX_HEADS, log_f, 0.0)
    ti = lax.broadcasted_iota(jnp.int32, (tm, tm), 0)
    si = lax.broadcasted_iota(jnp.int32, (tm, tm), 1)
    tri = jnp.where(ti >= si, 1.0, 0.0).astype(F32)
    seq_start = (i % (S // tm)) == 0
    carry = jnp.where(seq_start, 0.0, carry_scr[...])
    cum = jnp.dot(tri, log_f, precision=HIGHEST, preferred_element_type=F32) + carry
    carry_scr[...] = cum[tm - 1:tm, :]
    fc_out[...] = cum
    fr_out[0] = cum.T


def _fox_proj(x, mod, norm_g, w_in, b_f, q_norm_g, k_norm_g):
    tm = PROJ_TM
    nb = S // tm
    w_main = jnp.concatenate([w_in[:, :3 * D], w_in[:, 3 * D + FOX_HEADS:]], axis=1).astype(BF16)
    w_f = jnp.zeros((D, LANES), F32).at[:, :FOX_HEADS].set(w_in[:, 3 * D:3 * D + FOX_HEADS]).astype(BF16)
    bf = jnp.zeros((1, LANES), F32).at[0, :FOX_HEADS].set(b_f)
    qg = jnp.tile(q_norm_g, FOX_HEADS).reshape(1, D)
    kg = jnp.tile(k_norm_g, FOX_HEADS).reshape(1, D)
    full = lambda a: pl.BlockSpec(a.shape, lambda i: (0,) * a.ndim)
    tile = pl.BlockSpec((tm, D), lambda i: (i, 0))
    ops = [x, mod, norm_g.reshape(1, D), w_main, w_f, bf, qg, kg]
    specs = [tile, pl.BlockSpec((1, 6, D), lambda i: (i // nb, 0, 0))] + [full(a) for a in ops[2:]]
    return pl.pallas_call(
        _fox_proj_kernel,
        grid=(T // tm,),
        in_specs=specs,
        out_specs=[tile, tile, _vt_spec(tm), tile,
                   pl.BlockSpec((tm, LANES), lambda i: (i, 0)),
                   pl.BlockSpec((1, LANES, tm), lambda i: (i // nb, 0, i % nb))],
        out_shape=[jax.ShapeDtypeStruct((T, D), BF16)] * 2 + [_vt_shape(), jax.ShapeDtypeStruct((T, D), BF16),
                   jax.ShapeDtypeStruct((T, LANES), F32), jax.ShapeDtypeStruct((B, LANES, S), F32)],
        scratch_shapes=[pltpu.VMEM((1, LANES), F32)],
        compiler_params=_cparams("arbitrary"),
        name="fox_proj",
    )(*ops)


def _attn_kernel(fox, *refs):
    if fox:
        q_ref, k_ref, vt_ref, og_ref, fc_ref, fr_ref, o_ref, fs_scr = refs
    else:
        q_ref, k_ref, vt_ref, o_ref = refs
    tq = ATT_TQ
    nq = S // tq
    w2 = 2 * tq
    p = pl.program_id(1)
    qi = pl.program_id(2)

    if fox:
        @pl.when(qi == 0)
        def _():
            fcv = fc_ref[...]
            lane_s = lax.broadcasted_iota(jnp.int32, fcv.shape, 1)
            for j in range(2):
                colv = jnp.sum(jnp.where(lane_s == 2 * p + j, fcv, 0.0), axis=-1, keepdims=True)
                fs_scr[j] = jnp.broadcast_to(colv, fcv.shape)

    qb = q_ref[...]
    lane = lax.broadcasted_iota(jnp.int32, qb.shape, 1)
    per_head = FOX_HEAD if fox else LANES
    zero = jnp.zeros_like(qb)
    q_bd = jnp.concatenate([jnp.where(lane // per_head == j, qb, zero) for j in range(2)], axis=0)
    if fox:
        f_t = jnp.concatenate([fr_ref[0, pl.ds(2 * p + j, 1), :] for j in range(2)], axis=1)

    srow = lax.broadcasted_iota(jnp.int32, (tq, w2), 0)
    tcol = lax.broadcasted_iota(jnp.int32, (tq, w2), 1) % tq
    diag_mask = (srow <= tcol) if fox else ((srow // 64) <= (tcol // 64))
    sub = lax.broadcasted_iota(jnp.int32, (LANES, tq), 0)
    reps = tq // LANES

    def scores(lo, hi):
        st = lax.dot_general(k_ref[lo:hi, :], q_bd, (((1,), (1,)), ((), ())),
                             preferred_element_type=F32)
        if fox:
            f_s = jnp.concatenate([fs_scr[0, lo:hi]] * reps + [fs_scr[1, lo:hi]] * reps, axis=1)
            st = st + (f_t - f_s)
        return st

    def values(lo, hi, pt):
        vt_ones = jnp.concatenate([vt_ref[0, :, lo:hi], jnp.ones((16, hi - lo), BF16)], axis=0)
        return jnp.dot(vt_ones, pt.astype(BF16), preferred_element_type=F32)

    def query_tile(c):
        n_head = c * tq
        st_tail = jnp.where(diag_mask, scores(n_head, n_head + tq), -jnp.inf)
        mx = jnp.max(st_tail, axis=0, keepdims=True)
        if c > 0:
            st_head = scores(0, n_head)
            mx = jnp.maximum(mx, jnp.max(st_head, axis=0, keepdims=True))
        acc = values(n_head, n_head + tq, jnp.exp(st_tail - mx))
        if c > 0:
            acc = acc + values(0, n_head, jnp.exp(st_head - mx))
        out = acc[:LANES] / acc[LANES:LANES + 1]
        o = jnp.where(sub // 64 == 0, out[:, :tq], out[:, tq:]).T
        if fox:
            o = o * og_ref[...].astype(F32)
        o_ref[...] = o.astype(BF16)

    for c in range(nq):
        pl.when(qi == c)(functools.partial(query_tile, c))


def _attention(q, k, vt, fox_extras=None):
    fox = fox_extras is not None
    tq = ATT_TQ
    nq = S // tq
    qw = LANES if fox else 2 * LANES
    ops = [q, k, vt]
    specs = [pl.BlockSpec((tq, qw), lambda b, p, i: (b * nq + i, p)),
             pl.BlockSpec((S, qw), lambda b, p, i: (b, p)),
             pl.BlockSpec((1, LANES, S), lambda b, p, i: (b, p, 0))]
    scratch = []
    if fox:
        og, fc, fr = fox_extras
        ops += [og, fc, fr]
        specs += [pl.BlockSpec((tq, LANES), lambda b, p, i: (b * nq + i, p)),
                  pl.BlockSpec((S, LANES), lambda b, p, i: (b, 0)),
                  pl.BlockSpec((1, FOX_HEADS, tq), lambda b, p, i: (b, 0, i))]
        scratch.append(pltpu.VMEM((2, S, LANES), F32))
    return pl.pallas_call(
        functools.partial(_attn_kernel, fox),
        grid=(B, 8, nq),
        in_specs=specs,
        out_specs=pl.BlockSpec((tq, LANES), lambda b, p, i: (b * nq + i, p)),
        out_shape=jax.ShapeDtypeStruct((T, D), BF16),
        scratch_shapes=scratch,
        compiler_params=_cparams("parallel", "parallel", "arbitrary"),
        name="fox_attn" if fox else "mla_attn",
    )(*ops)


def _final_norm_kernel(x_ref, g_ref, o_ref):
    x = x_ref[...]
    o_ref[...] = x * lax.rsqrt(jnp.mean(x * x, axis=-1, keepdims=True) + NORM_EPS) * g_ref[...]


def _final_norm(x, g):
    tm = 512
    return pl.pallas_call(
        _final_norm_kernel,
        grid=(T // tm,),
        in_specs=[pl.BlockSpec((tm, D), lambda i: (i, 0)), pl.BlockSpec((1, D), lambda i: (0, 0))],
        out_specs=pl.BlockSpec((tm, D), lambda i: (i, 0)),
        out_shape=jax.ShapeDtypeStruct((T, D), F32),
        compiler_params=_cparams("parallel"),
        name="final_norm",
    )(x, g.reshape(1, D))


def kernel(x, c, positions, ada_w, ada_b, norm_mix_g, norm_ffn_g, final_norm_g, rw_mu, rw_w_rkv, rw_w_o, rw_w0, rw_w1, rw_w2, rw_a0, rw_a1, rw_a2, rw_g1, rw_g2, rw_k_k, rw_k_a, rw_r_k, rw_lnx_g, rw_lnx_b, rw_v0, rw_v1, rw_v2, mla_w_down, mla_q_norm_g, mla_kv_norm_g, mla_w_uq, mla_w_ukv, mla_w_o, fox_w_in, fox_b_f, fox_q_norm_g, fox_k_norm_g, fox_w_o, ffn_w_gate_up, ffn_w_down, moe_w_router, moe_b_router, moe_w_gate_up, moe_w_down):
    xf = x.reshape(T, D)
    mods = _ada_mods(c, ada_w, ada_b)
    ffn_wgu, ffn_wd = ffn_w_gate_up.astype(BF16), ffn_w_down.astype(BF16)
    moe_wgu, moe_wd = moe_w_gate_up.astype(BF16), moe_w_down.astype(BF16)
    v_first = None
    for i in range(DEPTH):
        mod = mods[i]
        kind, j = i % 3, i // 3
        if kind == 0:
            p = dict(mu=rw_mu[j], w_rkv=rw_w_rkv[j], w0=rw_w0[j], w1=rw_w1[j], w2=rw_w2[j],
                     a0=rw_a0[j], a1=rw_a1[j], a2=rw_a2[j], g1=rw_g1[j], g2=rw_g2[j],
                     k_k=rw_k_k[j], k_a=rw_k_a[j])
            if j > 0:
                p.update(v0=rw_v0[j - 1], v1=rw_v1[j - 1], v2=rw_v2[j - 1])
            r, lw, k, v, kk, a, g = _rwkv_proj(xf, mod, norm_mix_g[i], p, v_first if j > 0 else None)
            if j == 0:
                v_first = v
            y = _rwkv_scan(r, lw, k, v, kk, a, g, rw_r_k[j], rw_lnx_g[j], rw_lnx_b[j])
            xf = _proj_res(y, rw_w_o[j], xf, mod, 2)
        elif kind == 1:
            q, k, v = _mla_proj(xf, mod, norm_mix_g[i], positions, mla_w_down[j], mla_q_norm_g[j],
                                mla_kv_norm_g[j], mla_w_uq[j], mla_w_ukv[j])
            o = _attention(q, k, v)
            xf = _proj_res(o, mla_w_o[j], xf, mod, 2)
        else:
            q, k, v, og, fc, fr = _fox_proj(xf, mod, norm_mix_g[i], fox_w_in[j], fox_b_f[j],
                                            fox_q_norm_g[j], fox_k_norm_g[j])
            o = _attention(q, k, v, (og, fc, fr))
            xf = _proj_res(o, fox_w_o[j], xf, mod, 2)
        if i % 2 == 0:
            xf = _ffn_dense(xf, mod, norm_ffn_g[i], ffn_wgu, ffn_wd, i // 2)
        else:
            xf = _moe(xf, mod, norm_ffn_g[i], moe_w_router[i // 2], moe_b_router[i // 2],
                      moe_wgu, moe_wd, i // 2)
    return _final_norm(xf, final_norm_g).reshape(B, S, D)
```

```python
import functools
import math

import jax
import jax.numpy as jnp
from jax import lax
from jax.experimental import pallas as pl
from jax.experimental.pallas import tpu as pltpu

F32 = jnp.float32
BF16 = jnp.bfloat16
HIGHEST = lax.Precision.HIGHEST

D = 1024
B = 8
S = 2048
T = B * S
DEPTH = 4
NORM_EPS = 1e-6

RW_HEAD = 64
GN_EPS = 64e-5
EXP_NEG_HALF = math.exp(-0.5)

MLA_HEADS = 16
MLA_NOPE = 64
MLA_ROPE = 32
MLA_Q_LORA = 768
MLA_KV_LORA = 256
MLA_SCALE = (MLA_NOPE + MLA_ROPE) ** -0.5
ROPE_BASE = 10000.0

FOX_HEADS = 16
FOX_HEAD = 64
FOX_SCALE = FOX_HEAD ** -0.5

D_FF = 2816
N_EXPERTS = 8

LANES = 128
GROUP_SUM_WIDTH = 256
VMEM_LIMIT = 56 * 1024 * 1024

RW_CHUNK = 64
RW_GROUP = 4
RW_PAR = 4
ATT_TQ = 256
FFN_TM = 1024
FFN_SUB = 512
ROUTE_TM = 512
FFN_TF = 1408
PROJ_TM = 256
MOE_TW = 1024
MOE_STEP = 128
MOE_MAX = 512


def _cparams(*sem):
    return pltpu.CompilerParams(dimension_semantics=sem, vmem_limit_bytes=VMEM_LIMIT)


def _dot(a, b):
    return jnp.dot(a.astype(BF16), b.astype(BF16), preferred_element_type=F32)


def _dot_nt(a, b):
    return lax.dot_general(a.astype(BF16), b.astype(BF16), (((1,), (1,)), ((), ())),
                           preferred_element_type=F32)


def _dot_tn(a, b):
    return lax.dot_general(a.astype(BF16), b.astype(BF16), (((0,), (0,)), ((), ())),
                           preferred_element_type=F32)


def _sigmoid(z):
    return 0.5 * jnp.tanh(0.5 * z) + 0.5


def _modnorm(x, g, shift, scale):
    ms = jnp.mean(x * x, axis=-1, keepdims=True)
    y = x * lax.rsqrt(ms + NORM_EPS) * g
    return y * (1.0 + scale) + shift


def _group_sum64(x):
    width = GROUP_SUM_WIDTH
    gi = lax.broadcasted_iota(jnp.int32, (width, width), 0) // 64
    gj = lax.broadcasted_iota(jnp.int32, (width, width), 1) // 64
    ones = jnp.where(gi == gj, 1.0, 0.0).astype(BF16)
    xb = x.astype(BF16)
    outs = [jnp.dot(xb[:, s * width:(s + 1) * width], ones, preferred_element_type=F32)
            for s in range(x.shape[1] // width)]
    return outs[0] if len(outs) == 1 else jnp.concatenate(outs, axis=1)


def _store_transposed(vt_ref, v):
    vt_ref[0] = v.T.astype(BF16)


def _vt_spec(tm):
    nb = S // tm
    return pl.BlockSpec((1, D, tm), lambda i: (i // nb, 0, i % nb))


def _vt_shape():
    return jax.ShapeDtypeStruct((B, D, S), BF16)


def _ada_kernel(c_ref, w_ref, b_ref, o_ref):
    c = c_ref[...]
    cond = c * _sigmoid(c)
    o_ref[0] = _dot(cond, w_ref[0]) + b_ref[0]


def _ada_mods(c, ada_w, ada_b):
    tn = 1536
    out = pl.pallas_call(
        _ada_kernel,
        grid=(DEPTH, 6 * D // tn),
        in_specs=[pl.BlockSpec((B, D), lambda l, j: (0, 0)),
                  pl.BlockSpec((1, D, tn), lambda l, j: (l, 0, j)),
                  pl.BlockSpec((1, 1, tn), lambda l, j: (l, 0, j))],
        out_specs=pl.BlockSpec((1, B, tn), lambda l, j: (l, 0, j)),
        out_shape=jax.ShapeDtypeStruct((DEPTH, B, 6 * D), F32),
        compiler_params=_cparams("parallel", "parallel"),
        name="ada_mods",
    )(c, ada_w, ada_b.reshape(DEPTH, 1, 6 * D))
    return out.reshape(DEPTH, B, 6, D)


def _rwkv_proj_kernel(has_vres, *refs):
    if has_vres:
        (x_ref, xp_ref, mod_ref, ng_ref, mu_ref, wr_ref, wk_ref, wv_ref, w0_ref, w1_ref, w2_ref,
         a0_ref, a1_ref, a2_ref, g1_ref, g2_ref, kk_ref, ka_ref, v0_ref, v1_ref, v2_ref, vf_ref,
         r_out, lw_out, k_out, v_out, kk_out, a_out, g_out) = refs
    else:
        (x_ref, xp_ref, mod_ref, ng_ref, mu_ref, wr_ref, wk_ref, wv_ref, w0_ref, w1_ref, w2_ref,
         a0_ref, a1_ref, a2_ref, g1_ref, g2_ref, kk_ref, ka_ref,
         r_out, lw_out, k_out, v_out, kk_out, a_out, g_out) = refs
    i = pl.program_id(0)
    mod = mod_ref[0]
    shift, scale = mod[0:1], mod[1:2]
    g = ng_ref[...]
    h = _modnorm(x_ref[...], g, shift, scale)
    hp = _modnorm(xp_ref[...], g, shift, scale)
    seq_start = (i % (S // PROJ_TM)) == 0
    prev_row = jnp.where(seq_start, 0.0, hp[7:8, :])
    row = lax.broadcasted_iota(jnp.int32, h.shape, 0)
    prev = jnp.where(row == 0, prev_row, pltpu.roll(h, 1, axis=0))
    delta = prev - h
    mu = mu_ref[...]
    xr = h + delta * mu[0:1]
    xw = h + delta * mu[1:2]
    xk = h + delta * mu[2:3]
    xv = h + delta * mu[3:4]
    xa = h + delta * mu[4:5]
    xg = h + delta * mu[5:6]
    r = _dot(xr, wr_ref[...])
    k = _dot(xk, wk_ref[...])
    v = _dot(xv, wv_ref[...])
    w_raw = w0_ref[...] + _dot(jnp.tanh(_dot(xw, w1_ref[...])), w2_ref[...])
    lw_out[...] = -_sigmoid(w_raw) * EXP_NEG_HALF
    if has_vres:
        mix = _sigmoid(v0_ref[...] + _dot(_dot(xv, v1_ref[...]), v2_ref[...]))
        v = v + (vf_ref[...] - v) * mix
    a = _sigmoid(a0_ref[...] + _dot(_dot(xa, a1_ref[...]), a2_ref[...]))
    g_out[...] = _dot(_sigmoid(_dot(xg, g1_ref[...])), g2_ref[...])
    kk = k * kk_ref[...]
    norm = jnp.sqrt(_group_sum64(kk * kk))
    kk_out[...] = kk / jnp.maximum(norm, 1e-12)
    k_out[...] = k * (1.0 + (a - 1.0) * ka_ref[...])
    r_out[...] = r
    v_out[...] = v
    a_out[...] = a


def _rwkv_proj(x, mod, norm_g, p, v_first):
    has_vres = v_first is not None
    tm = PROJ_TM
    nb = S // tm
    row = lambda a: a.reshape(1, -1)
    full = lambda a: pl.BlockSpec(a.shape, lambda i: (0,) * a.ndim)
    tile = pl.BlockSpec((tm, D), lambda i: (i, 0))
    ops = [x, x, mod, row(norm_g), p["mu"],
           p["w_rkv"][0].astype(BF16), p["w_rkv"][1].astype(BF16), p["w_rkv"][2].astype(BF16),
           row(p["w0"]), p["w1"].astype(BF16), p["w2"].astype(BF16),
           row(p["a0"]), p["a1"].astype(BF16), p["a2"].astype(BF16),
           p["g1"].astype(BF16), p["g2"].astype(BF16), row(p["k_k"]), row(p["k_a"])]
    specs = [tile,
             pl.BlockSpec((8, D), lambda i: (jnp.maximum(i * (tm // 8) - 1, 0), 0)),
             pl.BlockSpec((1, 6, D), lambda i: (i // nb, 0, 0))]
    specs += [full(a) for a in ops[3:]]
    if has_vres:
        extra = [row(p["v0"]), p["v1"].astype(BF16), p["v2"].astype(BF16)]
        ops += extra + [v_first]
        specs += [full(a) for a in extra] + [tile]
    outs = pl.pallas_call(
        functools.partial(_rwkv_proj_kernel, has_vres),
        grid=(T // tm,),
        in_specs=specs,
        out_specs=[tile] * 7,
        out_shape=[jax.ShapeDtypeStruct((T, D), F32)] * 7,
        compiler_params=_cparams("parallel"),
        name="rwkv_proj",
    )(*ops)
    return outs


def _rwkv_scan_kernel(r_ref, lw_ref, k_ref, v_ref, kk_ref, a_ref, g_ref, rk_ref, lg_ref, lb_ref,
                      o_ref, s_ref):
    c = pl.program_id(2)
    W = RW_GROUP * RW_HEAD

    @pl.when(c == 0)
    def _():
        s_ref[...] = jnp.zeros_like(s_ref)

    slabs = [slice(p * W, (p + 1) * W) for p in range(RW_PAR)]
    cols = lambda ref: [ref[:, sl] for sl in slabs]
    out, s_new = _rwkv_blocks(cols(r_ref), cols(lw_ref), cols(k_ref), cols(v_ref), cols(kk_ref),
                              cols(a_ref), cols(g_ref), cols(rk_ref), cols(lg_ref), cols(lb_ref),
                              [s_ref[p] for p in range(RW_PAR)])
    for p, sl in enumerate(slabs):
        o_ref[:, sl] = out[p]
        s_ref[p] = s_new[p]


def _each(fn, *cols):
    return [fn(*args) for args in zip(*cols)]


def _rwkv_blocks(r, lw, k, v, kk, a, g, r_k, lnx_g, lnx_b, s_old):
    C = RW_CHUNK
    W = RW_GROUP * RW_HEAD

    mul = lambda x, y: x * y
    ti = lax.broadcasted_iota(jnp.int32, (C, C), 0)
    si = lax.broadcasted_iota(jnp.int32, (C, C), 1)
    tri = jnp.where(ti >= si, 1.0, 0.0).astype(F32)
    cl = _each(lambda x: jnp.dot(tri, x, precision=HIGHEST, preferred_element_type=F32), lw)
    cl_end = _each(lambda x: x[C - 1:C, :], cl)
    w_t = _each(jnp.exp, cl)
    w_prev = _each(lambda x, y: jnp.exp(x - y), cl, lw)
    w_inv = _each(lambda x: jnp.exp(-x), cl)
    w_rem = _each(lambda x, y: jnp.exp(x - y), cl_end, cl)
    bv = _each(mul, kk, a)
    r_hat = _each(mul, r, w_t)
    a_hat = _each(lambda x, y: -x * y, kk, w_prev)
    b_hat = _each(mul, bv, w_inv)
    k_hat = _each(mul, k, w_inv)
    b_til = _each(mul, bv, w_rem)
    k_til = _each(mul, k, w_rem)

    lane_head = lax.broadcasted_iota(jnp.int32, (C, W), 1) // RW_HEAD

    def stack(m):
        mb = m.astype(BF16)
        zero = jnp.zeros_like(mb)
        return jnp.concatenate([jnp.where(lane_head == hd, mb, zero) for hd in range(RW_GROUP)],
                               axis=0)

    def fold(m):
        return m[0:C] + m[C:2 * C] + m[2 * C:3 * C] + m[3 * C:4 * C]

    cat0 = lambda x, y: jnp.concatenate([x, y], axis=0)
    gram = _each(lambda ah, rh, bh, kh: _dot_nt(cat0(stack(ah), stack(rh)), cat0(stack(bh), stack(kh))),
                 a_hat, r_hat, b_hat, k_hat)
    n = RW_GROUP * C
    ri = lax.broadcasted_iota(jnp.int32, (n, n), 0)
    ci = lax.broadcasted_iota(jnp.int32, (n, n), 1)
    strict = (ri % C) > (ci % C)
    incl = (ri % C) >= (ci % C)
    same_head = (ri // RW_HEAD) == (ci // RW_HEAD)
    eye = jnp.where(ri == ci, 1.0, 0.0)
    l_ab = _each(lambda x: jnp.where(strict, x[:n, :n], 0.0), gram)
    l_ak = _each(lambda x: jnp.where(strict, x[:n, n:], 0.0), gram)
    m_rb = _each(lambda x: jnp.where(incl, x[n:, :n], 0.0), gram)
    m_rk = _each(lambda x: jnp.where(incl, x[n:, n:], 0.0), gram)

    inv = _each(lambda x: eye + x, l_ab)
    pw = l_ab
    for _ in range(int(math.log2(C)) - 1):
        pw = _each(lambda x: _dot(x, x), pw)
        inv = _each(lambda x, y: x + _dot(x, y), inv, pw)

    a_p = _each(lambda x, y: fold(_dot(x, stack(y))), inv, a_hat)
    t_l = _each(_dot, inv, l_ak)
    v_p = _each(lambda x, y: fold(_dot(x, stack(y))), t_l, v)
    r_p = _each(lambda x, y, z: x + fold(_dot(y, stack(z))), r_hat, m_rb, a_p)
    y0 = _each(lambda mb, mk, vp, vv: fold(_dot(jnp.concatenate([mb, mk], axis=1),
                                                cat0(stack(vp), stack(vv)))), m_rb, m_rk, v_p, v)
    a_til = _each(lambda x, y: jnp.where(same_head, _dot_tn(x, y), 0.0), b_til, a_p)
    d_new = _each(lambda vp, vv, bt, kt: jnp.where(same_head, _dot_tn(cat0(vp, vv), cat0(bt, kt)), 0.0),
                  v_p, v, b_til, k_til)
    y = _each(lambda x, s, z: _dot_nt(x, s) + z, r_p, s_old, y0)
    s_new = _each(lambda s, ce, at, dn: s * jnp.exp(ce) + _dot_nt(s, at) + dn,
                  s_old, cl_end, a_til, d_new)

    inv_n = 1.0 / RW_HEAD
    mean = _each(lambda x: _group_sum64(x) * inv_n, y)
    yc = _each(lambda x, m: x - m, y, mean)
    var = _each(lambda x: _group_sum64(x * x) * inv_n, yc)
    yn = _each(lambda x, vr, lg, lb: x * lax.rsqrt(vr + GN_EPS) * lg + lb, yc, var, lnx_g, lnx_b)
    bonus = _each(lambda rr, kx, rk, vv: _group_sum64(rr * kx * rk) * vv, r, k, r_k, v)
    out = _each(lambda x, bo, gg: ((x + bo) * gg).astype(BF16), yn, bonus, g)
    return out, s_new


def _rwkv_scan(r, lw, k, v, kk, a, g, r_k, lnx_g, lnx_b):
    C = RW_CHUNK
    W = RW_PAR * RW_GROUP * RW_HEAD
    nc = S // C
    slab = pl.BlockSpec((C, W), lambda b, gi, c: (b * nc + c, gi))
    prow = pl.BlockSpec((1, W), lambda b, gi, c: (0, gi))
    return pl.pallas_call(
        _rwkv_scan_kernel,
        grid=(B, D // W, nc),
        in_specs=[slab] * 7 + [prow] * 3,
        out_specs=slab,
        out_shape=jax.ShapeDtypeStruct((T, D), BF16),
        scratch_shapes=[pltpu.VMEM((RW_PAR, RW_GROUP * RW_HEAD, RW_GROUP * RW_HEAD), F32)],
        compiler_params=_cparams("parallel", "parallel", "arbitrary"),
        name="rwkv_scan",
    )(r, lw, k, v, kk, a, g, r_k.reshape(1, D), lnx_g.reshape(1, D), lnx_b.reshape(1, D))


def _proj_res_kernel(gate_row, a_ref, w_ref, x_ref, mod_ref, o_ref):
    y = jnp.dot(a_ref[...], w_ref[...], preferred_element_type=F32)
    o_ref[...] = x_ref[...] + mod_ref[0][gate_row:gate_row + 1] * y


def _proj_res(a, w, x, mod, gate_row):
    tm = 512
    nb = S // tm
    kdim = a.shape[1]
    return pl.pallas_call(
        functools.partial(_proj_res_kernel, gate_row),
        grid=(T // tm,),
        in_specs=[pl.BlockSpec((tm, kdim), lambda i: (i, 0)),
                  pl.BlockSpec((kdim, D), lambda i: (0, 0)),
                  pl.BlockSpec((tm, D), lambda i: (i, 0)),
                  pl.BlockSpec((1, 6, D), lambda i: (i // nb, 0, 0))],
        out_specs=pl.BlockSpec((tm, D), lambda i: (i, 0)),
        out_shape=jax.ShapeDtypeStruct((T, D), F32),
        compiler_params=_cparams("parallel"),
        name="proj_res",
    )(a, w.astype(BF16), x, mod)


def _ffn_kernel(nf, x_ref, mod_ref, ng_ref, wg_ref, wu_ref, wd_ref, o_ref, h_scr, acc_scr):
    f = pl.program_id(1)

    @pl.when(f == 0)
    def _():
        mod = mod_ref[0]
        h_scr[...] = _modnorm(x_ref[...], ng_ref[...], mod[3:4], mod[4:5]).astype(BF16)
        acc_scr[...] = jnp.zeros_like(acc_scr)

    for sub in range(h_scr.shape[0] // FFN_SUB):
        rows = slice(sub * FFN_SUB, (sub + 1) * FFN_SUB)
        h = h_scr[rows]
        gt = jnp.dot(h, wg_ref[0], preferred_element_type=F32)
        up = jnp.dot(h, wu_ref[0], preferred_element_type=F32)
        act = (gt * _sigmoid(gt) * up).astype(BF16)
        acc_scr[rows] += jnp.dot(act, wd_ref[0], preferred_element_type=F32)

    @pl.when(f == nf - 1)
    def _():
        o_ref[...] = x_ref[...] + mod_ref[0][5:6] * acc_scr[...]


def _ffn_dense(x, mod, norm_g, wgu, wd, layer):
    tm, tf = FFN_TM, FFN_TF
    nf = D_FF // tf
    nb = S // tm
    return pl.pallas_call(
        functools.partial(_ffn_kernel, nf),
        grid=(T // tm, nf),
        in_specs=[pl.BlockSpec((tm, D), lambda i, f: (i, 0)),
                  pl.BlockSpec((1, 6, D), lambda i, f: (i // nb, 0, 0)),
                  pl.BlockSpec((1, D), lambda i, f: (0, 0)),
                  pl.BlockSpec((1, D, tf), lambda i, f: (layer, 0, f)),
                  pl.BlockSpec((1, D, tf), lambda i, f: (layer, 0, nf + f)),
                  pl.BlockSpec((1, tf, D), lambda i, f: (layer, f, 0))],
        out_specs=pl.BlockSpec((tm, D), lambda i, f: (i, 0)),
        out_shape=jax.ShapeDtypeStruct((T, D), F32),
        scratch_shapes=[pltpu.VMEM((tm, D), BF16), pltpu.VMEM((tm, D), F32)],
        compiler_params=_cparams("parallel", "arbitrary"),
        name="ffn_dense",
    )(x, mod, norm_g.reshape(1, D), wgu, wgu, wd)


def _route_kernel(x_ref, mod_ref, ng_ref, wr_ref, br_ref,
                  h_out, cw_t_out, dest_t_out, cnt_out, carry_scr):
    i = pl.program_id(0)
    tm = x_ref.shape[0]
    mod = mod_ref[0]
    h = _modnorm(x_ref[...], ng_ref[...], mod[3:4], mod[4:5])
    h_out[...] = h.astype(BF16)
    logits = jnp.dot(h, wr_ref[...], precision=HIGHEST, preferred_element_type=F32) + br_ref[...]
    lane = lax.broadcasted_iota(jnp.int32, logits.shape, 1)
    neg = -jnp.inf
    logits = jnp.where(lane < N_EXPERTS, logits, neg)
    m1 = jnp.max(logits, axis=-1, keepdims=True)
    i1 = jnp.min(jnp.where(logits == m1, lane, LANES), axis=-1, keepdims=True)
    rest = jnp.where(lane == i1, neg, logits)
    m2 = jnp.max(rest, axis=-1, keepdims=True)
    i2 = jnp.min(jnp.where(rest == m2, lane, LANES), axis=-1, keepdims=True)
    e2 = jnp.exp(m2 - m1)
    w1 = 1.0 / (1.0 + e2)
    w2 = e2 / (1.0 + e2)
    cw_t_out[...] = (jnp.where(lane == i1, w1, 0.0) + jnp.where(lane == i2, w2, 0.0)).T

    sel =jnp.where((lane == i1) | (lane == i2), 1.0, 0.0)
    ti = lax.broadcasted_iota(jnp.int32, (tm, tm), 0)
    si = lax.broadcasted_iota(jnp.int32, (tm, tm), 1)
    before = jnp.where(ti > si, 1.0, 0.0).astype(BF16)
    window_start = (i % (MOE_TW // tm)) == 0
    carry = jnp.where(window_start, 0.0, carry_scr[...])
    rank = jnp.dot(before, sel.astype(BF16), preferred_element_type=F32) + carry
    dest_t_out[...] = jnp.where(sel > 0.0, rank, -1.0).T
    total = carry + jnp.sum(sel, axis=0, keepdims=True)
    carry_scr[...] = total
    cnt_out[0] = jnp.broadcast_to(total, (8, LANES))


def _moe_kernel(nf, cnt_ref, x_ref, mod_ref, h_ref, dest_t_ref, cw_t_ref, wg_ref, wu_ref, wd_ref,
                o_ref, xg_scr, yc_scr):
    w = pl.program_id(0)
    e = pl.program_id(1)
    f = pl.program_id(2)
    tw = MOE_TW
    n = cnt_ref[w * N_EXPERTS + e]

    @pl.when((e == 0) & (f == 0))
    def _():
        o_ref[...] = jnp.zeros_like(o_ref)

    dest_row = dest_t_ref[pl.ds(e, 1), :]
    cw_row = cw_t_ref[pl.ds(e, 1), :]

    def slot_tile(base, rows):
        slot = lax.broadcasted_iota(jnp.int32, (rows, tw), 0).astype(F32) + base
        match = dest_row == slot
        select = jnp.where(match, 1.0, 0.0).astype(BF16)
        span = pl.ds(base, rows)

        @pl.when(f == 0)
        def _():
            xg_scr[span, :] = jnp.dot(select, h_ref[...], preferred_element_type=F32).astype(BF16)

        xs = xg_scr[span, :]
        gt = jnp.dot(xs, wg_ref[0, 0], preferred_element_type=F32)
        up = jnp.dot(xs, wu_ref[0, 0], preferred_element_type=F32)
        act = (gt * _sigmoid(gt) * up).astype(BF16)
        y = jnp.dot(act, wd_ref[0, 0], preferred_element_type=F32)

        @pl.when(f == 0)
        def _():
            yc_scr[span, :] = y

        @pl.when((f > 0) & (f < nf - 1))
        def _():
            yc_scr[span, :] += y

        @pl.when(f == nf - 1)
        def _():
            w_slot = jnp.sum(jnp.where(match, cw_row, 0.0), axis=-1, keepdims=True)
            weighted = ((yc_scr[span, :] + y) * w_slot).astype(BF16)
            o_ref[...] += _dot_tn(select, weighted)

    sizes = list(range(MOE_STEP, MOE_MAX + 1, MOE_STEP))
    for lo, rows in zip([0] + sizes[:-1], sizes):
        pl.when((n > lo) & (n <= rows))(functools.partial(slot_tile, 0, rows))

    @pl.when(n > MOE_MAX)
    def _():
        def body(s, carry):
            slot_tile(pl.multiple_of(s * MOE_MAX, MOE_MAX), MOE_MAX)
            return carry
        lax.fori_loop(0, (n + MOE_MAX - 1) // MOE_MAX, body, 0)

    @pl.when((e == N_EXPERTS - 1) & (f == nf - 1))
    def _():
        o_ref[...] = x_ref[...] + mod_ref[0][5:6] * o_ref[...]


def _moe(x, mod, norm_g, w_router, b_router, wgu, wd, layer):
    tm, tf, tw = ROUTE_TM, FFN_TF, MOE_TW
    nf = D_FF // tf
    nb = S // tm
    nw = T // tw
    wr =jnp.zeros((D, LANES), F32).at[:, :N_EXPERTS].set(w_router)
    br = jnp.zeros((1, LANES), F32).at[0, :N_EXPERTS].set(b_router)
    tile = lambda width: pl.BlockSpec((tm, width), lambda i: (i, 0))
    assert nf >= 2
    lanes_t = pl.BlockSpec((LANES, tm), lambda i: (0, i))
    h, cw_t, dest_t, cnt = pl.pallas_call(
        _route_kernel,
        grid=(T // tm,),
        in_specs=[tile(D),
                  pl.BlockSpec((1, 6, D), lambda i: (i // nb, 0, 0)),
                  pl.BlockSpec((1, D), lambda i: (0, 0)),
                  pl.BlockSpec((D, LANES), lambda i: (0, 0)),
                  pl.BlockSpec((1, LANES), lambda i: (0, 0))],
        out_specs=[tile(D), lanes_t, lanes_t,
                   pl.BlockSpec((1, 8, LANES), lambda i: (i // (tw // tm), 0, 0))],
        out_shape=[jax.ShapeDtypeStruct((T, D), BF16), jax.ShapeDtypeStruct((LANES, T), F32),
                   jax.ShapeDtypeStruct((LANES, T), F32), jax.ShapeDtypeStruct((nw, 8, LANES), F32)],
        scratch_shapes=[pltpu.VMEM((1, LANES), F32)],
        compiler_params=_cparams("arbitrary"),
        name="moe_route",
    )(x, mod, norm_g.reshape(1, D), wr, br)
    counts = cnt[:, 0, :N_EXPERTS].astype(jnp.int32).reshape(nw * N_EXPERTS)
    cap = -(-tw // MOE_MAX) * MOE_MAX
    return pl.pallas_call(
        functools.partial(_moe_kernel, nf),
        grid_spec=pltpu.PrefetchScalarGridSpec(
            num_scalar_prefetch=1,
            grid=(nw, N_EXPERTS, nf),
            in_specs=[pl.BlockSpec((tw, D), lambda w, e, f, c: (w, 0)),
                      pl.BlockSpec((1, 6, D), lambda w, e, f, c: (w // (S // tw), 0, 0)),
                      pl.BlockSpec((tw, D), lambda w, e, f, c: (w, 0)),
                      pl.BlockSpec((LANES, tw), lambda w, e, f, c: (0, w)),
                      pl.BlockSpec((LANES, tw), lambda w, e, f, c: (0, w)),
                      pl.BlockSpec((1, 1, D, tf), lambda w, e, f, c: (layer, e, 0, f)),
                      pl.BlockSpec((1, 1, D, tf), lambda w, e, f, c: (layer, e, 0, nf + f)),
                      pl.BlockSpec((1, 1, tf, D), lambda w, e, f, c: (layer, e, f, 0))],
            out_specs=pl.BlockSpec((tw, D), lambda w, e, f, c: (w, 0)),
            scratch_shapes=[pltpu.VMEM((cap, D), BF16), pltpu.VMEM((cap, D), F32)]),
        out_shape=jax.ShapeDtypeStruct((T, D), F32),
        compiler_params=_cparams("parallel", "arbitrary", "arbitrary"),
        name="moe_experts",
    )(counts, x, mod, h, dest_t, cw_t, wgu, wgu, wd)


def _mla_proj_kernel(x_ref, mod_ref, ng_ref, pos_ref, invf_ref, wd_ref, qg_ref, kvg_ref,
                     wuq_ref, wukv_ref, q_out, k_out, v_out):
    mod = mod_ref[0]
    h = _modnorm(x_ref[...], ng_ref[...], mod[0:1], mod[1:2])
    down = _dot(h, wd_ref[...])
    cq = down[:, :MLA_Q_LORA]
    ckv = down[:, MLA_Q_LORA:MLA_Q_LORA + MLA_KV_LORA]
    kr = down[:, MLA_Q_LORA + MLA_KV_LORA:]
    cq = cq * lax.rsqrt(jnp.mean(cq * cq, axis=-1, keepdims=True) + NORM_EPS) * qg_ref[...]
    ckv = ckv * lax.rsqrt(jnp.mean(ckv * ckv, axis=-1, keepdims=True) + NORM_EPS) * kvg_ref[...]

    ang = pos_ref[...] * invf_ref[...]
    lane = lax.broadcasted_iota(jnp.int32, ang.shape, 1)
    is_rope = (lane >= MLA_NOPE) & (lane < MLA_NOPE + MLA_ROPE)
    cos_r = jnp.where(is_rope, jnp.cos(ang), 0.0)
    sin_r = jnp.where(is_rope, jnp.sin(ang), 0.0)
    cos_q = jnp.where(lane < MLA_NOPE, 1.0, cos_r)
    shift = LANES - MLA_ROPE

    k_rope = kr * cos_r + pltpu.roll(kr, shift, axis=1) * sin_r

    qf = _dot(cq, wuq_ref[...])
    kv = _dot(ckv, wukv_ref[...])
    for hd in range(MLA_HEADS):
        sl = slice(hd * LANES, (hd + 1) * LANES)
        qs = qf[:, sl]
        q_out[:, sl] = ((qs * cos_q + pltpu.roll(qs, shift, axis=1) * sin_r) * MLA_SCALE).astype(BF16)
        k_out[:, sl] = (kv[:, sl] + k_rope).astype(BF16)
    _store_transposed(v_out, kv[:, MLA_HEADS * LANES:])


def _rot_half_cols(w):
    half = MLA_ROPE // 2
    return jnp.concatenate([-w[..., half:], w[..., :half]], axis=-1)


def _mla_proj(x, mod, norm_g, positions, w_down, q_norm_g, kv_norm_g, w_uq, w_ukv):
    tm = PROJ_TM
    nb = S // tm
    half = MLA_ROPE // 2
    wr = w_down[:, MLA_Q_LORA + MLA_KV_LORA:]
    wd = jnp.concatenate([w_down[:, :MLA_Q_LORA + MLA_KV_LORA],
                          jnp.zeros((D, MLA_NOPE), F32), wr, _rot_half_cols(wr)], axis=1).astype(BF16)
    wq = w_uq.reshape(MLA_Q_LORA, MLA_HEADS, MLA_NOPE + MLA_ROPE)
    wq_r = wq[..., MLA_NOPE:]
    wuq = jnp.concatenate([wq[..., :MLA_NOPE], wq_r, _rot_half_cols(wq_r)], axis=-1)
    wuq = wuq.reshape(MLA_Q_LORA, MLA_HEADS * LANES).astype(BF16)
    wkv = w_ukv.reshape(MLA_KV_LORA, MLA_HEADS, 2 * MLA_NOPE)
    wk = jnp.concatenate([wkv[..., :MLA_NOPE], jnp.zeros_like(wkv[..., :MLA_NOPE])], axis=-1)
    wukv = jnp.concatenate([wk.reshape(MLA_KV_LORA, MLA_HEADS * LANES),
                            wkv[..., MLA_NOPE:].reshape(MLA_KV_LORA, D)], axis=1).astype(BF16)
    inv_freq = ROPE_BASE ** (-jnp.arange(half, dtype=F32) / half)
    lane = jnp.arange(LANES)
    invf = jnp.where((lane >= MLA_NOPE) & (lane < MLA_NOPE + MLA_ROPE),
                     inv_freq[(lane - MLA_NOPE) % half], 0.0).reshape(1, LANES).astype(F32)
    pos = jnp.broadcast_to(positions.reshape(T, 1).astype(F32), (T, LANES))
    full = lambda a: pl.BlockSpec(a.shape, lambda i: (0,) * a.ndim)
    ops = [x, mod, norm_g.reshape(1, D), pos, invf, wd, q_norm_g.reshape(1, -1),
           kv_norm_g.reshape(1, -1), wuq, wukv]
    specs = [pl.BlockSpec((tm, D), lambda i: (i, 0)),
             pl.BlockSpec((1, 6, D), lambda i: (i // nb, 0, 0)),
             full(ops[2]),
             pl.BlockSpec((tm, LANES), lambda i: (i, 0))] + [full(a) for a in ops[4:]]
    qw = MLA_HEADS * LANES
    return pl.pallas_call(
        _mla_proj_kernel,
        grid=(T // tm,),
        in_specs=specs,
        out_specs=[pl.BlockSpec((tm, qw), lambda i: (i, 0)),
                   pl.BlockSpec((tm, qw), lambda i: (i, 0)),
                   _vt_spec(tm)],
        out_shape=[jax.ShapeDtypeStruct((T, qw), BF16), jax.ShapeDtypeStruct((T, qw), BF16),
                   _vt_shape()],
        compiler_params=_cparams("parallel"),
        name="mla_proj",
    )(*ops)


def _fox_proj_kernel(x_ref, mod_ref, ng_ref, w_ref, wf_ref, bf_ref, qg_ref, kg_ref,
                     q_out, k_out, v_out, og_out, fc_out, fr_out, carry_scr):
    i = pl.program_id(0)
    tm = x_ref.shape[0]
    mod = mod_ref[0]
    h = _modnorm(x_ref[...], ng_ref[...], mod[0:1], mod[1:2]).astype(BF16)
    q = jnp.dot(h, w_ref[:, 0:D], preferred_element_type=F32)
    k = jnp.dot(h, w_ref[:, D:2 * D], preferred_element_type=F32)
    v = jnp.dot(h, w_ref[:, 2 * D:3 * D], preferred_element_type=F32)
    og = jnp.dot(h, w_ref[:, 3 * D:4 * D], preferred_element_type=F32)
    inv_n = 1.0 / FOX_HEAD
    q = q * lax.rsqrt(_group_sum64(q * q) * inv_n + NORM_EPS) * qg_ref[...] * FOX_SCALE
    k = k * lax.rsqrt(_group_sum64(k * k) * inv_n + NORM_EPS) * kg_ref[...]
    q_out[...] = q.astype(BF16)
    k_out[...] = k.astype(BF16)
    _store_transposed(v_out, v)
    og_out[...] = _sigmoid(og).astype(BF16)

    z = jnp.dot(h, wf_ref[...], preferred_element_type=F32) + bf_ref[...]
    lane = lax.broadcasted_iota(jnp.int32, z.shape, 1)
    log_f = jnp.minimum(z, 0.0) - jnp.log(1.0 + jnp.exp(-jnp.abs(z)))
    log_f = jnp.where(lane < FOX_HEADS, log_f, 0.0)
    ti = lax.broadcasted_iota(jnp.int32, (tm, tm), 0)
    si = lax.broadcasted_iota(jnp.int32, (tm, tm), 1)
    tri = jnp.where(ti >= si, 1.0, 0.0).astype(F32)
    seq_start = (i % (S // tm)) == 0
    carry = jnp.where(seq_start, 0.0, carry_scr[...])
    cum = jnp.dot(tri, log_f, precision=HIGHEST, preferred_element_type=F32) + carry
    carry_scr[...] = cum[tm - 1:tm, :]
    fc_out[...] = cum
    fr_out[0] = cum.T


def _fox_proj(x, mod, norm_g, w_in, b_f, q_norm_g, k_norm_g):
    tm = PROJ_TM
    nb = S // tm
    w_main = jnp.concatenate([w_in[:, :3 * D], w_in[:, 3 * D + FOX_HEADS:]], axis=1).astype(BF16)
    w_f = jnp.zeros((D, LANES), F32).at[:, :FOX_HEADS].set(w_in[:, 3 * D:3 * D + FOX_HEADS]).astype(BF16)
    bf = jnp.zeros((1, LANES), F32).at[0, :FOX_HEADS].set(b_f)
    qg = jnp.tile(q_norm_g, FOX_HEADS).reshape(1, D)
    kg = jnp.tile(k_norm_g, FOX_HEADS).reshape(1, D)
    full = lambda a: pl.BlockSpec(a.shape, lambda i: (0,) * a.ndim)
    tile = pl.BlockSpec((tm, D), lambda i: (i, 0))
    ops = [x, mod, norm_g.reshape(1, D), w_main, w_f, bf, qg, kg]
    specs = [tile, pl.BlockSpec((1, 6, D), lambda i: (i // nb, 0, 0))] + [full(a) for a in ops[2:]]
    return pl.pallas_call(
        _fox_proj_kernel,
        grid=(T // tm,),
        in_specs=specs,
        out_specs=[tile, tile, _vt_spec(tm), tile,
                   pl.BlockSpec((tm, LANES), lambda i: (i, 0)),
                   pl.BlockSpec((1, LANES, tm), lambda i: (i // nb, 0, i % nb))],
        out_shape=[jax.ShapeDtypeStruct((T, D), BF16)] * 2 + [_vt_shape(), jax.ShapeDtypeStruct((T, D), BF16),
                   jax.ShapeDtypeStruct((T, LANES), F32), jax.ShapeDtypeStruct((B, LANES, S), F32)],
        scratch_shapes=[pltpu.VMEM((1, LANES), F32)],
        compiler_params=_cparams("arbitrary"),
        name="fox_proj",
    )(*ops)


def _attn_kernel(fox, *refs):
    if fox:
        q_ref, k_ref, vt_ref, og_ref, fc_ref, fr_ref, o_ref, fs_scr = refs
    else:
        q_ref, k_ref, vt_ref, o_ref = refs
    tq = ATT_TQ
    nq = S // tq
    w2 = 2 * tq
    p = pl.program_id(1)
    qi = pl.program_id(2)

    if fox:
        @pl.when(qi == 0)
        def _():
            fcv = fc_ref[...]
            lane_s = lax.broadcasted_iota(jnp.int32, fcv.shape, 1)
            for j in range(2):
                colv = jnp.sum(jnp.where(lane_s == 2 * p + j, fcv, 0.0), axis=-1, keepdims=True)
                fs_scr[j] = jnp.broadcast_to(colv, fcv.shape)

    qb = q_ref[...]
    lane = lax.broadcasted_iota(jnp.int32, qb.shape, 1)
    per_head = FOX_HEAD if fox else LANES
    zero = jnp.zeros_like(qb)
    q_bd = jnp.concatenate([jnp.where(lane // per_head == j, qb, zero) for j in range(2)], axis=0)
    if fox:
        f_t = jnp.concatenate([fr_ref[0, pl.ds(2 * p + j, 1), :] for j in range(2)], axis=1)

    srow = lax.broadcasted_iota(jnp.int32, (tq, w2), 0)
    tcol = lax.broadcasted_iota(jnp.int32, (tq, w2), 1) % tq
    diag_mask = (srow <= tcol) if fox else ((srow // 64) <= (tcol // 64))
    sub = lax.broadcasted_iota(jnp.int32, (LANES, tq), 0)
    reps = tq // LANES

    def scores(lo, hi):
        st = lax.dot_general(k_ref[lo:hi, :], q_bd, (((1,), (1,)), ((), ())),
                             preferred_element_type=F32)
        if fox:
            f_s = jnp.concatenate([fs_scr[0, lo:hi]] * reps + [fs_scr[1, lo:hi]] * reps, axis=1)
            st = st + (f_t - f_s)
        return st

    def values(lo, hi, pt):
        vt_ones = jnp.concatenate([vt_ref[0, :, lo:hi], jnp.ones((16, hi - lo), BF16)], axis=0)
        return jnp.dot(vt_ones, pt.astype(BF16), preferred_element_type=F32)

    def query_tile(c):
        n_head = c * tq
        st_tail = jnp.where(diag_mask, scores(n_head, n_head + tq), -jnp.inf)
        mx = jnp.max(st_tail, axis=0, keepdims=True)
        if c > 0:
            st_head = scores(0, n_head)
            mx = jnp.maximum(mx, jnp.max(st_head, axis=0, keepdims=True))
        acc = values(n_head, n_head + tq, jnp.exp(st_tail - mx))
        if c > 0:
            acc = acc + values(0, n_head, jnp.exp(st_head - mx))
        out = acc[:LANES] / acc[LANES:LANES + 1]
        o = jnp.where(sub // 64 == 0, out[:, :tq], out[:, tq:]).T
        if fox:
            o = o * og_ref[...].astype(F32)
        o_ref[...] = o.astype(BF16)

    for c in range(nq):
        pl.when(qi == c)(functools.partial(query_tile, c))


def _attention(q, k, vt, fox_extras=None):
    fox = fox_extras is not None
    tq = ATT_TQ
    nq = S // tq
    qw = LANES if fox else 2 * LANES
    ops = [q, k, vt]
    specs = [pl.BlockSpec((tq, qw), lambda b, p, i: (b * nq + i, p)),
             pl.BlockSpec((S, qw), lambda b, p, i: (b, p)),
             pl.BlockSpec((1, LANES, S), lambda b, p, i: (b, p, 0))]
    scratch = []
    if fox:
        og, fc, fr = fox_extras
        ops += [og, fc, fr]
        specs += [pl.BlockSpec((tq, LANES), lambda b, p, i: (b * nq + i, p)),
                  pl.BlockSpec((S, LANES), lambda b, p, i: (b, 0)),
                  pl.BlockSpec((1, FOX_HEADS, tq), lambda b, p, i: (b, 0, i))]
        scratch.append(pltpu.VMEM((2, S, LANES), F32))
    return pl.pallas_call(
        functools.partial(_attn_kernel, fox),
        grid=(B, 8, nq),
        in_specs=specs,
        out_specs=pl.BlockSpec((tq, LANES), lambda b, p, i: (b * nq + i, p)),
        out_shape=jax.ShapeDtypeStruct((T, D), BF16),
        scratch_shapes=scratch,
        compiler_params=_cparams("parallel", "parallel", "arbitrary"),
        name="fox_attn" if fox else "mla_attn",
    )(*ops)


def _final_norm_kernel(x_ref, g_ref, o_ref):
    x = x_ref[...]
    o_ref[...] = x * lax.rsqrt(jnp.mean(x * x, axis=-1, keepdims=True) + NORM_EPS) * g_ref[...]


def _final_norm(x, g):
    tm = 512
    return pl.pallas_call(
        _final_norm_kernel,
        grid=(T // tm,),
        in_specs=[pl.BlockSpec((tm, D), lambda i: (i, 0)), pl.BlockSpec((1, D), lambda i: (0, 0))],
        out_specs=pl.BlockSpec((tm, D), lambda i: (i, 0)),
        out_shape=jax.ShapeDtypeStruct((T, D), F32),
        compiler_params=_cparams("parallel"),
        name="final_norm",
    )(x, g.reshape(1, D))


def kernel(x, c, positions, ada_w, ada_b, norm_mix_g, norm_ffn_g, final_norm_g, rw_mu, rw_w_rkv, rw_w_o, rw_w0, rw_w1, rw_w2, rw_a0, rw_a1, rw_a2, rw_g1, rw_g2, rw_k_k, rw_k_a, rw_r_k, rw_lnx_g, rw_lnx_b, rw_v0, rw_v1, rw_v2, mla_w_down, mla_q_norm_g, mla_kv_norm_g, mla_w_uq, mla_w_ukv, mla_w_o, fox_w_in, fox_b_f, fox_q_norm_g, fox_k_norm_g, fox_w_o, ffn_w_gate_up, ffn_w_down, moe_w_router, moe_b_router, moe_w_gate_up, moe_w_down):
    xf = x.reshape(T, D)
    mods = _ada_mods(c, ada_w, ada_b)
    ffn_wgu, ffn_wd = ffn_w_gate_up.astype(BF16), ffn_w_down.astype(BF16)
    moe_wgu, moe_wd = moe_w_gate_up.astype(BF16), moe_w_down.astype(BF16)
    v_first = None
    for i in range(DEPTH):
        mod = mods[i]
        kind, j = i % 3, i // 3
        if kind == 0:
            p = dict(mu=rw_mu[j], w_rkv=rw_w_rkv[j], w0=rw_w0[j], w1=rw_w1[j], w2=rw_w2[j],
                     a0=rw_a0[j], a1=rw_a1[j], a2=rw_a2[j], g1=rw_g1[j], g2=rw_g2[j],
                     k_k=rw_k_k[j], k_a=rw_k_a[j])
            if j > 0:
                p.update(v0=rw_v0[j - 1], v1=rw_v1[j - 1], v2=rw_v2[j - 1])
            r, lw, k, v, kk, a, g = _rwkv_proj(xf, mod, norm_mix_g[i], p, v_first if j > 0 else None)
            if j == 0:
                v_first = v
            y = _rwkv_scan(r, lw, k, v, kk, a, g, rw_r_k[j], rw_lnx_g[j], rw_lnx_b[j])
            xf = _proj_res(y, rw_w_o[j], xf, mod, 2)
        elif kind == 1:
            q, k, v = _mla_proj(xf, mod, norm_mix_g[i], positions, mla_w_down[j], mla_q_norm_g[j],
                                mla_kv_norm_g[j], mla_w_uq[j], mla_w_ukv[j])
            o = _attention(q, k, v)
            xf = _proj_res(o, mla_w_o[j], xf, mod, 2)
        else:
            q, k, v, og, fc, fr = _fox_proj(xf, mod, norm_mix_g[i], fox_w_in[j], fox_b_f[j],
                                            fox_q_norm_g[j], fox_k_norm_g[j])
            o = _attention(q, k, v, (og, fc, fr))
            xf = _proj_res(o, fox_w_o[j], xf, mod, 2)
        if i % 2 == 0:
            xf = _ffn_dense(xf, mod, norm_ffn_g[i], ffn_wgu, ffn_wd, i // 2)
        else:
            xf = _moe(xf, mod, norm_ffn_g[i], moe_w_router[i // 2], moe_b_router[i // 2],
                      moe_wgu, moe_wd, i // 2)
    return _final_norm(xf, final_norm_g).reshape(B, S, D)
```

```python
import functools
import math

import jax
import jax.numpy as jnp
from jax import lax
from jax.experimental import pallas as pl
from jax.experimental.pallas import tpu as pltpu

F32 = jnp.float32
BF16 = jnp.bfloat16
HIGHEST = lax.Precision.HIGHEST

D = 1024
B = 8
S = 2048
T = B * S
DEPTH = 4
NORM_EPS = 1e-6

RW_HEAD = 64
GN_EPS = 64e-5
EXP_NEG_HALF = math.exp(-0.5)

MLA_HEADS = 16
MLA_NOPE = 64
MLA_ROPE = 32
MLA_Q_LORA = 768
MLA_KV_LORA = 256
MLA_SCALE = (MLA_NOPE + MLA_ROPE) ** -0.5
ROPE_BASE = 10000.0

FOX_HEADS = 16
FOX_HEAD = 64
FOX_SCALE = FOX_HEAD ** -0.5

D_FF = 2816
N_EXPERTS = 8

LANES = 128
GROUP_SUM_WIDTH = 256
VMEM_LIMIT = 56 * 1024 * 1024

RW_CHUNK = 64
RW_GROUP = 4
RW_PAR = 4
ATT_TQ = 256
ATT_PAIRS = 4
FFN_TM = 1024
FFN_SUB = 512
ROUTE_TM = 512
FFN_TF = 1408
PROJ_TM = 256
MOE_TW = 1024
MOE_STEP = 128
MOE_MAX = 512


def _cparams(*sem):
    return pltpu.CompilerParams(dimension_semantics=sem, vmem_limit_bytes=VMEM_LIMIT)


def _dot(a, b):
    return jnp.dot(a.astype(BF16), b.astype(BF16), preferred_element_type=F32)


def _dot_nt(a, b):
    return lax.dot_general(a.astype(BF16), b.astype(BF16), (((1,), (1,)), ((), ())),
                           preferred_element_type=F32)


def _dot_tn(a, b):
    return lax.dot_general(a.astype(BF16), b.astype(BF16), (((0,), (0,)), ((), ())),
                           preferred_element_type=F32)


def _sigmoid(z):
    return 0.5 * jnp.tanh(0.5 * z) + 0.5


def _modnorm(x, g, shift, scale):
    ms = jnp.mean(x * x, axis=-1, keepdims=True)
    y = x * lax.rsqrt(ms + NORM_EPS) * g
    return y * (1.0 + scale) + shift


def _group_sum64(x):
    width = GROUP_SUM_WIDTH
    gi = lax.broadcasted_iota(jnp.int32, (width, width), 0) // 64
    gj = lax.broadcasted_iota(jnp.int32, (width, width), 1) // 64
    ones = jnp.where(gi == gj, 1.0, 0.0).astype(BF16)
    xb = x.astype(BF16)
    outs = [jnp.dot(xb[:, s * width:(s + 1) * width], ones, preferred_element_type=F32)
            for s in range(x.shape[1] // width)]
    return outs[0] if len(outs) == 1 else jnp.concatenate(outs, axis=1)


def _store_transposed(vt_ref, v):
    vt_ref[0] = v.T.astype(BF16)


def _vt_spec(tm):
    nb = S // tm
    return pl.BlockSpec((1, D, tm), lambda i: (i // nb, 0, i % nb))


def _vt_shape():
    return jax.ShapeDtypeStruct((B, D, S), BF16)


def _ada_kernel(c_ref, w_ref, b_ref, o_ref):
    c = c_ref[...]
    cond = c * _sigmoid(c)
    o_ref[0] = _dot(cond, w_ref[0]) + b_ref[0]


def _ada_mods(c, ada_w, ada_b):
    tn = 1536
    out = pl.pallas_call(
        _ada_kernel,
        grid=(DEPTH, 6 * D // tn),
        in_specs=[pl.BlockSpec((B, D), lambda l, j: (0, 0)),
                  pl.BlockSpec((1, D, tn), lambda l, j: (l, 0, j)),
                  pl.BlockSpec((1, 1, tn), lambda l, j: (l, 0, j))],
        out_specs=pl.BlockSpec((1, B, tn), lambda l, j: (l, 0, j)),
        out_shape=jax.ShapeDtypeStruct((DEPTH, B, 6 * D), F32),
        compiler_params=_cparams("parallel", "parallel"),
        name="ada_mods",
    )(c, ada_w, ada_b.reshape(DEPTH, 1, 6 * D))
    return out.reshape(DEPTH, B, 6, D)


def _rwkv_proj_kernel(has_vres, *refs):
    if has_vres:
        (x_ref, xp_ref, mod_ref, ng_ref, mu_ref, wr_ref, wk_ref, wv_ref, w0_ref, w1_ref, w2_ref,
         a0_ref, a1_ref, a2_ref, g1_ref, g2_ref, kk_ref, ka_ref, v0_ref, v1_ref, v2_ref, vf_ref,
         r_out, lw_out, k_out, v_out, kk_out, a_out, g_out) = refs
    else:
        (x_ref, xp_ref, mod_ref, ng_ref, mu_ref, wr_ref, wk_ref, wv_ref, w0_ref, w1_ref, w2_ref,
         a0_ref, a1_ref, a2_ref, g1_ref, g2_ref, kk_ref, ka_ref,
         r_out, lw_out, k_out, v_out, kk_out, a_out, g_out) = refs
    i = pl.program_id(0)
    mod = mod_ref[0]
    shift, scale = mod[0:1], mod[1:2]
    g = ng_ref[...]
    h = _modnorm(x_ref[...], g, shift, scale)
    hp = _modnorm(xp_ref[...], g, shift, scale)
    seq_start = (i % (S // PROJ_TM)) == 0
    prev_row = jnp.where(seq_start, 0.0, hp[7:8, :])
    row = lax.broadcasted_iota(jnp.int32, h.shape, 0)
    prev = jnp.where(row == 0, prev_row, pltpu.roll(h, 1, axis=0))
    delta = prev - h
    mu = mu_ref[...]
    xr = h + delta * mu[0:1]
    xw = h + delta * mu[1:2]
    xk = h + delta * mu[2:3]
    xv = h + delta * mu[3:4]
    xa = h + delta * mu[4:5]
    xg = h + delta * mu[5:6]
    r = _dot(xr, wr_ref[...])
    k = _dot(xk, wk_ref[...])
    v = _dot(xv, wv_ref[...])
    w_raw = w0_ref[...] + _dot(jnp.tanh(_dot(xw, w1_ref[...])), w2_ref[...])
    lw_out[...] = -_sigmoid(w_raw) * EXP_NEG_HALF
    if has_vres:
        mix = _sigmoid(v0_ref[...] + _dot(_dot(xv, v1_ref[...]), v2_ref[...]))
        v = v + (vf_ref[...] - v) * mix
    a = _sigmoid(a0_ref[...] + _dot(_dot(xa, a1_ref[...]), a2_ref[...]))
    g_out[...] = _dot(_sigmoid(_dot(xg, g1_ref[...])), g2_ref[...])
    kk = k * kk_ref[...]
    norm = jnp.sqrt(_group_sum64(kk * kk))
    kk_out[...] = kk / jnp.maximum(norm, 1e-12)
    k_out[...] = k * (1.0 + (a - 1.0) * ka_ref[...])
    r_out[...] = r
    v_out[...] = v
    a_out[...] = a


def _rwkv_proj(x, mod, norm_g, p, v_first):
    has_vres = v_first is not None
    tm = PROJ_TM
    nb = S // tm
    row = lambda a: a.reshape(1, -1)
    full = lambda a: pl.BlockSpec(a.shape, lambda i: (0,) * a.ndim)
    tile = pl.BlockSpec((tm, D), lambda i: (i, 0))
    ops = [x, x, mod, row(norm_g), p["mu"],
           p["w_rkv"][0].astype(BF16), p["w_rkv"][1].astype(BF16), p["w_rkv"][2].astype(BF16),
           row(p["w0"]), p["w1"].astype(BF16), p["w2"].astype(BF16),
           row(p["a0"]), p["a1"].astype(BF16), p["a2"].astype(BF16),
           p["g1"].astype(BF16), p["g2"].astype(BF16), row(p["k_k"]), row(p["k_a"])]
    specs = [tile,
             pl.BlockSpec((8, D), lambda i: (jnp.maximum(i * (tm // 8) - 1, 0), 0)),
             pl.BlockSpec((1, 6, D), lambda i: (i // nb, 0, 0))]
    specs += [full(a) for a in ops[3:]]
    if has_vres:
        extra = [row(p["v0"]), p["v1"].astype(BF16), p["v2"].astype(BF16)]
        ops += extra + [v_first]
        specs += [full(a) for a in extra] + [tile]
    outs = pl.pallas_call(
        functools.partial(_rwkv_proj_kernel, has_vres),
        grid=(T // tm,),
        in_specs=specs,
        out_specs=[tile] * 7,
        out_shape=[jax.ShapeDtypeStruct((T, D), F32)] * 7,
        compiler_params=_cparams("parallel"),
        name="rwkv_proj",
    )(*ops)
    return outs


def _rwkv_scan_kernel(r_ref, lw_ref, k_ref, v_ref, kk_ref, a_ref, g_ref, rk_ref, lg_ref, lb_ref,
                      o_ref, s_ref):
    c = pl.program_id(2)
    W = RW_GROUP * RW_HEAD

    @pl.when(c == 0)
    def _():
        s_ref[...] = jnp.zeros_like(s_ref)

    slabs = [slice(p * W, (p + 1) * W) for p in range(RW_PAR)]
    cols = lambda ref: [ref[:, sl] for sl in slabs]
    out, s_new = _rwkv_blocks(cols(r_ref), cols(lw_ref), cols(k_ref), cols(v_ref), cols(kk_ref),
                              cols(a_ref), cols(g_ref), cols(rk_ref), cols(lg_ref), cols(lb_ref),
                              [s_ref[p] for p in range(RW_PAR)])
    for p, sl in enumerate(slabs):
        o_ref[:, sl] = out[p]
        s_ref[p] = s_new[p]


def _each(fn, *cols):
    return [fn(*args) for args in zip(*cols)]


def _rwkv_blocks(r, lw, k, v, kk, a, g, r_k, lnx_g, lnx_b, s_old):
    C = RW_CHUNK
    W = RW_GROUP * RW_HEAD

    mul = lambda x, y: x * y
    ti = lax.broadcasted_iota(jnp.int32, (C, C), 0)
    si = lax.broadcasted_iota(jnp.int32, (C, C), 1)
    tri = jnp.where(ti >= si, 1.0, 0.0).astype(F32)
    cl = _each(lambda x: jnp.dot(tri, x, precision=HIGHEST, preferred_element_type=F32), lw)
    cl_end = _each(lambda x: x[C - 1:C, :], cl)
    w_t = _each(jnp.exp, cl)
    w_prev = _each(lambda x, y: jnp.exp(x - y), cl, lw)
    w_inv = _each(lambda x: jnp.exp(-x), cl)
    w_rem = _each(lambda x, y: jnp.exp(x - y), cl_end, cl)
    bv = _each(mul, kk, a)
    r_hat = _each(mul, r, w_t)
    a_hat = _each(lambda x, y: -x * y, kk, w_prev)
    b_hat = _each(mul, bv, w_inv)
    k_hat = _each(mul, k, w_inv)
    b_til = _each(mul, bv, w_rem)
    k_til = _each(mul, k, w_rem)

    lane_head = lax.broadcasted_iota(jnp.int32, (C, W), 1) // RW_HEAD

    def stack(m):
        mb = m.astype(BF16)
        zero = jnp.zeros_like(mb)
        return jnp.concatenate([jnp.where(lane_head == hd, mb, zero) for hd in range(RW_GROUP)],
                               axis=0)

    def fold(m):
        return m[0:C] + m[C:2 * C] + m[2 * C:3 * C] + m[3 * C:4 * C]

    cat0 = lambda x, y: jnp.concatenate([x, y], axis=0)
    gram = _each(lambda ah, rh, bh, kh: _dot_nt(cat0(stack(ah), stack(rh)), cat0(stack(bh), stack(kh))),
                 a_hat, r_hat, b_hat, k_hat)
    n = RW_GROUP * C
    ri = lax.broadcasted_iota(jnp.int32, (n, n), 0)
    ci = lax.broadcasted_iota(jnp.int32, (n, n), 1)
    strict = (ri % C) > (ci % C)
    incl = (ri % C) >= (ci % C)
    same_head = (ri // RW_HEAD) == (ci // RW_HEAD)
    eye = jnp.where(ri == ci, 1.0, 0.0)
    l_ab = _each(lambda x: jnp.where(strict, x[:n, :n], 0.0), gram)
    l_ak = _each(lambda x: jnp.where(strict, x[:n, n:], 0.0), gram)
    m_rb = _each(lambda x: jnp.where(incl, x[n:, :n], 0.0), gram)
    m_rk = _each(lambda x: jnp.where(incl, x[n:, n:], 0.0), gram)

    inv = _each(lambda x: eye + x, l_ab)
    pw = l_ab
    for _ in range(int(math.log2(C)) - 1):
        pw = _each(lambda x: _dot(x, x), pw)
        inv = _each(lambda x, y: x + _dot(x, y), inv, pw)

    a_p = _each(lambda x, y: fold(_dot(x, stack(y))), inv, a_hat)
    t_l = _each(_dot, inv, l_ak)
    v_p = _each(lambda x, y: fold(_dot(x, stack(y))), t_l, v)
    r_p = _each(lambda x, y, z: x + fold(_dot(y, stack(z))), r_hat, m_rb, a_p)
    y0 = _each(lambda mb, mk, vp, vv: fold(_dot(jnp.concatenate([mb, mk], axis=1),
                                                cat0(stack(vp), stack(vv)))), m_rb, m_rk, v_p, v)
    a_til = _each(lambda x, y: jnp.where(same_head, _dot_tn(x, y), 0.0), b_til, a_p)
    d_new = _each(lambda vp, vv, bt, kt: jnp.where(same_head, _dot_tn(cat0(vp, vv), cat0(bt, kt)), 0.0),
                  v_p, v, b_til, k_til)
    y = _each(lambda x, s, z: _dot_nt(x, s) + z, r_p, s_old, y0)
    s_new = _each(lambda s, ce, at, dn: s * jnp.exp(ce) + _dot_nt(s, at) + dn,
                  s_old, cl_end, a_til, d_new)

    inv_n = 1.0 / RW_HEAD
    mean = _each(lambda x: _group_sum64(x) * inv_n, y)
    yc = _each(lambda x, m: x - m, y, mean)
    var = _each(lambda x: _group_sum64(x * x) * inv_n, yc)
    yn = _each(lambda x, vr, lg, lb: x * lax.rsqrt(vr + GN_EPS) * lg + lb, yc, var, lnx_g, lnx_b)
    bonus = _each(lambda rr, kx, rk, vv: _group_sum64(rr * kx * rk) * vv, r, k, r_k, v)
    out = _each(lambda x, bo, gg: ((x + bo) * gg).astype(BF16), yn, bonus, g)
    return out, s_new


def _rwkv_scan(r, lw, k, v, kk, a, g, r_k, lnx_g, lnx_b):
    C = RW_CHUNK
    W = RW_PAR * RW_GROUP * RW_HEAD
    nc = S // C
    slab = pl.BlockSpec((C, W), lambda b, gi, c: (b * nc + c, gi))
    prow = pl.BlockSpec((1, W), lambda b, gi, c: (0, gi))
    return pl.pallas_call(
        _rwkv_scan_kernel,
        grid=(B, D // W, nc),
        in_specs=[slab] * 7 + [prow] * 3,
        out_specs=slab,
        out_shape=jax.ShapeDtypeStruct((T, D), BF16),
        scratch_shapes=[pltpu.VMEM((RW_PAR, RW_GROUP * RW_HEAD, RW_GROUP * RW_HEAD), F32)],
        compiler_params=_cparams("parallel", "parallel", "arbitrary"),
        name="rwkv_scan",
    )(r, lw, k, v, kk, a, g, r_k.reshape(1, D), lnx_g.reshape(1, D), lnx_b.reshape(1, D))


def _proj_res_kernel(gate_row, a_ref, w_ref, x_ref, mod_ref, o_ref):
    y = jnp.dot(a_ref[...], w_ref[...], preferred_element_type=F32)
    o_ref[...] = x_ref[...] + mod_ref[0][gate_row:gate_row + 1] * y


def _proj_res(a, w, x, mod, gate_row):
    tm = 512
    nb = S // tm
    kdim = a.shape[1]
    return pl.pallas_call(
        functools.partial(_proj_res_kernel, gate_row),
        grid=(T // tm,),
        in_specs=[pl.BlockSpec((tm, kdim), lambda i: (i, 0)),
                  pl.BlockSpec((kdim, D), lambda i: (0, 0)),
                  pl.BlockSpec((tm, D), lambda i: (i, 0)),
                  pl.BlockSpec((1, 6, D), lambda i: (i // nb, 0, 0))],
        out_specs=pl.BlockSpec((tm, D), lambda i: (i, 0)),
        out_shape=jax.ShapeDtypeStruct((T, D), F32),
        compiler_params=_cparams("parallel"),
        name="proj_res",
    )(a, w.astype(BF16), x, mod)


def _ffn_kernel(nf, x_ref, mod_ref, ng_ref, wg_ref, wu_ref, wd_ref, o_ref, h_scr, acc_scr):
    f = pl.program_id(1)

    @pl.when(f == 0)
    def _():
        mod = mod_ref[0]
        h_scr[...] = _modnorm(x_ref[...], ng_ref[...], mod[3:4], mod[4:5]).astype(BF16)
        acc_scr[...] = jnp.zeros_like(acc_scr)

    for sub in range(h_scr.shape[0] // FFN_SUB):
        rows = slice(sub * FFN_SUB, (sub + 1) * FFN_SUB)
        h = h_scr[rows]
        gt = jnp.dot(h, wg_ref[0], preferred_element_type=F32)
        up = jnp.dot(h, wu_ref[0], preferred_element_type=F32)
        act = (gt * _sigmoid(gt) * up).astype(BF16)
        acc_scr[rows] += jnp.dot(act, wd_ref[0], preferred_element_type=F32)

    @pl.when(f == nf - 1)
    def _():
        o_ref[...] = x_ref[...] + mod_ref[0][5:6] * acc_scr[...]


def _ffn_dense(x, mod, norm_g, wgu, wd, layer):
    tm, tf = FFN_TM, FFN_TF
    nf = D_FF // tf
    nb = S // tm
    return pl.pallas_call(
        functools.partial(_ffn_kernel, nf),
        grid=(T // tm, nf),
        in_specs=[pl.BlockSpec((tm, D), lambda i, f: (i, 0)),
                  pl.BlockSpec((1, 6, D), lambda i, f: (i // nb, 0, 0)),
                  pl.BlockSpec((1, D), lambda i, f: (0, 0)),
                  pl.BlockSpec((1, D, tf), lambda i, f: (layer, 0, f)),
                  pl.BlockSpec((1, D, tf), lambda i, f: (layer, 0, nf + f)),
                  pl.BlockSpec((1, tf, D), lambda i, f: (layer, f, 0))],
        out_specs=pl.BlockSpec((tm, D), lambda i, f: (i, 0)),
        out_shape=jax.ShapeDtypeStruct((T, D), F32),
        scratch_shapes=[pltpu.VMEM((tm, D), BF16), pltpu.VMEM((tm, D), F32)],
        compiler_params=_cparams("parallel", "arbitrary"),
        name="ffn_dense",
    )(x, mod, norm_g.reshape(1, D), wgu, wgu, wd)


def _route_kernel(x_ref, mod_ref, ng_ref, wr_ref, br_ref,
                  h_out, cw_t_out, dest_t_out, cnt_out, carry_scr):
    i = pl.program_id(0)
    tm = x_ref.shape[0]
    mod = mod_ref[0]
    h = _modnorm(x_ref[...], ng_ref[...], mod[3:4], mod[4:5])
    h_out[...] = h.astype(BF16)
    logits = jnp.dot(h, wr_ref[...], precision=HIGHEST, preferred_element_type=F32) + br_ref[...]
    lane = lax.broadcasted_iota(jnp.int32, logits.shape, 1)
    neg = -jnp.inf
    logits = jnp.where(lane < N_EXPERTS, logits, neg)
    m1 = jnp.max(logits, axis=-1, keepdims=True)
    i1 = jnp.min(jnp.where(logits == m1, lane, LANES), axis=-1, keepdims=True)
    rest = jnp.where(lane == i1, neg, logits)
    m2 = jnp.max(rest, axis=-1, keepdims=True)
    i2 = jnp.min(jnp.where(rest == m2, lane, LANES), axis=-1, keepdims=True)
    e2 = jnp.exp(m2 - m1)
    w1 = 1.0 / (1.0 + e2)
    w2 = e2 / (1.0 + e2)
    cw_t_out[...] = (jnp.where(lane == i1, w1, 0.0) + jnp.where(lane == i2, w2, 0.0)).T

    sel =jnp.where((lane == i1) | (lane == i2), 1.0, 0.0)
    ti = lax.broadcasted_iota(jnp.int32, (tm, tm), 0)
    si = lax.broadcasted_iota(jnp.int32, (tm, tm), 1)
    before = jnp.where(ti > si, 1.0, 0.0).astype(BF16)
    window_start = (i % (MOE_TW // tm)) == 0
    carry = jnp.where(window_start, 0.0, carry_scr[...])
    rank = jnp.dot(before, sel.astype(BF16), preferred_element_type=F32) + carry
    dest_t_out[...] = jnp.where(sel > 0.0, rank, -1.0).T
    total = carry + jnp.sum(sel, axis=0, keepdims=True)
    carry_scr[...] = total
    cnt_out[0] = jnp.broadcast_to(total, (8, LANES))


def _moe_kernel(nf, cnt_ref, x_ref, mod_ref, h_ref, dest_t_ref, cw_t_ref, wg_ref, wu_ref, wd_ref,
                o_ref, xg_scr, yc_scr):
    w = pl.program_id(0)
    e = pl.program_id(1)
    f = pl.program_id(2)
    tw = MOE_TW
    n = cnt_ref[w * N_EXPERTS + e]

    @pl.when((e == 0) & (f == 0))
    def _():
        o_ref[...] = jnp.zeros_like(o_ref)

    dest_row = dest_t_ref[pl.ds(e, 1), :]
    cw_row = cw_t_ref[pl.ds(e, 1), :]

    def slot_tile(base, rows):
        slot = lax.broadcasted_iota(jnp.int32, (rows, tw), 0).astype(F32) + base
        match = dest_row == slot
        select = jnp.where(match, 1.0, 0.0).astype(BF16)
        span = pl.ds(base, rows)

        @pl.when(f == 0)
        def _():
            xg_scr[span, :] = jnp.dot(select, h_ref[...], preferred_element_type=F32).astype(BF16)

        xs = xg_scr[span, :]
        gt = jnp.dot(xs, wg_ref[0, 0], preferred_element_type=F32)
        up = jnp.dot(xs, wu_ref[0, 0], preferred_element_type=F32)
        act = (gt * _sigmoid(gt) * up).astype(BF16)
        y = jnp.dot(act, wd_ref[0, 0], preferred_element_type=F32)

        @pl.when(f == 0)
        def _():
            yc_scr[span, :] = y

        @pl.when((f > 0) & (f < nf - 1))
        def _():
            yc_scr[span, :] += y

        @pl.when(f == nf - 1)
        def _():
            w_slot = jnp.sum(jnp.where(match, cw_row, 0.0), axis=-1, keepdims=True)
            weighted = ((yc_scr[span, :] + y) * w_slot).astype(BF16)
            o_ref[...] += _dot_tn(select, weighted)

    sizes = list(range(MOE_STEP, MOE_MAX + 1, MOE_STEP))
    for lo, rows in zip([0] + sizes[:-1], sizes):
        pl.when((n > lo) & (n <= rows))(functools.partial(slot_tile, 0, rows))

    @pl.when(n > MOE_MAX)
    def _():
        def body(s, carry):
            slot_tile(pl.multiple_of(s * MOE_MAX, MOE_MAX), MOE_MAX)
            return carry
        lax.fori_loop(0, (n + MOE_MAX - 1) // MOE_MAX, body, 0)

    @pl.when((e == N_EXPERTS - 1) & (f == nf - 1))
    def _():
        o_ref[...] = x_ref[...] + mod_ref[0][5:6] * o_ref[...]


def _moe(x, mod, norm_g, w_router, b_router, wgu, wd, layer):
    tm, tf, tw = ROUTE_TM, FFN_TF, MOE_TW
    nf = D_FF // tf
    nb = S // tm
    nw = T // tw
    wr =jnp.zeros((D, LANES), F32).at[:, :N_EXPERTS].set(w_router)
    br = jnp.zeros((1, LANES), F32).at[0, :N_EXPERTS].set(b_router)
    tile = lambda width: pl.BlockSpec((tm, width), lambda i: (i, 0))
    assert nf >= 2
    lanes_t = pl.BlockSpec((LANES, tm), lambda i: (0, i))
    h, cw_t, dest_t, cnt = pl.pallas_call(
        _route_kernel,
        grid=(T // tm,),
        in_specs=[tile(D),
                  pl.BlockSpec((1, 6, D), lambda i: (i // nb, 0, 0)),
                  pl.BlockSpec((1, D), lambda i: (0, 0)),
                  pl.BlockSpec((D, LANES), lambda i: (0, 0)),
                  pl.BlockSpec((1, LANES), lambda i: (0, 0))],
        out_specs=[tile(D), lanes_t, lanes_t,
                   pl.BlockSpec((1, 8, LANES), lambda i: (i // (tw // tm), 0, 0))],
        out_shape=[jax.ShapeDtypeStruct((T, D), BF16), jax.ShapeDtypeStruct((LANES, T), F32),
                   jax.ShapeDtypeStruct((LANES, T), F32), jax.ShapeDtypeStruct((nw, 8, LANES), F32)],
        scratch_shapes=[pltpu.VMEM((1, LANES), F32)],
        compiler_params=_cparams("arbitrary"),
        name="moe_route",
    )(x, mod, norm_g.reshape(1, D), wr, br)
    counts = cnt[:, 0, :N_EXPERTS].astype(jnp.int32).reshape(nw * N_EXPERTS)
    cap = -(-tw // MOE_MAX) * MOE_MAX
    return pl.pallas_call(
        functools.partial(_moe_kernel, nf),
        grid_spec=pltpu.PrefetchScalarGridSpec(
            num_scalar_prefetch=1,
            grid=(nw, N_EXPERTS, nf),
            in_specs=[pl.BlockSpec((tw, D), lambda w, e, f, c: (w, 0)),
                      pl.BlockSpec((1, 6, D), lambda w, e, f, c: (w // (S // tw), 0, 0)),
                      pl.BlockSpec((tw, D), lambda w, e, f, c: (w, 0)),
                      pl.BlockSpec((LANES, tw), lambda w, e, f, c: (0, w)),
                      pl.BlockSpec((LANES, tw), lambda w, e, f, c: (0, w)),
                      pl.BlockSpec((1, 1, D, tf), lambda w, e, f, c: (layer, e, 0, f)),
                      pl.BlockSpec((1, 1, D, tf), lambda w, e, f, c: (layer, e, 0, nf + f)),
                      pl.BlockSpec((1, 1, tf, D), lambda w, e, f, c: (layer, e, f, 0))],
            out_specs=pl.BlockSpec((tw, D), lambda w, e, f, c: (w, 0)),
            scratch_shapes=[pltpu.VMEM((cap, D), BF16), pltpu.VMEM((cap, D), F32)]),
        out_shape=jax.ShapeDtypeStruct((T, D), F32),
        compiler_params=_cparams("parallel", "arbitrary", "arbitrary"),
        name="moe_experts",
    )(counts, x, mod, h, dest_t, cw_t, wgu, wgu, wd)


def _mla_proj_kernel(x_ref, mod_ref, ng_ref, pos_ref, invf_ref, wd_ref, qg_ref, kvg_ref,
                     wuq_ref, wukv_ref, q_out, k_out, v_out):
    mod = mod_ref[0]
    h = _modnorm(x_ref[...], ng_ref[...], mod[0:1], mod[1:2])
    down = _dot(h, wd_ref[...])
    cq = down[:, :MLA_Q_LORA]
    ckv = down[:, MLA_Q_LORA:MLA_Q_LORA + MLA_KV_LORA]
    kr = down[:, MLA_Q_LORA + MLA_KV_LORA:]
    cq = cq * lax.rsqrt(jnp.mean(cq * cq, axis=-1, keepdims=True) + NORM_EPS) * qg_ref[...]
    ckv = ckv * lax.rsqrt(jnp.mean(ckv * ckv, axis=-1, keepdims=True) + NORM_EPS) * kvg_ref[...]

    ang = pos_ref[...] * invf_ref[...]
    lane = lax.broadcasted_iota(jnp.int32, ang.shape, 1)
    is_rope = (lane >= MLA_NOPE) & (lane < MLA_NOPE + MLA_ROPE)
    cos_r = jnp.where(is_rope, jnp.cos(ang), 0.0)
    sin_r = jnp.where(is_rope, jnp.sin(ang), 0.0)
    cos_q = jnp.where(lane < MLA_NOPE, 1.0, cos_r)
    shift = LANES - MLA_ROPE

    k_rope = kr * cos_r + pltpu.roll(kr, shift, axis=1) * sin_r

    qf = _dot(cq, wuq_ref[...])
    kv = _dot(ckv, wukv_ref[...])
    for hd in range(MLA_HEADS):
        sl = slice(hd * LANES, (hd + 1) * LANES)
        qs = qf[:, sl]
        q_out[:, sl] = ((qs * cos_q + pltpu.roll(qs, shift, axis=1) * sin_r) * MLA_SCALE).astype(BF16)
        k_out[:, sl] = (kv[:, sl] + k_rope).astype(BF16)
    _store_transposed(v_out, kv[:, MLA_HEADS * LANES:])


def _rot_half_cols(w):
    half = MLA_ROPE // 2
    return jnp.concatenate([-w[..., half:], w[..., :half]], axis=-1)


def _mla_proj(x, mod, norm_g, positions, w_down, q_norm_g, kv_norm_g, w_uq, w_ukv):
    tm = PROJ_TM
    nb = S // tm
    half = MLA_ROPE // 2
    wr = w_down[:, MLA_Q_LORA + MLA_KV_LORA:]
    wd = jnp.concatenate([w_down[:, :MLA_Q_LORA + MLA_KV_LORA],
                          jnp.zeros((D, MLA_NOPE), F32), wr, _rot_half_cols(wr)], axis=1).astype(BF16)
    wq = w_uq.reshape(MLA_Q_LORA, MLA_HEADS, MLA_NOPE + MLA_ROPE)
    wq_r = wq[..., MLA_NOPE:]
    wuq = jnp.concatenate([wq[..., :MLA_NOPE], wq_r, _rot_half_cols(wq_r)], axis=-1)
    wuq = wuq.reshape(MLA_Q_LORA, MLA_HEADS * LANES).astype(BF16)
    wkv = w_ukv.reshape(MLA_KV_LORA, MLA_HEADS, 2 * MLA_NOPE)
    wk = jnp.concatenate([wkv[..., :MLA_NOPE], jnp.zeros_like(wkv[..., :MLA_NOPE])], axis=-1)
    wukv = jnp.concatenate([wk.reshape(MLA_KV_LORA, MLA_HEADS * LANES),
                            wkv[..., MLA_NOPE:].reshape(MLA_KV_LORA, D)], axis=1).astype(BF16)
    inv_freq = ROPE_BASE ** (-jnp.arange(half, dtype=F32) / half)
    lane = jnp.arange(LANES)
    invf = jnp.where((lane >= MLA_NOPE) & (lane < MLA_NOPE + MLA_ROPE),
                     inv_freq[(lane - MLA_NOPE) % half], 0.0).reshape(1, LANES).astype(F32)
    pos = jnp.broadcast_to(positions.reshape(T, 1).astype(F32), (T, LANES))
    full = lambda a: pl.BlockSpec(a.shape, lambda i: (0,) * a.ndim)
    ops = [x, mod, norm_g.reshape(1, D), pos, invf, wd, q_norm_g.reshape(1, -1),
           kv_norm_g.reshape(1, -1), wuq, wukv]
    specs = [pl.BlockSpec((tm, D), lambda i: (i, 0)),
             pl.BlockSpec((1, 6, D), lambda i: (i // nb, 0, 0)),
             full(ops[2]),
             pl.BlockSpec((tm, LANES), lambda i: (i, 0))] + [full(a) for a in ops[4:]]
    qw = MLA_HEADS * LANES
    return pl.pallas_call(
        _mla_proj_kernel,
        grid=(T // tm,),
        in_specs=specs,
        out_specs=[pl.BlockSpec((tm, qw), lambda i: (i, 0)),
                   pl.BlockSpec((tm, qw), lambda i: (i, 0)),
                   _vt_spec(tm)],
        out_shape=[jax.ShapeDtypeStruct((T, qw), BF16), jax.ShapeDtypeStruct((T, qw), BF16),
                   _vt_shape()],
        compiler_params=_cparams("parallel"),
        name="mla_proj",
    )(*ops)


def _fox_proj_kernel(x_ref, mod_ref, ng_ref, w_ref, wf_ref, bf_ref, qg_ref, kg_ref,
                     q_out, k_out, v_out, og_out, fc_out, fr_out, carry_scr):
    i = pl.program_id(0)
    tm = x_ref.shape[0]
    mod = mod_ref[0]
    h = _modnorm(x_ref[...], ng_ref[...], mod[0:1], mod[1:2]).astype(BF16)
    q = jnp.dot(h, w_ref[:, 0:D], preferred_element_type=F32)
    k = jnp.dot(h, w_ref[:, D:2 * D], preferred_element_type=F32)
    v = jnp.dot(h, w_ref[:, 2 * D:3 * D], preferred_element_type=F32)
    og = jnp.dot(h, w_ref[:, 3 * D:4 * D], preferred_element_type=F32)
    inv_n = 1.0 / FOX_HEAD
    q = q * lax.rsqrt(_group_sum64(q * q) * inv_n + NORM_EPS) * qg_ref[...] * FOX_SCALE
    k = k * lax.rsqrt(_group_sum64(k * k) * inv_n + NORM_EPS) * kg_ref[...]
    q_out[...] = q.astype(BF16)
    k_out[...] = k.astype(BF16)
    _store_transposed(v_out, v)
    og_out[...] = _sigmoid(og).astype(BF16)

    z = jnp.dot(h, wf_ref[...], preferred_element_type=F32) + bf_ref[...]
    lane = lax.broadcasted_iota(jnp.int32, z.shape, 1)
    log_f = jnp.minimum(z, 0.0) - jnp.log(1.0 + jnp.exp(-jnp.abs(z)))
    log_f = jnp.where(lane < FOX_HEADS, log_f, 0.0)
    ti = lax.broadcasted_iota(jnp.int32, (tm, tm), 0)
    si = lax.broadcasted_iota(jnp.int32, (tm, tm), 1)
    tri = jnp.where(ti >= si, 1.0, 0.0).astype(F32)
    seq_start = (i % (S // tm)) == 0
    carry = jnp.where(seq_start, 0.0, carry_scr[...])
    cum = jnp.dot(tri, log_f, precision=HIGHEST, preferred_element_type=F32) + carry
    carry_scr[...] = cum[tm - 1:tm, :]
    fc_out[...] = cum
    fr_out[0] = cum.T


def _fox_proj(x, mod, norm_g, w_in, b_f, q_norm_g, k_norm_g):
    tm = PROJ_TM
    nb = S // tm
    w_main = jnp.concatenate([w_in[:, :3 * D], w_in[:, 3 * D + FOX_HEADS:]], axis=1).astype(BF16)
    w_f = jnp.zeros((D, LANES), F32).at[:, :FOX_HEADS].set(w_in[:, 3 * D:3 * D + FOX_HEADS]).astype(BF16)
    bf = jnp.zeros((1, LANES), F32).at[0, :FOX_HEADS].set(b_f)
    qg = jnp.tile(q_norm_g, FOX_HEADS).reshape(1, D)
    kg = jnp.tile(k_norm_g, FOX_HEADS).reshape(1, D)
    full = lambda a: pl.BlockSpec(a.shape, lambda i: (0,) * a.ndim)
    tile = pl.BlockSpec((tm, D), lambda i: (i, 0))
    ops = [x, mod, norm_g.reshape(1, D), w_main, w_f, bf, qg, kg]
    specs = [tile, pl.BlockSpec((1, 6, D), lambda i: (i // nb, 0, 0))] + [full(a) for a in ops[2:]]
    return pl.pallas_call(
        _fox_proj_kernel,
        grid=(T // tm,),
        in_specs=specs,
        out_specs=[tile, tile, _vt_spec(tm), tile,
                   pl.BlockSpec((tm, LANES), lambda i: (i, 0)),
                   pl.BlockSpec((1, LANES, tm), lambda i: (i // nb, 0, i % nb))],
        out_shape=[jax.ShapeDtypeStruct((T, D), BF16)] * 2 + [_vt_shape(), jax.ShapeDtypeStruct((T, D), BF16),
                   jax.ShapeDtypeStruct((T, LANES), F32), jax.ShapeDtypeStruct((B, LANES, S), F32)],
        scratch_shapes=[pltpu.VMEM((1, LANES), F32)],
        compiler_params=_cparams("arbitrary"),
        name="fox_proj",
    )(*ops)


def _attn_kernel(fox, *refs):
    if fox:
        q_ref, k_ref, vt_ref, og_ref, fc_ref, fr_ref, o_ref, fs_scr = refs
    else:
        q_ref, k_ref, vt_ref, o_ref = refs
    tq = ATT_TQ
    nq = S // tq
    w2 = 2 * tq
    qw = q_ref.shape[1] // ATT_PAIRS
    per_head = qw // 2
    g = pl.program_id(1)
    qi = pl.program_id(2)
    pairs = range(ATT_PAIRS)

    def head(u, j):
        return 2 * (ATT_PAIRS * g + u) + j

    if fox:
        @pl.when(qi == 0)
        def _():
            fcv = fc_ref[...]
            lane_s = lax.broadcasted_iota(jnp.int32, fcv.shape, 1)
            for u in pairs:
                for j in range(2):
                    colv = jnp.sum(jnp.where(lane_s == head(u, j), fcv, 0.0), axis=-1, keepdims=True)
                    fs_scr[2 * u + j] = jnp.broadcast_to(colv, fcv.shape)

    lane = lax.broadcasted_iota(jnp.int32, (tq, qw), 1)
    q_bd, f_t = [], []
    for u in pairs:
        qb = q_ref[:, u * qw:(u + 1) * qw]
        zero = jnp.zeros_like(qb)
        q_bd.append(jnp.concatenate([jnp.where(lane // per_head == j, qb, zero) for j in range(2)],
                                    axis=0))
        if fox:
            f_t.append(jnp.concatenate([fr_ref[0, pl.ds(head(u, j), 1), :] for j in range(2)], axis=1))

    srow = lax.broadcasted_iota(jnp.int32, (tq, w2), 0)
    tcol = lax.broadcasted_iota(jnp.int32, (tq, w2), 1) % tq
    diag_mask = (srow <= tcol) if fox else ((srow // 64) <= (tcol // 64))
    sub = lax.broadcasted_iota(jnp.int32, (LANES, tq), 0)
    reps = tq // LANES

    def scores(u, lo, hi):
        st = lax.dot_general(k_ref[lo:hi, u * qw:(u + 1) * qw], q_bd[u], (((1,), (1,)), ((), ())),
                             preferred_element_type=F32)
        if fox:
            f_s = jnp.concatenate([fs_scr[2 * u, lo:hi]] * reps + [fs_scr[2 * u + 1, lo:hi]] * reps,
                                  axis=1)
            st = st + (f_t[u] - f_s)
        return st

    def values(u, lo, hi, pt):
        vt_ones = jnp.concatenate([vt_ref[0, u * LANES:(u + 1) * LANES, lo:hi],
                                   jnp.ones((16, hi - lo), BF16)], axis=0)
        return jnp.dot(vt_ones, pt.astype(BF16), preferred_element_type=F32)

    def query_tile(c):
        n_head = c * tq
        st_tail = [jnp.where(diag_mask, scores(u, n_head, n_head + tq), -jnp.inf) for u in pairs]
        mx = [jnp.max(st, axis=0, keepdims=True) for st in st_tail]
        if c > 0:
            st_head = [scores(u, 0, n_head) for u in pairs]
            mx = [jnp.maximum(m, jnp.max(st, axis=0, keepdims=True)) for m, st in zip(mx, st_head)]
        acc = [values(u, n_head, n_head + tq, jnp.exp(st_tail[u] - mx[u])) for u in pairs]
        if c > 0:
            acc = [acc[u] + values(u, 0, n_head, jnp.exp(st_head[u] - mx[u])) for u in pairs]
        for u in pairs:
            out = acc[u][:LANES] / acc[u][LANES:LANES + 1]
            o = jnp.where(sub // 64 == 0, out[:, :tq], out[:, tq:]).T
            cols = slice(u * LANES, (u + 1) * LANES)
            if fox:
                o = o * og_ref[:, cols].astype(F32)
            o_ref[:, cols] = o.astype(BF16)

    for c in range(nq):
        pl.when(qi == c)(functools.partial(query_tile, c))


def _attention(q, k, vt, fox_extras=None):
    fox = fox_extras is not None
    tq = ATT_TQ
    nq = S // tq
    qw = ATT_PAIRS * (LANES if fox else 2 * LANES)
    vw = ATT_PAIRS * LANES
    ops = [q, k, vt]
    specs = [pl.BlockSpec((tq, qw), lambda b, g, i: (b * nq + i, g)),
             pl.BlockSpec((S, qw), lambda b, g, i: (b, g)),
             pl.BlockSpec((1, vw, S), lambda b, g, i: (b, g, 0))]
    scratch = []
    if fox:
        og, fc, fr = fox_extras
        ops += [og, fc, fr]
        specs += [pl.BlockSpec((tq, vw), lambda b, g, i: (b * nq + i, g)),
                  pl.BlockSpec((S, LANES), lambda b, g, i: (b, 0)),
                  pl.BlockSpec((1, FOX_HEADS, tq), lambda b, g, i: (b, 0, i))]
        scratch.append(pltpu.VMEM((2 * ATT_PAIRS, S, LANES), F32))
    return pl.pallas_call(
        functools.partial(_attn_kernel, fox),
        grid=(B, D // vw, nq),
        in_specs=specs,
        out_specs=pl.BlockSpec((tq, vw), lambda b, g, i: (b * nq + i, g)),
        out_shape=jax.ShapeDtypeStruct((T, D), BF16),
        scratch_shapes=scratch,
        compiler_params=_cparams("parallel", "parallel", "arbitrary"),
        name="fox_attn" if fox else "mla_attn",
    )(*ops)


def _final_norm_kernel(x_ref, g_ref, o_ref):
    x = x_ref[...]
    o_ref[...] = x * lax.rsqrt(jnp.mean(x * x, axis=-1, keepdims=True) + NORM_EPS) * g_ref[...]


def _final_norm(x, g):
    tm = 512
    return pl.pallas_call(
        _final_norm_kernel,
        grid=(T // tm,),
        in_specs=[pl.BlockSpec((tm, D), lambda i: (i, 0)), pl.BlockSpec((1, D), lambda i: (0, 0))],
        out_specs=pl.BlockSpec((tm, D), lambda i: (i, 0)),
        out_shape=jax.ShapeDtypeStruct((T, D), F32),
        compiler_params=_cparams("parallel"),
        name="final_norm",
    )(x, g.reshape(1, D))


def kernel(x, c, positions, ada_w, ada_b, norm_mix_g, norm_ffn_g, final_norm_g, rw_mu, rw_w_rkv, rw_w_o, rw_w0, rw_w1, rw_w2, rw_a0, rw_a1, rw_a2, rw_g1, rw_g2, rw_k_k, rw_k_a, rw_r_k, rw_lnx_g, rw_lnx_b, rw_v0, rw_v1, rw_v2, mla_w_down, mla_q_norm_g, mla_kv_norm_g, mla_w_uq, mla_w_ukv, mla_w_o, fox_w_in, fox_b_f, fox_q_norm_g, fox_k_norm_g, fox_w_o, ffn_w_gate_up, ffn_w_down, moe_w_router, moe_b_router, moe_w_gate_up, moe_w_down):
    xf = x.reshape(T, D)
    mods = _ada_mods(c, ada_w, ada_b)
    ffn_wgu, ffn_wd = ffn_w_gate_up.astype(BF16), ffn_w_down.astype(BF16)
    moe_wgu, moe_wd = moe_w_gate_up.astype(BF16), moe_w_down.astype(BF16)
    v_first = None
    for i in range(DEPTH):
        mod = mods[i]
        kind, j = i % 3, i // 3
        if kind == 0:
            p = dict(mu=rw_mu[j], w_rkv=rw_w_rkv[j], w0=rw_w0[j], w1=rw_w1[j], w2=rw_w2[j],
                     a0=rw_a0[j], a1=rw_a1[j], a2=rw_a2[j], g1=rw_g1[j], g2=rw_g2[j],
                     k_k=rw_k_k[j], k_a=rw_k_a[j])
            if j > 0:
                p.update(v0=rw_v0[j - 1], v1=rw_v1[j - 1], v2=rw_v2[j - 1])
            r, lw, k, v, kk, a, g = _rwkv_proj(xf, mod, norm_mix_g[i], p, v_first if j > 0 else None)
            if j == 0:
                v_first = v
            y = _rwkv_scan(r, lw, k, v, kk, a, g, rw_r_k[j], rw_lnx_g[j], rw_lnx_b[j])
            xf = _proj_res(y, rw_w_o[j], xf, mod, 2)
        elif kind == 1:
            q, k, v = _mla_proj(xf, mod, norm_mix_g[i], positions, mla_w_down[j], mla_q_norm_g[j],
                                mla_kv_norm_g[j], mla_w_uq[j], mla_w_ukv[j])
            o = _attention(q, k, v)
            xf = _proj_res(o, mla_w_o[j], xf, mod, 2)
        else:
            q, k, v, og, fc, fr = _fox_proj(xf, mod, norm_mix_g[i], fox_w_in[j], fox_b_f[j],
                                            fox_q_norm_g[j], fox_k_norm_g[j])
            o = _attention(q, k, v, (og, fc, fr))
            xf = _proj_res(o, fox_w_o[j], xf, mod, 2)
        if i % 2 == 0:
            xf = _ffn_dense(xf, mod, norm_ffn_g[i], ffn_wgu, ffn_wd, i // 2)
        else:
            xf = _moe(xf, mod, norm_ffn_g[i], moe_w_router[i // 2], moe_b_router[i // 2],
                      moe_wgu, moe_wd, i // 2)
    return _final_norm(xf, final_norm_g).reshape(B, S, D)
```

```python
import functools
import math

import jax
import jax.numpy as jnp
from jax import lax
from jax.experimental import pallas as pl
from jax.experimental.pallas import tpu as pltpu

F32 = jnp.float32
BF16 = jnp.bfloat16
HIGHEST = lax.Precision.HIGHEST

D = 1024
B = 8
S = 2048
T = B * S
DEPTH = 4
NORM_EPS = 1e-6

RW_HEAD = 64
GN_EPS = 64e-5
EXP_NEG_HALF = math.exp(-0.5)

MLA_HEADS = 16
MLA_NOPE = 64
MLA_ROPE = 32
MLA_Q_LORA = 768
MLA_KV_LORA = 256
MLA_SCALE = (MLA_NOPE + MLA_ROPE) ** -0.5
ROPE_BASE = 10000.0

FOX_HEADS = 16
FOX_HEAD = 64
FOX_SCALE = FOX_HEAD ** -0.5

D_FF = 2816
N_EXPERTS = 8

LANES = 128
GROUP_SUM_WIDTH = 256
VMEM_LIMIT = 56 * 1024 * 1024

RW_CHUNK = 64
RW_GROUP = 4
RW_PAR = 4
ATT_TQ = 256
ATT_PAIRS = 4
FFN_TM = 1024
FFN_SUB = 512
ROUTE_TM = 512
FFN_TF = 1408
PROJ_TM = 256
MOE_TW = 1024
MOE_STEP = 128
MOE_MAX = 512


def _cparams(*sem):
    return pltpu.CompilerParams(dimension_semantics=sem, vmem_limit_bytes=VMEM_LIMIT)


def _dot(a, b):
    return jnp.dot(a.astype(BF16), b.astype(BF16), preferred_element_type=F32)


def _dot_nt(a, b):
    return lax.dot_general(a.astype(BF16), b.astype(BF16), (((1,), (1,)), ((), ())),
                           preferred_element_type=F32)


def _dot_tn(a, b):
    return lax.dot_general(a.astype(BF16), b.astype(BF16), (((0,), (0,)), ((), ())),
                           preferred_element_type=F32)


def _sigmoid(z):
    return 0.5 * jnp.tanh(0.5 * z) + 0.5


def _modnorm(x, g, shift, scale):
    ms = jnp.mean(x * x, axis=-1, keepdims=True)
    y = x * lax.rsqrt(ms + NORM_EPS) * g
    return y * (1.0 + scale) + shift


def _group_sum64(x):
    width = GROUP_SUM_WIDTH
    gi = lax.broadcasted_iota(jnp.int32, (width, width), 0) // 64
    gj = lax.broadcasted_iota(jnp.int32, (width, width), 1) // 64
    ones = jnp.where(gi == gj, 1.0, 0.0).astype(BF16)
    xb = x.astype(BF16)
    outs = [jnp.dot(xb[:, s * width:(s + 1) * width], ones, preferred_element_type=F32)
            for s in range(x.shape[1] // width)]
    return outs[0] if len(outs) == 1 else jnp.concatenate(outs, axis=1)


def _store_transposed(vt_ref, v):
    vt_ref[0] = v.T.astype(BF16)


def _vt_spec(tm):
    nb = S // tm
    return pl.BlockSpec((1, D, tm), lambda i: (i // nb, 0, i % nb))


def _vt_shape():
    return jax.ShapeDtypeStruct((B, D, S), BF16)


def _ada_kernel(c_ref, w_ref, b_ref, o_ref):
    c = c_ref[...]
    cond = c * _sigmoid(c)
    o_ref[0] = _dot(cond, w_ref[0]) + b_ref[0]


def _ada_mods(c, ada_w, ada_b):
    tn = 1536
    out = pl.pallas_call(
        _ada_kernel,
        grid=(DEPTH, 6 * D // tn),
        in_specs=[pl.BlockSpec((B, D), lambda l, j: (0, 0)),
                  pl.BlockSpec((1, D, tn), lambda l, j: (l, 0, j)),
                  pl.BlockSpec((1, 1, tn), lambda l, j: (l, 0, j))],
        out_specs=pl.BlockSpec((1, B, tn), lambda l, j: (l, 0, j)),
        out_shape=jax.ShapeDtypeStruct((DEPTH, B, 6 * D), F32),
        compiler_params=_cparams("parallel", "parallel"),
        name="ada_mods",
    )(c, ada_w, ada_b.reshape(DEPTH, 1, 6 * D))
    return out.reshape(DEPTH, B, 6, D)


def _rwkv_proj_kernel(has_vres, *refs):
    if has_vres:
        (x_ref, xp_ref, mod_ref, ng_ref, mu_ref, wr_ref, wk_ref, wv_ref, w0_ref, w1_ref, w2_ref,
         a0_ref, a1_ref, a2_ref, g1_ref, g2_ref, kk_ref, ka_ref, v0_ref, v1_ref, v2_ref, vf_ref,
         r_out, lw_out, k_out, v_out, kk_out, a_out, g_out) = refs
    else:
        (x_ref, xp_ref, mod_ref, ng_ref, mu_ref, wr_ref, wk_ref, wv_ref, w0_ref, w1_ref, w2_ref,
         a0_ref, a1_ref, a2_ref, g1_ref, g2_ref, kk_ref, ka_ref,
         r_out, lw_out, k_out, v_out, kk_out, a_out, g_out) = refs
    i = pl.program_id(0)
    mod = mod_ref[0]
    shift, scale = mod[0:1], mod[1:2]
    g = ng_ref[...]
    h = _modnorm(x_ref[...], g, shift, scale)
    hp = _modnorm(xp_ref[...], g, shift, scale)
    seq_start = (i % (S // PROJ_TM)) == 0
    prev_row = jnp.where(seq_start, 0.0, hp[7:8, :])
    row = lax.broadcasted_iota(jnp.int32, h.shape, 0)
    prev = jnp.where(row == 0, prev_row, pltpu.roll(h, 1, axis=0))
    delta = prev - h
    mu = mu_ref[...]
    xr = h + delta * mu[0:1]
    xw = h + delta * mu[1:2]
    xk = h + delta * mu[2:3]
    xv = h + delta * mu[3:4]
    xa = h + delta * mu[4:5]
    xg = h + delta * mu[5:6]
    r = _dot(xr, wr_ref[...])
    k = _dot(xk, wk_ref[...])
    v = _dot(xv, wv_ref[...])
    w_raw = w0_ref[...] + _dot(jnp.tanh(_dot(xw, w1_ref[...])), w2_ref[...])
    lw_out[...] = -_sigmoid(w_raw) * EXP_NEG_HALF
    if has_vres:
        mix = _sigmoid(v0_ref[...] + _dot(_dot(xv, v1_ref[...]), v2_ref[...]))
        v = v + (vf_ref[...] - v) * mix
    a = _sigmoid(a0_ref[...] + _dot(_dot(xa, a1_ref[...]), a2_ref[...]))
    g_out[...] = _dot(_sigmoid(_dot(xg, g1_ref[...])), g2_ref[...])
    kk = k * kk_ref[...]
    norm = jnp.sqrt(_group_sum64(kk * kk))
    kk_out[...] = kk / jnp.maximum(norm, 1e-12)
    k_out[...] = k * (1.0 + (a - 1.0) * ka_ref[...])
    r_out[...] = r
    v_out[...] = v
    a_out[...] = a


def _rwkv_proj(x, mod, norm_g, p, v_first):
    has_vres = v_first is not None
    tm = PROJ_TM
    nb = S // tm
    row = lambda a: a.reshape(1, -1)
    full = lambda a: pl.BlockSpec(a.shape, lambda i: (0,) * a.ndim)
    tile = pl.BlockSpec((tm, D), lambda i: (i, 0))
    ops = [x, x, mod, row(norm_g), p["mu"],
           p["w_rkv"][0].astype(BF16), p["w_rkv"][1].astype(BF16), p["w_rkv"][2].astype(BF16),
           row(p["w0"]), p["w1"].astype(BF16), p["w2"].astype(BF16),
           row(p["a0"]), p["a1"].astype(BF16), p["a2"].astype(BF16),
           p["g1"].astype(BF16), p["g2"].astype(BF16), row(p["k_k"]), row(p["k_a"])]
    specs = [tile,
             pl.BlockSpec((8, D), lambda i: (jnp.maximum(i * (tm // 8) - 1, 0), 0)),
             pl.BlockSpec((1, 6, D), lambda i: (i // nb, 0, 0))]
    specs += [full(a) for a in ops[3:]]
    if has_vres:
        extra = [row(p["v0"]), p["v1"].astype(BF16), p["v2"].astype(BF16)]
        ops += extra + [v_first]
        specs += [full(a) for a in extra] + [tile]
    outs = pl.pallas_call(
        functools.partial(_rwkv_proj_kernel, has_vres),
        grid=(T // tm,),
        in_specs=specs,
        out_specs=[tile] * 7,
        out_shape=[jax.ShapeDtypeStruct((T, D), F32)] * 7,
        compiler_params=_cparams("parallel"),
        name="rwkv_proj",
    )(*ops)
    return outs


def _rwkv_scan_kernel(r_ref, lw_ref, k_ref, v_ref, kk_ref, a_ref, g_ref, rk_ref, lg_ref, lb_ref,
                      o_ref, s_ref):
    c = pl.program_id(2)
    W = RW_GROUP * RW_HEAD

    @pl.when(c == 0)
    def _():
        s_ref[...] = jnp.zeros_like(s_ref)

    slabs = [slice(p * W, (p + 1) * W) for p in range(RW_PAR)]
    cols = lambda ref: [ref[:, sl] for sl in slabs]
    out, s_new = _rwkv_blocks(cols(r_ref), cols(lw_ref), cols(k_ref), cols(v_ref), cols(kk_ref),
                              cols(a_ref), cols(g_ref), cols(rk_ref), cols(lg_ref), cols(lb_ref),
                              [s_ref[p] for p in range(RW_PAR)])
    for p, sl in enumerate(slabs):
        o_ref[:, sl] = out[p]
        s_ref[p] = s_new[p]


def _cumsum_rows(x):
    row = lax.broadcasted_iota(jnp.int32, x.shape, 0)
    step = 1
    while step < x.shape[0]:
        x = x + jnp.where(row >= step, pltpu.roll(x, step, axis=0), 0.0)
        step *= 2
    return x


def _each(fn, *cols):
    return [fn(*args) for args in zip(*cols)]


def _rwkv_blocks(r, lw, k, v, kk, a, g, r_k, lnx_g, lnx_b, s_old):
    C = RW_CHUNK
    W = RW_GROUP * RW_HEAD

    mul = lambda x, y: x * y
    cl = _each(_cumsum_rows, lw)
    cl_end = _each(lambda x: x[C - 1:C, :], cl)
    w_t = _each(jnp.exp, cl)
    w_prev = _each(lambda x, y: jnp.exp(x - y), cl, lw)
    w_inv = _each(lambda x: jnp.exp(-x), cl)
    w_rem = _each(lambda x, y: jnp.exp(x - y), cl_end, cl)
    bv = _each(mul, kk, a)
    r_hat = _each(mul, r, w_t)
    a_hat = _each(lambda x, y: -x * y, kk, w_prev)
    b_hat = _each(mul, bv, w_inv)
    k_hat = _each(mul, k, w_inv)
    b_til = _each(mul, bv, w_rem)
    k_til = _each(mul, k, w_rem)

    lane_head = lax.broadcasted_iota(jnp.int32, (C, W), 1) // RW_HEAD

    def stack(m):
        mb = m.astype(BF16)
        zero = jnp.zeros_like(mb)
        return jnp.concatenate([jnp.where(lane_head == hd, mb, zero) for hd in range(RW_GROUP)],
                               axis=0)

    def fold(m):
        return m[0:C] + m[C:2 * C] + m[2 * C:3 * C] + m[3 * C:4 * C]

    cat0 = lambda x, y: jnp.concatenate([x, y], axis=0)
    gram = _each(lambda ah, rh, bh, kh: _dot_nt(cat0(stack(ah), stack(rh)), cat0(stack(bh), stack(kh))),
                 a_hat, r_hat, b_hat, k_hat)
    n = RW_GROUP * C
    ri = lax.broadcasted_iota(jnp.int32, (n, n), 0)
    ci = lax.broadcasted_iota(jnp.int32, (n, n), 1)
    strict = (ri % C) > (ci % C)
    incl = (ri % C) >= (ci % C)
    same_head = (ri // RW_HEAD) == (ci // RW_HEAD)
    eye = jnp.where(ri == ci, 1.0, 0.0)
    l_ab = _each(lambda x: jnp.where(strict, x[:n, :n], 0.0), gram)
    l_ak = _each(lambda x: jnp.where(strict, x[:n, n:], 0.0), gram)
    m_rb = _each(lambda x: jnp.where(incl, x[n:, :n], 0.0), gram)
    m_rk = _each(lambda x: jnp.where(incl, x[n:, n:], 0.0), gram)

    inv = _each(lambda x: eye + x, l_ab)
    pw = l_ab
    for _ in range(int(math.log2(C)) - 1):
        pw = _each(lambda x: _dot(x, x), pw)
        inv = _each(lambda x, y: x + _dot(x, y), inv, pw)

    a_p = _each(lambda x, y: fold(_dot(x, stack(y))), inv, a_hat)
    t_l = _each(_dot, inv, l_ak)
    v_p = _each(lambda x, y: fold(_dot(x, stack(y))), t_l, v)
    r_p = _each(lambda x, y, z: x + fold(_dot(y, stack(z))), r_hat, m_rb, a_p)
    y0 = _each(lambda mb, mk, vp, vv: fold(_dot(jnp.concatenate([mb, mk], axis=1),
                                                cat0(stack(vp), stack(vv)))), m_rb, m_rk, v_p, v)
    a_til = _each(lambda x, y: jnp.where(same_head, _dot_tn(x, y), 0.0), b_til, a_p)
    d_new = _each(lambda vp, vv, bt, kt: jnp.where(same_head, _dot_tn(cat0(vp, vv), cat0(bt, kt)), 0.0),
                  v_p, v, b_til, k_til)
    y = _each(lambda x, s, z: _dot_nt(x, s) + z, r_p, s_old, y0)
    s_new = _each(lambda s, ce, at, dn: s * jnp.exp(ce) + _dot_nt(s, at) + dn,
                  s_old, cl_end, a_til, d_new)

    inv_n = 1.0 / RW_HEAD
    mean = _each(lambda x: _group_sum64(x) * inv_n, y)
    yc = _each(lambda x, m: x - m, y, mean)
    var = _each(lambda x: _group_sum64(x * x) * inv_n, yc)
    yn = _each(lambda x, vr, lg, lb: x * lax.rsqrt(vr + GN_EPS) * lg + lb, yc, var, lnx_g, lnx_b)
    bonus = _each(lambda rr, kx, rk, vv: _group_sum64(rr * kx * rk) * vv, r, k, r_k, v)
    out = _each(lambda x, bo, gg: ((x + bo) * gg).astype(BF16), yn, bonus, g)
    return out, s_new


def _rwkv_scan(r, lw, k, v, kk, a, g, r_k, lnx_g, lnx_b):
    C = RW_CHUNK
    W = RW_PAR * RW_GROUP * RW_HEAD
    nc = S // C
    slab = pl.BlockSpec((C, W), lambda b, gi, c: (b * nc + c, gi))
    prow = pl.BlockSpec((1, W), lambda b, gi, c: (0, gi))
    return pl.pallas_call(
        _rwkv_scan_kernel,
        grid=(B, D // W, nc),
        in_specs=[slab] * 7 + [prow] * 3,
        out_specs=slab,
        out_shape=jax.ShapeDtypeStruct((T, D), BF16),
        scratch_shapes=[pltpu.VMEM((RW_PAR, RW_GROUP * RW_HEAD, RW_GROUP * RW_HEAD), F32)],
        compiler_params=_cparams("parallel", "parallel", "arbitrary"),
        name="rwkv_scan",
    )(r, lw, k, v, kk, a, g, r_k.reshape(1, D), lnx_g.reshape(1, D), lnx_b.reshape(1, D))


def _proj_res_kernel(gate_row, a_ref, w_ref, x_ref, mod_ref, o_ref):
    y = jnp.dot(a_ref[...], w_ref[...], preferred_element_type=F32)
    o_ref[...] = x_ref[...] + mod_ref[0][gate_row:gate_row + 1] * y


def _proj_res(a, w, x, mod, gate_row):
    tm = 512
    nb = S // tm
    kdim = a.shape[1]
    return pl.pallas_call(
        functools.partial(_proj_res_kernel, gate_row),
        grid=(T // tm,),
        in_specs=[pl.BlockSpec((tm, kdim), lambda i: (i, 0)),
                  pl.BlockSpec((kdim, D), lambda i: (0, 0)),
                  pl.BlockSpec((tm, D), lambda i: (i, 0)),
                  pl.BlockSpec((1, 6, D), lambda i: (i // nb, 0, 0))],
        out_specs=pl.BlockSpec((tm, D), lambda i: (i, 0)),
        out_shape=jax.ShapeDtypeStruct((T, D), F32),
        compiler_params=_cparams("parallel"),
        name="proj_res",
    )(a, w.astype(BF16), x, mod)


def _ffn_kernel(nf, x_ref, mod_ref, ng_ref, wg_ref, wu_ref, wd_ref, o_ref, h_scr, acc_scr):
    f = pl.program_id(1)

    @pl.when(f == 0)
    def _():
        mod = mod_ref[0]
        h_scr[...] = _modnorm(x_ref[...], ng_ref[...], mod[3:4], mod[4:5]).astype(BF16)
        acc_scr[...] = jnp.zeros_like(acc_scr)

    for sub in range(h_scr.shape[0] // FFN_SUB):
        rows = slice(sub * FFN_SUB, (sub + 1) * FFN_SUB)
        h = h_scr[rows]
        gt = jnp.dot(h, wg_ref[0], preferred_element_type=F32)
        up = jnp.dot(h, wu_ref[0], preferred_element_type=F32)
        act = (gt * _sigmoid(gt) * up).astype(BF16)
        acc_scr[rows] += jnp.dot(act, wd_ref[0], preferred_element_type=F32)

    @pl.when(f == nf - 1)
    def _():
        o_ref[...] = x_ref[...] + mod_ref[0][5:6] * acc_scr[...]


def _ffn_dense(x, mod, norm_g, wgu, wd, layer):
    tm, tf = FFN_TM, FFN_TF
    nf = D_FF // tf
    nb = S // tm
    return pl.pallas_call(
        functools.partial(_ffn_kernel, nf),
        grid=(T // tm, nf),
        in_specs=[pl.BlockSpec((tm, D), lambda i, f: (i, 0)),
                  pl.BlockSpec((1, 6, D), lambda i, f: (i // nb, 0, 0)),
                  pl.BlockSpec((1, D), lambda i, f: (0, 0)),
                  pl.BlockSpec((1, D, tf), lambda i, f: (layer, 0, f)),
                  pl.BlockSpec((1, D, tf), lambda i, f: (layer, 0, nf + f)),
                  pl.BlockSpec((1, tf, D), lambda i, f: (layer, f, 0))],
        out_specs=pl.BlockSpec((tm, D), lambda i, f: (i, 0)),
        out_shape=jax.ShapeDtypeStruct((T, D), F32),
        scratch_shapes=[pltpu.VMEM((tm, D), BF16), pltpu.VMEM((tm, D), F32)],
        compiler_params=_cparams("parallel", "arbitrary"),
        name="ffn_dense",
    )(x, mod, norm_g.reshape(1, D), wgu, wgu, wd)


def _route_kernel(x_ref, mod_ref, ng_ref, wr_ref, br_ref,
                  h_out, cw_t_out, dest_t_out, cnt_out, carry_scr):
    i = pl.program_id(0)
    tm = x_ref.shape[0]
    mod = mod_ref[0]
    h = _modnorm(x_ref[...], ng_ref[...], mod[3:4], mod[4:5])
    h_out[...] = h.astype(BF16)
    logits = jnp.dot(h, wr_ref[...], precision=HIGHEST, preferred_element_type=F32) + br_ref[...]
    lane = lax.broadcasted_iota(jnp.int32, logits.shape, 1)
    neg = -jnp.inf
    logits = jnp.where(lane < N_EXPERTS, logits, neg)
    m1 = jnp.max(logits, axis=-1, keepdims=True)
    i1 = jnp.min(jnp.where(logits == m1, lane, LANES), axis=-1, keepdims=True)
    rest = jnp.where(lane == i1, neg, logits)
    m2 = jnp.max(rest, axis=-1, keepdims=True)
    i2 = jnp.min(jnp.where(rest == m2, lane, LANES), axis=-1, keepdims=True)
    e2 = jnp.exp(m2 - m1)
    w1 = 1.0 / (1.0 + e2)
    w2 = e2 / (1.0 + e2)
    cw_t_out[...] = (jnp.where(lane == i1, w1, 0.0) + jnp.where(lane == i2, w2, 0.0)).T

    sel =jnp.where((lane == i1) | (lane == i2), 1.0, 0.0)
    ti = lax.broadcasted_iota(jnp.int32, (tm, tm), 0)
    si = lax.broadcasted_iota(jnp.int32, (tm, tm), 1)
    before = jnp.where(ti > si, 1.0, 0.0).astype(BF16)
    window_start = (i % (MOE_TW // tm)) == 0
    carry = jnp.where(window_start, 0.0, carry_scr[...])
    rank = jnp.dot(before, sel.astype(BF16), preferred_element_type=F32) + carry
    dest_t_out[...] = jnp.where(sel > 0.0, rank, -1.0).T
    total = carry + jnp.sum(sel, axis=0, keepdims=True)
    carry_scr[...] = total
    cnt_out[0] = jnp.broadcast_to(total, (8, LANES))


def _moe_kernel(nf, has_final, cnt_ref, x_ref, mod_ref, h_ref, dest_t_ref, cw_t_ref, wg_ref, wu_ref,
                wd_ref, *rest):
    fin_ref = rest[0] if has_final else None
    o_ref, xg_scr, yc_scr = rest[-3:]
    _moe_body(nf, cnt_ref, x_ref, mod_ref, h_ref, dest_t_ref, cw_t_ref, wg_ref, wu_ref, wd_ref,
              fin_ref, o_ref, xg_scr, yc_scr)


def _moe_body(nf, cnt_ref, x_ref, mod_ref, h_ref, dest_t_ref, cw_t_ref, wg_ref, wu_ref, wd_ref,
              fin_ref, o_ref, xg_scr, yc_scr):
    w = pl.program_id(0)
    e = pl.program_id(1)
    f = pl.program_id(2)
    tw = MOE_TW
    n = cnt_ref[w * N_EXPERTS + e]

    @pl.when((e == 0) & (f == 0))
    def _():
        o_ref[...] = jnp.zeros_like(o_ref)

    dest_row = dest_t_ref[pl.ds(e, 1), :]
    cw_row = cw_t_ref[pl.ds(e, 1), :]

    def slot_tile(base, rows):
        slot = lax.broadcasted_iota(jnp.int32, (rows, tw), 0).astype(F32) + base
        match = dest_row == slot
        select = jnp.where(match, 1.0, 0.0).astype(BF16)
        span = pl.ds(base, rows)

        @pl.when(f == 0)
        def _():
            xg_scr[span, :] = jnp.dot(select, h_ref[...], preferred_element_type=F32).astype(BF16)

        xs = xg_scr[span, :]
        gt = jnp.dot(xs, wg_ref[0, 0], preferred_element_type=F32)
        up = jnp.dot(xs, wu_ref[0, 0], preferred_element_type=F32)
        act = (gt * _sigmoid(gt) * up).astype(BF16)
        y = jnp.dot(act, wd_ref[0, 0], preferred_element_type=F32)

        @pl.when(f == 0)
        def _():
            yc_scr[span, :] = y

        @pl.when((f > 0) & (f < nf - 1))
        def _():
            yc_scr[span, :] += y

        @pl.when(f == nf - 1)
        def _():
            w_slot = jnp.sum(jnp.where(match, cw_row, 0.0), axis=-1, keepdims=True)
            weighted = ((yc_scr[span, :] + y) * w_slot).astype(BF16)
            o_ref[...] += _dot_tn(select, weighted)

    sizes = list(range(MOE_STEP, MOE_MAX + 1, MOE_STEP))
    for lo, rows in zip([0] + sizes[:-1], sizes):
        pl.when((n > lo) & (n <= rows))(functools.partial(slot_tile, 0, rows))

    @pl.when(n > MOE_MAX)
    def _():
        def body(s, carry):
            slot_tile(pl.multiple_of(s * MOE_MAX, MOE_MAX), MOE_MAX)
            return carry
        lax.fori_loop(0, (n + MOE_MAX - 1) // MOE_MAX, body, 0)

    @pl.when((e == N_EXPERTS - 1) & (f == nf - 1))
    def _():
        y = x_ref[...] + mod_ref[0][5:6] * o_ref[...]
        if fin_ref is not None:
            y = y * lax.rsqrt(jnp.mean(y * y, axis=-1, keepdims=True) + NORM_EPS) * fin_ref[...]
        o_ref[...] = y


def _moe(x, mod, norm_g, w_router, b_router, wgu, wd, layer, final_g=None):
    tm, tf, tw = ROUTE_TM, FFN_TF, MOE_TW
    nf = D_FF // tf
    nb = S // tm
    nw = T // tw
    wr =jnp.zeros((D, LANES), F32).at[:, :N_EXPERTS].set(w_router)
    br = jnp.zeros((1, LANES), F32).at[0, :N_EXPERTS].set(b_router)
    tile = lambda width: pl.BlockSpec((tm, width), lambda i: (i, 0))
    assert nf >= 2
    lanes_t = pl.BlockSpec((LANES, tm), lambda i: (0, i))
    h, cw_t, dest_t, cnt = pl.pallas_call(
        _route_kernel,
        grid=(T // tm,),
        in_specs=[tile(D),
                  pl.BlockSpec((1, 6, D), lambda i: (i // nb, 0, 0)),
                  pl.BlockSpec((1, D), lambda i: (0, 0)),
                  pl.BlockSpec((D, LANES), lambda i: (0, 0)),
                  pl.BlockSpec((1, LANES), lambda i: (0, 0))],
        out_specs=[tile(D), lanes_t, lanes_t,
                   pl.BlockSpec((1, 8, LANES), lambda i: (i // (tw // tm), 0, 0))],
        out_shape=[jax.ShapeDtypeStruct((T, D), BF16), jax.ShapeDtypeStruct((LANES, T), F32),
                   jax.ShapeDtypeStruct((LANES, T), F32), jax.ShapeDtypeStruct((nw, 8, LANES), F32)],
        scratch_shapes=[pltpu.VMEM((1, LANES), F32)],
        compiler_params=_cparams("arbitrary"),
        name="moe_route",
    )(x, mod, norm_g.reshape(1, D), wr, br)
    counts = cnt[:, 0, :N_EXPERTS].astype(jnp.int32).reshape(nw * N_EXPERTS)
    cap = -(-tw // MOE_MAX) * MOE_MAX
    has_final = final_g is not None
    extra_ops = [final_g.reshape(1, D)] if has_final else []
    extra_specs = [pl.BlockSpec((1, D), lambda w, e, f, c: (0, 0))] if has_final else []
    return pl.pallas_call(
        functools.partial(_moe_kernel, nf, has_final),
        grid_spec=pltpu.PrefetchScalarGridSpec(
            num_scalar_prefetch=1,
            grid=(nw, N_EXPERTS, nf),
            in_specs=[pl.BlockSpec((tw, D), lambda w, e, f, c: (w, 0)),
                      pl.BlockSpec((1, 6, D), lambda w, e, f, c: (w // (S // tw), 0, 0)),
                      pl.BlockSpec((tw, D), lambda w, e, f, c: (w, 0)),
                      pl.BlockSpec((LANES, tw), lambda w, e, f, c: (0, w)),
                      pl.BlockSpec((LANES, tw), lambda w, e, f, c: (0, w)),
                      pl.BlockSpec((1, 1, D, tf), lambda w, e, f, c: (layer, e, 0, f)),
                      pl.BlockSpec((1, 1, D, tf), lambda w, e, f, c: (layer, e, 0, nf + f)),
                      pl.BlockSpec((1, 1, tf, D), lambda w, e, f, c: (layer, e, f, 0))] + extra_specs,
            out_specs=pl.BlockSpec((tw, D), lambda w, e, f, c: (w, 0)),
            scratch_shapes=[pltpu.VMEM((cap, D), BF16), pltpu.VMEM((cap, D), F32)]),
        out_shape=jax.ShapeDtypeStruct((T, D), F32),
        compiler_params=_cparams("parallel", "arbitrary", "arbitrary"),
        name="moe_experts",
    )(counts, x, mod, h, dest_t, cw_t, wgu, wgu, wd, *extra_ops)


def _mla_proj_kernel(x_ref, mod_ref, ng_ref, pos_ref, invf_ref, wd_ref, qg_ref, kvg_ref,
                     wuq_ref, wukv_ref, q_out, k_out, v_out):
    mod = mod_ref[0]
    h = _modnorm(x_ref[...], ng_ref[...], mod[0:1], mod[1:2])
    down = _dot(h, wd_ref[...])
    cq = down[:, :MLA_Q_LORA]
    ckv = down[:, MLA_Q_LORA:MLA_Q_LORA + MLA_KV_LORA]
    kr = down[:, MLA_Q_LORA + MLA_KV_LORA:]
    cq = cq * lax.rsqrt(jnp.mean(cq * cq, axis=-1, keepdims=True) + NORM_EPS) * qg_ref[...]
    ckv = ckv * lax.rsqrt(jnp.mean(ckv * ckv, axis=-1, keepdims=True) + NORM_EPS) * kvg_ref[...]

    ang = pos_ref[...] * invf_ref[...]
    lane = lax.broadcasted_iota(jnp.int32, ang.shape, 1)
    is_rope = (lane >= MLA_NOPE) & (lane < MLA_NOPE + MLA_ROPE)
    cos_r = jnp.where(is_rope, jnp.cos(ang), 0.0)
    sin_r = jnp.where(is_rope, jnp.sin(ang), 0.0)
    cos_q = jnp.where(lane < MLA_NOPE, 1.0, cos_r)
    shift = LANES - MLA_ROPE

    k_rope = kr * cos_r + pltpu.roll(kr, shift, axis=1) * sin_r

    qf = _dot(cq, wuq_ref[...])
    kv = _dot(ckv, wukv_ref[...])
    for hd in range(MLA_HEADS):
        sl = slice(hd * LANES, (hd + 1) * LANES)
        qs = qf[:, sl]
        q_out[:, sl] = ((qs * cos_q + pltpu.roll(qs, shift, axis=1) * sin_r) * MLA_SCALE).astype(BF16)
        k_out[:, sl] = (kv[:, sl] + k_rope).astype(BF16)
    _store_transposed(v_out, kv[:, MLA_HEADS * LANES:])


def _rot_half_cols(w):
    half = MLA_ROPE // 2
    return jnp.concatenate([-w[..., half:], w[..., :half]], axis=-1)


def _mla_proj(x, mod, norm_g, positions, w_down, q_norm_g, kv_norm_g, w_uq, w_ukv):
    tm = PROJ_TM
    nb = S // tm
    half = MLA_ROPE // 2
    wr = w_down[:, MLA_Q_LORA + MLA_KV_LORA:]
    wd = jnp.concatenate([w_down[:, :MLA_Q_LORA + MLA_KV_LORA],
                          jnp.zeros((D, MLA_NOPE), F32), wr, _rot_half_cols(wr)], axis=1).astype(BF16)
    wq = w_uq.reshape(MLA_Q_LORA, MLA_HEADS, MLA_NOPE + MLA_ROPE)
    wq_r = wq[..., MLA_NOPE:]
    wuq = jnp.concatenate([wq[..., :MLA_NOPE], wq_r, _rot_half_cols(wq_r)], axis=-1)
    wuq = wuq.reshape(MLA_Q_LORA, MLA_HEADS * LANES).astype(BF16)
    wkv = w_ukv.reshape(MLA_KV_LORA, MLA_HEADS, 2 * MLA_NOPE)
    wk = jnp.concatenate([wkv[..., :MLA_NOPE], jnp.zeros_like(wkv[..., :MLA_NOPE])], axis=-1)
    wukv = jnp.concatenate([wk.reshape(MLA_KV_LORA, MLA_HEADS * LANES),
                            wkv[..., MLA_NOPE:].reshape(MLA_KV_LORA, D)], axis=1).astype(BF16)
    inv_freq = ROPE_BASE ** (-jnp.arange(half, dtype=F32) / half)
    lane = jnp.arange(LANES)
    invf = jnp.where((lane >= MLA_NOPE) & (lane < MLA_NOPE + MLA_ROPE),
                     inv_freq[(lane - MLA_NOPE) % half], 0.0).reshape(1, LANES).astype(F32)
    pos = jnp.broadcast_to(positions.reshape(T, 1).astype(F32), (T, LANES))
    full = lambda a: pl.BlockSpec(a.shape, lambda i: (0,) * a.ndim)
    ops = [x, mod, norm_g.reshape(1, D), pos, invf, wd, q_norm_g.reshape(1, -1),
           kv_norm_g.reshape(1, -1), wuq, wukv]
    specs = [pl.BlockSpec((tm, D), lambda i: (i, 0)),
             pl.BlockSpec((1, 6, D), lambda i: (i // nb, 0, 0)),
             full(ops[2]),
             pl.BlockSpec((tm, LANES), lambda i: (i, 0))] + [full(a) for a in ops[4:]]
    qw = MLA_HEADS * LANES
    return pl.pallas_call(
        _mla_proj_kernel,
        grid=(T // tm,),
        in_specs=specs,
        out_specs=[pl.BlockSpec((tm, qw), lambda i: (i, 0)),
                   pl.BlockSpec((tm, qw), lambda i: (i, 0)),
                   _vt_spec(tm)],
        out_shape=[jax.ShapeDtypeStruct((T, qw), BF16), jax.ShapeDtypeStruct((T, qw), BF16),
                   _vt_shape()],
        compiler_params=_cparams("parallel"),
        name="mla_proj",
    )(*ops)


def _fox_proj_kernel(x_ref, mod_ref, ng_ref, w_ref, wf_ref, bf_ref, qg_ref, kg_ref,
                     q_out, k_out, v_out, og_out, fc_out, fr_out, carry_scr):
    i = pl.program_id(0)
    tm = x_ref.shape[0]
    mod = mod_ref[0]
    h = _modnorm(x_ref[...], ng_ref[...], mod[0:1], mod[1:2]).astype(BF16)
    q = jnp.dot(h, w_ref[:, 0:D], preferred_element_type=F32)
    k = jnp.dot(h, w_ref[:, D:2 * D], preferred_element_type=F32)
    v = jnp.dot(h, w_ref[:, 2 * D:3 * D], preferred_element_type=F32)
    og = jnp.dot(h, w_ref[:, 3 * D:4 * D], preferred_element_type=F32)
    inv_n = 1.0 / FOX_HEAD
    q = q * lax.rsqrt(_group_sum64(q * q) * inv_n + NORM_EPS) * qg_ref[...] * FOX_SCALE
    k = k * lax.rsqrt(_group_sum64(k * k) * inv_n + NORM_EPS) * kg_ref[...]
    q_out[...] = q.astype(BF16)
    k_out[...] = k.astype(BF16)
    _store_transposed(v_out, v)
    og_out[...] = _sigmoid(og).astype(BF16)

    z = jnp.dot(h, wf_ref[...], preferred_element_type=F32) + bf_ref[...]
    lane = lax.broadcasted_iota(jnp.int32, z.shape, 1)
    log_f = jnp.minimum(z, 0.0) - jnp.log(1.0 + jnp.exp(-jnp.abs(z)))
    log_f = jnp.where(lane < FOX_HEADS, log_f, 0.0)
    ti = lax.broadcasted_iota(jnp.int32, (tm, tm), 0)
    si = lax.broadcasted_iota(jnp.int32, (tm, tm), 1)
    tri = jnp.where(ti >= si, 1.0, 0.0).astype(F32)
    seq_start = (i % (S // tm)) == 0
    carry = jnp.where(seq_start, 0.0, carry_scr[...])
    cum = jnp.dot(tri, log_f, precision=HIGHEST, preferred_element_type=F32) + carry
    carry_scr[...] = cum[tm - 1:tm, :]
    fc_out[...] = cum
    fr_out[0] = cum.T


def _fox_proj(x, mod, norm_g, w_in, b_f, q_norm_g, k_norm_g):
    tm = PROJ_TM
    nb = S // tm
    w_main = jnp.concatenate([w_in[:, :3 * D], w_in[:, 3 * D + FOX_HEADS:]], axis=1).astype(BF16)
    w_f = jnp.zeros((D, LANES), F32).at[:, :FOX_HEADS].set(w_in[:, 3 * D:3 * D + FOX_HEADS]).astype(BF16)
    bf = jnp.zeros((1, LANES), F32).at[0, :FOX_HEADS].set(b_f)
    qg = jnp.tile(q_norm_g, FOX_HEADS).reshape(1, D)
    kg = jnp.tile(k_norm_g, FOX_HEADS).reshape(1, D)
    full = lambda a: pl.BlockSpec(a.shape, lambda i: (0,) * a.ndim)
    tile = pl.BlockSpec((tm, D), lambda i: (i, 0))
    ops = [x, mod, norm_g.reshape(1, D), w_main, w_f, bf, qg, kg]
    specs = [tile, pl.BlockSpec((1, 6, D), lambda i: (i // nb, 0, 0))] + [full(a) for a in ops[2:]]
    return pl.pallas_call(
        _fox_proj_kernel,
        grid=(T // tm,),
        in_specs=specs,
        out_specs=[tile, tile, _vt_spec(tm), tile,
                   pl.BlockSpec((tm, LANES), lambda i: (i, 0)),
                   pl.BlockSpec((1, LANES, tm), lambda i: (i // nb, 0, i % nb))],
        out_shape=[jax.ShapeDtypeStruct((T, D), BF16)] * 2 + [_vt_shape(), jax.ShapeDtypeStruct((T, D), BF16),
                   jax.ShapeDtypeStruct((T, LANES), F32), jax.ShapeDtypeStruct((B, LANES, S), F32)],
        scratch_shapes=[pltpu.VMEM((1, LANES), F32)],
        compiler_params=_cparams("arbitrary"),
        name="fox_proj",
    )(*ops)


def _attn_kernel(fox, *refs):
    if fox:
        q_ref, k_ref, vt_ref, og_ref, fc_ref, fr_ref, o_ref, fs_scr = refs
    else:
        q_ref, k_ref, vt_ref, o_ref = refs
    tq = ATT_TQ
    nq = S // tq
    w2 = 2 * tq
    qw = q_ref.shape[1] // ATT_PAIRS
    per_head = qw // 2
    g = pl.program_id(1)
    qi = pl.program_id(2)
    pairs = range(ATT_PAIRS)

    def head(u, j):
        return 2 * (ATT_PAIRS * g + u) + j

    if fox:
        @pl.when(qi == 0)
        def _():
            fcv = fc_ref[...]
            lane_s = lax.broadcasted_iota(jnp.int32, fcv.shape, 1)
            for u in pairs:
                for j in range(2):
                    colv = jnp.sum(jnp.where(lane_s == head(u, j), fcv, 0.0), axis=-1, keepdims=True)
                    fs_scr[2 * u + j] = jnp.broadcast_to(colv, fcv.shape)

    lane = lax.broadcasted_iota(jnp.int32, (tq, qw), 1)
    q_bd, f_t = [], []
    for u in pairs:
        qb = q_ref[:, u * qw:(u + 1) * qw]
        zero = jnp.zeros_like(qb)
        q_bd.append(jnp.concatenate([jnp.where(lane // per_head == j, qb, zero) for j in range(2)],
                                    axis=0))
        if fox:
            f_t.append(jnp.concatenate([fr_ref[0, pl.ds(head(u, j), 1), :] for j in range(2)], axis=1))

    srow = lax.broadcasted_iota(jnp.int32, (tq, w2), 0)
    tcol = lax.broadcasted_iota(jnp.int32, (tq, w2), 1) % tq
    diag_mask = (srow <= tcol) if fox else ((srow // 64) <= (tcol // 64))
    sub = lax.broadcasted_iota(jnp.int32, (LANES, tq), 0)
    reps = tq // LANES

    def scores(u, lo, hi):
        st = lax.dot_general(k_ref[lo:hi, u * qw:(u + 1) * qw], q_bd[u], (((1,), (1,)), ((), ())),
                             preferred_element_type=F32)
        if fox:
            f_s = jnp.concatenate([fs_scr[2 * u, lo:hi]] * reps + [fs_scr[2 * u + 1, lo:hi]] * reps,
                                  axis=1)
            st = st + (f_t[u] - f_s)
        return st

    def values(u, lo, hi, pt):
        vt_ones = jnp.concatenate([vt_ref[0, u * LANES:(u + 1) * LANES, lo:hi],
                                   jnp.ones((16, hi - lo), BF16)], axis=0)
        return jnp.dot(vt_ones, pt.astype(BF16), preferred_element_type=F32)

    def query_tile(c):
        n_head = c * tq
        st_tail = [jnp.where(diag_mask, scores(u, n_head, n_head + tq), -jnp.inf) for u in pairs]
        mx = [jnp.max(st, axis=0, keepdims=True) for st in st_tail]
        if c > 0:
            st_head = [scores(u, 0, n_head) for u in pairs]
            mx = [jnp.maximum(m, jnp.max(st, axis=0, keepdims=True)) for m, st in zip(mx, st_head)]
        acc = [values(u, n_head, n_head + tq, jnp.exp(st_tail[u] - mx[u])) for u in pairs]
        if c > 0:
            acc = [acc[u] + values(u, 0, n_head, jnp.exp(st_head[u] - mx[u])) for u in pairs]
        for u in pairs:
            out = acc[u][:LANES] / acc[u][LANES:LANES + 1]
            o = jnp.where(sub // 64 == 0, out[:, :tq], out[:, tq:]).T
            cols = slice(u * LANES, (u + 1) * LANES)
            if fox:
                o = o * og_ref[:, cols].astype(F32)
            o_ref[:, cols] = o.astype(BF16)

    for c in range(nq):
        pl.when(qi == c)(functools.partial(query_tile, c))


def _attention(q, k, vt, fox_extras=None):
    fox = fox_extras is not None
    tq = ATT_TQ
    nq = S // tq
    qw = ATT_PAIRS * (LANES if fox else 2 * LANES)
    vw = ATT_PAIRS * LANES
    ops = [q, k, vt]
    specs = [pl.BlockSpec((tq, qw), lambda b, g, i: (b * nq + i, g)),
             pl.BlockSpec((S, qw), lambda b, g, i: (b, g)),
             pl.BlockSpec((1, vw, S), lambda b, g, i: (b, g, 0))]
    scratch = []
    if fox:
        og, fc, fr = fox_extras
        ops += [og, fc, fr]
        specs += [pl.BlockSpec((tq, vw), lambda b, g, i: (b * nq + i, g)),
                  pl.BlockSpec((S, LANES), lambda b, g, i: (b, 0)),
                  pl.BlockSpec((1, FOX_HEADS, tq), lambda b, g, i: (b, 0, i))]
        scratch.append(pltpu.VMEM((2 * ATT_PAIRS, S, LANES), F32))
    return pl.pallas_call(
        functools.partial(_attn_kernel, fox),
        grid=(B, D // vw, nq),
        in_specs=specs,
        out_specs=pl.BlockSpec((tq, vw), lambda b, g, i: (b * nq + i, g)),
        out_shape=jax.ShapeDtypeStruct((T, D), BF16),
        scratch_shapes=scratch,
        compiler_params=_cparams("parallel", "parallel", "arbitrary"),
        name="fox_attn" if fox else "mla_attn",
    )(*ops)


assert DEPTH % 2 == 0

def kernel(x, c, positions, ada_w, ada_b, norm_mix_g, norm_ffn_g, final_norm_g, rw_mu, rw_w_rkv, rw_w_o, rw_w0, rw_w1, rw_w2, rw_a0, rw_a1, rw_a2, rw_g1, rw_g2, rw_k_k, rw_k_a, rw_r_k, rw_lnx_g, rw_lnx_b, rw_v0, rw_v1, rw_v2, mla_w_down, mla_q_norm_g, mla_kv_norm_g, mla_w_uq, mla_w_ukv, mla_w_o, fox_w_in, fox_b_f, fox_q_norm_g, fox_k_norm_g, fox_w_o, ffn_w_gate_up, ffn_w_down, moe_w_router, moe_b_router, moe_w_gate_up, moe_w_down):
    xf = x.reshape(T, D)
    mods = _ada_mods(c, ada_w, ada_b)
    ffn_wgu, ffn_wd = ffn_w_gate_up.astype(BF16), ffn_w_down.astype(BF16)
    moe_wgu, moe_wd = moe_w_gate_up.astype(BF16), moe_w_down.astype(BF16)
    v_first = None
    for i in range(DEPTH):
        mod = mods[i]
        kind, j = i % 3, i // 3
        if kind == 0:
            p = dict(mu=rw_mu[j], w_rkv=rw_w_rkv[j], w0=rw_w0[j], w1=rw_w1[j], w2=rw_w2[j],
                     a0=rw_a0[j], a1=rw_a1[j], a2=rw_a2[j], g1=rw_g1[j], g2=rw_g2[j],
                     k_k=rw_k_k[j], k_a=rw_k_a[j])
            if j > 0:
                p.update(v0=rw_v0[j - 1], v1=rw_v1[j - 1], v2=rw_v2[j - 1])
            r, lw, k, v, kk, a, g = _rwkv_proj(xf, mod, norm_mix_g[i], p, v_first if j > 0 else None)
            if j == 0:
                v_first = v
            y = _rwkv_scan(r, lw, k, v, kk, a, g, rw_r_k[j], rw_lnx_g[j], rw_lnx_b[j])
            xf = _proj_res(y, rw_w_o[j], xf, mod, 2)
        elif kind == 1:
            q, k, v = _mla_proj(xf, mod, norm_mix_g[i], positions, mla_w_down[j], mla_q_norm_g[j],
                                mla_kv_norm_g[j], mla_w_uq[j], mla_w_ukv[j])
            o = _attention(q, k, v)
            xf = _proj_res(o, mla_w_o[j], xf, mod, 2)
        else:
            q, k, v, og, fc, fr = _fox_proj(xf, mod, norm_mix_g[i], fox_w_in[j], fox_b_f[j],
                                            fox_q_norm_g[j], fox_k_norm_g[j])
            o = _attention(q, k, v, (og, fc, fr))
            xf = _proj_res(o, fox_w_o[j], xf, mod, 2)
        if i % 2 == 0:
            xf = _ffn_dense(xf, mod, norm_ffn_g[i], ffn_wgu, ffn_wd, i // 2)
        else:
            xf = _moe(xf, mod, norm_ffn_g[i], moe_w_router[i // 2], moe_b_router[i // 2],
                      moe_wgu, moe_wd, i // 2, final_norm_g if i == DEPTH - 1 else None)
    return xf.reshape(B, S, D)
```

```python
import functools
import math

import jax
import jax.numpy as jnp
from jax import lax
from jax.experimental import pallas as pl
from jax.experimental.pallas import tpu as pltpu

F32 = jnp.float32
BF16 = jnp.bfloat16
HIGHEST = lax.Precision.HIGHEST

D = 1024
B = 8
S = 2048
T = B * S
DEPTH = 4
NORM_EPS = 1e-6

RW_HEAD = 64
GN_EPS = 64e-5
EXP_NEG_HALF = math.exp(-0.5)

MLA_HEADS = 16
MLA_NOPE = 64
MLA_ROPE = 32
MLA_Q_LORA = 768
MLA_KV_LORA = 256
MLA_SCALE = (MLA_NOPE + MLA_ROPE) ** -0.5
ROPE_BASE = 10000.0

FOX_HEADS = 16
FOX_HEAD = 64
FOX_SCALE = FOX_HEAD ** -0.5

D_FF = 2816
N_EXPERTS = 8

LANES = 128
GROUP_SUM_WIDTH = 256
VMEM_LIMIT = 56 * 1024 * 1024

RW_CHUNK = 64
RW_GROUP = 4
RW_PAR = 4
ATT_TQ = 256
ATT_PAIRS = 4
FFN_TM = 1024
FFN_SUB = 512
ROUTE_TM = 512
FFN_TF = 1408
PROJ_TM = 256
MOE_TW = 1024
MOE_STEP = 128
MOE_MAX = 512


def _cparams(*sem):
    return pltpu.CompilerParams(dimension_semantics=sem, vmem_limit_bytes=VMEM_LIMIT)


def _dot(a, b):
    return jnp.dot(a.astype(BF16), b.astype(BF16), preferred_element_type=F32)


def _dot_nt(a, b):
    return lax.dot_general(a.astype(BF16), b.astype(BF16), (((1,), (1,)), ((), ())),
                           preferred_element_type=F32)


def _dot_tn(a, b):
    return lax.dot_general(a.astype(BF16), b.astype(BF16), (((0,), (0,)), ((), ())),
                           preferred_element_type=F32)


def _sigmoid(z):
    return 0.5 * jnp.tanh(0.5 * z) + 0.5


def _modnorm(x, g, shift, scale):
    ms = jnp.mean(x * x, axis=-1, keepdims=True)
    y = x * lax.rsqrt(ms + NORM_EPS) * g
    return y * (1.0 + scale) + shift


def _group_sum64(x):
    width = GROUP_SUM_WIDTH
    gi = lax.broadcasted_iota(jnp.int32, (width, width), 0) // 64
    gj = lax.broadcasted_iota(jnp.int32, (width, width), 1) // 64
    ones = jnp.where(gi == gj, 1.0, 0.0).astype(BF16)
    xb = x.astype(BF16)
    outs = [jnp.dot(xb[:, s * width:(s + 1) * width], ones, preferred_element_type=F32)
            for s in range(x.shape[1] // width)]
    return outs[0] if len(outs) == 1 else jnp.concatenate(outs, axis=1)


def _store_transposed(vt_ref, v):
    vt_ref[0] = v.T.astype(BF16)


def _vt_spec(tm):
    nb = S // tm
    return pl.BlockSpec((1, D, tm), lambda i: (i // nb, 0, i % nb))


def _vt_shape():
    return jax.ShapeDtypeStruct((B, D, S), BF16)


def _ada_kernel(c_ref, w_ref, b_ref, o_ref):
    c = c_ref[...]
    cond = c * _sigmoid(c)
    o_ref[0] = _dot(cond, w_ref[0]) + b_ref[0]


def _ada_mods(c, ada_w, ada_b):
    tn = 1536
    out = pl.pallas_call(
        _ada_kernel,
        grid=(DEPTH, 6 * D // tn),
        in_specs=[pl.BlockSpec((B, D), lambda l, j: (0, 0)),
                  pl.BlockSpec((1, D, tn), lambda l, j: (l, 0, j)),
                  pl.BlockSpec((1, 1, tn), lambda l, j: (l, 0, j))],
        out_specs=pl.BlockSpec((1, B, tn), lambda l, j: (l, 0, j)),
        out_shape=jax.ShapeDtypeStruct((DEPTH, B, 6 * D), F32),
        compiler_params=_cparams("parallel", "parallel"),
        name="ada_mods",
    )(c, ada_w, ada_b.reshape(DEPTH, 1, 6 * D))
    return out.reshape(DEPTH, B, 6, D)


def _rwkv_proj_kernel(has_vres, *refs):
    if has_vres:
        (x_ref, xp_ref, mod_ref, ng_ref, mu_ref, wr_ref, wk_ref, wv_ref, w0_ref, w1_ref, w2_ref,
         a0_ref, a1_ref, a2_ref, g1_ref, g2_ref, kk_ref, ka_ref, v0_ref, v1_ref, v2_ref, vf_ref,
         r_out, lw_out, k_out, v_out, kk_out, a_out, g_out) = refs
    else:
        (x_ref, xp_ref, mod_ref, ng_ref, mu_ref, wr_ref, wk_ref, wv_ref, w0_ref, w1_ref, w2_ref,
         a0_ref, a1_ref, a2_ref, g1_ref, g2_ref, kk_ref, ka_ref,
         r_out, lw_out, k_out, v_out, kk_out, a_out, g_out) = refs
    i = pl.program_id(0)
    mod = mod_ref[0]
    shift, scale = mod[0:1], mod[1:2]
    g = ng_ref[...]
    h = _modnorm(x_ref[...], g, shift, scale)
    hp = _modnorm(xp_ref[...], g, shift, scale)
    seq_start = (i % (S // PROJ_TM)) == 0
    prev_row = jnp.where(seq_start, 0.0, hp[7:8, :])
    row = lax.broadcasted_iota(jnp.int32, h.shape, 0)
    prev = jnp.where(row == 0, prev_row, pltpu.roll(h, 1, axis=0))
    delta = prev - h
    mu = mu_ref[...]
    xr = h + delta * mu[0:1]
    xw = h + delta * mu[1:2]
    xk = h + delta * mu[2:3]
    xv = h + delta * mu[3:4]
    xa = h + delta * mu[4:5]
    xg = h + delta * mu[5:6]
    r = _dot(xr, wr_ref[...])
    k = _dot(xk, wk_ref[...])
    v = _dot(xv, wv_ref[...])
    w_raw = w0_ref[...] + _dot(jnp.tanh(_dot(xw, w1_ref[...])), w2_ref[...])
    lw_out[...] = -_sigmoid(w_raw) * EXP_NEG_HALF
    if has_vres:
        mix = _sigmoid(v0_ref[...] + _dot(_dot(xv, v1_ref[...]), v2_ref[...]))
        v = v + (vf_ref[...] - v) * mix
    a = _sigmoid(a0_ref[...] + _dot(_dot(xa, a1_ref[...]), a2_ref[...]))
    g_out[...] = _dot(_sigmoid(_dot(xg, g1_ref[...])), g2_ref[...])
    kk = k * kk_ref[...]
    norm = jnp.sqrt(_group_sum64(kk * kk))
    kk_out[...] = kk / jnp.maximum(norm, 1e-12)
    k_out[...] = k * (1.0 + (a - 1.0) * ka_ref[...])
    r_out[...] = r
    v_out[...] = v
    a_out[...] = a


def _rwkv_proj(x, mod, norm_g, p, v_first):
    has_vres = v_first is not None
    tm = PROJ_TM
    nb = S // tm
    row = lambda a: a.reshape(1, -1)
    full = lambda a: pl.BlockSpec(a.shape, lambda i: (0,) * a.ndim)
    tile = pl.BlockSpec((tm, D), lambda i: (i, 0))
    ops = [x, x, mod, row(norm_g), p["mu"],
           p["w_rkv"][0].astype(BF16), p["w_rkv"][1].astype(BF16), p["w_rkv"][2].astype(BF16),
           row(p["w0"]), p["w1"].astype(BF16), p["w2"].astype(BF16),
           row(p["a0"]), p["a1"].astype(BF16), p["a2"].astype(BF16),
           p["g1"].astype(BF16), p["g2"].astype(BF16), row(p["k_k"]), row(p["k_a"])]
    specs = [tile,
             pl.BlockSpec((8, D), lambda i: (jnp.maximum(i * (tm // 8) - 1, 0), 0)),
             pl.BlockSpec((1, 6, D), lambda i: (i // nb, 0, 0))]
    specs += [full(a) for a in ops[3:]]
    if has_vres:
        extra = [row(p["v0"]), p["v1"].astype(BF16), p["v2"].astype(BF16)]
        ops += extra + [v_first]
        specs += [full(a) for a in extra] + [tile]
    outs = pl.pallas_call(
        functools.partial(_rwkv_proj_kernel, has_vres),
        grid=(T // tm,),
        in_specs=specs,
        out_specs=[tile] * 7,
        out_shape=[jax.ShapeDtypeStruct((T, D), F32)] * 7,
        compiler_params=_cparams("parallel"),
        name="rwkv_proj",
    )(*ops)
    return outs


def _rwkv_scan_kernel(r_ref, lw_ref, k_ref, v_ref, kk_ref, a_ref, g_ref, rk_ref, lg_ref, lb_ref,
                      o_ref, s_ref):
    c = pl.program_id(2)
    W = RW_GROUP * RW_HEAD

    @pl.when(c == 0)
    def _():
        s_ref[...] = jnp.zeros_like(s_ref)

    slabs = [slice(p * W, (p + 1) * W) for p in range(RW_PAR)]
    cols = lambda ref: [ref[:, sl] for sl in slabs]
    out, s_new = _rwkv_blocks(cols(r_ref), cols(lw_ref), cols(k_ref), cols(v_ref), cols(kk_ref),
                              cols(a_ref), cols(g_ref), cols(rk_ref), cols(lg_ref), cols(lb_ref),
                              [s_ref[p] for p in range(RW_PAR)])
    for p, sl in enumerate(slabs):
        o_ref[:, sl] = out[p]
        s_ref[p] = s_new[p]


def _cumsum_rows(x):
    row = lax.broadcasted_iota(jnp.int32, x.shape, 0)
    step = 1
    while step < x.shape[0]:
        x = x + jnp.where(row >= step, pltpu.roll(x, step, axis=0), 0.0)
        step *= 2
    return x


def _each(fn, *cols):
    return [fn(*args) for args in zip(*cols)]


def _rwkv_blocks(r, lw, k, v, kk, a, g, r_k, lnx_g, lnx_b, s_old):
    C = RW_CHUNK
    W = RW_GROUP * RW_HEAD

    mul = lambda x, y: x * y
    cl = _each(_cumsum_rows, lw)
    cl_end = _each(lambda x: x[C - 1:C, :], cl)
    w_t = _each(jnp.exp, cl)
    w_prev = _each(lambda x, y: jnp.exp(x - y), cl, lw)
    w_inv = _each(lambda x: jnp.exp(-x), cl)
    w_rem = _each(lambda x, y: jnp.exp(x - y), cl_end, cl)
    bv = _each(mul, kk, a)
    r_hat = _each(mul, r, w_t)
    a_hat = _each(lambda x, y: -x * y, kk, w_prev)
    b_hat = _each(mul, bv, w_inv)
    k_hat = _each(mul, k, w_inv)
    b_til = _each(mul, bv, w_rem)
    k_til = _each(mul, k, w_rem)

    lane_head = lax.broadcasted_iota(jnp.int32, (C, W), 1) // RW_HEAD

    def stack(m):
        mb = m.astype(BF16)
        zero = jnp.zeros_like(mb)
        return jnp.concatenate([jnp.where(lane_head == hd, mb, zero) for hd in range(RW_GROUP)],
                               axis=0)

    cat0 = lambda x, y: jnp.concatenate([x, y], axis=0)
    gram = _each(lambda ah, rh, bh, kh: _dot_nt(cat0(stack(ah), stack(rh)), cat0(stack(bh), stack(kh))),
                 a_hat, r_hat, b_hat, k_hat)
    n = RW_GROUP * C
    ri = lax.broadcasted_iota(jnp.int32, (n, n), 0)
    ci = lax.broadcasted_iota(jnp.int32, (n, n), 1)
    strict = (ri % C) > (ci % C)
    incl = (ri % C) >= (ci % C)
    same_head = (ri // RW_HEAD) == (ci // RW_HEAD)
    eye = jnp.where(ri == ci, 1.0, 0.0)
    l_ab = _each(lambda x: jnp.where(strict, x[:n, :n], 0.0), gram)
    l_ak = _each(lambda x: jnp.where(strict, x[:n, n:], 0.0), gram)
    m_rb = _each(lambda x: jnp.where(incl, x[n:, :n], 0.0), gram)
    m_rk = _each(lambda x: jnp.where(incl, x[n:, n:], 0.0), gram)

    lane2 = lax.broadcasted_iota(jnp.int32, (C, 2 * RW_HEAD), 1)

    def packed(m):
        cols = [jnp.where(lane2 < RW_HEAD,
                          m[(2 * j) * C:(2 * j + 1) * C, j * 2 * RW_HEAD:(j + 1) * 2 * RW_HEAD],
                          m[(2 * j + 1) * C:(2 * j + 2) * C, j * 2 * RW_HEAD:(j + 1) * 2 * RW_HEAD])
                for j in range(RW_GROUP // 2)]
        return jnp.concatenate(cols, axis=1)

    def rows(m):
        blocks = []
        for hd in range(RW_GROUP):
            j = hd // 2
            blk = m[hd * C:(hd + 1) * C, j * 2 * RW_HEAD:(j + 1) * 2 * RW_HEAD]
            blocks.append(jnp.where((lane2 // RW_HEAD) == hd % 2, blk, 0.0))
        return jnp.concatenate(blocks, axis=0)

    def diag_products(x_rows, y_packed):
        yb = y_packed.astype(BF16)
        return _dot(x_rows, jnp.concatenate([yb, yb], axis=0))

    rounds = int(math.log2(C)) - 1
    inv_p = _each(lambda x: packed(eye + x), l_ab)
    inv_r = _each(lambda x: rows(eye + x), l_ab)
    pw_p = _each(packed, l_ab)
    pw_r = _each(rows, l_ab)
    for i in range(rounds):
        sq = _each(diag_products, pw_r, pw_p)
        pw_p = _each(packed, sq)
        upd = _each(diag_products, inv_r, pw_p)
        inv_p = _each(lambda x, y: x + packed(y), inv_p, upd)
        if i < rounds - 1:
            pw_r = _each(rows, sq)
            inv_r = _each(lambda x, y: x + rows(y), inv_r, upd)

    a_p = _each(lambda x, y: _dot(x, stack(y)), inv_p, a_hat)
    t_l = _each(_dot, inv_p, l_ak)
    v_p = _each(lambda x, y: _dot(x, stack(y)), t_l, v)
    m_rb_p = _each(packed, m_rb)
    m_rk_p = _each(packed, m_rk)
    r_p = _each(lambda x, y, z: x + _dot(y, stack(z)), r_hat, m_rb_p, a_p)
    y0 = _each(lambda mb, mk, vp, vv: _dot(jnp.concatenate([mb, mk], axis=1),
                                           cat0(stack(vp), stack(vv))), m_rb_p, m_rk_p, v_p, v)
    a_til = _each(lambda x, y: jnp.where(same_head, _dot_tn(x, y), 0.0), b_til, a_p)
    d_new = _each(lambda vp, vv, bt, kt: jnp.where(same_head, _dot_tn(cat0(vp, vv), cat0(bt, kt)), 0.0),
                  v_p, v, b_til, k_til)
    y = _each(lambda x, s, z: _dot_nt(x, s) + z, r_p, s_old, y0)
    s_new = _each(lambda s, ce, at, dn: s * jnp.exp(ce) + _dot_nt(s, at) + dn,
                  s_old, cl_end, a_til, d_new)

    inv_n = 1.0 / RW_HEAD
    mean = _each(lambda x: _group_sum64(x) * inv_n, y)
    yc = _each(lambda x, m: x - m, y, mean)
    var = _each(lambda x: _group_sum64(x * x) * inv_n, yc)
    yn = _each(lambda x, vr, lg, lb: x * lax.rsqrt(vr + GN_EPS) * lg + lb, yc, var, lnx_g, lnx_b)
    bonus = _each(lambda rr, kx, rk, vv: _group_sum64(rr * kx * rk) * vv, r, k, r_k, v)
    out = _each(lambda x, bo, gg: ((x + bo) * gg).astype(BF16), yn, bonus, g)
    return out, s_new


def _rwkv_scan(r, lw, k, v, kk, a, g, r_k, lnx_g, lnx_b):
    C = RW_CHUNK
    W = RW_PAR * RW_GROUP * RW_HEAD
    nc = S // C
    slab = pl.BlockSpec((C, W), lambda b, gi, c: (b * nc + c, gi))
    prow = pl.BlockSpec((1, W), lambda b, gi, c: (0, gi))
    return pl.pallas_call(
        _rwkv_scan_kernel,
        grid=(B, D // W, nc),
        in_specs=[slab] * 7 + [prow] * 3,
        out_specs=slab,
        out_shape=jax.ShapeDtypeStruct((T, D), BF16),
        scratch_shapes=[pltpu.VMEM((RW_PAR, RW_GROUP * RW_HEAD, RW_GROUP * RW_HEAD), F32)],
        compiler_params=_cparams("parallel", "parallel", "arbitrary"),
        name="rwkv_scan",
    )(r, lw, k, v, kk, a, g, r_k.reshape(1, D), lnx_g.reshape(1, D), lnx_b.reshape(1, D))


def _proj_res_kernel(gate_row, a_ref, w_ref, x_ref, mod_ref, o_ref):
    y = jnp.dot(a_ref[...], w_ref[...], preferred_element_type=F32)
    o_ref[...] = x_ref[...] + mod_ref[0][gate_row:gate_row + 1] * y


def _proj_res(a, w, x, mod, gate_row):
    tm = 512
    nb = S // tm
    kdim = a.shape[1]
    return pl.pallas_call(
        functools.partial(_proj_res_kernel, gate_row),
        grid=(T // tm,),
        in_specs=[pl.BlockSpec((tm, kdim), lambda i: (i, 0)),
                  pl.BlockSpec((kdim, D), lambda i: (0, 0)),
                  pl.BlockSpec((tm, D), lambda i: (i, 0)),
                  pl.BlockSpec((1, 6, D), lambda i: (i // nb, 0, 0))],
        out_specs=pl.BlockSpec((tm, D), lambda i: (i, 0)),
        out_shape=jax.ShapeDtypeStruct((T, D), F32),
        compiler_params=_cparams("parallel"),
        name="proj_res",
    )(a, w.astype(BF16), x, mod)


def _ffn_kernel(nf, x_ref, mod_ref, ng_ref, wg_ref, wu_ref, wd_ref, o_ref, h_scr, acc_scr):
    f = pl.program_id(1)

    @pl.when(f == 0)
    def _():
        mod = mod_ref[0]
        h_scr[...] = _modnorm(x_ref[...], ng_ref[...], mod[3:4], mod[4:5]).astype(BF16)
        acc_scr[...] = jnp.zeros_like(acc_scr)

    for sub in range(h_scr.shape[0] // FFN_SUB):
        rows = slice(sub * FFN_SUB, (sub + 1) * FFN_SUB)
        h = h_scr[rows]
        gt = jnp.dot(h, wg_ref[0], preferred_element_type=F32)
        up = jnp.dot(h, wu_ref[0], preferred_element_type=F32)
        act = (gt * _sigmoid(gt) * up).astype(BF16)
        acc_scr[rows] += jnp.dot(act, wd_ref[0], preferred_element_type=F32)

    @pl.when(f == nf - 1)
    def _():
        o_ref[...] = x_ref[...] + mod_ref[0][5:6] * acc_scr[...]


def _ffn_dense(x, mod, norm_g, wgu, wd, layer):
    tm, tf = FFN_TM, FFN_TF
    nf = D_FF // tf
    nb = S // tm
    return pl.pallas_call(
        functools.partial(_ffn_kernel, nf),
        grid=(T // tm, nf),
        in_specs=[pl.BlockSpec((tm, D), lambda i, f: (i, 0)),
                  pl.BlockSpec((1, 6, D), lambda i, f: (i // nb, 0, 0)),
                  pl.BlockSpec((1, D), lambda i, f: (0, 0)),
                  pl.BlockSpec((1, D, tf), lambda i, f: (layer, 0, f)),
                  pl.BlockSpec((1, D, tf), lambda i, f: (layer, 0, nf + f)),
                  pl.BlockSpec((1, tf, D), lambda i, f: (layer, f, 0))],
        out_specs=pl.BlockSpec((tm, D), lambda i, f: (i, 0)),
        out_shape=jax.ShapeDtypeStruct((T, D), F32),
        scratch_shapes=[pltpu.VMEM((tm, D), BF16), pltpu.VMEM((tm, D), F32)],
        compiler_params=_cparams("parallel", "arbitrary"),
        name="ffn_dense",
    )(x, mod, norm_g.reshape(1, D), wgu, wgu, wd)


def _route_kernel(x_ref, mod_ref, ng_ref, wr_ref, br_ref,
                  h_out, cw_t_out, dest_t_out, cnt_out, carry_scr):
    i = pl.program_id(0)
    tm = x_ref.shape[0]
    mod = mod_ref[0]
    h = _modnorm(x_ref[...], ng_ref[...], mod[3:4], mod[4:5])
    h_out[...] = h.astype(BF16)
    logits = jnp.dot(h, wr_ref[...], precision=HIGHEST, preferred_element_type=F32) + br_ref[...]
    lane = lax.broadcasted_iota(jnp.int32, logits.shape, 1)
    neg = -jnp.inf
    logits = jnp.where(lane < N_EXPERTS, logits, neg)
    m1 = jnp.max(logits, axis=-1, keepdims=True)
    i1 = jnp.min(jnp.where(logits == m1, lane, LANES), axis=-1, keepdims=True)
    rest = jnp.where(lane == i1, neg, logits)
    m2 = jnp.max(rest, axis=-1, keepdims=True)
    i2 = jnp.min(jnp.where(rest == m2, lane, LANES), axis=-1, keepdims=True)
    e2 = jnp.exp(m2 - m1)
    w1 = 1.0 / (1.0 + e2)
    w2 = e2 / (1.0 + e2)
    cw_t_out[...] = (jnp.where(lane == i1, w1, 0.0) + jnp.where(lane == i2, w2, 0.0)).T

    sel =jnp.where((lane == i1) | (lane == i2), 1.0, 0.0)
    ti = lax.broadcasted_iota(jnp.int32, (tm, tm), 0)
    si = lax.broadcasted_iota(jnp.int32, (tm, tm), 1)
    before = jnp.where(ti > si, 1.0, 0.0).astype(BF16)
    window_start = (i % (MOE_TW // tm)) == 0
    carry = jnp.where(window_start, 0.0, carry_scr[...])
    rank = jnp.dot(before, sel.astype(BF16), preferred_element_type=F32) + carry
    dest_t_out[...] = jnp.where(sel > 0.0, rank, -1.0).T
    total = carry + jnp.sum(sel, axis=0, keepdims=True)
    carry_scr[...] = total
    cnt_out[0] = jnp.broadcast_to(total, (8, LANES))


def _moe_kernel(nf, has_final, cnt_ref, x_ref, mod_ref, h_ref, dest_t_ref, cw_t_ref, wg_ref, wu_ref,
                wd_ref, *rest):
    fin_ref = rest[0] if has_final else None
    o_ref, xg_scr, yc_scr = rest[-3:]
    _moe_body(nf, cnt_ref, x_ref, mod_ref, h_ref, dest_t_ref, cw_t_ref, wg_ref, wu_ref, wd_ref,
              fin_ref, o_ref, xg_scr, yc_scr)


def _moe_body(nf, cnt_ref, x_ref, mod_ref, h_ref, dest_t_ref, cw_t_ref, wg_ref, wu_ref, wd_ref,
              fin_ref, o_ref, xg_scr, yc_scr):
    w = pl.program_id(0)
    e = pl.program_id(1)
    f = pl.program_id(2)
    tw = MOE_TW
    n = cnt_ref[w * N_EXPERTS + e]

    @pl.when((e == 0) & (f == 0))
    def _():
        o_ref[...] = jnp.zeros_like(o_ref)

    dest_row = dest_t_ref[pl.ds(e, 1), :]
    cw_row = cw_t_ref[pl.ds(e, 1), :]

    def slot_tile(base, rows):
        slot = lax.broadcasted_iota(jnp.int32, (rows, tw), 0).astype(F32) + base
        match = dest_row == slot
        select = jnp.where(match, 1.0, 0.0).astype(BF16)
        span = pl.ds(base, rows)

        @pl.when(f == 0)
        def _():
            xg_scr[span, :] = jnp.dot(select, h_ref[...], preferred_element_type=F32).astype(BF16)

        xs = xg_scr[span, :]
        gt = jnp.dot(xs, wg_ref[0, 0], preferred_element_type=F32)
        up = jnp.dot(xs, wu_ref[0, 0], preferred_element_type=F32)
        act = (gt * _sigmoid(gt) * up).astype(BF16)
        y = jnp.dot(act, wd_ref[0, 0], preferred_element_type=F32)

        @pl.when(f == 0)
        def _():
            yc_scr[span, :] = y

        @pl.when((f > 0) & (f < nf - 1))
        def _():
            yc_scr[span, :] += y

        @pl.when(f == nf - 1)
        def _():
            w_slot = jnp.sum(jnp.where(match, cw_row, 0.0), axis=-1, keepdims=True)
            weighted = ((yc_scr[span, :] + y) * w_slot).astype(BF16)
            o_ref[...] += _dot_tn(select, weighted)

    sizes = list(range(MOE_STEP, MOE_MAX + 1, MOE_STEP))
    for lo, rows in zip([0] + sizes[:-1], sizes):
        pl.when((n > lo) & (n <= rows))(functools.partial(slot_tile, 0, rows))

    @pl.when(n > MOE_MAX)
    def _():
        def body(s, carry):
            slot_tile(pl.multiple_of(s * MOE_MAX, MOE_MAX), MOE_MAX)
            return carry
        lax.fori_loop(0, (n + MOE_MAX - 1) // MOE_MAX, body, 0)

    @pl.when((e == N_EXPERTS - 1) & (f == nf - 1))
    def _():
        y = x_ref[...] + mod_ref[0][5:6] * o_ref[...]
        if fin_ref is not None:
            y = y * lax.rsqrt(jnp.mean(y * y, axis=-1, keepdims=True) + NORM_EPS) * fin_ref[...]
        o_ref[...] = y


def _moe(x, mod, norm_g, w_router, b_router, wgu, wd, layer, final_g=None):
    tm, tf, tw = ROUTE_TM, FFN_TF, MOE_TW
    nf = D_FF // tf
    nb = S // tm
    nw = T // tw
    wr =jnp.zeros((D, LANES), F32).at[:, :N_EXPERTS].set(w_router)
    br = jnp.zeros((1, LANES), F32).at[0, :N_EXPERTS].set(b_router)
    tile = lambda width: pl.BlockSpec((tm, width), lambda i: (i, 0))
    assert nf >= 2
    lanes_t = pl.BlockSpec((LANES, tm), lambda i: (0, i))
    h, cw_t, dest_t, cnt = pl.pallas_call(
        _route_kernel,
        grid=(T // tm,),
        in_specs=[tile(D),
                  pl.BlockSpec((1, 6, D), lambda i: (i // nb, 0, 0)),
                  pl.BlockSpec((1, D), lambda i: (0, 0)),
                  pl.BlockSpec((D, LANES), lambda i: (0, 0)),
                  pl.BlockSpec((1, LANES), lambda i: (0, 0))],
        out_specs=[tile(D), lanes_t, lanes_t,
                   pl.BlockSpec((1, 8, LANES), lambda i: (i // (tw // tm), 0, 0))],
        out_shape=[jax.ShapeDtypeStruct((T, D), BF16), jax.ShapeDtypeStruct((LANES, T), F32),
                   jax.ShapeDtypeStruct((LANES, T), F32), jax.ShapeDtypeStruct((nw, 8, LANES), F32)],
        scratch_shapes=[pltpu.VMEM((1, LANES), F32)],
        compiler_params=_cparams("arbitrary"),
        name="moe_route",
    )(x, mod, norm_g.reshape(1, D), wr, br)
    counts = cnt[:, 0, :N_EXPERTS].astype(jnp.int32).reshape(nw * N_EXPERTS)
    cap = -(-tw // MOE_MAX) * MOE_MAX
    has_final = final_g is not None
    extra_ops = [final_g.reshape(1, D)] if has_final else []
    extra_specs = [pl.BlockSpec((1, D), lambda w, e, f, c: (0, 0))] if has_final else []
    return pl.pallas_call(
        functools.partial(_moe_kernel, nf, has_final),
        grid_spec=pltpu.PrefetchScalarGridSpec(
            num_scalar_prefetch=1,
            grid=(nw, N_EXPERTS, nf),
            in_specs=[pl.BlockSpec((tw, D), lambda w, e, f, c: (w, 0)),
                      pl.BlockSpec((1, 6, D), lambda w, e, f, c: (w // (S // tw), 0, 0)),
                      pl.BlockSpec((tw, D), lambda w, e, f, c: (w, 0)),
                      pl.BlockSpec((LANES, tw), lambda w, e, f, c: (0, w)),
                      pl.BlockSpec((LANES, tw), lambda w, e, f, c: (0, w)),
                      pl.BlockSpec((1, 1, D, tf), lambda w, e, f, c: (layer, e, 0, f)),
                      pl.BlockSpec((1, 1, D, tf), lambda w, e, f, c: (layer, e, 0, nf + f)),
                      pl.BlockSpec((1, 1, tf, D), lambda w, e, f, c: (layer, e, f, 0))] + extra_specs,
            out_specs=pl.BlockSpec((tw, D), lambda w, e, f, c: (w, 0)),
            scratch_shapes=[pltpu.VMEM((cap, D), BF16), pltpu.VMEM((cap, D), F32)]),
        out_shape=jax.ShapeDtypeStruct((T, D), F32),
        compiler_params=_cparams("parallel", "arbitrary", "arbitrary"),
        name="moe_experts",
    )(counts, x, mod, h, dest_t, cw_t, wgu, wgu, wd, *extra_ops)


def _mla_proj_kernel(x_ref, mod_ref, ng_ref, pos_ref, invf_ref, wd_ref, qg_ref, kvg_ref,
                     wuq_ref, wukv_ref, q_out, k_out, v_out):
    mod = mod_ref[0]
    h = _modnorm(x_ref[...], ng_ref[...], mod[0:1], mod[1:2])
    down = _dot(h, wd_ref[...])
    cq = down[:, :MLA_Q_LORA]
    ckv = down[:, MLA_Q_LORA:MLA_Q_LORA + MLA_KV_LORA]
    kr = down[:, MLA_Q_LORA + MLA_KV_LORA:]
    cq = cq * lax.rsqrt(jnp.mean(cq * cq, axis=-1, keepdims=True) + NORM_EPS) * qg_ref[...]
    ckv = ckv * lax.rsqrt(jnp.mean(ckv * ckv, axis=-1, keepdims=True) + NORM_EPS) * kvg_ref[...]

    ang = pos_ref[...] * invf_ref[...]
    lane = lax.broadcasted_iota(jnp.int32, ang.shape, 1)
    is_rope = (lane >= MLA_NOPE) & (lane < MLA_NOPE + MLA_ROPE)
    cos_r = jnp.where(is_rope, jnp.cos(ang), 0.0)
    sin_r = jnp.where(is_rope, jnp.sin(ang), 0.0)
    cos_q = jnp.where(lane < MLA_NOPE, 1.0, cos_r)
    shift = LANES - MLA_ROPE

    k_rope = kr * cos_r + pltpu.roll(kr, shift, axis=1) * sin_r

    qf = _dot(cq, wuq_ref[...])
    kv = _dot(ckv, wukv_ref[...])
    for hd in range(MLA_HEADS):
        sl = slice(hd * LANES, (hd + 1) * LANES)
        qs = qf[:, sl]
        q_out[:, sl] = ((qs * cos_q + pltpu.roll(qs, shift, axis=1) * sin_r) * MLA_SCALE).astype(BF16)
        k_out[:, sl] = (kv[:, sl] + k_rope).astype(BF16)
    _store_transposed(v_out, kv[:, MLA_HEADS * LANES:])


def _rot_half_cols(w):
    half = MLA_ROPE // 2
    return jnp.concatenate([-w[..., half:], w[..., :half]], axis=-1)


def _mla_proj(x, mod, norm_g, positions, w_down, q_norm_g, kv_norm_g, w_uq, w_ukv):
    tm = PROJ_TM
    nb = S // tm
    half = MLA_ROPE // 2
    wr = w_down[:, MLA_Q_LORA + MLA_KV_LORA:]
    wd = jnp.concatenate([w_down[:, :MLA_Q_LORA + MLA_KV_LORA],
                          jnp.zeros((D, MLA_NOPE), F32), wr, _rot_half_cols(wr)], axis=1).astype(BF16)
    wq = w_uq.reshape(MLA_Q_LORA, MLA_HEADS, MLA_NOPE + MLA_ROPE)
    wq_r = wq[..., MLA_NOPE:]
    wuq = jnp.concatenate([wq[..., :MLA_NOPE], wq_r, _rot_half_cols(wq_r)], axis=-1)
    wuq = wuq.reshape(MLA_Q_LORA, MLA_HEADS * LANES).astype(BF16)
    wkv = w_ukv.reshape(MLA_KV_LORA, MLA_HEADS, 2 * MLA_NOPE)
    wk = jnp.concatenate([wkv[..., :MLA_NOPE], jnp.zeros_like(wkv[..., :MLA_NOPE])], axis=-1)
    wukv = jnp.concatenate([wk.reshape(MLA_KV_LORA, MLA_HEADS * LANES),
                            wkv[..., MLA_NOPE:].reshape(MLA_KV_LORA, D)], axis=1).astype(BF16)
    inv_freq = ROPE_BASE ** (-jnp.arange(half, dtype=F32) / half)
    lane = jnp.arange(LANES)
    invf = jnp.where((lane >= MLA_NOPE) & (lane < MLA_NOPE + MLA_ROPE),
                     inv_freq[(lane - MLA_NOPE) % half], 0.0).reshape(1, LANES).astype(F32)
    pos = jnp.broadcast_to(positions.reshape(T, 1).astype(F32), (T, LANES))
    full = lambda a: pl.BlockSpec(a.shape, lambda i: (0,) * a.ndim)
    ops = [x, mod, norm_g.reshape(1, D), pos, invf, wd, q_norm_g.reshape(1, -1),
           kv_norm_g.reshape(1, -1), wuq, wukv]
    specs = [pl.BlockSpec((tm, D), lambda i: (i, 0)),
             pl.BlockSpec((1, 6, D), lambda i: (i // nb, 0, 0)),
             full(ops[2]),
             pl.BlockSpec((tm, LANES), lambda i: (i, 0))] + [full(a) for a in ops[4:]]
    qw = MLA_HEADS * LANES
    return pl.pallas_call(
        _mla_proj_kernel,
        grid=(T // tm,),
        in_specs=specs,
        out_specs=[pl.BlockSpec((tm, qw), lambda i: (i, 0)),
                   pl.BlockSpec((tm, qw), lambda i: (i, 0)),
                   _vt_spec(tm)],
        out_shape=[jax.ShapeDtypeStruct((T, qw), BF16), jax.ShapeDtypeStruct((T, qw), BF16),
                   _vt_shape()],
        compiler_params=_cparams("parallel"),
        name="mla_proj",
    )(*ops)


def _fox_proj_kernel(x_ref, mod_ref, ng_ref, w_ref, wf_ref, bf_ref, qg_ref, kg_ref,
                     q_out, k_out, v_out, og_out, fc_out, fr_out, carry_scr):
    i = pl.program_id(0)
    tm = x_ref.shape[0]
    mod = mod_ref[0]
    h = _modnorm(x_ref[...], ng_ref[...], mod[0:1], mod[1:2]).astype(BF16)
    q = jnp.dot(h, w_ref[:, 0:D], preferred_element_type=F32)
    k = jnp.dot(h, w_ref[:, D:2 * D], preferred_element_type=F32)
    v = jnp.dot(h, w_ref[:, 2 * D:3 * D], preferred_element_type=F32)
    og = jnp.dot(h, w_ref[:, 3 * D:4 * D], preferred_element_type=F32)
    inv_n = 1.0 / FOX_HEAD
    q = q * lax.rsqrt(_group_sum64(q * q) * inv_n + NORM_EPS) * qg_ref[...] * FOX_SCALE
    k = k * lax.rsqrt(_group_sum64(k * k) * inv_n + NORM_EPS) * kg_ref[...]
    q_out[...] = q.astype(BF16)
    k_out[...] = k.astype(BF16)
    _store_transposed(v_out, v)
    og_out[...] = _sigmoid(og).astype(BF16)

    z = jnp.dot(h, wf_ref[...], preferred_element_type=F32) + bf_ref[...]
    lane = lax.broadcasted_iota(jnp.int32, z.shape, 1)
    log_f = jnp.minimum(z, 0.0) - jnp.log(1.0 + jnp.exp(-jnp.abs(z)))
    log_f = jnp.where(lane < FOX_HEADS, log_f, 0.0)
    ti = lax.broadcasted_iota(jnp.int32, (tm, tm), 0)
    si = lax.broadcasted_iota(jnp.int32, (tm, tm), 1)
    tri = jnp.where(ti >= si, 1.0, 0.0).astype(F32)
    seq_start = (i % (S // tm)) == 0
    carry = jnp.where(seq_start, 0.0, carry_scr[...])
    cum = jnp.dot(tri, log_f, precision=HIGHEST, preferred_element_type=F32) + carry
    carry_scr[...] = cum[tm - 1:tm, :]
    fc_out[...] = cum
    fr_out[0] = cum.T


def _fox_proj(x, mod, norm_g, w_in, b_f, q_norm_g, k_norm_g):
    tm = PROJ_TM
    nb = S // tm
    w_main = jnp.concatenate([w_in[:, :3 * D], w_in[:, 3 * D + FOX_HEADS:]], axis=1).astype(BF16)
    w_f = jnp.zeros((D, LANES), F32).at[:, :FOX_HEADS].set(w_in[:, 3 * D:3 * D + FOX_HEADS]).astype(BF16)
    bf = jnp.zeros((1, LANES), F32).at[0, :FOX_HEADS].set(b_f)
    qg = jnp.tile(q_norm_g, FOX_HEADS).reshape(1, D)
    kg = jnp.tile(k_norm_g, FOX_HEADS).reshape(1, D)
    full = lambda a: pl.BlockSpec(a.shape, lambda i: (0,) * a.ndim)
    tile = pl.BlockSpec((tm, D), lambda i: (i, 0))
    ops = [x, mod, norm_g.reshape(1, D), w_main, w_f, bf, qg, kg]
    specs = [tile, pl.BlockSpec((1, 6, D), lambda i: (i // nb, 0, 0))] + [full(a) for a in ops[2:]]
    return pl.pallas_call(
        _fox_proj_kernel,
        grid=(T // tm,),
        in_specs=specs,
        out_specs=[tile, tile, _vt_spec(tm), tile,
                   pl.BlockSpec((tm, LANES), lambda i: (i, 0)),
                   pl.BlockSpec((1, LANES, tm), lambda i: (i // nb, 0, i % nb))],
        out_shape=[jax.ShapeDtypeStruct((T, D), BF16)] * 2 + [_vt_shape(), jax.ShapeDtypeStruct((T, D), BF16),
                   jax.ShapeDtypeStruct((T, LANES), F32), jax.ShapeDtypeStruct((B, LANES, S), F32)],
        scratch_shapes=[pltpu.VMEM((1, LANES), F32)],
        compiler_params=_cparams("arbitrary"),
        name="fox_proj",
    )(*ops)


def _attn_kernel(fox, *refs):
    if fox:
        q_ref, k_ref, vt_ref, og_ref, fc_ref, fr_ref, o_ref, fs_scr = refs
    else:
        q_ref, k_ref, vt_ref, o_ref = refs
    tq = ATT_TQ
    nq = S // tq
    w2 = 2 * tq
    qw = q_ref.shape[1] // ATT_PAIRS
    per_head = qw // 2
    g = pl.program_id(1)
    qi = pl.program_id(2)
    pairs = range(ATT_PAIRS)

    def head(u, j):
        return 2 * (ATT_PAIRS * g + u) + j

    if fox:
        @pl.when(qi == 0)
        def _():
            fcv = fc_ref[...]
            lane_s = lax.broadcasted_iota(jnp.int32, fcv.shape, 1)
            for u in pairs:
                for j in range(2):
                    colv = jnp.sum(jnp.where(lane_s == head(u, j), fcv, 0.0), axis=-1, keepdims=True)
                    fs_scr[2 * u + j] = jnp.broadcast_to(colv, fcv.shape)

    lane = lax.broadcasted_iota(jnp.int32, (tq, qw), 1)
    q_bd, f_t = [], []
    for u in pairs:
        qb = q_ref[:, u * qw:(u + 1) * qw]
        zero = jnp.zeros_like(qb)
        q_bd.append(jnp.concatenate([jnp.where(lane // per_head == j, qb, zero) for j in range(2)],
                                    axis=0))
        if fox:
            f_t.append(jnp.concatenate([fr_ref[0, pl.ds(head(u, j), 1), :] for j in range(2)], axis=1))

    srow = lax.broadcasted_iota(jnp.int32, (tq, w2), 0)
    tcol = lax.broadcasted_iota(jnp.int32, (tq, w2), 1) % tq
    diag_mask = (srow <= tcol) if fox else ((srow // 64) <= (tcol // 64))
    sub = lax.broadcasted_iota(jnp.int32, (LANES, tq), 0)
    reps = tq // LANES

    def scores(u, lo, hi):
        st = lax.dot_general(k_ref[lo:hi, u * qw:(u + 1) * qw], q_bd[u], (((1,), (1,)), ((), ())),
                             preferred_element_type=F32)
        if fox:
            f_s = jnp.concatenate([fs_scr[2 * u, lo:hi]] * reps + [fs_scr[2 * u + 1, lo:hi]] * reps,
                                  axis=1)
            st = st + (f_t[u] - f_s)
        return st

    def values(u, lo, hi, pt):
        vt_ones = jnp.concatenate([vt_ref[0, u * LANES:(u + 1) * LANES, lo:hi],
                                   jnp.ones((16, hi - lo), BF16)], axis=0)
        return jnp.dot(vt_ones, pt.astype(BF16), preferred_element_type=F32)

    def query_tile(c):
        n_head = c * tq
        st_tail = [jnp.where(diag_mask, scores(u, n_head, n_head + tq), -jnp.inf) for u in pairs]
        mx = [jnp.max(st, axis=0, keepdims=True) for st in st_tail]
        if c > 0:
            st_head = [scores(u, 0, n_head) for u in pairs]
            mx = [jnp.maximum(m, jnp.max(st, axis=0, keepdims=True)) for m, st in zip(mx, st_head)]
        acc = [values(u, n_head, n_head + tq, jnp.exp(st_tail[u] - mx[u])) for u in pairs]
        if c > 0:
            acc = [acc[u] + values(u, 0, n_head, jnp.exp(st_head[u] - mx[u])) for u in pairs]
        for u in pairs:
            out = acc[u][:LANES] / acc[u][LANES:LANES + 1]
            o = jnp.where(sub // 64 == 0, out[:, :tq], out[:, tq:]).T
            cols = slice(u * LANES, (u + 1) * LANES)
            if fox:
                o = o * og_ref[:, cols].astype(F32)
            o_ref[:, cols] = o.astype(BF16)

    for c in range(nq):
        pl.when(qi == c)(functools.partial(query_tile, c))


def _attention(q, k, vt, fox_extras=None):
    fox = fox_extras is not None
    tq = ATT_TQ
    nq = S // tq
    qw = ATT_PAIRS * (LANES if fox else 2 * LANES)
    vw = ATT_PAIRS * LANES
    ops = [q, k, vt]
    specs = [pl.BlockSpec((tq, qw), lambda b, g, i: (b * nq + i, g)),
             pl.BlockSpec((S, qw), lambda b, g, i: (b, g)),
             pl.BlockSpec((1, vw, S), lambda b, g, i: (b, g, 0))]
    scratch = []
    if fox:
        og, fc, fr = fox_extras
        ops += [og, fc, fr]
        specs += [pl.BlockSpec((tq, vw), lambda b, g, i: (b * nq + i, g)),
                  pl.BlockSpec((S, LANES), lambda b, g, i: (b, 0)),
                  pl.BlockSpec((1, FOX_HEADS, tq), lambda b, g, i: (b, 0, i))]
        scratch.append(pltpu.VMEM((2 * ATT_PAIRS, S, LANES), F32))
    return pl.pallas_call(
        functools.partial(_attn_kernel, fox),
        grid=(B, D // vw, nq),
        in_specs=specs,
        out_specs=pl.BlockSpec((tq, vw), lambda b, g, i: (b * nq + i, g)),
        out_shape=jax.ShapeDtypeStruct((T, D), BF16),
        scratch_shapes=scratch,
        compiler_params=_cparams("parallel", "parallel", "arbitrary"),
        name="fox_attn" if fox else "mla_attn",
    )(*ops)


assert DEPTH % 2 == 0

def kernel(x, c, positions, ada_w, ada_b, norm_mix_g, norm_ffn_g, final_norm_g, rw_mu, rw_w_rkv, rw_w_o, rw_w0, rw_w1, rw_w2, rw_a0, rw_a1, rw_a2, rw_g1, rw_g2, rw_k_k, rw_k_a, rw_r_k, rw_lnx_g, rw_lnx_b, rw_v0, rw_v1, rw_v2, mla_w_down, mla_q_norm_g, mla_kv_norm_g, mla_w_uq, mla_w_ukv, mla_w_o, fox_w_in, fox_b_f, fox_q_norm_g, fox_k_norm_g, fox_w_o, ffn_w_gate_up, ffn_w_down, moe_w_router, moe_b_router, moe_w_gate_up, moe_w_down):
    xf = x.reshape(T, D)
    mods = _ada_mods(c, ada_w, ada_b)
    ffn_wgu, ffn_wd = ffn_w_gate_up.astype(BF16), ffn_w_down.astype(BF16)
    moe_wgu, moe_wd = moe_w_gate_up.astype(BF16), moe_w_down.astype(BF16)
    v_first = None
    for i in range(DEPTH):
        mod = mods[i]
        kind, j = i % 3, i // 3
        if kind == 0:
            p = dict(mu=rw_mu[j], w_rkv=rw_w_rkv[j], w0=rw_w0[j], w1=rw_w1[j], w2=rw_w2[j],
                     a0=rw_a0[j], a1=rw_a1[j], a2=rw_a2[j], g1=rw_g1[j], g2=rw_g2[j],
                     k_k=rw_k_k[j], k_a=rw_k_a[j])
            if j > 0:
                p.update(v0=rw_v0[j - 1], v1=rw_v1[j - 1], v2=rw_v2[j - 1])
            r, lw, k, v, kk, a, g = _rwkv_proj(xf, mod, norm_mix_g[i], p, v_first if j > 0 else None)
            if j == 0:
                v_first = v
            y = _rwkv_scan(r, lw, k, v, kk, a, g, rw_r_k[j], rw_lnx_g[j], rw_lnx_b[j])
            xf = _proj_res(y, rw_w_o[j], xf, mod, 2)
        elif kind == 1:
            q, k, v = _mla_proj(xf, mod, norm_mix_g[i], positions, mla_w_down[j], mla_q_norm_g[j],
                                mla_kv_norm_g[j], mla_w_uq[j], mla_w_ukv[j])
            o = _attention(q, k, v)
            xf = _proj_res(o, mla_w_o[j], xf, mod, 2)
        else:
            q, k, v, og, fc, fr = _fox_proj(xf, mod, norm_mix_g[i], fox_w_in[j], fox_b_f[j],
                                            fox_q_norm_g[j], fox_k_norm_g[j])
            o = _attention(q, k, v, (og, fc, fr))
            xf = _proj_res(o, fox_w_o[j], xf, mod, 2)
        if i % 2 == 0:
            xf = _ffn_dense(xf, mod, norm_ffn_g[i], ffn_wgu, ffn_wd, i // 2)
        else:
            xf = _moe(xf, mod, norm_ffn_g[i], moe_w_router[i // 2], moe_b_router[i // 2],
                      moe_wgu, moe_wd, i // 2, final_norm_g if i == DEPTH - 1 else None)
    return xf.reshape(B, S, D)
```

```python
import functools
import math

import jax
import jax.numpy as jnp
from jax import lax
from jax.experimental import pallas as pl
from jax.experimental.pallas import tpu as pltpu

F32 = jnp.float32
BF16 = jnp.bfloat16
HIGHEST = lax.Precision.HIGHEST

D = 1024
B = 8
S = 2048
T = B * S
DEPTH = 4
NORM_EPS = 1e-6

RW_HEAD = 64
GN_EPS = 64e-5
EXP_NEG_HALF = math.exp(-0.5)
LOG2E = math.log2(math.e)

MLA_HEADS = 16
MLA_NOPE = 64
MLA_ROPE = 32
MLA_Q_LORA = 768
MLA_KV_LORA = 256
MLA_SCALE = (MLA_NOPE + MLA_ROPE) ** -0.5
ROPE_BASE = 10000.0

FOX_HEADS = 16
FOX_HEAD = 64
FOX_SCALE = FOX_HEAD ** -0.5

D_FF = 2816
N_EXPERTS = 8

LANES = 128
GROUP_SUM_WIDTH = 256
VMEM_LIMIT = 56 * 1024 * 1024

RW_CHUNK = 64
RW_GROUP = 4
RW_PAR = 4
ATT_TQ = 256
ATT_PAIRS = 4
FFN_TM = 1024
FFN_SUB = 512
ROUTE_TM = 512
FFN_TF = 1408
PROJ_TM = 256
MOE_TW = 1024
MOE_STEP = 128
MOE_MAX = 512


def _cparams(*sem):
    return pltpu.CompilerParams(dimension_semantics=sem, vmem_limit_bytes=VMEM_LIMIT)


def _dot(a, b):
    return jnp.dot(a.astype(BF16), b.astype(BF16), preferred_element_type=F32)


def _dot_nt(a, b):
    return lax.dot_general(a.astype(BF16), b.astype(BF16), (((1,), (1,)), ((), ())),
                           preferred_element_type=F32)


def _dot_tn(a, b):
    return lax.dot_general(a.astype(BF16), b.astype(BF16), (((0,), (0,)), ((), ())),
                           preferred_element_type=F32)


def _sigmoid(z):
    return 0.5 * jnp.tanh(0.5 * z) + 0.5


def _modnorm(x, g, shift, scale):
    ms = jnp.mean(x * x, axis=-1, keepdims=True)
    y = x * lax.rsqrt(ms + NORM_EPS) * g
    return y * (1.0 + scale) + shift


def _group_sum64(x):
    width = GROUP_SUM_WIDTH
    gi = lax.broadcasted_iota(jnp.int32, (width, width), 0) // 64
    gj = lax.broadcasted_iota(jnp.int32, (width, width), 1) // 64
    ones = jnp.where(gi == gj, 1.0, 0.0).astype(BF16)
    xb = x.astype(BF16)
    outs = [jnp.dot(xb[:, s * width:(s + 1) * width], ones, preferred_element_type=F32)
            for s in range(x.shape[1] // width)]
    return outs[0] if len(outs) == 1 else jnp.concatenate(outs, axis=1)


def _store_transposed(vt_ref, v):
    vt_ref[0] = v.T.astype(BF16)


def _vt_spec(tm):
    nb = S // tm
    return pl.BlockSpec((1, D, tm), lambda i: (i // nb, 0, i % nb))


def _vt_shape():
    return jax.ShapeDtypeStruct((B, D, S), BF16)


def _ada_kernel(c_ref, w_ref, b_ref, o_ref):
    c = c_ref[...]
    cond = c * _sigmoid(c)
    o_ref[0] = _dot(cond, w_ref[0]) + b_ref[0]


def _ada_mods(c, ada_w, ada_b):
    tn = 1536
    out = pl.pallas_call(
        _ada_kernel,
        grid=(DEPTH, 6 * D // tn),
        in_specs=[pl.BlockSpec((B, D), lambda l, j: (0, 0)),
                  pl.BlockSpec((1, D, tn), lambda l, j: (l, 0, j)),
                  pl.BlockSpec((1, 1, tn), lambda l, j: (l, 0, j))],
        out_specs=pl.BlockSpec((1, B, tn), lambda l, j: (l, 0, j)),
        out_shape=jax.ShapeDtypeStruct((DEPTH, B, 6 * D), F32),
        compiler_params=_cparams("parallel", "parallel"),
        name="ada_mods",
    )(c, ada_w, ada_b.reshape(DEPTH, 1, 6 * D))
    return out.reshape(DEPTH, B, 6, D)


def _rwkv_proj_kernel(has_vres, *refs):
    if has_vres:
        (x_ref, xp_ref, mod_ref, ng_ref, mu_ref, wr_ref, wk_ref, wv_ref, w0_ref, w1_ref, w2_ref,
         a0_ref, a1_ref, a2_ref, g1_ref, g2_ref, kk_ref, ka_ref, v0_ref, v1_ref, v2_ref, vf_ref,
         r_out, lw_out, k_out, v_out, kk_out, a_out, g_out) = refs
    else:
        (x_ref, xp_ref, mod_ref, ng_ref, mu_ref, wr_ref, wk_ref, wv_ref, w0_ref, w1_ref, w2_ref,
         a0_ref, a1_ref, a2_ref, g1_ref, g2_ref, kk_ref, ka_ref,
         r_out, lw_out, k_out, v_out, kk_out, a_out, g_out) = refs
    i = pl.program_id(0)
    mod = mod_ref[0]
    shift, scale = mod[0:1], mod[1:2]
    g = ng_ref[...]
    h = _modnorm(x_ref[...], g, shift, scale)
    hp = _modnorm(xp_ref[...], g, shift, scale)
    seq_start = (i % (S // PROJ_TM)) == 0
    prev_row = jnp.where(seq_start, 0.0, hp[7:8, :])
    row = lax.broadcasted_iota(jnp.int32, h.shape, 0)
    prev = jnp.where(row == 0, prev_row, pltpu.roll(h, 1, axis=0))
    delta = prev - h
    mu = mu_ref[...]
    xr = h + delta * mu[0:1]
    xw = h + delta * mu[1:2]
    xk = h + delta * mu[2:3]
    xv = h + delta * mu[3:4]
    xa = h + delta * mu[4:5]
    xg = h + delta * mu[5:6]
    r = _dot(xr, wr_ref[...])
    k = _dot(xk, wk_ref[...])
    v = _dot(xv, wv_ref[...])
    w_raw = w0_ref[...] + _dot(jnp.tanh(_dot(xw, w1_ref[...])), w2_ref[...])
    lw_out[...] = -_sigmoid(w_raw) * EXP_NEG_HALF
    if has_vres:
        mix = _sigmoid(v0_ref[...] + _dot(_dot(xv, v1_ref[...]), v2_ref[...]))
        v = v + (vf_ref[...] - v) * mix
    a = _sigmoid(a0_ref[...] + _dot(_dot(xa, a1_ref[...]), a2_ref[...]))
    g_out[...] = _dot(_sigmoid(_dot(xg, g1_ref[...])), g2_ref[...])
    kk = k * kk_ref[...]
    kk_out[...] = kk * lax.rsqrt(jnp.maximum(_group_sum64(kk * kk), 1e-24))
    k_out[...] = k * (1.0 + (a - 1.0) * ka_ref[...])
    r_out[...] = r
    v_out[...] = v
    a_out[...] = a


def _rwkv_proj(x, mod, norm_g, p, v_first):
    has_vres = v_first is not None
    tm = PROJ_TM
    nb = S // tm
    row = lambda a: a.reshape(1, -1)
    full = lambda a: pl.BlockSpec(a.shape, lambda i: (0,) * a.ndim)
    tile = pl.BlockSpec((tm, D), lambda i: (i, 0))
    ops = [x, x, mod, row(norm_g), p["mu"],
           p["w_rkv"][0].astype(BF16), p["w_rkv"][1].astype(BF16), p["w_rkv"][2].astype(BF16),
           row(p["w0"]), p["w1"].astype(BF16), p["w2"].astype(BF16),
           row(p["a0"]), p["a1"].astype(BF16), p["a2"].astype(BF16),
           p["g1"].astype(BF16), p["g2"].astype(BF16), row(p["k_k"]), row(p["k_a"])]
    specs = [tile,
             pl.BlockSpec((8, D), lambda i: (jnp.maximum(i * (tm // 8) - 1, 0), 0)),
             pl.BlockSpec((1, 6, D), lambda i: (i // nb, 0, 0))]
    specs += [full(a) for a in ops[3:]]
    if has_vres:
        extra = [row(p["v0"]), p["v1"].astype(BF16), p["v2"].astype(BF16)]
        ops += extra + [v_first]
        specs += [full(a) for a in extra] + [tile]
    outs = pl.pallas_call(
        functools.partial(_rwkv_proj_kernel, has_vres),
        grid=(T // tm,),
        in_specs=specs,
        out_specs=[tile] * 7,
        out_shape=[jax.ShapeDtypeStruct((T, D), F32)] * 7,
        compiler_params=_cparams("parallel"),
        name="rwkv_proj",
    )(*ops)
    return outs


def _rwkv_scan_kernel(r_ref, lw_ref, k_ref, v_ref, kk_ref, a_ref, g_ref, rk_ref, lg_ref, lb_ref,
                      o_ref, s_ref):
    c = pl.program_id(2)
    W = RW_GROUP * RW_HEAD

    @pl.when(c == 0)
    def _():
        s_ref[...] = jnp.zeros_like(s_ref)

    slabs = [slice(p * W, (p + 1) * W) for p in range(RW_PAR)]
    cols = lambda ref: [ref[:, sl] for sl in slabs]
    out, s_new = _rwkv_blocks(cols(r_ref), cols(lw_ref), cols(k_ref), cols(v_ref), cols(kk_ref),
                              cols(a_ref), cols(g_ref), cols(rk_ref), cols(lg_ref), cols(lb_ref),
                              [s_ref[p] for p in range(RW_PAR)])
    for p, sl in enumerate(slabs):
        o_ref[:, sl] = out[p]
        s_ref[p] = s_new[p]


def _cumsum_rows(x):
    row = lax.broadcasted_iota(jnp.int32, x.shape, 0)
    step = 1
    while step < x.shape[0]:
        x = x + jnp.where(row >= step, pltpu.roll(x, step, axis=0), 0.0)
        step *= 2
    return x


def _each(fn, *cols):
    return [fn(*args) for args in zip(*cols)]


def _rwkv_blocks(r, lw, k, v, kk, a, g, r_k, lnx_g, lnx_b, s_old):
    C = RW_CHUNK
    W = RW_GROUP * RW_HEAD

    mul = lambda x, y: x * y
    cl = _each(_cumsum_rows, lw)
    cl_end = _each(lambda x: x[C - 1:C, :], cl)
    w_t = _each(jnp.exp, cl)
    w_prev = _each(lambda x, y: jnp.exp(x - y), cl, lw)
    w_inv = _each(lambda x: jnp.exp(-x), cl)
    w_rem = _each(lambda x, y: jnp.exp(x - y), cl_end, cl)
    bv = _each(mul, kk, a)
    r_hat = _each(mul, r, w_t)
    a_hat = _each(lambda x, y: -x * y, kk, w_prev)
    b_hat = _each(mul, bv, w_inv)
    k_hat = _each(mul, k, w_inv)
    b_til = _each(mul, bv, w_rem)
    k_til = _each(mul, k, w_rem)

    lane_head = lax.broadcasted_iota(jnp.int32, (C, W), 1) // RW_HEAD

    def stack(m):
        mb = m.astype(BF16)
        zero = jnp.zeros_like(mb)
        return jnp.concatenate([jnp.where(lane_head == hd, mb, zero) for hd in range(RW_GROUP)],
                               axis=0)

    cat0 = lambda x, y: jnp.concatenate([x, y], axis=0)
    gram = _each(lambda ah, rh, bh, kh: _dot_nt(cat0(stack(ah), stack(rh)), cat0(stack(bh), stack(kh))),
                 a_hat, r_hat, b_hat, k_hat)
    n = RW_GROUP * C
    ri = lax.broadcasted_iota(jnp.int32, (n, n), 0)
    ci = lax.broadcasted_iota(jnp.int32, (n, n), 1)
    strict = (ri % C) > (ci % C)
    incl = (ri % C) >= (ci % C)
    same_head = (ri // RW_HEAD) == (ci // RW_HEAD)
    eye = jnp.where(ri == ci, 1.0, 0.0)
    l_ab = _each(lambda x: jnp.where(strict, x[:n, :n], 0.0), gram)
    l_ak = _each(lambda x: jnp.where(strict, x[:n, n:], 0.0), gram)
    m_rb = _each(lambda x: jnp.where(incl, x[n:, :n], 0.0), gram)
    m_rk = _each(lambda x: jnp.where(incl, x[n:, n:], 0.0), gram)

    lane2 = lax.broadcasted_iota(jnp.int32, (C, 2 * RW_HEAD), 1)

    def packed(m):
        cols = [jnp.where(lane2 < RW_HEAD,
                          m[(2 * j) * C:(2 * j + 1) * C, j * 2 * RW_HEAD:(j + 1) * 2 * RW_HEAD],
                          m[(2 * j + 1) * C:(2 * j + 2) * C, j * 2 * RW_HEAD:(j + 1) * 2 * RW_HEAD])
                for j in range(RW_GROUP // 2)]
        return jnp.concatenate(cols, axis=1)

    def rows(m):
        blocks = []
        for hd in range(RW_GROUP):
            j = hd // 2
            blk = m[hd * C:(hd + 1) * C, j * 2 * RW_HEAD:(j + 1) * 2 * RW_HEAD]
            blocks.append(jnp.where((lane2 // RW_HEAD) == hd % 2, blk, 0.0))
        return jnp.concatenate(blocks, axis=0)

    def diag_products(x_rows, y_packed):
        yb = y_packed.astype(BF16)
        return _dot(x_rows, jnp.concatenate([yb, yb], axis=0))

    rounds = int(math.log2(C)) - 1
    inv_p = _each(lambda x: packed(eye + x), l_ab)
    inv_r = _each(lambda x: rows(eye + x), l_ab)
    pw_p = _each(packed, l_ab)
    pw_r = _each(rows, l_ab)
    for i in range(rounds):
        sq = _each(diag_products, pw_r, pw_p)
        pw_p = _each(packed, sq)
        upd = _each(diag_products, inv_r, pw_p)
        inv_p = _each(lambda x, y: x + packed(y), inv_p, upd)
        if i < rounds - 1:
            pw_r = _each(rows, sq)
            inv_r = _each(lambda x, y: x + rows(y), inv_r, upd)

    a_p = _each(lambda x, y: _dot(x, stack(y)), inv_p, a_hat)
    t_l = _each(_dot, inv_p, l_ak)
    v_p = _each(lambda x, y: _dot(x, stack(y)), t_l, v)
    m_rb_p = _each(packed, m_rb)
    m_rk_p = _each(packed, m_rk)
    r_p = _each(lambda x, y, z: x + _dot(y, stack(z)), r_hat, m_rb_p, a_p)
    y0 = _each(lambda mb, mk, vp, vv: _dot(jnp.concatenate([mb, mk], axis=1),
                                           cat0(stack(vp), stack(vv))), m_rb_p, m_rk_p, v_p, v)
    a_til = _each(lambda x, y: jnp.where(same_head, _dot_tn(x, y), 0.0), b_til, a_p)
    d_new = _each(lambda vp, vv, bt, kt: jnp.where(same_head, _dot_tn(cat0(vp, vv), cat0(bt, kt)), 0.0),
                  v_p, v, b_til, k_til)
    y = _each(lambda x, s, z: _dot_nt(x, s) + z, r_p, s_old, y0)
    s_new = _each(lambda s, ce, at, dn: s * jnp.exp(ce) + _dot_nt(s, at) + dn,
                  s_old, cl_end, a_til, d_new)

    inv_n = 1.0 / RW_HEAD
    mean = _each(lambda x: _group_sum64(x) * inv_n, y)
    yc = _each(lambda x, m: x - m, y, mean)
    var = _each(lambda x: _group_sum64(x * x) * inv_n, yc)
    yn = _each(lambda x, vr, lg, lb: x * lax.rsqrt(vr + GN_EPS) * lg + lb, yc, var, lnx_g, lnx_b)
    bonus = _each(lambda rr, kx, rk, vv: _group_sum64(rr * kx * rk) * vv, r, k, r_k, v)
    out = _each(lambda x, bo, gg: ((x + bo) * gg).astype(BF16), yn, bonus, g)
    return out, s_new


def _rwkv_scan(r, lw, k, v, kk, a, g, r_k, lnx_g, lnx_b):
    C = RW_CHUNK
    W = RW_PAR * RW_GROUP * RW_HEAD
    nc = S // C
    slab = pl.BlockSpec((C, W), lambda b, gi, c: (b * nc + c, gi))
    prow = pl.BlockSpec((1, W), lambda b, gi, c: (0, gi))
    return pl.pallas_call(
        _rwkv_scan_kernel,
        grid=(B, D // W, nc),
        in_specs=[slab] * 7 + [prow] * 3,
        out_specs=slab,
        out_shape=jax.ShapeDtypeStruct((T, D), BF16),
        scratch_shapes=[pltpu.VMEM((RW_PAR, RW_GROUP * RW_HEAD, RW_GROUP * RW_HEAD), F32)],
        compiler_params=_cparams("parallel", "parallel", "arbitrary"),
        name="rwkv_scan",
    )(r, lw, k, v, kk, a, g, r_k.reshape(1, D), lnx_g.reshape(1, D), lnx_b.reshape(1, D))


def _proj_res_kernel(gate_row, a_ref, w_ref, x_ref, mod_ref, o_ref):
    y = jnp.dot(a_ref[...], w_ref[...], preferred_element_type=F32)
    o_ref[...] = x_ref[...] + mod_ref[0][gate_row:gate_row + 1] * y


def _proj_res(a, w, x, mod, gate_row):
    tm = 512
    nb = S // tm
    kdim = a.shape[1]
    return pl.pallas_call(
        functools.partial(_proj_res_kernel, gate_row),
        grid=(T // tm,),
        in_specs=[pl.BlockSpec((tm, kdim), lambda i: (i, 0)),
                  pl.BlockSpec((kdim, D), lambda i: (0, 0)),
                  pl.BlockSpec((tm, D), lambda i: (i, 0)),
                  pl.BlockSpec((1, 6, D), lambda i: (i // nb, 0, 0))],
        out_specs=pl.BlockSpec((tm, D), lambda i: (i, 0)),
        out_shape=jax.ShapeDtypeStruct((T, D), F32),
        compiler_params=_cparams("parallel"),
        name="proj_res",
    )(a, w.astype(BF16), x, mod)


def _ffn_kernel(nf, x_ref, mod_ref, ng_ref, wg_ref, wu_ref, wd_ref, o_ref, h_scr, acc_scr):
    f = pl.program_id(1)

    @pl.when(f == 0)
    def _():
        mod = mod_ref[0]
        h_scr[...] = _modnorm(x_ref[...], ng_ref[...], mod[3:4], mod[4:5]).astype(BF16)
        acc_scr[...] = jnp.zeros_like(acc_scr)

    for sub in range(h_scr.shape[0] // FFN_SUB):
        rows = slice(sub * FFN_SUB, (sub + 1) * FFN_SUB)
        h = h_scr[rows]
        gt = jnp.dot(h, wg_ref[0], preferred_element_type=F32)
        up = jnp.dot(h, wu_ref[0], preferred_element_type=F32)
        act = (gt * _sigmoid(gt) * up).astype(BF16)
        acc_scr[rows] += jnp.dot(act, wd_ref[0], preferred_element_type=F32)

    @pl.when(f == nf - 1)
    def _():
        o_ref[...] = x_ref[...] + mod_ref[0][5:6] * acc_scr[...]


def _ffn_dense(x, mod, norm_g, wgu, wd, layer):
    tm, tf = FFN_TM, FFN_TF
    nf = D_FF // tf
    nb = S // tm
    return pl.pallas_call(
        functools.partial(_ffn_kernel, nf),
        grid=(T // tm, nf),
        in_specs=[pl.BlockSpec((tm, D), lambda i, f: (i, 0)),
                  pl.BlockSpec((1, 6, D), lambda i, f: (i // nb, 0, 0)),
                  pl.BlockSpec((1, D), lambda i, f: (0, 0)),
                  pl.BlockSpec((1, D, tf), lambda i, f: (layer, 0, f)),
                  pl.BlockSpec((1, D, tf), lambda i, f: (layer, 0, nf + f)),
                  pl.BlockSpec((1, tf, D), lambda i, f: (layer, f, 0))],
        out_specs=pl.BlockSpec((tm, D), lambda i, f: (i, 0)),
        out_shape=jax.ShapeDtypeStruct((T, D), F32),
        scratch_shapes=[pltpu.VMEM((tm, D), BF16), pltpu.VMEM((tm, D), F32)],
        compiler_params=_cparams("parallel", "arbitrary"),
        name="ffn_dense",
    )(x, mod, norm_g.reshape(1, D), wgu, wgu, wd)


def _route_kernel(x_ref, mod_ref, ng_ref, wr_ref, br_ref,
                  h_out, cw_t_out, dest_t_out, cnt_out, carry_scr):
    i = pl.program_id(0)
    tm = x_ref.shape[0]
    mod = mod_ref[0]
    h = _modnorm(x_ref[...], ng_ref[...], mod[3:4], mod[4:5])
    h_out[...] = h.astype(BF16)
    logits = jnp.dot(h, wr_ref[...], precision=HIGHEST, preferred_element_type=F32) + br_ref[...]
    lane = lax.broadcasted_iota(jnp.int32, logits.shape, 1)
    neg = -jnp.inf
    logits = jnp.where(lane < N_EXPERTS, logits, neg)
    m1 = jnp.max(logits, axis=-1, keepdims=True)
    i1 = jnp.min(jnp.where(logits == m1, lane, LANES), axis=-1, keepdims=True)
    rest = jnp.where(lane == i1, neg, logits)
    m2 = jnp.max(rest, axis=-1, keepdims=True)
    i2 = jnp.min(jnp.where(rest == m2, lane, LANES), axis=-1, keepdims=True)
    e2 = jnp.exp(m2 - m1)
    w1 = 1.0 / (1.0 + e2)
    w2 = e2 / (1.0 + e2)
    cw_t_out[...] = (jnp.where(lane == i1, w1, 0.0) + jnp.where(lane == i2, w2, 0.0)).T

    sel =jnp.where((lane == i1) | (lane == i2), 1.0, 0.0)
    ti = lax.broadcasted_iota(jnp.int32, (tm, tm), 0)
    si = lax.broadcasted_iota(jnp.int32, (tm, tm), 1)
    before = jnp.where(ti > si, 1.0, 0.0).astype(BF16)
    window_start = (i % (MOE_TW // tm)) == 0
    carry = jnp.where(window_start, 0.0, carry_scr[...])
    rank = jnp.dot(before, sel.astype(BF16), preferred_element_type=F32) + carry
    dest_t_out[...] = jnp.where(sel > 0.0, rank, -1.0).T
    total = carry + jnp.sum(sel, axis=0, keepdims=True)
    carry_scr[...] = total
    cnt_out[0] = jnp.broadcast_to(total, (8, LANES))


def _moe_kernel(nf, has_final, cnt_ref, x_ref, mod_ref, h_ref, dest_t_ref, cw_t_ref, wg_ref, wu_ref,
                wd_ref, *rest):
    fin_ref = rest[0] if has_final else None
    o_ref, xg_scr, yc_scr = rest[-3:]
    _moe_body(nf, cnt_ref, x_ref, mod_ref, h_ref, dest_t_ref, cw_t_ref, wg_ref, wu_ref, wd_ref,
              fin_ref, o_ref, xg_scr, yc_scr)


def _moe_body(nf, cnt_ref, x_ref, mod_ref, h_ref, dest_t_ref, cw_t_ref, wg_ref, wu_ref, wd_ref,
              fin_ref, o_ref, xg_scr, yc_scr):
    w = pl.program_id(0)
    e = pl.program_id(1)
    f = pl.program_id(2)
    tw = MOE_TW
    n = cnt_ref[w * N_EXPERTS + e]

    @pl.when((e == 0) & (f == 0))
    def _():
        o_ref[...] = jnp.zeros_like(o_ref)

    dest_row = dest_t_ref[pl.ds(e, 1), :]
    cw_row = cw_t_ref[pl.ds(e, 1), :]

    def slot_tile(base, rows):
        slot = lax.broadcasted_iota(jnp.int32, (rows, tw), 0).astype(F32) + base
        match = dest_row == slot
        select = jnp.where(match, 1.0, 0.0).astype(BF16)
        span = pl.ds(base, rows)

        @pl.when(f == 0)
        def _():
            xg_scr[span, :] = jnp.dot(select, h_ref[...], preferred_element_type=F32).astype(BF16)

        xs = xg_scr[span, :]
        gt = jnp.dot(xs, wg_ref[0, 0], preferred_element_type=F32)
        up = jnp.dot(xs, wu_ref[0, 0], preferred_element_type=F32)
        act = (gt * _sigmoid(gt) * up).astype(BF16)
        y = jnp.dot(act, wd_ref[0, 0], preferred_element_type=F32)

        @pl.when(f == 0)
        def _():
            yc_scr[span, :] = y

        @pl.when((f > 0) & (f < nf - 1))
        def _():
            yc_scr[span, :] += y

        @pl.when(f == nf - 1)
        def _():
            w_slot = jnp.sum(jnp.where(match, cw_row, 0.0), axis=-1, keepdims=True)
            weighted = ((yc_scr[span, :] + y) * w_slot).astype(BF16)
            o_ref[...] += _dot_tn(select, weighted)

    sizes = list(range(MOE_STEP, MOE_MAX + 1, MOE_STEP))
    for lo, rows in zip([0] + sizes[:-1], sizes):
        pl.when((n > lo) & (n <= rows))(functools.partial(slot_tile, 0, rows))

    @pl.when(n > MOE_MAX)
    def _():
        def body(s, carry):
            slot_tile(pl.multiple_of(s * MOE_MAX, MOE_MAX), MOE_MAX)
            return carry
        lax.fori_loop(0, (n + MOE_MAX - 1) // MOE_MAX, body, 0)

    @pl.when((e == N_EXPERTS - 1) & (f == nf - 1))
    def _():
        y = x_ref[...] + mod_ref[0][5:6] * o_ref[...]
        if fin_ref is not None:
            y = y * lax.rsqrt(jnp.mean(y * y, axis=-1, keepdims=True) + NORM_EPS) * fin_ref[...]
        o_ref[...] = y


def _moe(x, mod, norm_g, w_router, b_router, wgu, wd, layer, final_g=None):
    tm, tf, tw = ROUTE_TM, FFN_TF, MOE_TW
    nf = D_FF // tf
    nb = S // tm
    nw = T // tw
    wr =jnp.zeros((D, LANES), F32).at[:, :N_EXPERTS].set(w_router)
    br = jnp.zeros((1, LANES), F32).at[0, :N_EXPERTS].set(b_router)
    tile = lambda width: pl.BlockSpec((tm, width), lambda i: (i, 0))
    assert nf >= 2
    lanes_t = pl.BlockSpec((LANES, tm), lambda i: (0, i))
    h, cw_t, dest_t, cnt = pl.pallas_call(
        _route_kernel,
        grid=(T // tm,),
        in_specs=[tile(D),
                  pl.BlockSpec((1, 6, D), lambda i: (i // nb, 0, 0)),
                  pl.BlockSpec((1, D), lambda i: (0, 0)),
                  pl.BlockSpec((D, LANES), lambda i: (0, 0)),
                  pl.BlockSpec((1, LANES), lambda i: (0, 0))],
        out_specs=[tile(D), lanes_t, lanes_t,
                   pl.BlockSpec((1, 8, LANES), lambda i: (i // (tw // tm), 0, 0))],
        out_shape=[jax.ShapeDtypeStruct((T, D), BF16), jax.ShapeDtypeStruct((LANES, T), F32),
                   jax.ShapeDtypeStruct((LANES, T), F32), jax.ShapeDtypeStruct((nw, 8, LANES), F32)],
        scratch_shapes=[pltpu.VMEM((1, LANES), F32)],
        compiler_params=_cparams("arbitrary"),
        name="moe_route",
    )(x, mod, norm_g.reshape(1, D), wr, br)
    counts = cnt[:, 0, :N_EXPERTS].astype(jnp.int32).reshape(nw * N_EXPERTS)
    cap = -(-tw // MOE_MAX) * MOE_MAX
    has_final = final_g is not None
    extra_ops = [final_g.reshape(1, D)] if has_final else []
    extra_specs = [pl.BlockSpec((1, D), lambda w, e, f, c: (0, 0))] if has_final else []
    return pl.pallas_call(
        functools.partial(_moe_kernel, nf, has_final),
        grid_spec=pltpu.PrefetchScalarGridSpec(
            num_scalar_prefetch=1,
            grid=(nw, N_EXPERTS, nf),
            in_specs=[pl.BlockSpec((tw, D), lambda w, e, f, c: (w, 0)),
                      pl.BlockSpec((1, 6, D), lambda w, e, f, c: (w // (S // tw), 0, 0)),
                      pl.BlockSpec((tw, D), lambda w, e, f, c: (w, 0)),
                      pl.BlockSpec((LANES, tw), lambda w, e, f, c: (0, w)),
                      pl.BlockSpec((LANES, tw), lambda w, e, f, c: (0, w)),
                      pl.BlockSpec((1, 1, D, tf), lambda w, e, f, c: (layer, e, 0, f)),
                      pl.BlockSpec((1, 1, D, tf), lambda w, e, f, c: (layer, e, 0, nf + f)),
                      pl.BlockSpec((1, 1, tf, D), lambda w, e, f, c: (layer, e, f, 0))] + extra_specs,
            out_specs=pl.BlockSpec((tw, D), lambda w, e, f, c: (w, 0)),
            scratch_shapes=[pltpu.VMEM((cap, D), BF16), pltpu.VMEM((cap, D), F32)]),
        out_shape=jax.ShapeDtypeStruct((T, D), F32),
        compiler_params=_cparams("parallel", "arbitrary", "arbitrary"),
        name="moe_experts",
    )(counts, x, mod, h, dest_t, cw_t, wgu, wgu, wd, *extra_ops)


def _mla_proj_kernel(x_ref, mod_ref, ng_ref, pos_ref, invf_ref, wd_ref, qg_ref, kvg_ref,
                     wuq_ref, wukv_ref, q_out, k_out, v_out):
    mod = mod_ref[0]
    h = _modnorm(x_ref[...], ng_ref[...], mod[0:1], mod[1:2])
    down = _dot(h, wd_ref[...])
    cq = down[:, :MLA_Q_LORA]
    ckv = down[:, MLA_Q_LORA:MLA_Q_LORA + MLA_KV_LORA]
    kr = down[:, MLA_Q_LORA + MLA_KV_LORA:]
    cq = cq * lax.rsqrt(jnp.mean(cq * cq, axis=-1, keepdims=True) + NORM_EPS) * qg_ref[...]
    ckv = ckv * lax.rsqrt(jnp.mean(ckv * ckv, axis=-1, keepdims=True) + NORM_EPS) * kvg_ref[...]

    ang = pos_ref[...] * invf_ref[...]
    lane = lax.broadcasted_iota(jnp.int32, ang.shape, 1)
    is_rope = (lane >= MLA_NOPE) & (lane < MLA_NOPE + MLA_ROPE)
    cos_r = jnp.where(is_rope, jnp.cos(ang), 0.0)
    sin_r = jnp.where(is_rope, jnp.sin(ang), 0.0)
    cos_q = jnp.where(lane < MLA_NOPE, 1.0, cos_r)
    shift = LANES - MLA_ROPE

    k_rope = kr * cos_r + pltpu.roll(kr, shift, axis=1) * sin_r

    qf = _dot(cq, wuq_ref[...])
    kv = _dot(ckv, wukv_ref[...])
    for hd in range(MLA_HEADS):
        sl = slice(hd * LANES, (hd + 1) * LANES)
        qs = qf[:, sl]
        q_out[:, sl] = ((qs * cos_q + pltpu.roll(qs, shift, axis=1) * sin_r)
                        * (MLA_SCALE * LOG2E)).astype(BF16)
        k_out[:, sl] = (kv[:, sl] + k_rope).astype(BF16)
    _store_transposed(v_out, kv[:, MLA_HEADS * LANES:])


def _rot_half_cols(w):
    half = MLA_ROPE // 2
    return jnp.concatenate([-w[..., half:], w[..., :half]], axis=-1)


def _mla_proj(x, mod, norm_g, positions, w_down, q_norm_g, kv_norm_g, w_uq, w_ukv):
    tm = PROJ_TM
    nb = S // tm
    half = MLA_ROPE // 2
    wr = w_down[:, MLA_Q_LORA + MLA_KV_LORA:]
    wd = jnp.concatenate([w_down[:, :MLA_Q_LORA + MLA_KV_LORA],
                          jnp.zeros((D, MLA_NOPE), F32), wr, _rot_half_cols(wr)], axis=1).astype(BF16)
    wq = w_uq.reshape(MLA_Q_LORA, MLA_HEADS, MLA_NOPE + MLA_ROPE)
    wq_r = wq[..., MLA_NOPE:]
    wuq = jnp.concatenate([wq[..., :MLA_NOPE], wq_r, _rot_half_cols(wq_r)], axis=-1)
    wuq = wuq.reshape(MLA_Q_LORA, MLA_HEADS * LANES).astype(BF16)
    wkv = w_ukv.reshape(MLA_KV_LORA, MLA_HEADS, 2 * MLA_NOPE)
    wk = jnp.concatenate([wkv[..., :MLA_NOPE], jnp.zeros_like(wkv[..., :MLA_NOPE])], axis=-1)
    wukv = jnp.concatenate([wk.reshape(MLA_KV_LORA, MLA_HEADS * LANES),
                            wkv[..., MLA_NOPE:].reshape(MLA_KV_LORA, D)], axis=1).astype(BF16)
    inv_freq = ROPE_BASE ** (-jnp.arange(half, dtype=F32) / half)
    lane = jnp.arange(LANES)
    invf = jnp.where((lane >= MLA_NOPE) & (lane < MLA_NOPE + MLA_ROPE),
                     inv_freq[(lane - MLA_NOPE) % half], 0.0).reshape(1, LANES).astype(F32)
    pos = jnp.broadcast_to(positions.reshape(T, 1).astype(F32), (T, LANES))
    full = lambda a: pl.BlockSpec(a.shape, lambda i: (0,) * a.ndim)
    ops = [x, mod, norm_g.reshape(1, D), pos, invf, wd, q_norm_g.reshape(1, -1),
           kv_norm_g.reshape(1, -1), wuq, wukv]
    specs = [pl.BlockSpec((tm, D), lambda i: (i, 0)),
             pl.BlockSpec((1, 6, D), lambda i: (i // nb, 0, 0)),
             full(ops[2]),
             pl.BlockSpec((tm, LANES), lambda i: (i, 0))] + [full(a) for a in ops[4:]]
    qw = MLA_HEADS * LANES
    return pl.pallas_call(
        _mla_proj_kernel,
        grid=(T // tm,),
        in_specs=specs,
        out_specs=[pl.BlockSpec((tm, qw), lambda i: (i, 0)),
                   pl.BlockSpec((tm, qw), lambda i: (i, 0)),
                   _vt_spec(tm)],
        out_shape=[jax.ShapeDtypeStruct((T, qw), BF16), jax.ShapeDtypeStruct((T, qw), BF16),
                   _vt_shape()],
        compiler_params=_cparams("parallel"),
        name="mla_proj",
    )(*ops)


def _fox_proj_kernel(x_ref, mod_ref, ng_ref, w_ref, wf_ref, bf_ref, qg_ref, kg_ref,
                     q_out, k_out, v_out, og_out, fc_out, fr_out, carry_scr):
    i = pl.program_id(0)
    tm = x_ref.shape[0]
    mod = mod_ref[0]
    h = _modnorm(x_ref[...], ng_ref[...], mod[0:1], mod[1:2]).astype(BF16)
    q = jnp.dot(h, w_ref[:, 0:D], preferred_element_type=F32)
    k = jnp.dot(h, w_ref[:, D:2 * D], preferred_element_type=F32)
    v = jnp.dot(h, w_ref[:, 2 * D:3 * D], preferred_element_type=F32)
    og = jnp.dot(h, w_ref[:, 3 * D:4 * D], preferred_element_type=F32)
    inv_n = 1.0 / FOX_HEAD
    q = q * lax.rsqrt(_group_sum64(q * q) * inv_n + NORM_EPS) * qg_ref[...] * (FOX_SCALE * LOG2E)
    k = k * lax.rsqrt(_group_sum64(k * k) * inv_n + NORM_EPS) * kg_ref[...]
    q_out[...] = q.astype(BF16)
    k_out[...] = k.astype(BF16)
    _store_transposed(v_out, v)
    og_out[...] = _sigmoid(og).astype(BF16)

    z = jnp.dot(h, wf_ref[...], preferred_element_type=F32) + bf_ref[...]
    lane = lax.broadcasted_iota(jnp.int32, z.shape, 1)
    log_f = jnp.minimum(z, 0.0) - jnp.log(1.0 + jnp.exp(-jnp.abs(z)))
    log_f = jnp.where(lane < FOX_HEADS, log_f, 0.0)
    ti = lax.broadcasted_iota(jnp.int32, (tm, tm), 0)
    si = lax.broadcasted_iota(jnp.int32, (tm, tm), 1)
    tri = jnp.where(ti >= si, 1.0, 0.0).astype(F32)
    seq_start = (i % (S // tm)) == 0
    carry = jnp.where(seq_start, 0.0, carry_scr[...])
    cum = jnp.dot(tri, log_f, precision=HIGHEST, preferred_element_type=F32) + carry
    carry_scr[...] = cum[tm - 1:tm, :]
    fc_out[...] = cum
    fr_out[0] = cum.T


def _fox_proj(x, mod, norm_g, w_in, b_f, q_norm_g, k_norm_g):
    tm = PROJ_TM
    nb = S // tm
    w_main = jnp.concatenate([w_in[:, :3 * D], w_in[:, 3 * D + FOX_HEADS:]], axis=1).astype(BF16)
    w_f = jnp.zeros((D, LANES), F32).at[:, :FOX_HEADS].set(w_in[:, 3 * D:3 * D + FOX_HEADS]).astype(BF16)
    bf = jnp.zeros((1, LANES), F32).at[0, :FOX_HEADS].set(b_f)
    qg = jnp.tile(q_norm_g, FOX_HEADS).reshape(1, D)
    kg = jnp.tile(k_norm_g, FOX_HEADS).reshape(1, D)
    full = lambda a: pl.BlockSpec(a.shape, lambda i: (0,) * a.ndim)
    tile = pl.BlockSpec((tm, D), lambda i: (i, 0))
    ops = [x, mod, norm_g.reshape(1, D), w_main, w_f, bf, qg, kg]
    specs = [tile, pl.BlockSpec((1, 6, D), lambda i: (i // nb, 0, 0))] + [full(a) for a in ops[2:]]
    return pl.pallas_call(
        _fox_proj_kernel,
        grid=(T // tm,),
        in_specs=specs,
        out_specs=[tile, tile, _vt_spec(tm), tile,
                   pl.BlockSpec((tm, LANES), lambda i: (i, 0)),
                   pl.BlockSpec((1, LANES, tm), lambda i: (i // nb, 0, i % nb))],
        out_shape=[jax.ShapeDtypeStruct((T, D), BF16)] * 2 + [_vt_shape(), jax.ShapeDtypeStruct((T, D), BF16),
                   jax.ShapeDtypeStruct((T, LANES), F32), jax.ShapeDtypeStruct((B, LANES, S), F32)],
        scratch_shapes=[pltpu.VMEM((1, LANES), F32)],
        compiler_params=_cparams("arbitrary"),
        name="fox_proj",
    )(*ops)


def _attn_kernel(fox, *refs):
    if fox:
        q_ref, k_ref, vt_ref, og_ref, fc_ref, fr_ref, o_ref, fs_scr = refs
    else:
        q_ref, k_ref, vt_ref, o_ref = refs
    tq = ATT_TQ
    nq = S // tq
    w2 = 2 * tq
    qw = q_ref.shape[1] // ATT_PAIRS
    per_head = qw // 2
    g = pl.program_id(1)
    qi = pl.program_id(2)
    pairs = range(ATT_PAIRS)

    def head(u, j):
        return 2 * (ATT_PAIRS * g + u) + j

    if fox:
        @pl.when(qi == 0)
        def _():
            fcv = fc_ref[...]
            lane_s = lax.broadcasted_iota(jnp.int32, fcv.shape, 1)
            for u in pairs:
                for j in range(2):
                    colv = jnp.sum(jnp.where(lane_s == head(u, j), fcv, 0.0), axis=-1, keepdims=True)
                    fs_scr[2 * u + j] = jnp.broadcast_to(colv * LOG2E, fcv.shape)

    lane = lax.broadcasted_iota(jnp.int32, (tq, qw), 1)
    q_bd, f_t = [], []
    for u in pairs:
        qb = q_ref[:, u * qw:(u + 1) * qw]
        zero = jnp.zeros_like(qb)
        q_bd.append(jnp.concatenate([jnp.where(lane // per_head == j, qb, zero) for j in range(2)],
                                    axis=0))
        if fox:
            f_t.append(jnp.concatenate([fr_ref[0, pl.ds(head(u, j), 1), :] for j in range(2)], axis=1)
                       * LOG2E)

    srow = lax.broadcasted_iota(jnp.int32, (tq, w2), 0)
    tcol = lax.broadcasted_iota(jnp.int32, (tq, w2), 1) % tq
    diag_mask = (srow <= tcol) if fox else ((srow // 64) <= (tcol // 64))
    sub = lax.broadcasted_iota(jnp.int32, (LANES, tq), 0)
    reps = tq // LANES

    def scores(u, lo, hi):
        st = lax.dot_general(k_ref[lo:hi, u * qw:(u + 1) * qw], q_bd[u], (((1,), (1,)), ((), ())),
                             preferred_element_type=F32)
        if fox:
            f_s = jnp.concatenate([fs_scr[2 * u, lo:hi]] * reps + [fs_scr[2 * u + 1, lo:hi]] * reps,
                                  axis=1)
            st = st + (f_t[u] - f_s)
        return st

    def probs(st, mx):
        return jnp.exp2(st - mx).astype(BF16)

    def values(u, lo, hi, pt):
        vt_ones = jnp.concatenate([vt_ref[0, u * LANES:(u + 1) * LANES, lo:hi],
                                   jnp.ones((16, hi - lo), BF16)], axis=0)
        return jnp.dot(vt_ones, pt, preferred_element_type=F32)

    def query_tile(c):
        n_head = c * tq
        st_tail = [jnp.where(diag_mask, scores(u, n_head, n_head + tq), -jnp.inf) for u in pairs]
        mx = [jnp.max(st, axis=0, keepdims=True) for st in st_tail]
        if c > 0:
            st_head = [scores(u, 0, n_head) for u in pairs]
            mx = [jnp.maximum(m, jnp.max(st, axis=0, keepdims=True)) for m, st in zip(mx, st_head)]
        acc = [values(u, n_head, n_head + tq, probs(st_tail[u], mx[u])) for u in pairs]
        if c > 0:
            acc = [acc[u] + values(u, 0, n_head, probs(st_head[u], mx[u])) for u in pairs]
        for u in pairs:
            out = acc[u][:LANES] / acc[u][LANES:LANES + 1]
            o = jnp.where(sub // 64 == 0, out[:, :tq], out[:, tq:]).T
            cols = slice(u * LANES, (u + 1) * LANES)
            if fox:
                o = o * og_ref[:, cols].astype(F32)
            o_ref[:, cols] = o.astype(BF16)

    for c in range(nq):
        pl.when(qi == c)(functools.partial(query_tile, c))


def _attention(q, k, vt, fox_extras=None):
    fox = fox_extras is not None
    tq = ATT_TQ
    nq = S // tq
    qw = ATT_PAIRS * (LANES if fox else 2 * LANES)
    vw = ATT_PAIRS * LANES
    ops = [q, k, vt]
    specs = [pl.BlockSpec((tq, qw), lambda b, g, i: (b * nq + i, g)),
             pl.BlockSpec((S, qw), lambda b, g, i: (b, g)),
             pl.BlockSpec((1, vw, S), lambda b, g, i: (b, g, 0))]
    scratch = []
    if fox:
        og, fc, fr = fox_extras
        ops += [og, fc, fr]
        specs += [pl.BlockSpec((tq, vw), lambda b, g, i: (b * nq + i, g)),
                  pl.BlockSpec((S, LANES), lambda b, g, i: (b, 0)),
                  pl.BlockSpec((1, FOX_HEADS, tq), lambda b, g, i: (b, 0, i))]
        scratch.append(pltpu.VMEM((2 * ATT_PAIRS, S, LANES), F32))
    return pl.pallas_call(
        functools.partial(_attn_kernel, fox),
        grid=(B, D // vw, nq),
        in_specs=specs,
        out_specs=pl.BlockSpec((tq, vw), lambda b, g, i: (b * nq + i, g)),
        out_shape=jax.ShapeDtypeStruct((T, D), BF16),
        scratch_shapes=scratch,
        compiler_params=_cparams("parallel", "parallel", "arbitrary"),
        name="fox_attn" if fox else "mla_attn",
    )(*ops)


assert DEPTH % 2 == 0

def kernel(x, c, positions, ada_w, ada_b, norm_mix_g, norm_ffn_g, final_norm_g, rw_mu, rw_w_rkv, rw_w_o, rw_w0, rw_w1, rw_w2, rw_a0, rw_a1, rw_a2, rw_g1, rw_g2, rw_k_k, rw_k_a, rw_r_k, rw_lnx_g, rw_lnx_b, rw_v0, rw_v1, rw_v2, mla_w_down, mla_q_norm_g, mla_kv_norm_g, mla_w_uq, mla_w_ukv, mla_w_o, fox_w_in, fox_b_f, fox_q_norm_g, fox_k_norm_g, fox_w_o, ffn_w_gate_up, ffn_w_down, moe_w_router, moe_b_router, moe_w_gate_up, moe_w_down):
    xf = x.reshape(T, D)
    mods = _ada_mods(c, ada_w, ada_b)
    ffn_wgu, ffn_wd = ffn_w_gate_up.astype(BF16), ffn_w_down.astype(BF16)
    moe_wgu, moe_wd = moe_w_gate_up.astype(BF16), moe_w_down.astype(BF16)
    v_first = None
    for i in range(DEPTH):
        mod = mods[i]
        kind, j = i % 3, i // 3
        if kind == 0:
            p = dict(mu=rw_mu[j], w_rkv=rw_w_rkv[j], w0=rw_w0[j], w1=rw_w1[j], w2=rw_w2[j],
                     a0=rw_a0[j], a1=rw_a1[j], a2=rw_a2[j], g1=rw_g1[j], g2=rw_g2[j],
                     k_k=rw_k_k[j], k_a=rw_k_a[j])
            if j > 0:
                p.update(v0=rw_v0[j - 1], v1=rw_v1[j - 1], v2=rw_v2[j - 1])
            r, lw, k, v, kk, a, g = _rwkv_proj(xf, mod, norm_mix_g[i], p, v_first if j > 0 else None)
            if j == 0:
                v_first = v
            y = _rwkv_scan(r, lw, k, v, kk, a, g, rw_r_k[j], rw_lnx_g[j], rw_lnx_b[j])
            xf = _proj_res(y, rw_w_o[j], xf, mod, 2)
        elif kind == 1:
            q, k, v = _mla_proj(xf, mod, norm_mix_g[i], positions, mla_w_down[j], mla_q_norm_g[j],
                                mla_kv_norm_g[j], mla_w_uq[j], mla_w_ukv[j])
            o = _attention(q, k, v)
            xf = _proj_res(o, mla_w_o[j], xf, mod, 2)
        else:
            q, k, v, og, fc, fr = _fox_proj(xf, mod, norm_mix_g[i], fox_w_in[j], fox_b_f[j],
                                            fox_q_norm_g[j], fox_k_norm_g[j])
            o = _attention(q, k, v, (og, fc, fr))
            xf = _proj_res(o, fox_w_o[j], xf, mod, 2)
        if i % 2 == 0:
            xf = _ffn_dense(xf, mod, norm_ffn_g[i], ffn_wgu, ffn_wd, i // 2)
        else:
            xf = _moe(xf, mod, norm_ffn_g[i], moe_w_router[i // 2], moe_b_router[i // 2],
                      moe_wgu, moe_wd, i // 2, final_norm_g if i == DEPTH - 1 else None)
    return xf.reshape(B, S, D)
```

```python
import functools
import math

import jax
import jax.numpy as jnp
from jax import lax
from jax.experimental import pallas as pl
from jax.experimental.pallas import tpu as pltpu

F32 = jnp.float32
BF16 = jnp.bfloat16
HIGHEST = lax.Precision.HIGHEST

D = 1024
B = 8
S = 2048
T = B * S
DEPTH = 4
NORM_EPS = 1e-6

RW_HEAD = 64
GN_EPS = 64e-5
EXP_NEG_HALF = math.exp(-0.5)
LOG2E = math.log2(math.e)

MLA_HEADS = 16
MLA_NOPE = 64
MLA_ROPE = 32
MLA_Q_LORA = 768
MLA_KV_LORA = 256
MLA_SCALE = (MLA_NOPE + MLA_ROPE) ** -0.5
ROPE_BASE = 10000.0

FOX_HEADS = 16
FOX_HEAD = 64
FOX_SCALE = FOX_HEAD ** -0.5

D_FF = 2816
N_EXPERTS = 8

LANES = 128
GROUP_SUM_WIDTH = 256
VMEM_LIMIT = 56 * 1024 * 1024

RW_CHUNK = 64
RW_GROUP = 4
RW_PAR = 4
ATT_TQ = 256
ATT_PAIRS = 4
FFN_TM = 1024
FFN_SUB = 512
ROUTE_TM = 512
FFN_TF = 1408
PROJ_TM = 256
MOE_TW = 1024
MOE_STEP = 64
MOE_MAX = 512


def _cparams(*sem):
    return pltpu.CompilerParams(dimension_semantics=sem, vmem_limit_bytes=VMEM_LIMIT)


def _dot(a, b):
    return jnp.dot(a.astype(BF16), b.astype(BF16), preferred_element_type=F32)


def _dot_nt(a, b):
    return lax.dot_general(a.astype(BF16), b.astype(BF16), (((1,), (1,)), ((), ())),
                           preferred_element_type=F32)


def _dot_tn(a, b):
    return lax.dot_general(a.astype(BF16), b.astype(BF16), (((0,), (0,)), ((), ())),
                           preferred_element_type=F32)


def _sigmoid(z):
    return 0.5 * jnp.tanh(0.5 * z) + 0.5


def _modnorm(x, g, shift, scale):
    ms = jnp.mean(x * x, axis=-1, keepdims=True)
    y = x * lax.rsqrt(ms + NORM_EPS) * g
    return y * (1.0 + scale) + shift


def _group_sum64(x):
    width = GROUP_SUM_WIDTH
    gi = lax.broadcasted_iota(jnp.int32, (width, width), 0) // 64
    gj = lax.broadcasted_iota(jnp.int32, (width, width), 1) // 64
    ones = jnp.where(gi == gj, 1.0, 0.0).astype(BF16)
    xb = x.astype(BF16)
    outs = [jnp.dot(xb[:, s * width:(s + 1) * width], ones, preferred_element_type=F32)
            for s in range(x.shape[1] // width)]
    return outs[0] if len(outs) == 1 else jnp.concatenate(outs, axis=1)


def _store_transposed(vt_ref, v):
    vt_ref[0] = v.T.astype(BF16)


def _vt_spec(tm):
    nb = S // tm
    return pl.BlockSpec((1, D, tm), lambda i: (i // nb, 0, i % nb))


def _vt_shape():
    return jax.ShapeDtypeStruct((B, D, S), BF16)


def _ada_kernel(c_ref, w_ref, b_ref, o_ref):
    c = c_ref[...]
    cond = c * _sigmoid(c)
    o_ref[0] = _dot(cond, w_ref[0]) + b_ref[0]


def _ada_mods(c, ada_w, ada_b):
    tn = 1536
    out = pl.pallas_call(
        _ada_kernel,
        grid=(DEPTH, 6 * D // tn),
        in_specs=[pl.BlockSpec((B, D), lambda l, j: (0, 0)),
                  pl.BlockSpec((1, D, tn), lambda l, j: (l, 0, j)),
                  pl.BlockSpec((1, 1, tn), lambda l, j: (l, 0, j))],
        out_specs=pl.BlockSpec((1, B, tn), lambda l, j: (l, 0, j)),
        out_shape=jax.ShapeDtypeStruct((DEPTH, B, 6 * D), F32),
        compiler_params=_cparams("parallel", "parallel"),
        name="ada_mods",
    )(c, ada_w, ada_b.reshape(DEPTH, 1, 6 * D))
    return out.reshape(DEPTH, B, 6, D)


def _rwkv_proj_kernel(has_vres, *refs):
    if has_vres:
        (x_ref, xp_ref, mod_ref, ng_ref, mu_ref, wr_ref, wk_ref, wv_ref, w0_ref, w1_ref, w2_ref,
         a0_ref, a1_ref, a2_ref, g1_ref, g2_ref, kk_ref, ka_ref, v0_ref, v1_ref, v2_ref, vf_ref,
         r_out, lw_out, k_out, v_out, kk_out, a_out, g_out) = refs
    else:
        (x_ref, xp_ref, mod_ref, ng_ref, mu_ref, wr_ref, wk_ref, wv_ref, w0_ref, w1_ref, w2_ref,
         a0_ref, a1_ref, a2_ref, g1_ref, g2_ref, kk_ref, ka_ref,
         r_out, lw_out, k_out, v_out, kk_out, a_out, g_out) = refs
    i = pl.program_id(0)
    mod = mod_ref[0]
    shift, scale = mod[0:1], mod[1:2]
    g = ng_ref[...]
    h = _modnorm(x_ref[...], g, shift, scale)
    hp = _modnorm(xp_ref[...], g, shift, scale)
    seq_start = (i % (S // PROJ_TM)) == 0
    prev_row = jnp.where(seq_start, 0.0, hp[7:8, :])
    row = lax.broadcasted_iota(jnp.int32, h.shape, 0)
    prev = jnp.where(row == 0, prev_row, pltpu.roll(h, 1, axis=0))
    delta = prev - h
    mu = mu_ref[...]
    xr = h + delta * mu[0:1]
    xw = h + delta * mu[1:2]
    xk = h + delta * mu[2:3]
    xv = h + delta * mu[3:4]
    xa = h + delta * mu[4:5]
    xg = h + delta * mu[5:6]
    r = _dot(xr, wr_ref[...])
    k = _dot(xk, wk_ref[...])
    v = _dot(xv, wv_ref[...])
    w_raw = w0_ref[...] + _dot(jnp.tanh(_dot(xw, w1_ref[...])), w2_ref[...])
    lw_out[...] = -_sigmoid(w_raw) * EXP_NEG_HALF
    if has_vres:
        mix = _sigmoid(v0_ref[...] + _dot(_dot(xv, v1_ref[...]), v2_ref[...]))
        v = v + (vf_ref[...] - v) * mix
    a = _sigmoid(a0_ref[...] + _dot(_dot(xa, a1_ref[...]), a2_ref[...]))
    g_out[...] = _dot(_sigmoid(_dot(xg, g1_ref[...])), g2_ref[...])
    kk = k * kk_ref[...]
    kk_out[...] = kk * lax.rsqrt(jnp.maximum(_group_sum64(kk * kk), 1e-24))
    k_out[...] = k * (1.0 + (a - 1.0) * ka_ref[...])
    r_out[...] = r
    v_out[...] = v
    a_out[...] = a


def _rwkv_proj(x, mod, norm_g, p, v_first):
    has_vres = v_first is not None
    tm = PROJ_TM
    nb = S // tm
    row = lambda a: a.reshape(1, -1)
    full = lambda a: pl.BlockSpec(a.shape, lambda i: (0,) * a.ndim)
    tile = pl.BlockSpec((tm, D), lambda i: (i, 0))
    ops = [x, x, mod, row(norm_g), p["mu"],
           p["w_rkv"][0].astype(BF16), p["w_rkv"][1].astype(BF16), p["w_rkv"][2].astype(BF16),
           row(p["w0"]), p["w1"].astype(BF16), p["w2"].astype(BF16),
           row(p["a0"]), p["a1"].astype(BF16), p["a2"].astype(BF16),
           p["g1"].astype(BF16), p["g2"].astype(BF16), row(p["k_k"]), row(p["k_a"])]
    specs = [tile,
             pl.BlockSpec((8, D), lambda i: (jnp.maximum(i * (tm // 8) - 1, 0), 0)),
             pl.BlockSpec((1, 6, D), lambda i: (i // nb, 0, 0))]
    specs += [full(a) for a in ops[3:]]
    if has_vres:
        extra = [row(p["v0"]), p["v1"].astype(BF16), p["v2"].astype(BF16)]
        ops += extra + [v_first]
        specs += [full(a) for a in extra] + [tile]
    outs = pl.pallas_call(
        functools.partial(_rwkv_proj_kernel, has_vres),
        grid=(T // tm,),
        in_specs=specs,
        out_specs=[tile] * 7,
        out_shape=[jax.ShapeDtypeStruct((T, D), F32)] * 7,
        compiler_params=_cparams("parallel"),
        name="rwkv_proj",
    )(*ops)
    return outs


def _rwkv_scan_kernel(r_ref, lw_ref, k_ref, v_ref, kk_ref, a_ref, g_ref, rk_ref, lg_ref, lb_ref,
                      o_ref, s_ref):
    c = pl.program_id(2)
    W = RW_GROUP * RW_HEAD

    @pl.when(c == 0)
    def _():
        s_ref[...] = jnp.zeros_like(s_ref)

    slabs = [slice(p * W, (p + 1) * W) for p in range(RW_PAR)]
    cols = lambda ref: [ref[:, sl] for sl in slabs]
    out, s_new = _rwkv_blocks(cols(r_ref), cols(lw_ref), cols(k_ref), cols(v_ref), cols(kk_ref),
                              cols(a_ref), cols(g_ref), cols(rk_ref), cols(lg_ref), cols(lb_ref),
                              [s_ref[p] for p in range(RW_PAR)])
    for p, sl in enumerate(slabs):
        o_ref[:, sl] = out[p]
        s_ref[p] = s_new[p]


def _cumsum_rows(x):
    row = lax.broadcasted_iota(jnp.int32, x.shape, 0)
    step = 1
    while step < x.shape[0]:
        x = x + jnp.where(row >= step, pltpu.roll(x, step, axis=0), 0.0)
        step *= 2
    return x


def _each(fn, *cols):
    return [fn(*args) for args in zip(*cols)]


def _rwkv_blocks(r, lw, k, v, kk, a, g, r_k, lnx_g, lnx_b, s_old):
    C = RW_CHUNK
    W = RW_GROUP * RW_HEAD

    mul = lambda x, y: x * y
    cl = _each(_cumsum_rows, lw)
    cl_end = _each(lambda x: x[C - 1:C, :], cl)
    w_t = _each(jnp.exp, cl)
    w_prev = _each(lambda x, y: jnp.exp(x - y), cl, lw)
    w_inv = _each(lambda x: jnp.exp(-x), cl)
    w_rem = _each(lambda x, y: jnp.exp(x - y), cl_end, cl)
    bv = _each(mul, kk, a)
    r_hat = _each(mul, r, w_t)
    a_hat = _each(lambda x, y: -x * y, kk, w_prev)
    b_hat = _each(mul, bv, w_inv)
    k_hat = _each(mul, k, w_inv)
    b_til = _each(mul, bv, w_rem)
    k_til = _each(mul, k, w_rem)

    lane_head = lax.broadcasted_iota(jnp.int32, (C, W), 1) // RW_HEAD

    def stack(m):
        mb = m.astype(BF16)
        zero = jnp.zeros_like(mb)
        return jnp.concatenate([jnp.where(lane_head == hd, mb, zero) for hd in range(RW_GROUP)],
                               axis=0)

    cat0 = lambda x, y: jnp.concatenate([x, y], axis=0)
    gram = _each(lambda ah, rh, bh, kh: _dot_nt(cat0(stack(ah), stack(rh)), cat0(stack(bh), stack(kh))),
                 a_hat, r_hat, b_hat, k_hat)
    n = RW_GROUP * C
    ri = lax.broadcasted_iota(jnp.int32, (n, n), 0)
    ci = lax.broadcasted_iota(jnp.int32, (n, n), 1)
    strict = (ri % C) > (ci % C)
    incl = (ri % C) >= (ci % C)
    same_head = (ri // RW_HEAD) == (ci // RW_HEAD)
    eye = jnp.where(ri == ci, 1.0, 0.0)
    l_ab = _each(lambda x: jnp.where(strict, x[:n, :n], 0.0), gram)
    l_ak = _each(lambda x: jnp.where(strict, x[:n, n:], 0.0), gram)
    m_rb = _each(lambda x: jnp.where(incl, x[n:, :n], 0.0), gram)
    m_rk = _each(lambda x: jnp.where(incl, x[n:, n:], 0.0), gram)

    lane2 = lax.broadcasted_iota(jnp.int32, (C, 2 * RW_HEAD), 1)

    def packed(m):
        cols = [jnp.where(lane2 < RW_HEAD,
                          m[(2 * j) * C:(2 * j + 1) * C, j * 2 * RW_HEAD:(j + 1) * 2 * RW_HEAD],
                          m[(2 * j + 1) * C:(2 * j + 2) * C, j * 2 * RW_HEAD:(j + 1) * 2 * RW_HEAD])
                for j in range(RW_GROUP // 2)]
        return jnp.concatenate(cols, axis=1)

    def rows(m):
        blocks = []
        for hd in range(RW_GROUP):
            j = hd // 2
            blk = m[hd * C:(hd + 1) * C, j * 2 * RW_HEAD:(j + 1) * 2 * RW_HEAD]
            blocks.append(jnp.where((lane2 // RW_HEAD) == hd % 2, blk, 0.0))
        return jnp.concatenate(blocks, axis=0)

    def diag_products(x_rows, y_packed):
        yb = y_packed.astype(BF16)
        return _dot(x_rows, jnp.concatenate([yb, yb], axis=0))

    rounds = int(math.log2(C)) - 1
    inv_p = _each(lambda x: packed(eye + x), l_ab)
    inv_r = _each(lambda x: rows(eye + x), l_ab)
    pw_p = _each(packed, l_ab)
    pw_r = _each(rows, l_ab)
    for i in range(rounds):
        sq = _each(diag_products, pw_r, pw_p)
        pw_p = _each(packed, sq)
        upd = _each(diag_products, inv_r, pw_p)
        inv_p = _each(lambda x, y: x + packed(y), inv_p, upd)
        if i < rounds - 1:
            pw_r = _each(rows, sq)
            inv_r = _each(lambda x, y: x + rows(y), inv_r, upd)

    a_p = _each(lambda x, y: _dot(x, stack(y)), inv_p, a_hat)
    t_l = _each(_dot, inv_p, l_ak)
    v_p = _each(lambda x, y: _dot(x, stack(y)), t_l, v)
    m_rb_p = _each(packed, m_rb)
    m_rk_p = _each(packed, m_rk)
    r_p = _each(lambda x, y, z: x + _dot(y, stack(z)), r_hat, m_rb_p, a_p)
    y0 = _each(lambda mb, mk, vp, vv: _dot(jnp.concatenate([mb, mk], axis=1),
                                           cat0(stack(vp), stack(vv))), m_rb_p, m_rk_p, v_p, v)
    a_til = _each(lambda x, y: jnp.where(same_head, _dot_tn(x, y), 0.0), b_til, a_p)
    d_new = _each(lambda vp, vv, bt, kt: jnp.where(same_head, _dot_tn(cat0(vp, vv), cat0(bt, kt)), 0.0),
                  v_p, v, b_til, k_til)
    y = _each(lambda x, s, z: _dot_nt(x, s) + z, r_p, s_old, y0)
    s_new = _each(lambda s, ce, at, dn: s * jnp.exp(ce) + _dot_nt(s, at) + dn,
                  s_old, cl_end, a_til, d_new)

    inv_n = 1.0 / RW_HEAD
    mean = _each(lambda x: _group_sum64(x) * inv_n, y)
    yc = _each(lambda x, m: x - m, y, mean)
    var = _each(lambda x: _group_sum64(x * x) * inv_n, yc)
    yn = _each(lambda x, vr, lg, lb: x * lax.rsqrt(vr + GN_EPS) * lg + lb, yc, var, lnx_g, lnx_b)
    bonus = _each(lambda rr, kx, rk, vv: _group_sum64(rr * kx * rk) * vv, r, k, r_k, v)
    out = _each(lambda x, bo, gg: ((x + bo) * gg).astype(BF16), yn, bonus, g)
    return out, s_new


def _rwkv_scan(r, lw, k, v, kk, a, g, r_k, lnx_g, lnx_b):
    C = RW_CHUNK
    W = RW_PAR * RW_GROUP * RW_HEAD
    nc = S // C
    slab = pl.BlockSpec((C, W), lambda b, gi, c: (b * nc + c, gi))
    prow = pl.BlockSpec((1, W), lambda b, gi, c: (0, gi))
    return pl.pallas_call(
        _rwkv_scan_kernel,
        grid=(B, D // W, nc),
        in_specs=[slab] * 7 + [prow] * 3,
        out_specs=slab,
        out_shape=jax.ShapeDtypeStruct((T, D), BF16),
        scratch_shapes=[pltpu.VMEM((RW_PAR, RW_GROUP * RW_HEAD, RW_GROUP * RW_HEAD), F32)],
        compiler_params=_cparams("parallel", "parallel", "arbitrary"),
        name="rwkv_scan",
    )(r, lw, k, v, kk, a, g, r_k.reshape(1, D), lnx_g.reshape(1, D), lnx_b.reshape(1, D))


def _proj_res_kernel(gate_row, a_ref, w_ref, x_ref, mod_ref, o_ref):
    y = jnp.dot(a_ref[...], w_ref[...], preferred_element_type=F32)
    o_ref[...] = x_ref[...] + mod_ref[0][gate_row:gate_row + 1] * y


def _proj_res(a, w, x, mod, gate_row):
    tm = 512
    nb = S // tm
    kdim = a.shape[1]
    return pl.pallas_call(
        functools.partial(_proj_res_kernel, gate_row),
        grid=(T // tm,),
        in_specs=[pl.BlockSpec((tm, kdim), lambda i: (i, 0)),
                  pl.BlockSpec((kdim, D), lambda i: (0, 0)),
                  pl.BlockSpec((tm, D), lambda i: (i, 0)),
                  pl.BlockSpec((1, 6, D), lambda i: (i // nb, 0, 0))],
        out_specs=pl.BlockSpec((tm, D), lambda i: (i, 0)),
        out_shape=jax.ShapeDtypeStruct((T, D), F32),
        compiler_params=_cparams("parallel"),
        name="proj_res",
    )(a, w.astype(BF16), x, mod)


def _ffn_kernel(nf, x_ref, mod_ref, ng_ref, wg_ref, wu_ref, wd_ref, o_ref, h_scr, acc_scr):
    f = pl.program_id(1)

    @pl.when(f == 0)
    def _():
        mod = mod_ref[0]
        h_scr[...] = _modnorm(x_ref[...], ng_ref[...], mod[3:4], mod[4:5]).astype(BF16)
        acc_scr[...] = jnp.zeros_like(acc_scr)

    for sub in range(h_scr.shape[0] // FFN_SUB):
        rows = slice(sub * FFN_SUB, (sub + 1) * FFN_SUB)
        h = h_scr[rows]
        gt = jnp.dot(h, wg_ref[0], preferred_element_type=F32)
        up = jnp.dot(h, wu_ref[0], preferred_element_type=F32)
        act = (gt * _sigmoid(gt) * up).astype(BF16)
        acc_scr[rows] += jnp.dot(act, wd_ref[0], preferred_element_type=F32)

    @pl.when(f == nf - 1)
    def _():
        o_ref[...] = x_ref[...] + mod_ref[0][5:6] * acc_scr[...]


def _ffn_dense(x, mod, norm_g, wgu, wd, layer):
    tm, tf = FFN_TM, FFN_TF
    nf = D_FF // tf
    nb = S // tm
    return pl.pallas_call(
        functools.partial(_ffn_kernel, nf),
        grid=(T // tm, nf),
        in_specs=[pl.BlockSpec((tm, D), lambda i, f: (i, 0)),
                  pl.BlockSpec((1, 6, D), lambda i, f: (i // nb, 0, 0)),
                  pl.BlockSpec((1, D), lambda i, f: (0, 0)),
                  pl.BlockSpec((1, D, tf), lambda i, f: (layer, 0, f)),
                  pl.BlockSpec((1, D, tf), lambda i, f: (layer, 0, nf + f)),
                  pl.BlockSpec((1, tf, D), lambda i, f: (layer, f, 0))],
        out_specs=pl.BlockSpec((tm, D), lambda i, f: (i, 0)),
        out_shape=jax.ShapeDtypeStruct((T, D), F32),
        scratch_shapes=[pltpu.VMEM((tm, D), BF16), pltpu.VMEM((tm, D), F32)],
        compiler_params=_cparams("parallel", "arbitrary"),
        name="ffn_dense",
    )(x, mod, norm_g.reshape(1, D), wgu, wgu, wd)


def _route_kernel(x_ref, mod_ref, ng_ref, wr_ref, br_ref,
                  h_out, cw_t_out, dest_t_out, cnt_out, carry_scr):
    i = pl.program_id(0)
    tm = x_ref.shape[0]
    mod = mod_ref[0]
    h = _modnorm(x_ref[...], ng_ref[...], mod[3:4], mod[4:5])
    h_out[...] = h.astype(BF16)
    logits = jnp.dot(h, wr_ref[...], precision=HIGHEST, preferred_element_type=F32) + br_ref[...]
    lane = lax.broadcasted_iota(jnp.int32, logits.shape, 1)
    neg = -jnp.inf
    logits = jnp.where(lane < N_EXPERTS, logits, neg)
    m1 = jnp.max(logits, axis=-1, keepdims=True)
    i1 = jnp.min(jnp.where(logits == m1, lane, LANES), axis=-1, keepdims=True)
    rest = jnp.where(lane == i1, neg, logits)
    m2 = jnp.max(rest, axis=-1, keepdims=True)
    i2 = jnp.min(jnp.where(rest == m2, lane, LANES), axis=-1, keepdims=True)
    e2 = jnp.exp(m2 - m1)
    w1 = 1.0 / (1.0 + e2)
    w2 = e2 / (1.0 + e2)
    cw_t_out[...] = (jnp.where(lane == i1, w1, 0.0) + jnp.where(lane == i2, w2, 0.0)).T

    sel =jnp.where((lane == i1) | (lane == i2), 1.0, 0.0)
    ti = lax.broadcasted_iota(jnp.int32, (tm, tm), 0)
    si = lax.broadcasted_iota(jnp.int32, (tm, tm), 1)
    before = jnp.where(ti > si, 1.0, 0.0).astype(BF16)
    window_start = (i % (MOE_TW // tm)) == 0
    carry = jnp.where(window_start, 0.0, carry_scr[...])
    rank = jnp.dot(before, sel.astype(BF16), preferred_element_type=F32) + carry
    dest_t_out[...] = jnp.where(sel > 0.0, rank, -1.0).T
    total = carry + jnp.sum(sel, axis=0, keepdims=True)
    carry_scr[...] = total
    cnt_out[0] = jnp.broadcast_to(total, (8, LANES))


def _moe_kernel(nf, has_final, cnt_ref, x_ref, mod_ref, h_ref, dest_t_ref, cw_t_ref, wg_ref, wu_ref,
                wd_ref, *rest):
    fin_ref = rest[0] if has_final else None
    o_ref, xg_scr, yc_scr = rest[-3:]
    _moe_body(nf, cnt_ref, x_ref, mod_ref, h_ref, dest_t_ref, cw_t_ref, wg_ref, wu_ref, wd_ref,
              fin_ref, o_ref, xg_scr, yc_scr)


def _moe_body(nf, cnt_ref, x_ref, mod_ref, h_ref, dest_t_ref, cw_t_ref, wg_ref, wu_ref, wd_ref,
              fin_ref, o_ref, xg_scr, yc_scr):
    w = pl.program_id(0)
    e = pl.program_id(1)
    f = pl.program_id(2)
    tw = MOE_TW
    n = cnt_ref[w * N_EXPERTS + e]

    @pl.when((e == 0) & (f == 0))
    def _():
        o_ref[...] = jnp.zeros_like(o_ref)

    dest_row = dest_t_ref[pl.ds(e, 1), :]
    cw_row = cw_t_ref[pl.ds(e, 1), :]

    def slot_tile(base, rows):
        slot = lax.broadcasted_iota(jnp.int32, (rows, tw), 0).astype(F32) + base
        match = dest_row == slot
        select = jnp.where(match, 1.0, 0.0).astype(BF16)
        span = pl.ds(base, rows)

        @pl.when(f == 0)
        def _():
            xg_scr[span, :] = jnp.dot(select, h_ref[...], preferred_element_type=F32).astype(BF16)

        xs = xg_scr[span, :]
        gt = jnp.dot(xs, wg_ref[0, 0], preferred_element_type=F32)
        up = jnp.dot(xs, wu_ref[0, 0], preferred_element_type=F32)
        act = (gt * _sigmoid(gt) * up).astype(BF16)
        y = jnp.dot(act, wd_ref[0, 0], preferred_element_type=F32)

        @pl.when(f == 0)
        def _():
            yc_scr[span, :] = y

        @pl.when((f > 0) & (f < nf - 1))
        def _():
            yc_scr[span, :] += y

        @pl.when(f == nf - 1)
        def _():
            w_slot = jnp.sum(jnp.where(match, cw_row, 0.0), axis=-1, keepdims=True)
            weighted = ((yc_scr[span, :] + y) * w_slot).astype(BF16)
            o_ref[...] += _dot_tn(select, weighted)

    sizes = list(range(MOE_STEP, MOE_MAX + 1, MOE_STEP))
    for lo, rows in zip([0] + sizes[:-1], sizes):
        pl.when((n > lo) & (n <= rows))(functools.partial(slot_tile, 0, rows))

    @pl.when(n > MOE_MAX)
    def _():
        def body(s, carry):
            slot_tile(pl.multiple_of(s * MOE_MAX, MOE_MAX), MOE_MAX)
            return carry
        lax.fori_loop(0, (n + MOE_MAX - 1) // MOE_MAX, body, 0)

    @pl.when((e == N_EXPERTS - 1) & (f == nf - 1))
    def _():
        y = x_ref[...] + mod_ref[0][5:6] * o_ref[...]
        if fin_ref is not None:
            y = y * lax.rsqrt(jnp.mean(y * y, axis=-1, keepdims=True) + NORM_EPS) * fin_ref[...]
        o_ref[...] = y


def _moe(x, mod, norm_g, w_router, b_router, wgu, wd, layer, final_g=None):
    tm, tf, tw = ROUTE_TM, FFN_TF, MOE_TW
    nf = D_FF // tf
    nb = S // tm
    nw = T // tw
    wr =jnp.zeros((D, LANES), F32).at[:, :N_EXPERTS].set(w_router)
    br = jnp.zeros((1, LANES), F32).at[0, :N_EXPERTS].set(b_router)
    tile = lambda width: pl.BlockSpec((tm, width), lambda i: (i, 0))
    assert nf >= 2
    lanes_t = pl.BlockSpec((LANES, tm), lambda i: (0, i))
    h, cw_t, dest_t, cnt = pl.pallas_call(
        _route_kernel,
        grid=(T // tm,),
        in_specs=[tile(D),
                  pl.BlockSpec((1, 6, D), lambda i: (i // nb, 0, 0)),
                  pl.BlockSpec((1, D), lambda i: (0, 0)),
                  pl.BlockSpec((D, LANES), lambda i: (0, 0)),
                  pl.BlockSpec((1, LANES), lambda i: (0, 0))],
        out_specs=[tile(D), lanes_t, lanes_t,
                   pl.BlockSpec((1, 8, LANES), lambda i: (i // (tw // tm), 0, 0))],
        out_shape=[jax.ShapeDtypeStruct((T, D), BF16), jax.ShapeDtypeStruct((LANES, T), F32),
                   jax.ShapeDtypeStruct((LANES, T), F32), jax.ShapeDtypeStruct((nw, 8, LANES), F32)],
        scratch_shapes=[pltpu.VMEM((1, LANES), F32)],
        compiler_params=_cparams("arbitrary"),
        name="moe_route",
    )(x, mod, norm_g.reshape(1, D), wr, br)
    counts = cnt[:, 0, :N_EXPERTS].astype(jnp.int32).reshape(nw * N_EXPERTS)
    cap = -(-tw // MOE_MAX) * MOE_MAX
    has_final = final_g is not None
    extra_ops = [final_g.reshape(1, D)] if has_final else []
    extra_specs = [pl.BlockSpec((1, D), lambda w, e, f, c: (0, 0))] if has_final else []
    return pl.pallas_call(
        functools.partial(_moe_kernel, nf, has_final),
        grid_spec=pltpu.PrefetchScalarGridSpec(
            num_scalar_prefetch=1,
            grid=(nw, N_EXPERTS, nf),
            in_specs=[pl.BlockSpec((tw, D), lambda w, e, f, c: (w, 0)),
                      pl.BlockSpec((1, 6, D), lambda w, e, f, c: (w // (S // tw), 0, 0)),
                      pl.BlockSpec((tw, D), lambda w, e, f, c: (w, 0)),
                      pl.BlockSpec((LANES, tw), lambda w, e, f, c: (0, w)),
                      pl.BlockSpec((LANES, tw), lambda w, e, f, c: (0, w)),
                      pl.BlockSpec((1, 1, D, tf), lambda w, e, f, c: (layer, e, 0, f)),
                      pl.BlockSpec((1, 1, D, tf), lambda w, e, f, c: (layer, e, 0, nf + f)),
                      pl.BlockSpec((1, 1, tf, D), lambda w, e, f, c: (layer, e, f, 0))] + extra_specs,
            out_specs=pl.BlockSpec((tw, D), lambda w, e, f, c: (w, 0)),
            scratch_shapes=[pltpu.VMEM((cap, D), BF16), pltpu.VMEM((cap, D), F32)]),
        out_shape=jax.ShapeDtypeStruct((T, D), F32),
        compiler_params=_cparams("parallel", "arbitrary", "arbitrary"),
        name="moe_experts",
    )(counts, x, mod, h, dest_t, cw_t, wgu, wgu, wd, *extra_ops)


def _mla_proj_kernel(x_ref, mod_ref, ng_ref, pos_ref, invf_ref, wd_ref, qg_ref, kvg_ref,
                     wuq_ref, wukv_ref, q_out, k_out, v_out):
    mod = mod_ref[0]
    h = _modnorm(x_ref[...], ng_ref[...], mod[0:1], mod[1:2])
    down = _dot(h, wd_ref[...])
    cq = down[:, :MLA_Q_LORA]
    ckv = down[:, MLA_Q_LORA:MLA_Q_LORA + MLA_KV_LORA]
    kr = down[:, MLA_Q_LORA + MLA_KV_LORA:]
    cq = cq * lax.rsqrt(jnp.mean(cq * cq, axis=-1, keepdims=True) + NORM_EPS) * qg_ref[...]
    ckv = ckv * lax.rsqrt(jnp.mean(ckv * ckv, axis=-1, keepdims=True) + NORM_EPS) * kvg_ref[...]

    ang = pos_ref[...] * invf_ref[...]
    lane = lax.broadcasted_iota(jnp.int32, ang.shape, 1)
    is_rope = (lane >= MLA_NOPE) & (lane < MLA_NOPE + MLA_ROPE)
    cos_r = jnp.where(is_rope, jnp.cos(ang), 0.0)
    sin_r = jnp.where(is_rope, jnp.sin(ang), 0.0)
    cos_q = jnp.where(lane < MLA_NOPE, 1.0, cos_r)
    shift = LANES - MLA_ROPE

    k_rope = kr * cos_r + pltpu.roll(kr, shift, axis=1) * sin_r

    qf = _dot(cq, wuq_ref[...])
    kv = _dot(ckv, wukv_ref[...])
    for hd in range(MLA_HEADS):
        sl = slice(hd * LANES, (hd + 1) * LANES)
        qs = qf[:, sl]
        q_out[:, sl] = ((qs * cos_q + pltpu.roll(qs, shift, axis=1) * sin_r)
                        * (MLA_SCALE * LOG2E)).astype(BF16)
        k_out[:, sl] = (kv[:, sl] + k_rope).astype(BF16)
    _store_transposed(v_out, kv[:, MLA_HEADS * LANES:])


def _rot_half_cols(w):
    half = MLA_ROPE // 2
    return jnp.concatenate([-w[..., half:], w[..., :half]], axis=-1)


def _mla_proj(x, mod, norm_g, positions, w_down, q_norm_g, kv_norm_g, w_uq, w_ukv):
    tm = PROJ_TM
    nb = S // tm
    half = MLA_ROPE // 2
    wr = w_down[:, MLA_Q_LORA + MLA_KV_LORA:]
    wd = jnp.concatenate([w_down[:, :MLA_Q_LORA + MLA_KV_LORA],
                          jnp.zeros((D, MLA_NOPE), F32), wr, _rot_half_cols(wr)], axis=1).astype(BF16)
    wq = w_uq.reshape(MLA_Q_LORA, MLA_HEADS, MLA_NOPE + MLA_ROPE)
    wq_r = wq[..., MLA_NOPE:]
    wuq = jnp.concatenate([wq[..., :MLA_NOPE], wq_r, _rot_half_cols(wq_r)], axis=-1)
    wuq = wuq.reshape(MLA_Q_LORA, MLA_HEADS * LANES).astype(BF16)
    wkv = w_ukv.reshape(MLA_KV_LORA, MLA_HEADS, 2 * MLA_NOPE)
    wk = jnp.concatenate([wkv[..., :MLA_NOPE], jnp.zeros_like(wkv[..., :MLA_NOPE])], axis=-1)
    wukv = jnp.concatenate([wk.reshape(MLA_KV_LORA, MLA_HEADS * LANES),
                            wkv[..., MLA_NOPE:].reshape(MLA_KV_LORA, D)], axis=1).astype(BF16)
    inv_freq = ROPE_BASE ** (-jnp.arange(half, dtype=F32) / half)
    lane = jnp.arange(LANES)
    invf = jnp.where((lane >= MLA_NOPE) & (lane < MLA_NOPE + MLA_ROPE),
                     inv_freq[(lane - MLA_NOPE) % half], 0.0).reshape(1, LANES).astype(F32)
    pos = jnp.broadcast_to(positions.reshape(T, 1).astype(F32), (T, LANES))
    full = lambda a: pl.BlockSpec(a.shape, lambda i: (0,) * a.ndim)
    ops = [x, mod, norm_g.reshape(1, D), pos, invf, wd, q_norm_g.reshape(1, -1),
           kv_norm_g.reshape(1, -1), wuq, wukv]
    specs = [pl.BlockSpec((tm, D), lambda i: (i, 0)),
             pl.BlockSpec((1, 6, D), lambda i: (i // nb, 0, 0)),
             full(ops[2]),
             pl.BlockSpec((tm, LANES), lambda i: (i, 0))] + [full(a) for a in ops[4:]]
    qw = MLA_HEADS * LANES
    return pl.pallas_call(
        _mla_proj_kernel,
        grid=(T // tm,),
        in_specs=specs,
        out_specs=[pl.BlockSpec((tm, qw), lambda i: (i, 0)),
                   pl.BlockSpec((tm, qw), lambda i: (i, 0)),
                   _vt_spec(tm)],
        out_shape=[jax.ShapeDtypeStruct((T, qw), BF16), jax.ShapeDtypeStruct((T, qw), BF16),
                   _vt_shape()],
        compiler_params=_cparams("parallel"),
        name="mla_proj",
    )(*ops)


def _fox_proj_kernel(x_ref, mod_ref, ng_ref, w_ref, wf_ref, bf_ref, qg_ref, kg_ref,
                     q_out, k_out, v_out, og_out, fc_out, fr_out, carry_scr):
    i = pl.program_id(0)
    tm = x_ref.shape[0]
    mod = mod_ref[0]
    h = _modnorm(x_ref[...], ng_ref[...], mod[0:1], mod[1:2]).astype(BF16)
    q = jnp.dot(h, w_ref[:, 0:D], preferred_element_type=F32)
    k = jnp.dot(h, w_ref[:, D:2 * D], preferred_element_type=F32)
    v = jnp.dot(h, w_ref[:, 2 * D:3 * D], preferred_element_type=F32)
    og = jnp.dot(h, w_ref[:, 3 * D:4 * D], preferred_element_type=F32)
    inv_n = 1.0 / FOX_HEAD
    q = q * lax.rsqrt(_group_sum64(q * q) * inv_n + NORM_EPS) * qg_ref[...] * (FOX_SCALE * LOG2E)
    k = k * lax.rsqrt(_group_sum64(k * k) * inv_n + NORM_EPS) * kg_ref[...]
    q_out[...] = q.astype(BF16)
    k_out[...] = k.astype(BF16)
    _store_transposed(v_out, v)
    og_out[...] = _sigmoid(og).astype(BF16)

    z = jnp.dot(h, wf_ref[...], preferred_element_type=F32) + bf_ref[...]
    lane = lax.broadcasted_iota(jnp.int32, z.shape, 1)
    log_f = jnp.minimum(z, 0.0) - jnp.log(1.0 + jnp.exp(-jnp.abs(z)))
    log_f = jnp.where(lane < FOX_HEADS, log_f, 0.0)
    ti = lax.broadcasted_iota(jnp.int32, (tm, tm), 0)
    si = lax.broadcasted_iota(jnp.int32, (tm, tm), 1)
    tri = jnp.where(ti >= si, 1.0, 0.0).astype(F32)
    seq_start = (i % (S // tm)) == 0
    carry = jnp.where(seq_start, 0.0, carry_scr[...])
    cum = jnp.dot(tri, log_f, precision=HIGHEST, preferred_element_type=F32) + carry
    carry_scr[...] = cum[tm - 1:tm, :]
    fc_out[...] = cum
    fr_out[0] = cum.T


def _fox_proj(x, mod, norm_g, w_in, b_f, q_norm_g, k_norm_g):
    tm = PROJ_TM
    nb = S // tm
    w_main = jnp.concatenate([w_in[:, :3 * D], w_in[:, 3 * D + FOX_HEADS:]], axis=1).astype(BF16)
    w_f = jnp.zeros((D, LANES), F32).at[:, :FOX_HEADS].set(w_in[:, 3 * D:3 * D + FOX_HEADS]).astype(BF16)
    bf = jnp.zeros((1, LANES), F32).at[0, :FOX_HEADS].set(b_f)
    qg = jnp.tile(q_norm_g, FOX_HEADS).reshape(1, D)
    kg = jnp.tile(k_norm_g, FOX_HEADS).reshape(1, D)
    full = lambda a: pl.BlockSpec(a.shape, lambda i: (0,) * a.ndim)
    tile = pl.BlockSpec((tm, D), lambda i: (i, 0))
    ops = [x, mod, norm_g.reshape(1, D), w_main, w_f, bf, qg, kg]
    specs = [tile, pl.BlockSpec((1, 6, D), lambda i: (i // nb, 0, 0))] + [full(a) for a in ops[2:]]
    return pl.pallas_call(
        _fox_proj_kernel,
        grid=(T // tm,),
        in_specs=specs,
        out_specs=[tile, tile, _vt_spec(tm), tile,
                   pl.BlockSpec((tm, LANES), lambda i: (i, 0)),
                   pl.BlockSpec((1, LANES, tm), lambda i: (i // nb, 0, i % nb))],
        out_shape=[jax.ShapeDtypeStruct((T, D), BF16)] * 2 + [_vt_shape(), jax.ShapeDtypeStruct((T, D), BF16),
                   jax.ShapeDtypeStruct((T, LANES), F32), jax.ShapeDtypeStruct((B, LANES, S), F32)],
        scratch_shapes=[pltpu.VMEM((1, LANES), F32)],
        compiler_params=_cparams("arbitrary"),
        name="fox_proj",
    )(*ops)


def _attn_kernel(fox, *refs):
    if fox:
        q_ref, k_ref, vt_ref, og_ref, fc_ref, fr_ref, o_ref, fs_scr = refs
    else:
        q_ref, k_ref, vt_ref, o_ref = refs
    tq = ATT_TQ
    nq = S // tq
    w2 = 2 * tq
    qw = q_ref.shape[1] // ATT_PAIRS
    per_head = qw // 2
    g = pl.program_id(1)
    qi = pl.program_id(2)
    pairs = range(ATT_PAIRS)

    def head(u, j):
        return 2 * (ATT_PAIRS * g + u) + j

    if fox:
        @pl.when(qi == 0)
        def _():
            fcv = fc_ref[...]
            lane_s = lax.broadcasted_iota(jnp.int32, fcv.shape, 1)
            for u in pairs:
                for j in range(2):
                    colv = jnp.sum(jnp.where(lane_s == head(u, j), fcv, 0.0), axis=-1, keepdims=True)
                    fs_scr[2 * u + j] = jnp.broadcast_to(colv * LOG2E, fcv.shape)

    lane = lax.broadcasted_iota(jnp.int32, (tq, qw), 1)
    q_bd, f_t = [], []
    for u in pairs:
        qb = q_ref[:, u * qw:(u + 1) * qw]
        zero = jnp.zeros_like(qb)
        q_bd.append(jnp.concatenate([jnp.where(lane // per_head == j, qb, zero) for j in range(2)],
                                    axis=0))
        if fox:
            f_t.append(jnp.concatenate([fr_ref[0, pl.ds(head(u, j), 1), :] for j in range(2)], axis=1)
                       * LOG2E)

    srow = lax.broadcasted_iota(jnp.int32, (tq, w2), 0)
    tcol = lax.broadcasted_iota(jnp.int32, (tq, w2), 1) % tq
    diag_mask = (srow <= tcol) if fox else ((srow // 64) <= (tcol // 64))
    sub = lax.broadcasted_iota(jnp.int32, (LANES, tq), 0)
    reps = tq // LANES

    def scores(u, lo, hi):
        st = lax.dot_general(k_ref[lo:hi, u * qw:(u + 1) * qw], q_bd[u], (((1,), (1,)), ((), ())),
                             preferred_element_type=F32)
        if fox:
            f_s = jnp.concatenate([fs_scr[2 * u, lo:hi]] * reps + [fs_scr[2 * u + 1, lo:hi]] * reps,
                                  axis=1)
            st = st + (f_t[u] - f_s)
        return st

    def probs(st, mx):
        return jnp.exp2(st - mx).astype(BF16)

    def values(u, lo, hi, pt):
        vt_ones = jnp.concatenate([vt_ref[0, u * LANES:(u + 1) * LANES, lo:hi],
                                   jnp.ones((16, hi - lo), BF16)], axis=0)
        return jnp.dot(vt_ones, pt, preferred_element_type=F32)

    def query_tile(c):
        n_head = c * tq
        st_tail = [jnp.where(diag_mask, scores(u, n_head, n_head + tq), -jnp.inf) for u in pairs]
        mx = [jnp.max(st, axis=0, keepdims=True) for st in st_tail]
        if c > 0:
            st_head = [scores(u, 0, n_head) for u in pairs]
            mx = [jnp.maximum(m, jnp.max(st, axis=0, keepdims=True)) for m, st in zip(mx, st_head)]
        acc = [values(u, n_head, n_head + tq, probs(st_tail[u], mx[u])) for u in pairs]
        if c > 0:
            acc = [acc[u] + values(u, 0, n_head, probs(st_head[u], mx[u])) for u in pairs]
        for u in pairs:
            out = acc[u][:LANES] / acc[u][LANES:LANES + 1]
            o = jnp.where(sub // 64 == 0, out[:, :tq], out[:, tq:]).T
            cols = slice(u * LANES, (u + 1) * LANES)
            if fox:
                o = o * og_ref[:, cols].astype(F32)
            o_ref[:, cols] = o.astype(BF16)

    for c in range(nq):
        pl.when(qi == c)(functools.partial(query_tile, c))


def _attention(q, k, vt, fox_extras=None):
    fox = fox_extras is not None
    tq = ATT_TQ
    nq = S // tq
    qw = ATT_PAIRS * (LANES if fox else 2 * LANES)
    vw = ATT_PAIRS * LANES
    ops = [q, k, vt]
    specs = [pl.BlockSpec((tq, qw), lambda b, g, i: (b * nq + i, g)),
             pl.BlockSpec((S, qw), lambda b, g, i: (b, g)),
             pl.BlockSpec((1, vw, S), lambda b, g, i: (b, g, 0))]
    scratch = []
    if fox:
        og, fc, fr = fox_extras
        ops += [og, fc, fr]
        specs += [pl.BlockSpec((tq, vw), lambda b, g, i: (b * nq + i, g)),
                  pl.BlockSpec((S, LANES), lambda b, g, i: (b, 0)),
                  pl.BlockSpec((1, FOX_HEADS, tq), lambda b, g, i: (b, 0, i))]
        scratch.append(pltpu.VMEM((2 * ATT_PAIRS, S, LANES), F32))
    return pl.pallas_call(
        functools.partial(_attn_kernel, fox),
        grid=(B, D // vw, nq),
        in_specs=specs,
        out_specs=pl.BlockSpec((tq, vw), lambda b, g, i: (b * nq + i, g)),
        out_shape=jax.ShapeDtypeStruct((T, D), BF16),
        scratch_shapes=scratch,
        compiler_params=_cparams("parallel", "parallel", "arbitrary"),
        name="fox_attn" if fox else "mla_attn",
    )(*ops)


assert DEPTH % 2 == 0

def kernel(x, c, positions, ada_w, ada_b, norm_mix_g, norm_ffn_g, final_norm_g, rw_mu, rw_w_rkv, rw_w_o, rw_w0, rw_w1, rw_w2, rw_a0, rw_a1, rw_a2, rw_g1, rw_g2, rw_k_k, rw_k_a, rw_r_k, rw_lnx_g, rw_lnx_b, rw_v0, rw_v1, rw_v2, mla_w_down, mla_q_norm_g, mla_kv_norm_g, mla_w_uq, mla_w_ukv, mla_w_o, fox_w_in, fox_b_f, fox_q_norm_g, fox_k_norm_g, fox_w_o, ffn_w_gate_up, ffn_w_down, moe_w_router, moe_b_router, moe_w_gate_up, moe_w_down):
    xf = x.reshape(T, D)
    mods = _ada_mods(c, ada_w, ada_b)
    ffn_wgu, ffn_wd = ffn_w_gate_up.astype(BF16), ffn_w_down.astype(BF16)
    moe_wgu, moe_wd = moe_w_gate_up.astype(BF16), moe_w_down.astype(BF16)
    v_first = None
    for i in range(DEPTH):
        mod = mods[i]
        kind, j = i % 3, i // 3
        if kind == 0:
            p = dict(mu=rw_mu[j], w_rkv=rw_w_rkv[j], w0=rw_w0[j], w1=rw_w1[j], w2=rw_w2[j],
                     a0=rw_a0[j], a1=rw_a1[j], a2=rw_a2[j], g1=rw_g1[j], g2=rw_g2[j],
                     k_k=rw_k_k[j], k_a=rw_k_a[j])
            if j > 0:
                p.update(v0=rw_v0[j - 1], v1=rw_v1[j - 1], v2=rw_v2[j - 1])
            r, lw, k, v, kk, a, g = _rwkv_proj(xf, mod, norm_mix_g[i], p, v_first if j > 0 else None)
            if j == 0:
                v_first = v
            y = _rwkv_scan(r, lw, k, v, kk, a, g, rw_r_k[j], rw_lnx_g[j], rw_lnx_b[j])
            xf = _proj_res(y, rw_w_o[j], xf, mod, 2)
        elif kind == 1:
            q, k, v = _mla_proj(xf, mod, norm_mix_g[i], positions, mla_w_down[j], mla_q_norm_g[j],
                                mla_kv_norm_g[j], mla_w_uq[j], mla_w_ukv[j])
            o = _attention(q, k, v)
            xf = _proj_res(o, mla_w_o[j], xf, mod, 2)
        else:
            q, k, v, og, fc, fr = _fox_proj(xf, mod, norm_mix_g[i], fox_w_in[j], fox_b_f[j],
                                            fox_q_norm_g[j], fox_k_norm_g[j])
            o = _attention(q, k, v, (og, fc, fr))
            xf = _proj_res(o, fox_w_o[j], xf, mod, 2)
        if i % 2 == 0:
            xf = _ffn_dense(xf, mod, norm_ffn_g[i], ffn_wgu, ffn_wd, i // 2)
        else:
            xf = _moe(xf, mod, norm_ffn_g[i], moe_w_router[i // 2], moe_b_router[i // 2],
                      moe_wgu, moe_wd, i // 2, final_norm_g if i == DEPTH - 1 else None)
    return xf.reshape(B, S, D)
```
